```python
import jax
import jax.numpy as jnp
from jax import lax
import numpy as np

D_MODEL = 1024
BATCH = 8
SEQ = 2048
DEPTH = 1

GRID_W = 64
MEM_LEN = 256
N_BRANCHES = 3
RET_HEADS = 4
RET_QK_DIM = 128
RET_V_DIM = 256
RET_CHUNK = 128
ROPE_BASE = 10000.0
NA_HEADS = 8
NA_HEAD_DIM = 64
NA_ROWS = 8
NA_COLS = 16
XA_HEADS = 4
XA_HEAD_DIM = 128
MOE_GROUPS = 4
MOE_EXPERTS_PER_GROUP = 4
MOE_TOP_K = 2
MOE_D_FF = 512
RMS_EPS = 1e-6
GN_EPS = 1e-5

RET_QK_WIDTH = RET_HEADS * RET_QK_DIM
RET_V_WIDTH = RET_HEADS * RET_V_DIM
NA_WIDTH = NA_HEADS * NA_HEAD_DIM
XA_WIDTH = XA_HEADS * XA_HEAD_DIM
IN_SPLITS = (RET_QK_WIDTH, RET_QK_WIDTH, RET_V_WIDTH, RET_V_WIDTH, NA_WIDTH, NA_WIDTH, NA_WIDTH, XA_WIDTH, N_BRANCHES * D_MODEL)
IN_WIDTH = 2 * RET_QK_WIDTH + 2 * RET_V_WIDTH + 3 * NA_WIDTH + XA_WIDTH + N_BRANCHES * D_MODEL

kernel_name = 'hybrid_retention_natten_memory_hmoe_encoder'


def _rmsnorm(x, g):
    xf = x.astype(jnp.float32)
    y = xf * lax.rsqrt(jnp.mean(xf * xf, axis=-1, keepdims=True) + RMS_EPS)
    return (y * g.astype(jnp.float32)).astype(x.dtype)


def _rope(a):
    seq, dh = a.shape[1], a.shape[-1]
    half = dh // 2
    inv_freq = ROPE_BASE ** (-jnp.arange(half, dtype=jnp.float32) / half)
    ang = jnp.arange(seq, dtype=jnp.float32)[:, None] * inv_freq[None, :]
    cos = jnp.cos(ang)[None, :, None, :]
    sin = jnp.sin(ang)[None, :, None, :]
    a1, a2 = a[..., :half], a[..., half:]
    return jnp.concatenate([a1 * cos - a2 * sin, a1 * sin + a2 * cos], axis=-1)


def _retention_one_direction(q, k, v, log_gamma, strict):
    b, h, s, dk = q.shape
    dv = v.shape[-1]
    c = RET_CHUNK
    n = s // c
    qc = q.reshape(b, h, n, c, dk)
    kc = k.reshape(b, h, n, c, dk)
    vc = v.reshape(b, h, n, c, dv)
    pos = jnp.arange(c, dtype=jnp.float32)
    diff = pos[:, None] - pos[None, :]
    mask = (diff > 0) if strict else (diff >= 0)
    decay = jnp.where(mask[None], jnp.exp(jnp.where(mask, diff, 0.0)[None] * log_gamma[:, None, None]), 0.0)
    scores = jnp.einsum('bhnid,bhnjd->bhnij', qc, kc) * decay[None, :, None]
    y_inner = jnp.einsum('bhnij,bhnje->bhnie', scores, vc)
    k_decay = jnp.exp((c - 1 - pos)[None, :] * log_gamma[:, None])
    q_decay = jnp.exp((pos + 1)[None, :] * log_gamma[:, None])
    chunk_decay = jnp.exp(c * log_gamma)[None, :, None, None]
    kv = jnp.einsum('bhnjd,hj,bhnje->bhnde', kc, k_decay, vc)

    def step(state, kv_chunk):
        return state * chunk_decay + kv_chunk, state

    _, prev = lax.scan(step, jnp.zeros((b, h, dk, dv), jnp.float32), jnp.moveaxis(kv, 2, 0))
    prev = jnp.moveaxis(prev, 0, 2)
    y_cross = jnp.einsum('bhnid,hi,bhnde->bhnie', qc, q_decay, prev)
    return (y_inner + y_cross).reshape(b, h, s, dv)


def _retention_branch(rq, rk, rv, rg, dec_f, dec_b, gn_gain, w_o):
    b, s, _ = rq.shape
    q = _rope(rq.reshape(b, s, RET_HEADS, RET_QK_DIM).astype(jnp.float32)) * (RET_QK_DIM ** -0.5)
    k = _rope(rk.reshape(b, s, RET_HEADS, RET_QK_DIM).astype(jnp.float32))
    v = rv.reshape(b, s, RET_HEADS, RET_V_DIM).astype(jnp.float32)
    q, k, v = q.transpose(0, 2, 1, 3), k.transpose(0, 2, 1, 3), v.transpose(0, 2, 1, 3)
    lg_f = jax.nn.log_sigmoid(dec_f.astype(jnp.float32))
    lg_b = jax.nn.log_sigmoid(dec_b.astype(jnp.float32))
    y_fwd = _retention_one_direction(q, k, v, lg_f, False)
    y_bwd = jnp.flip(_retention_one_direction(jnp.flip(q, 2), jnp.flip(k, 2), jnp.flip(v, 2), lg_b, True), 2)
    y = y_fwd + y_bwd
    mu = jnp.mean(y, axis=-1, keepdims=True)
    var = jnp.mean(jnp.square(y - mu), axis=-1, keepdims=True)
    y = (y - mu) * lax.rsqrt(var + GN_EPS)
    y = y.transpose(0, 2, 1, 3).reshape(b, s, RET_V_WIDTH) * gn_gain.astype(jnp.float32)
    return (jax.nn.silu(rg) * y.astype(rg.dtype)) @ w_o


def _neighbourhood_attention(nq, nk, nv, rpb):
    b, s, _ = nq.shape
    rows_n = s // GRID_W
    w = GRID_W
    kr = min(NA_ROWS, rows_n)
    kc = NA_COLS

    def to_grid(a):
        return a.reshape(b, rows_n, w, NA_HEADS, NA_HEAD_DIM).transpose(0, 3, 1, 2, 4)

    qg, kg, vg = to_grid(nq), to_grid(nk), to_grid(nv)
    rows = jnp.arange(rows_n)
    row_start = jnp.clip(rows - kr // 2, 0, rows_n - kr)
    row_idx = row_start[:, None] + jnp.arange(kr)[None, :]
    cols = jnp.arange(w)
    col_start = jnp.clip(cols - kc // 2, 0, w - kc)
    col_off = cols[None, :] - col_start[:, None]
    col_mask = (col_off >= 0) & (col_off < kc)
    k_rows = jnp.take(kg, row_idx, axis=2)
    v_rows = jnp.take(vg, row_idx, axis=2)
    sc = jnp.einsum('bhrqd,bhrwkd->bhrqwk', qg, k_rows).astype(jnp.float32) * (NA_HEAD_DIM ** -0.5)
    rel_r = row_idx - rows[:, None] + (NA_ROWS - 1)
    rel_c = jnp.clip(cols[None, :] - cols[:, None], -(kc - 1), kc - 1) + (kc - 1)
    bias = rpb[:, rel_r[:, None, :, None], rel_c[None, :, None, :]].astype(jnp.float32)
    sc = jnp.where(col_mask[:, None, :], sc + bias, -jnp.inf)
    p = jax.nn.softmax(sc, axis=(-2, -1)).astype(vg.dtype)
    o = jnp.einsum('bhrqwk,bhrwkd->bhrqd', p, v_rows)
    return o.transpose(0, 2, 3, 1, 4).reshape(b, s, NA_WIDTH)


def _memory_cross_attention(xq, mem_n, w_kv):
    b, s, _ = xq.shape
    m = mem_n.shape[1]
    mk, mv = jnp.split(mem_n @ w_kv, 2, axis=-1)
    q = xq.reshape(b, s, XA_HEADS, XA_HEAD_DIM)
    k = mk.reshape(b, m, XA_HEADS, XA_HEAD_DIM)
    v = mv.reshape(b, m, XA_HEADS, XA_HEAD_DIM)
    sc = jnp.einsum('bshd,bmhd->bhsm', q, k).astype(jnp.float32) * (XA_HEAD_DIM ** -0.5)
    p = jax.nn.softmax(sc, axis=-1).astype(v.dtype)
    return jnp.einsum('bhsm,bmhd->bshd', p, v).reshape(b, s, XA_WIDTH)


def _hierarchical_moe(h, w_rg, b_rg, w_re, b_re, w_gate, w_up, w_down):
    b, s, d = h.shape
    t = h.reshape(b * s, d)
    g_n, e_n = MOE_GROUPS, MOE_EXPERTS_PER_GROUP
    grp_p = jax.nn.softmax((t @ w_rg + b_rg).astype(jnp.float32), axis=-1)
    grp_w, grp_idx = lax.top_k(grp_p, 1)
    exp_logits = (t @ w_re + b_re).astype(jnp.float32).reshape(-1, g_n, e_n)
    in_grp = jnp.take_along_axis(exp_logits, grp_idx[:, :, None], axis=1)[:, 0]
    top_w, top_idx = lax.top_k(jax.nn.softmax(in_grp, axis=-1), MOE_TOP_K)
    top_w = top_w / jnp.sum(top_w, axis=-1, keepdims=True) * grp_w
    exp_w = jnp.sum(jax.nn.one_hot(top_idx, e_n, dtype=jnp.float32) * top_w[..., None], axis=1)
    combine = (jax.nn.one_hot(grp_idx[:, 0], g_n, dtype=jnp.float32)[:, :, None] * exp_w[:, None, :]).astype(t.dtype)
    out = jnp.zeros_like(t)
    for g in range(g_n):
        a = jnp.einsum('td,edf->tef', t, w_gate[g])
        u = jnp.einsum('td,edf->tef', t, w_up[g])
        hid = jax.nn.silu(a) * u * combine[:, g, :, None]
        out = out + jnp.einsum('tef,efd->td', hid, w_down[g])
    return out.reshape(b, s, d)


def setup_inputs(seed: int = 0) -> dict:
    key = jax.random.key(seed)
    ks = jax.random.split(key, 24)
    f32 = jnp.float32
    d = D_MODEL
    n_l = DEPTH
    g_n, e_n, f_n = MOE_GROUPS, MOE_EXPERTS_PER_GROUP, MOE_D_FF

    def nrm(k, shape, fan_in):
        return jax.random.normal(k, shape, f32) * (fan_in ** -0.5)

    def gain(k, shape):
        return 1.0 + 0.02 * jax.random.normal(k, shape, f32)

    base_gamma = 1.0 - 2.0 ** (-5.0 - jnp.arange(RET_HEADS, dtype=f32))
    base_logit = jnp.log(base_gamma) - jnp.log1p(-base_gamma)
    return {
        'x': jax.random.normal(ks[0], (BATCH, SEQ, d), f32),
        'mem': jax.random.normal(ks[1], (BATCH, MEM_LEN, d), f32),
        'g_mix': gain(ks[2], (n_l, d)),
        'w_in': nrm(ks[3], (n_l, d, IN_WIDTH), d),
        'ret_decay_fwd': base_logit[None] + 0.05 * jax.random.normal(ks[4], (n_l, RET_HEADS), f32),
        'ret_decay_bwd': base_logit[None] + 0.05 * jax.random.normal(ks[5], (n_l, RET_HEADS), f32),
        'ret_norm_gain': gain(ks[6], (n_l, RET_V_WIDTH)),
        'w_ret_o': nrm(ks[7], (n_l, RET_V_WIDTH, d), RET_V_WIDTH),
        'na_rpb': 0.1 * jax.random.normal(ks[8], (n_l, NA_HEADS, 2 * NA_ROWS - 1, 2 * NA_COLS - 1), f32),
        'w_na_o': nrm(ks[9], (n_l, NA_WIDTH, d), NA_WIDTH),
        'g_mem': gain(ks[10], (n_l, d)),
        'w_mem_kv': nrm(ks[11], (n_l, d, 2 * XA_WIDTH), d),
        'w_xa_o': nrm(ks[12], (n_l, XA_WIDTH, d), XA_WIDTH),
        'w_out': nrm(ks[13], (n_l, d, d), d),
        'g_ffn': gain(ks[14], (n_l, d)),
        'w_router_group': nrm(ks[15], (n_l, d, g_n), d),
        'b_router_group': 0.01 * jax.random.normal(ks[16], (n_l, g_n), f32),
        'w_router_expert': nrm(ks[17], (n_l, d, g_n * e_n), d),
        'b_router_expert': 0.01 * jax.random.normal(ks[18], (n_l, g_n * e_n), f32),
        'w_exp_gate': nrm(ks[19], (n_l, g_n, e_n, d, f_n), d),
        'w_exp_up': nrm(ks[20], (n_l, g_n, e_n, d, f_n), d),
        'w_exp_down': nrm(ks[21], (n_l, g_n, e_n, f_n, d), f_n),
        'g_final': gain(ks[22], (d,)),
    }


def reference(x, mem, g_mix, w_in, ret_decay_fwd, ret_decay_bwd, ret_norm_gain, w_ret_o, na_rpb, w_na_o, g_mem, w_mem_kv, w_xa_o, w_out, g_ffn, w_router_group, b_router_group, w_router_expert, b_router_expert, w_exp_gate, w_exp_up, w_exp_down, g_final):
    split_points = [int(p) for p in np.cumsum(np.array(IN_SPLITS))[:-1]]
    for l in range(DEPTH):
        h = _rmsnorm(x, g_mix[l])
        rq, rk, rv, rg, nq, nk, nv, xq, gate_logits = jnp.split(h @ w_in[l], split_points, axis=-1)
        y_ret = _retention_branch(rq, rk, rv, rg, ret_decay_fwd[l], ret_decay_bwd[l], ret_norm_gain[l], w_ret_o[l])
        y_na = _neighbourhood_attention(nq, nk, nv, na_rpb[l]) @ w_na_o[l]
        y_xa = _memory_cross_attention(xq, _rmsnorm(mem, g_mem[l]), w_mem_kv[l]) @ w_xa_o[l]
        g_ret, g_na, g_xa = jnp.split(jax.nn.sigmoid(gate_logits), N_BRANCHES, axis=-1)
        x = x + (g_ret * y_ret + g_na * y_na + g_xa * y_xa) @ w_out[l]
        x = x + _hierarchical_moe(_rmsnorm(x, g_ffn[l]), w_router_group[l], b_router_group[l], w_router_expert[l], b_router_expert[l], w_exp_gate[l], w_exp_up[l], w_exp_down[l])
    return _rmsnorm(x, g_final)
```

```python
import functools

import jax
import jax.numpy as jnp
import numpy as np
from jax import lax
from jax.experimental import pallas as pl
from jax.experimental.pallas import tpu as pltpu

D_MODEL = 1024
GRID_W = 64
N_BRANCHES = 3
RET_HEADS = 4
RET_QK_DIM = 128
RET_V_DIM = 256
RET_CHUNK = 128
ROPE_BASE = 10000.0
NA_HEADS = 8
NA_HEAD_DIM = 64
NA_ROWS = 8
NA_COLS = 16
XA_HEADS = 4
XA_HEAD_DIM = 128
MOE_GROUPS = 4
MOE_EXPERTS_PER_GROUP = 4
MOE_TOP_K = 2
MOE_D_FF = 512
RMS_EPS = 1e-6
GN_EPS = 1e-5

RET_QK_WIDTH = RET_HEADS * RET_QK_DIM
RET_V_WIDTH = RET_HEADS * RET_V_DIM
NA_WIDTH = NA_HEADS * NA_HEAD_DIM
XA_WIDTH = XA_HEADS * XA_HEAD_DIM
IN_WIDTH = 2 * RET_QK_WIDTH + 2 * RET_V_WIDTH + 3 * NA_WIDTH + XA_WIDTH + N_BRANCHES * D_MODEL

OFF_RQ = 0
OFF_RK = OFF_RQ + RET_QK_WIDTH
OFF_RV = OFF_RK + RET_QK_WIDTH
OFF_RG = OFF_RV + RET_V_WIDTH
OFF_NQ = OFF_RG + RET_V_WIDTH
OFF_NK = OFF_NQ + NA_WIDTH
OFF_NV = OFF_NK + NA_WIDTH
OFF_XQ = OFF_NV + NA_WIDTH
OFF_GATE = OFF_XQ + XA_WIDTH

N_EXPERTS = MOE_GROUPS * MOE_EXPERTS_PER_GROUP
LANES = 128
ROUTER_EXPERT_LANE0 = MOE_GROUPS

VMEM_LIMIT = 48 * 1024 * 1024

F32 = jnp.float32
BF16 = jnp.bfloat16


def _params(*sem):
    return pltpu.CompilerParams(dimension_semantics=sem, vmem_limit_bytes=VMEM_LIMIT)


def _rmsnorm_f32(x, g):
    return x * lax.rsqrt(jnp.mean(x * x, axis=-1, keepdims=True) + RMS_EPS) * g


def _sigmoid(x):
    return 1.0 / (1.0 + jnp.exp(-x))


def _dot(a, b):
    return jnp.dot(a, b, preferred_element_type=F32)


def _dot_nt(a, b):
    return lax.dot_general(a, b, (((1,), (1,)), ((), ())), preferred_element_type=F32)


def _inproj_kernel(x_ref, g_ref, w_ref, o_ref, h_scr):
    @pl.when(pl.program_id(1) == 0)
    def _():
        h_scr[...] = _rmsnorm_f32(x_ref[...], g_ref[...]).astype(BF16)

    o_ref[...] = _dot(h_scr[...], w_ref[...]).astype(o_ref.dtype)


def _inproj(x2d, g, w_bf16, tm=1024, tn=1024):
    t, d = x2d.shape
    n = w_bf16.shape[1]
    return pl.pallas_call(
        _inproj_kernel,
        grid=(t // tm, n // tn),
        in_specs=[
            pl.BlockSpec((tm, d), lambda i, j: (i, 0)),
            pl.BlockSpec((1, d), lambda i, j: (0, 0)),
            pl.BlockSpec((d, tn), lambda i, j: (0, j)),
        ],
        out_specs=pl.BlockSpec((tm, tn), lambda i, j: (i, j)),
        out_shape=jax.ShapeDtypeStruct((t, n), BF16),
        scratch_shapes=[pltpu.VMEM((tm, d), BF16)],
        compiler_params=_params("parallel", "arbitrary"),
        name="inproj",
    )(x2d, g, w_bf16)


def _log_sigmoid(x):
    return jnp.minimum(x, 0.0) - jnp.log1p(jnp.exp(-jnp.abs(x)))


def _retention_kernel(decf_ref, decb_ref, q_ref, k_ref, v_ref, rg_ref, cos_ref, sin_ref, gn_ref, o_ref,
                      qs, ks, y_scr, st_scr):
    h = pl.program_id(1)
    c = RET_CHUNK
    seq = q_ref.shape[0]
    n_chunks = seq // c
    half = RET_QK_DIM // 2

    lgf = _log_sigmoid(jnp.full((1, 1), decf_ref[h], F32))
    lgb = _log_sigmoid(jnp.full((1, 1), decb_ref[h], F32))

    cos = cos_ref[...]
    sin = sin_ref[...]
    q = q_ref[...].astype(F32)
    qs[...] = ((q * cos + pltpu.roll(q, half, 1) * sin) * (RET_QK_DIM ** -0.5)).astype(BF16)
    k = k_ref[...].astype(F32)
    ks[...] = (k * cos + pltpu.roll(k, half, 1) * sin).astype(BF16)

    ii = lax.broadcasted_iota(jnp.int32, (c, c), 0)
    jj = lax.broadcasted_iota(jnp.int32, (c, c), 1)
    diff = (ii - jj).astype(F32)
    dmat = jnp.exp(jnp.where(diff >= 0, diff * lgf, (-diff) * lgb))
    pos = lax.broadcasted_iota(jnp.int32, (c, 1), 0).astype(F32)
    qd_f = jnp.exp((pos + 1.0) * lgf)
    kd_f = jnp.exp((c - 1.0 - pos) * lgf)
    cd_f = jnp.exp(c * lgf)
    qd_b = jnp.exp((c - pos) * lgb)
    kd_b = jnp.exp(pos * lgb)
    cd_b = jnp.exp(c * lgb)

    st_scr[...] = jnp.zeros_like(st_scr)

    def fwd(n, carry):
        off = pl.multiple_of(n * c, c)
        qc = qs[pl.ds(off, c), :]
        kc = ks[pl.ds(off, c), :]
        vc = v_ref[pl.ds(off, c), :]
        s = _dot_nt(qc, kc) * dmat
        y = _dot(s.astype(BF16), vc)
        st = st_scr[...]
        y = y + _dot((qc.astype(F32) * qd_f).astype(BF16), st.astype(BF16))
        y_scr[pl.ds(off, c), :] = y
        kt = (kc.astype(F32) * kd_f).T.astype(BF16)
        st_scr[...] = st * cd_f + _dot(kt, vc)
        return carry

    lax.fori_loop(0, n_chunks, fwd, 0)

    st_scr[...] = jnp.zeros_like(st_scr)
    gn = gn_ref[...]

    def bwd(i, carry):
        n = n_chunks - 1 - i
        off = pl.multiple_of(n * c, c)
        qc = qs[pl.ds(off, c), :]
        kc = ks[pl.ds(off, c), :]
        vc = v_ref[pl.ds(off, c), :]
        st = st_scr[...]
        y = y_scr[pl.ds(off, c), :] + _dot((qc.astype(F32) * qd_b).astype(BF16), st.astype(BF16))
        kt = (kc.astype(F32) * kd_b).T.astype(BF16)
        st_scr[...] = st * cd_b + _dot(kt, vc)
        mu = jnp.mean(y, axis=-1, keepdims=True)
        yc = y - mu
        var = jnp.mean(yc * yc, axis=-1, keepdims=True)
        yn = yc * lax.rsqrt(var + GN_EPS) * gn
        rg = rg_ref[pl.ds(off, c), :].astype(F32)
        o_ref[pl.ds(off, c), :] = (rg * _sigmoid(rg) * yn).astype(o_ref.dtype)
        return carry

    lax.fori_loop(0, n_chunks, bwd, 0)


def _retention(proj3, dec_f, dec_b, cos_t, sin_t, gn_gain):
    b, s, _ = proj3.shape
    qb, vb = RET_QK_DIM, RET_V_DIM
    smem = pl.BlockSpec(memory_space=pltpu.SMEM)
    return pl.pallas_call(
        _retention_kernel,
        grid=(b, RET_HEADS),
        in_specs=[
            smem,
            smem,
            pl.BlockSpec((None, s, qb), lambda i, h: (i, 0, OFF_RQ // qb + h)),
            pl.BlockSpec((None, s, qb), lambda i, h: (i, 0, OFF_RK // qb + h)),
            pl.BlockSpec((None, s, vb), lambda i, h: (i, 0, OFF_RV // vb + h)),
            pl.BlockSpec((None, s, vb), lambda i, h: (i, 0, OFF_RG // vb + h)),
            pl.BlockSpec((s, qb), lambda i, h: (0, 0)),
            pl.BlockSpec((s, qb), lambda i, h: (0, 0)),
            pl.BlockSpec((1, vb), lambda i, h: (0, h)),
        ],
        out_specs=pl.BlockSpec((None, s, vb), lambda i, h: (i, 0, h)),
        out_shape=jax.ShapeDtypeStruct((b, s, RET_V_WIDTH), BF16),
        scratch_shapes=[
            pltpu.VMEM((s, qb), BF16),
            pltpu.VMEM((s, qb), BF16),
            pltpu.VMEM((s, vb), F32),
            pltpu.VMEM((qb, vb), F32),
        ],
        compiler_params=_params("parallel", "parallel"),
        name="retention",
    )(dec_f, dec_b, proj3, proj3, proj3, proj3, cos_t, sin_t, gn_gain)


def _rope_tables(seq):
    half = RET_QK_DIM // 2
    inv_freq = ROPE_BASE ** (-jnp.arange(half, dtype=F32) / half)
    ang = jnp.arange(seq, dtype=F32)[:, None] * inv_freq[None, :]
    cos, sin = jnp.cos(ang), jnp.sin(ang)
    return jnp.concatenate([cos, cos], axis=-1), jnp.concatenate([-sin, sin], axis=-1)


def _na_bias_table(rpb):
    w = GRID_W
    cols = np.arange(w)
    col_start = np.clip(cols - NA_COLS // 2, 0, w - NA_COLS)
    col_off = cols[None, :] - col_start[:, None]
    col_mask = (col_off >= 0) & (col_off < NA_COLS)
    rel_c = np.clip(cols[None, :] - cols[:, None], -(NA_COLS - 1), NA_COLS - 1) + (NA_COLS - 1)
    offs = np.arange(NA_ROWS)
    rel_r = offs[:, None] + np.arange(NA_ROWS)[None, :]
    tab = rpb[:, rel_r[:, None, :, None], rel_c[None, :, None, :]].astype(F32)
    tab = jnp.where(col_mask[None, None, :, None, :], tab, -jnp.inf)
    return tab.reshape(rpb.shape[0], NA_ROWS, w, NA_ROWS * w)


def _na_kernel(q_ref, k_ref, v_ref, bias_ref, o_ref):
    w = GRID_W
    seq = q_ref.shape[0]
    rows_n = seq // w
    kr = NA_ROWS
    lane = lax.broadcasted_iota(jnp.int32, (w, LANES), 1)
    first = lane < NA_HEAD_DIM
    scale = NA_HEAD_DIM ** -0.5

    def body(r, carry):
        rs = jnp.clip(r - kr // 2, 0, rows_n - kr)
        off = rs - r + (NA_ROWS - 1)
        qr = q_ref[pl.ds(pl.multiple_of(r * w, w), w), :]
        koff = pl.multiple_of(rs * w, w)
        kk = k_ref[pl.ds(koff, kr * w), :]
        vv = v_ref[pl.ds(koff, kr * w), :]
        outs = []
        for hh in range(2):
            keep = first if hh == 0 else jnp.logical_not(first)
            qm = jnp.where(keep, qr, jnp.zeros_like(qr))
            s = _dot_nt(qm, kk) * scale + bias_ref[hh, off]
            m = jnp.max(s, axis=-1, keepdims=True)
            p = jnp.exp(s - m)
            l = jnp.sum(p, axis=-1, keepdims=True)
            outs.append(_dot(p.astype(BF16), vv) / l)
        o_ref[pl.ds(pl.multiple_of(r * w, w), w), :] = jnp.where(first, outs[0], outs[1]).astype(o_ref.dtype)
        return carry

    lax.fori_loop(0, rows_n, body, 0)


def _neighbourhood_attention(proj3, bias_tab):
    b, s, _ = proj3.shape
    pairs = NA_HEADS // 2
    return pl.pallas_call(
        _na_kernel,
        grid=(b, pairs),
        in_specs=[
            pl.BlockSpec((None, s, LANES), lambda i, p: (i, 0, OFF_NQ // LANES + p)),
            pl.BlockSpec((None, s, LANES), lambda i, p: (i, 0, OFF_NK // LANES + p)),
            pl.BlockSpec((None, s, LANES), lambda i, p: (i, 0, OFF_NV // LANES + p)),
            pl.BlockSpec((2, NA_ROWS, GRID_W, NA_ROWS * GRID_W), lambda i, p: (p, 0, 0, 0)),
        ],
        out_specs=pl.BlockSpec((None, s, LANES), lambda i, p: (i, 0, p)),
        out_shape=jax.ShapeDtypeStruct((b, s, NA_WIDTH), BF16),
        compiler_params=_params("parallel", "parallel"),
        name="nbr_attn",
    )(proj3, proj3, proj3, bias_tab)


def _xa_kernel(q_ref, mem_ref, g_ref, wkv_ref, o_ref, kv_scr):
    @pl.when(pl.program_id(1) == 0)
    def _():
        mn = _rmsnorm_f32(mem_ref[...], g_ref[...]).astype(BF16)
        kv_scr[...] = _dot(mn, wkv_ref[...]).astype(BF16)

    dh = XA_HEAD_DIM
    scale = dh ** -0.5
    for h in range(XA_HEADS):
        q = q_ref[:, h * dh:(h + 1) * dh]
        k = kv_scr[:, h * dh:(h + 1) * dh]
        v = kv_scr[:, XA_WIDTH + h * dh:XA_WIDTH + (h + 1) * dh]
        s = _dot_nt(q, k) * scale
        m = jnp.max(s, axis=-1, keepdims=True)
        p = jnp.exp(s - m)
        l = jnp.sum(p, axis=-1, keepdims=True)
        o_ref[:, h * dh:(h + 1) * dh] = (_dot(p.astype(BF16), v) / l).astype(o_ref.dtype)


def _memory_attention(proj3, mem, g_mem, wkv_bf16, ts=1024):
    b, s, _ = proj3.shape
    m, d = mem.shape[1], mem.shape[2]
    return pl.pallas_call(
        _xa_kernel,
        grid=(b, s // ts),
        in_specs=[
            pl.BlockSpec((None, ts, XA_WIDTH), lambda i, j: (i, j, OFF_XQ // XA_WIDTH)),
            pl.BlockSpec((None, m, d), lambda i, j: (i, 0, 0)),
            pl.BlockSpec((1, d), lambda i, j: (0, 0)),
            pl.BlockSpec((d, 2 * XA_WIDTH), lambda i, j: (0, 0)),
        ],
        out_specs=pl.BlockSpec((None, ts, XA_WIDTH), lambda i, j: (i, j, 0)),
        out_shape=jax.ShapeDtypeStruct((b, s, XA_WIDTH), BF16),
        scratch_shapes=[pltpu.VMEM((m, 2 * XA_WIDTH), BF16)],
        compiler_params=_params("parallel", "arbitrary"),
        name="mem_attn",
    )(proj3, mem, g_mem, wkv_bf16)


def _masked_lane_max(v, mask):
    return jnp.max(jnp.where(mask, v, -jnp.inf), axis=-1, keepdims=True)


def _first_lane_eq(v, target, mask, lane):
    return jnp.min(jnp.where(mask & (v == target), lane, float(LANES)), axis=-1, keepdims=True)


def _route(logits):
    g_n, e_n = MOE_GROUPS, MOE_EXPERTS_PER_GROUP
    lane = lax.broadcasted_iota(jnp.int32, logits.shape, 1).astype(F32)
    is_grp = lane < g_n
    gmax = _masked_lane_max(logits, is_grp)
    gsum = jnp.sum(jnp.where(is_grp, jnp.exp(logits - gmax), 0.0), axis=-1, keepdims=True)
    grp_w = 1.0 / gsum
    gidx = _first_lane_eq(logits, gmax, is_grp, lane)
    lo = ROUTER_EXPERT_LANE0 + gidx * e_n
    in_grp = (lane >= lo) & (lane < lo + e_n)
    emax = _masked_lane_max(logits, in_grp)
    ex = jnp.where(in_grp, jnp.exp(logits - emax), 0.0)
    prob = ex / jnp.sum(ex, axis=-1, keepdims=True)
    p1 = _masked_lane_max(prob, in_grp)
    i1 = _first_lane_eq(prob, p1, in_grp, lane)
    rest = in_grp & (lane != i1)
    p2 = _masked_lane_max(prob, rest)
    i2 = _first_lane_eq(prob, p2, rest, lane)
    tot = p1 + p2
    w1 = p1 / tot * grp_w
    w2 = p2 / tot * grp_w
    return jnp.where(lane == i1, w1, jnp.where(lane == i2, w2, 0.0))


def _merge_kernel(yret_ref, yna_ref, yxa_ref, gr_ref, gn_ref, gx_ref, x_ref, wro_ref, wno_ref, wxo_ref, wout_ref,
                  gffn_ref, wr_ref, br_ref, x2_ref, h2_ref, comb_ref):
    y_ret = _dot(yret_ref[...], wro_ref[...])
    y_na = _dot(yna_ref[...], wno_ref[...])
    y_xa = _dot(yxa_ref[...], wxo_ref[...])
    mix = (_sigmoid(gr_ref[...].astype(F32)) * y_ret + _sigmoid(gn_ref[...].astype(F32)) * y_na
           + _sigmoid(gx_ref[...].astype(F32)) * y_xa)
    x2 = x_ref[...] + _dot(mix.astype(BF16), wout_ref[...])
    x2_ref[...] = x2
    h2 = _rmsnorm_f32(x2, gffn_ref[...])
    h2_ref[...] = h2.astype(BF16)
    logits = jnp.dot(h2, wr_ref[...], preferred_element_type=F32, precision=lax.Precision.HIGHEST) + br_ref[...]
    comb_ref[...] = _route(logits)


def _merge(y_ret, y_na, y_xa, proj, x2d, w_ret_o, w_na_o, w_xa_o, w_out, g_ffn, w_router, b_router, tm=512):
    t, d = x2d.shape
    gate_blk = OFF_GATE // d
    full = lambda a: pl.BlockSpec(a.shape, lambda i: (0,) * a.ndim)
    return pl.pallas_call(
        _merge_kernel,
        grid=(t // tm,),
        in_specs=[
            pl.BlockSpec((tm, RET_V_WIDTH), lambda i: (i, 0)),
            pl.BlockSpec((tm, NA_WIDTH), lambda i: (i, 0)),
            pl.BlockSpec((tm, XA_WIDTH), lambda i: (i, 0)),
            pl.BlockSpec((tm, d), lambda i: (i, gate_blk)),
            pl.BlockSpec((tm, d), lambda i: (i, gate_blk + 1)),
            pl.BlockSpec((tm, d), lambda i: (i, gate_blk + 2)),
            pl.BlockSpec((tm, d), lambda i: (i, 0)),
            full(w_ret_o), full(w_na_o), full(w_xa_o), full(w_out), full(g_ffn), full(w_router), full(b_router),
        ],
        out_specs=[
            pl.BlockSpec((tm, d), lambda i: (i, 0)),
            pl.BlockSpec((tm, d), lambda i: (i, 0)),
            pl.BlockSpec((tm, LANES), lambda i: (i, 0)),
        ],
        out_shape=[
            jax.ShapeDtypeStruct((t, d), F32),
            jax.ShapeDtypeStruct((t, d), BF16),
            jax.ShapeDtypeStruct((t, LANES), F32),
        ],
        compiler_params=_params("parallel"),
        name="merge_router",
    )(y_ret, y_na, y_xa, proj, proj, proj, x2d, w_ret_o, w_na_o, w_xa_o, w_out, g_ffn, w_router, b_router)


def _moe_kernel(h_ref, comb_ref, x2_ref, wg_ref, wu_ref, wd_ref, gfin_ref, o_ref, acc, *, final_norm):
    e = pl.program_id(1)

    @pl.when(e == 0)
    def _():
        acc[...] = jnp.zeros_like(acc)

    h = h_ref[...]
    a = _dot(h, wg_ref[...])
    u = _dot(h, wu_ref[...])
    lane = lax.broadcasted_iota(jnp.int32, comb_ref.shape, 1)
    cw = jnp.sum(jnp.where(lane == e + ROUTER_EXPERT_LANE0, comb_ref[...], 0.0), axis=-1, keepdims=True)
    hid = a * _sigmoid(a) * u * cw
    acc[...] += _dot(hid.astype(BF16), wd_ref[...])

    @pl.when(e == pl.num_programs(1) - 1)
    def _():
        y = x2_ref[...] + acc[...]
        if final_norm:
            y = _rmsnorm_f32(y, gfin_ref[...])
        o_ref[...] = y


def _moe(h2, comb, x2, wg, wu, wd, g_final, final_norm, tm=1024):
    t, d = x2.shape
    n_e, _, f = wg.shape
    return pl.pallas_call(
        functools.partial(_moe_kernel, final_norm=final_norm),
        grid=(t // tm, n_e),
        in_specs=[
            pl.BlockSpec((tm, d), lambda i, e: (i, 0)),
            pl.BlockSpec((tm, LANES), lambda i, e: (i, 0)),
            pl.BlockSpec((tm, d), lambda i, e: (i, 0)),
            pl.BlockSpec((None, d, f), lambda i, e: (e, 0, 0)),
            pl.BlockSpec((None, d, f), lambda i, e: (e, 0, 0)),
            pl.BlockSpec((None, f, d), lambda i, e: (e, 0, 0)),
            pl.BlockSpec((1, d), lambda i, e: (0, 0)),
        ],
        out_specs=pl.BlockSpec((tm, d), lambda i, e: (i, 0)),
        out_shape=jax.ShapeDtypeStruct((t, d), F32),
        scratch_shapes=[pltpu.VMEM((tm, d), F32)],
        compiler_params=_params("parallel", "arbitrary"),
        name="experts",
    )(h2, comb, x2, wg, wu, wd, g_final)


def kernel(x, mem, g_mix, w_in, ret_decay_fwd, ret_decay_bwd, ret_norm_gain, w_ret_o, na_rpb, w_na_o, g_mem, w_mem_kv, w_xa_o, w_out, g_ffn, w_router_group, b_router_group, w_router_expert, b_router_expert, w_exp_gate, w_exp_up, w_exp_down, g_final):
    b, s, d = x.shape
    depth = w_in.shape[0]
    t = b * s
    cos_t, sin_t = _rope_tables(s)
    row = lambda v: v.reshape(1, -1).astype(F32)
    x2d = x.reshape(t, d)
    for l in range(depth):
        proj = _inproj(x2d, row(g_mix[l]), w_in[l].astype(BF16))
        proj3 = proj.reshape(b, s, IN_WIDTH)
        y_ret = _retention(proj3, ret_decay_fwd[l].astype(F32), ret_decay_bwd[l].astype(F32), cos_t, sin_t,
                           row(ret_norm_gain[l]))
        y_na = _neighbourhood_attention(proj3, _na_bias_table(na_rpb[l]))
        y_xa = _memory_attention(proj3, mem, row(g_mem[l]), w_mem_kv[l].astype(BF16))
        n_r = MOE_GROUPS + N_EXPERTS
        w_router = jnp.pad(jnp.concatenate([w_router_group[l], w_router_expert[l]], axis=1).astype(F32),
                           ((0, 0), (0, LANES - n_r)))
        b_router = jnp.pad(jnp.concatenate([b_router_group[l], b_router_expert[l]]).astype(F32),
                           (0, LANES - n_r)).reshape(1, LANES)
        x2, h2, comb = _merge(y_ret.reshape(t, -1), y_na.reshape(t, -1), y_xa.reshape(t, -1), proj, x2d,
                              w_ret_o[l].astype(BF16), w_na_o[l].astype(BF16), w_xa_o[l].astype(BF16),
                              w_out[l].astype(BF16), row(g_ffn[l]), w_router, b_router)
        f = w_exp_gate.shape[-1]
        x2d = _moe(h2, comb, x2,
                   w_exp_gate[l].reshape(N_EXPERTS, d, f).astype(BF16),
                   w_exp_up[l].reshape(N_EXPERTS, d, f).astype(BF16),
                   w_exp_down[l].reshape(N_EXPERTS, f, d).astype(BF16),
                   row(g_final), final_norm=(l == depth - 1))
    return x2d.reshape(b, s, d)
```

```python
import functools

import jax
import jax.numpy as jnp
import numpy as np
from jax import lax
from jax.experimental import pallas as pl
from jax.experimental.pallas import tpu as pltpu

D_MODEL = 1024
GRID_W = 64
N_BRANCHES = 3
RET_HEADS = 4
RET_QK_DIM = 128
RET_V_DIM = 256
RET_CHUNK = 128
ROPE_BASE = 10000.0
NA_HEADS = 8
NA_HEAD_DIM = 64
NA_ROWS = 8
NA_COLS = 16
XA_HEADS = 4
XA_HEAD_DIM = 128
MOE_GROUPS = 4
MOE_EXPERTS_PER_GROUP = 4
MOE_TOP_K = 2
MOE_D_FF = 512
RMS_EPS = 1e-6
GN_EPS = 1e-5

RET_QK_WIDTH = RET_HEADS * RET_QK_DIM
RET_V_WIDTH = RET_HEADS * RET_V_DIM
NA_WIDTH = NA_HEADS * NA_HEAD_DIM
XA_WIDTH = XA_HEADS * XA_HEAD_DIM
IN_WIDTH = 2 * RET_QK_WIDTH + 2 * RET_V_WIDTH + 3 * NA_WIDTH + XA_WIDTH + N_BRANCHES * D_MODEL

OFF_RQ = 0
OFF_RK = OFF_RQ + RET_QK_WIDTH
OFF_RV = OFF_RK + RET_QK_WIDTH
OFF_RG = OFF_RV + RET_V_WIDTH
OFF_NQ = OFF_RG + RET_V_WIDTH
OFF_NK = OFF_NQ + NA_WIDTH
OFF_NV = OFF_NK + NA_WIDTH
OFF_XQ = OFF_NV + NA_WIDTH
OFF_GATE = OFF_XQ + XA_WIDTH

N_EXPERTS = MOE_GROUPS * MOE_EXPERTS_PER_GROUP
LANES = 128
ROUTER_EXPERT_LANE0 = MOE_GROUPS

VMEM_LIMIT = 48 * 1024 * 1024

F32 = jnp.float32
BF16 = jnp.bfloat16


def _params(*sem):
    return pltpu.CompilerParams(dimension_semantics=sem, vmem_limit_bytes=VMEM_LIMIT)


def _rmsnorm_f32(x, g):
    return x * lax.rsqrt(jnp.mean(x * x, axis=-1, keepdims=True) + RMS_EPS) * g


def _sigmoid(x):
    return 1.0 / (1.0 + jnp.exp(-x))


def _dot(a, b):
    return jnp.dot(a, b, preferred_element_type=F32)


def _dot_nt(a, b):
    return lax.dot_general(a, b, (((1,), (1,)), ((), ())), preferred_element_type=F32)


def _inproj_kernel(x_ref, g_ref, w_ref, o_ref, h_scr):
    @pl.when(pl.program_id(1) == 0)
    def _():
        h_scr[...] = _rmsnorm_f32(x_ref[...], g_ref[...]).astype(BF16)

    o_ref[...] = _dot(h_scr[...], w_ref[...]).astype(o_ref.dtype)


def _inproj(x2d, g, w_bf16, tm=1024, tn=1024):
    t, d = x2d.shape
    n = w_bf16.shape[1]
    return pl.pallas_call(
        _inproj_kernel,
        grid=(t // tm, n // tn),
        in_specs=[
            pl.BlockSpec((tm, d), lambda i, j: (i, 0)),
            pl.BlockSpec((1, d), lambda i, j: (0, 0)),
            pl.BlockSpec((d, tn), lambda i, j: (0, j)),
        ],
        out_specs=pl.BlockSpec((tm, tn), lambda i, j: (i, j)),
        out_shape=jax.ShapeDtypeStruct((t, n), BF16),
        scratch_shapes=[pltpu.VMEM((tm, d), BF16)],
        compiler_params=_params("parallel", "arbitrary"),
        name="inproj",
    )(x2d, g, w_bf16)


def _log_sigmoid(x):
    return jnp.minimum(x, 0.0) - jnp.log1p(jnp.exp(-jnp.abs(x)))


def _retention_kernel(decf_ref, decb_ref, q_ref, k_ref, v_ref, rg_ref, cos_ref, sin_ref, gn_ref, o_ref,
                      qs, ks, y_scr, st_scr):
    h = pl.program_id(1)
    c = RET_CHUNK
    seq = q_ref.shape[0]
    n_chunks = seq // c
    half = RET_QK_DIM // 2

    lgf = _log_sigmoid(jnp.full((1, 1), decf_ref[h], F32))
    lgb = _log_sigmoid(jnp.full((1, 1), decb_ref[h], F32))

    cos = cos_ref[...]
    sin = sin_ref[...]
    q = q_ref[...].astype(F32)
    qs[...] = ((q * cos + pltpu.roll(q, half, 1) * sin) * (RET_QK_DIM ** -0.5)).astype(BF16)
    k = k_ref[...].astype(F32)
    ks[...] = (k * cos + pltpu.roll(k, half, 1) * sin).astype(BF16)

    ii = lax.broadcasted_iota(jnp.int32, (c, c), 0)
    jj = lax.broadcasted_iota(jnp.int32, (c, c), 1)
    diff = (ii - jj).astype(F32)
    dmat = jnp.exp(jnp.where(diff >= 0, diff * lgf, (-diff) * lgb))
    pos = lax.broadcasted_iota(jnp.int32, (c, 1), 0).astype(F32)
    qd_f = jnp.exp((pos + 1.0) * lgf)
    kd_f = jnp.exp((c - 1.0 - pos) * lgf)
    cd_f = jnp.exp(c * lgf)
    qd_b = jnp.exp((c - pos) * lgb)
    kd_b = jnp.exp(pos * lgb)
    cd_b = jnp.exp(c * lgb)

    st_scr[...] = jnp.zeros_like(st_scr)

    def fwd(n):
        off = n * c
        qc = qs[pl.ds(off, c), :]
        kc = ks[pl.ds(off, c), :]
        vc = v_ref[pl.ds(off, c), :]
        s = _dot_nt(qc, kc) * dmat
        y = _dot(s.astype(BF16), vc)
        st = st_scr[...]
        y = y + _dot((qc.astype(F32) * qd_f).astype(BF16), st.astype(BF16))
        y_scr[pl.ds(off, c), :] = y
        kt = (kc.astype(F32) * kd_f).T.astype(BF16)
        st_scr[...] = st * cd_f + _dot(kt, vc)

    for n in range(n_chunks):
        fwd(n)

    st_scr[...] = jnp.zeros_like(st_scr)
    gn = gn_ref[...]

    def bwd(n):
        off = n * c
        qc = qs[pl.ds(off, c), :]
        kc = ks[pl.ds(off, c), :]
        vc = v_ref[pl.ds(off, c), :]
        st = st_scr[...]
        y = y_scr[pl.ds(off, c), :] + _dot((qc.astype(F32) * qd_b).astype(BF16), st.astype(BF16))
        kt = (kc.astype(F32) * kd_b).T.astype(BF16)
        st_scr[...] = st * cd_b + _dot(kt, vc)
        mu = jnp.mean(y, axis=-1, keepdims=True)
        yc = y - mu
        var = jnp.mean(yc * yc, axis=-1, keepdims=True)
        yn = yc * lax.rsqrt(var + GN_EPS) * gn
        rg = rg_ref[pl.ds(off, c), :].astype(F32)
        o_ref[pl.ds(off, c), :] = (rg * _sigmoid(rg) * yn).astype(o_ref.dtype)

    for n in reversed(range(n_chunks)):
        bwd(n)


def _retention(proj3, dec_f, dec_b, cos_t, sin_t, gn_gain):
    b, s, _ = proj3.shape
    qb, vb = RET_QK_DIM, RET_V_DIM
    smem = pl.BlockSpec(memory_space=pltpu.SMEM)
    return pl.pallas_call(
        _retention_kernel,
        grid=(b, RET_HEADS),
        in_specs=[
            smem,
            smem,
            pl.BlockSpec((None, s, qb), lambda i, h: (i, 0, OFF_RQ // qb + h)),
            pl.BlockSpec((None, s, qb), lambda i, h: (i, 0, OFF_RK // qb + h)),
            pl.BlockSpec((None, s, vb), lambda i, h: (i, 0, OFF_RV // vb + h)),
            pl.BlockSpec((None, s, vb), lambda i, h: (i, 0, OFF_RG // vb + h)),
            pl.BlockSpec((s, qb), lambda i, h: (0, 0)),
            pl.BlockSpec((s, qb), lambda i, h: (0, 0)),
            pl.BlockSpec((1, vb), lambda i, h: (0, h)),
        ],
        out_specs=pl.BlockSpec((None, s, vb), lambda i, h: (i, 0, h)),
        out_shape=jax.ShapeDtypeStruct((b, s, RET_V_WIDTH), BF16),
        scratch_shapes=[
            pltpu.VMEM((s, qb), BF16),
            pltpu.VMEM((s, qb), BF16),
            pltpu.VMEM((s, vb), F32),
            pltpu.VMEM((qb, vb), F32),
        ],
        compiler_params=_params("parallel", "parallel"),
        name="retention",
    )(dec_f, dec_b, proj3, proj3, proj3, proj3, cos_t, sin_t, gn_gain)


def _rope_tables(seq):
    half = RET_QK_DIM // 2
    inv_freq = ROPE_BASE ** (-jnp.arange(half, dtype=F32) / half)
    ang = jnp.arange(seq, dtype=F32)[:, None] * inv_freq[None, :]
    cos, sin = jnp.cos(ang), jnp.sin(ang)
    return jnp.concatenate([cos, cos], axis=-1), jnp.concatenate([-sin, sin], axis=-1)


def _na_bias_table(rpb):
    heads = rpb.shape[0]
    w = GRID_W
    cols = np.arange(w)
    col_start = np.clip(cols - NA_COLS // 2, 0, w - NA_COLS)
    col_off = cols[None, :] - col_start[:, None]
    col_mask = (col_off >= 0) & (col_off < NA_COLS)
    rel_c = np.clip(cols[None, :] - cols[:, None], -(NA_COLS - 1), NA_COLS - 1) + (NA_COLS - 1)
    onehot = (rel_c[:, :, None] == np.arange(2 * NA_COLS - 1)).astype(np.float32)
    toe = jnp.einsum('hrc,qkc->hrqk', rpb.astype(F32), onehot, precision=lax.Precision.HIGHEST)
    toe = jnp.where(col_mask[None, None], toe, -jnp.inf)
    tab = jnp.stack([toe[:, off:off + NA_ROWS] for off in range(NA_ROWS)], axis=1)
    tab = tab.transpose(0, 1, 3, 2, 4).reshape(heads // 2, 2, NA_ROWS, w, NA_ROWS * w)
    return tab.transpose(0, 2, 1, 3, 4).reshape(heads // 2, NA_ROWS, 2 * w, NA_ROWS * w)


NA_UNROLL = 4


def _na_kernel(q_ref, k_ref, v_ref, bias_ref, o_ref):
    w = GRID_W
    seq = q_ref.shape[0]
    rows_n = seq // w
    kr = NA_ROWS
    first = lax.broadcasted_iota(jnp.int32, (w, LANES), 1) < NA_HEAD_DIM
    scale = NA_HEAD_DIM ** -0.5

    def body(r, carry):
        rs = jnp.clip(r - kr // 2, 0, rows_n - kr)
        off = rs - r + (NA_ROWS - 1)
        qoff = pl.multiple_of(r * w, w)
        qr = q_ref[pl.ds(qoff, w), :] * scale
        zero = jnp.zeros_like(qr)
        q2 = jnp.concatenate([jnp.where(first, qr, zero), jnp.where(first, zero, qr)], axis=0)
        koff = pl.multiple_of(rs * w, w)
        kk = k_ref[pl.ds(koff, kr * w), :]
        vv = v_ref[pl.ds(koff, kr * w), :]
        s = _dot_nt(q2, kk) + bias_ref[off]
        m = jnp.max(s, axis=-1, keepdims=True)
        p = jnp.exp(s - m)
        l = jnp.sum(p, axis=-1, keepdims=True)
        o2 = _dot(p.astype(BF16), vv) / l
        o_ref[pl.ds(qoff, w), :] = jnp.where(first, o2[:w], o2[w:]).astype(o_ref.dtype)
        return carry

    lax.fori_loop(0, rows_n, body, 0, unroll=NA_UNROLL)


def _neighbourhood_attention(proj3, bias_tab):
    b, s, _ = proj3.shape
    pairs = NA_HEADS // 2
    return pl.pallas_call(
        _na_kernel,
        grid=(b, pairs),
        in_specs=[
            pl.BlockSpec((None, s, LANES), lambda i, p: (i, 0, OFF_NQ // LANES + p)),
            pl.BlockSpec((None, s, LANES), lambda i, p: (i, 0, OFF_NK // LANES + p)),
            pl.BlockSpec((None, s, LANES), lambda i, p: (i, 0, OFF_NV // LANES + p)),
            pl.BlockSpec((None, NA_ROWS, 2 * GRID_W, NA_ROWS * GRID_W), lambda i, p: (p, 0, 0, 0)),
        ],
        out_specs=pl.BlockSpec((None, s, LANES), lambda i, p: (i, 0, p)),
        out_shape=jax.ShapeDtypeStruct((b, s, NA_WIDTH), BF16),
        compiler_params=_params("parallel", "parallel"),
        name="nbr_attn",
    )(proj3, proj3, proj3, bias_tab)


def _xa_kernel(q_ref, mem_ref, g_ref, wkv_ref, o_ref, kv_scr):
    @pl.when(pl.program_id(1) == 0)
    def _():
        mn = _rmsnorm_f32(mem_ref[...], g_ref[...]).astype(BF16)
        kv_scr[...] = _dot(mn, wkv_ref[...]).astype(BF16)

    dh = XA_HEAD_DIM
    scale = dh ** -0.5
    for h in range(XA_HEADS):
        q = q_ref[:, h * dh:(h + 1) * dh]
        k = kv_scr[:, h * dh:(h + 1) * dh]
        v = kv_scr[:, XA_WIDTH + h * dh:XA_WIDTH + (h + 1) * dh]
        s = _dot_nt(q, k) * scale
        m = jnp.max(s, axis=-1, keepdims=True)
        p = jnp.exp(s - m)
        l = jnp.sum(p, axis=-1, keepdims=True)
        o_ref[:, h * dh:(h + 1) * dh] = (_dot(p.astype(BF16), v) / l).astype(o_ref.dtype)


def _memory_attention(proj3, mem, g_mem, wkv_bf16, ts=1024):
    b, s, _ = proj3.shape
    m, d = mem.shape[1], mem.shape[2]
    return pl.pallas_call(
        _xa_kernel,
        grid=(b, s // ts),
        in_specs=[
            pl.BlockSpec((None, ts, XA_WIDTH), lambda i, j: (i, j, OFF_XQ // XA_WIDTH)),
            pl.BlockSpec((None, m, d), lambda i, j: (i, 0, 0)),
            pl.BlockSpec((1, d), lambda i, j: (0, 0)),
            pl.BlockSpec((d, 2 * XA_WIDTH), lambda i, j: (0, 0)),
        ],
        out_specs=pl.BlockSpec((None, ts, XA_WIDTH), lambda i, j: (i, j, 0)),
        out_shape=jax.ShapeDtypeStruct((b, s, XA_WIDTH), BF16),
        scratch_shapes=[pltpu.VMEM((m, 2 * XA_WIDTH), BF16)],
        compiler_params=_params("parallel", "arbitrary"),
        name="mem_attn",
    )(proj3, mem, g_mem, wkv_bf16)


def _masked_lane_max(v, mask):
    return jnp.max(jnp.where(mask, v, -jnp.inf), axis=-1, keepdims=True)


def _first_lane_eq(v, target, mask, lane):
    return jnp.min(jnp.where(mask & (v == target), lane, float(LANES)), axis=-1, keepdims=True)


def _route(logits):
    g_n, e_n = MOE_GROUPS, MOE_EXPERTS_PER_GROUP
    lane = lax.broadcasted_iota(jnp.int32, logits.shape, 1).astype(F32)
    is_grp = lane < g_n
    gmax = _masked_lane_max(logits, is_grp)
    gsum = jnp.sum(jnp.where(is_grp, jnp.exp(logits - gmax), 0.0), axis=-1, keepdims=True)
    grp_w = 1.0 / gsum
    gidx = _first_lane_eq(logits, gmax, is_grp, lane)
    lo = ROUTER_EXPERT_LANE0 + gidx * e_n
    in_grp = (lane >= lo) & (lane < lo + e_n)
    emax = _masked_lane_max(logits, in_grp)
    ex = jnp.where(in_grp, jnp.exp(logits - emax), 0.0)
    prob = ex / jnp.sum(ex, axis=-1, keepdims=True)
    p1 = _masked_lane_max(prob, in_grp)
    i1 = _first_lane_eq(prob, p1, in_grp, lane)
    rest = in_grp & (lane != i1)
    p2 = _masked_lane_max(prob, rest)
    i2 = _first_lane_eq(prob, p2, rest, lane)
    tot = p1 + p2
    w1 = p1 / tot * grp_w
    w2 = p2 / tot * grp_w
    return jnp.where(lane == i1, w1, jnp.where(lane == i2, w2, 0.0))


def _merge_kernel(yret_ref, yna_ref, yxa_ref, gr_ref, gn_ref, gx_ref, x_ref, wro_ref, wno_ref, wxo_ref, wout_ref,
                  gffn_ref, wr_ref, br_ref, x2_ref, h2_ref, comb_ref):
    y_ret = _dot(yret_ref[...], wro_ref[...])
    y_na = _dot(yna_ref[...], wno_ref[...])
    y_xa = _dot(yxa_ref[...], wxo_ref[...])
    mix = (_sigmoid(gr_ref[...].astype(F32)) * y_ret + _sigmoid(gn_ref[...].astype(F32)) * y_na
           + _sigmoid(gx_ref[...].astype(F32)) * y_xa)
    x2 = x_ref[...] + _dot(mix.astype(BF16), wout_ref[...])
    x2_ref[...] = x2
    h2 = _rmsnorm_f32(x2, gffn_ref[...])
    h2_ref[...] = h2.astype(BF16)
    logits = jnp.dot(h2, wr_ref[...], preferred_element_type=F32, precision=lax.Precision.HIGHEST) + br_ref[...]
    comb_ref[...] = _route(logits)


def _merge(y_ret, y_na, y_xa, proj, x2d, w_ret_o, w_na_o, w_xa_o, w_out, g_ffn, w_router, b_router, tm=512):
    t, d = x2d.shape
    gate_blk = OFF_GATE // d
    full = lambda a: pl.BlockSpec(a.shape, lambda i: (0,) * a.ndim)
    return pl.pallas_call(
        _merge_kernel,
        grid=(t // tm,),
        in_specs=[
            pl.BlockSpec((tm, RET_V_WIDTH), lambda i: (i, 0)),
            pl.BlockSpec((tm, NA_WIDTH), lambda i: (i, 0)),
            pl.BlockSpec((tm, XA_WIDTH), lambda i: (i, 0)),
            pl.BlockSpec((tm, d), lambda i: (i, gate_blk)),
            pl.BlockSpec((tm, d), lambda i: (i, gate_blk + 1)),
            pl.BlockSpec((tm, d), lambda i: (i, gate_blk + 2)),
            pl.BlockSpec((tm, d), lambda i: (i, 0)),
            full(w_ret_o), full(w_na_o), full(w_xa_o), full(w_out), full(g_ffn), full(w_router), full(b_router),
        ],
        out_specs=[
            pl.BlockSpec((tm, d), lambda i: (i, 0)),
            pl.BlockSpec((tm, d), lambda i: (i, 0)),
            pl.BlockSpec((tm, LANES), lambda i: (i, 0)),
        ],
        out_shape=[
            jax.ShapeDtypeStruct((t, d), F32),
            jax.ShapeDtypeStruct((t, d), BF16),
            jax.ShapeDtypeStruct((t, LANES), F32),
        ],
        compiler_params=_params("parallel"),
        name="merge_router",
    )(y_ret, y_na, y_xa, proj, proj, proj, x2d, w_ret_o, w_na_o, w_xa_o, w_out, g_ffn, w_router, b_router)


def _moe_kernel(h_ref, comb_ref, x2_ref, wg_ref, wu_ref, wd_ref, gfin_ref, o_ref, acc, *, final_norm):
    e = pl.program_id(1)

    @pl.when(e == 0)
    def _():
        acc[...] = jnp.zeros_like(acc)

    h = h_ref[...]
    a = _dot(h, wg_ref[...])
    u = _dot(h, wu_ref[...])
    lane = lax.broadcasted_iota(jnp.int32, comb_ref.shape, 1)
    cw = jnp.sum(jnp.where(lane == e + ROUTER_EXPERT_LANE0, comb_ref[...], 0.0), axis=-1, keepdims=True)
    hid = a * _sigmoid(a) * u * cw
    acc[...] += _dot(hid.astype(BF16), wd_ref[...])

    @pl.when(e == pl.num_programs(1) - 1)
    def _():
        y = x2_ref[...] + acc[...]
        if final_norm:
            y = _rmsnorm_f32(y, gfin_ref[...])
        o_ref[...] = y


def _moe(h2, comb, x2, wg, wu, wd, g_final, final_norm, tm=1024):
    t, d = x2.shape
    n_e, _, f = wg.shape
    return pl.pallas_call(
        functools.partial(_moe_kernel, final_norm=final_norm),
        grid=(t // tm, n_e),
        in_specs=[
            pl.BlockSpec((tm, d), lambda i, e: (i, 0)),
            pl.BlockSpec((tm, LANES), lambda i, e: (i, 0)),
            pl.BlockSpec((tm, d), lambda i, e: (i, 0)),
            pl.BlockSpec((None, d, f), lambda i, e: (e, 0, 0)),
            pl.BlockSpec((None, d, f), lambda i, e: (e, 0, 0)),
            pl.BlockSpec((None, f, d), lambda i, e: (e, 0, 0)),
            pl.BlockSpec((1, d), lambda i, e: (0, 0)),
        ],
        out_specs=pl.BlockSpec((tm, d), lambda i, e: (i, 0)),
        out_shape=jax.ShapeDtypeStruct((t, d), F32),
        scratch_shapes=[pltpu.VMEM((tm, d), F32)],
        compiler_params=_params("parallel", "arbitrary"),
        name="experts",
    )(h2, comb, x2, wg, wu, wd, g_final)


def kernel(x, mem, g_mix, w_in, ret_decay_fwd, ret_decay_bwd, ret_norm_gain, w_ret_o, na_rpb, w_na_o, g_mem, w_mem_kv, w_xa_o, w_out, g_ffn, w_router_group, b_router_group, w_router_expert, b_router_expert, w_exp_gate, w_exp_up, w_exp_down, g_final):
    b, s, d = x.shape
    depth = w_in.shape[0]
    t = b * s
    cos_t, sin_t = _rope_tables(s)
    row = lambda v: v.reshape(1, -1).astype(F32)
    x2d = x.reshape(t, d)
    for l in range(depth):
        proj = _inproj(x2d, row(g_mix[l]), w_in[l].astype(BF16))
        proj3 = proj.reshape(b, s, IN_WIDTH)
        y_ret = _retention(proj3, ret_decay_fwd[l].astype(F32), ret_decay_bwd[l].astype(F32), cos_t, sin_t,
                           row(ret_norm_gain[l]))
        y_na = _neighbourhood_attention(proj3, _na_bias_table(na_rpb[l]))
        y_xa = _memory_attention(proj3, mem, row(g_mem[l]), w_mem_kv[l].astype(BF16))
        n_r = MOE_GROUPS + N_EXPERTS
        w_router = jnp.pad(jnp.concatenate([w_router_group[l], w_router_expert[l]], axis=1).astype(F32),
                           ((0, 0), (0, LANES - n_r)))
        b_router = jnp.pad(jnp.concatenate([b_router_group[l], b_router_expert[l]]).astype(F32),
                           (0, LANES - n_r)).reshape(1, LANES)
        x2, h2, comb = _merge(y_ret.reshape(t, -1), y_na.reshape(t, -1), y_xa.reshape(t, -1), proj, x2d,
                              w_ret_o[l].astype(BF16), w_na_o[l].astype(BF16), w_xa_o[l].astype(BF16),
                              w_out[l].astype(BF16), row(g_ffn[l]), w_router, b_router)
        f = w_exp_gate.shape[-1]
        x2d = _moe(h2, comb, x2,
                   w_exp_gate[l].reshape(N_EXPERTS, d, f).astype(BF16),
                   w_exp_up[l].reshape(N_EXPERTS, d, f).astype(BF16),
                   w_exp_down[l].reshape(N_EXPERTS, f, d).astype(BF16),
                   row(g_final), final_norm=(l == depth - 1))
    return x2d.reshape(b, s, d)
```

```python
import functools

import jax
import jax.numpy as jnp
import numpy as np
from jax import lax
from jax.experimental import pallas as pl
from jax.experimental.pallas import tpu as pltpu

D_MODEL = 1024
GRID_W = 64
N_BRANCHES = 3
RET_HEADS = 4
RET_QK_DIM = 128
RET_V_DIM = 256
RET_CHUNK = 128
ROPE_BASE = 10000.0
NA_HEADS = 8
NA_HEAD_DIM = 64
NA_ROWS = 8
NA_COLS = 16
XA_HEADS = 4
XA_HEAD_DIM = 128
MOE_GROUPS = 4
MOE_EXPERTS_PER_GROUP = 4
MOE_TOP_K = 2
MOE_D_FF = 512
RMS_EPS = 1e-6
GN_EPS = 1e-5

RET_QK_WIDTH = RET_HEADS * RET_QK_DIM
RET_V_WIDTH = RET_HEADS * RET_V_DIM
NA_WIDTH = NA_HEADS * NA_HEAD_DIM
XA_WIDTH = XA_HEADS * XA_HEAD_DIM
IN_WIDTH = 2 * RET_QK_WIDTH + 2 * RET_V_WIDTH + 3 * NA_WIDTH + XA_WIDTH + N_BRANCHES * D_MODEL

OFF_RQ = 0
OFF_RK = OFF_RQ + RET_QK_WIDTH
OFF_RV = OFF_RK + RET_QK_WIDTH
OFF_RG = OFF_RV + RET_V_WIDTH
OFF_NQ = OFF_RG + RET_V_WIDTH
OFF_NK = OFF_NQ + NA_WIDTH
OFF_NV = OFF_NK + NA_WIDTH
OFF_XQ = OFF_NV + NA_WIDTH
OFF_GATE = OFF_XQ + XA_WIDTH

N_EXPERTS = MOE_GROUPS * MOE_EXPERTS_PER_GROUP
LANES = 128
ROUTER_EXPERT_LANE0 = MOE_GROUPS

VMEM_LIMIT = 48 * 1024 * 1024

F32 = jnp.float32
BF16 = jnp.bfloat16


def _params(*sem):
    return pltpu.CompilerParams(dimension_semantics=sem, vmem_limit_bytes=VMEM_LIMIT)


def _rmsnorm_f32(x, g):
    return x * lax.rsqrt(jnp.mean(x * x, axis=-1, keepdims=True) + RMS_EPS) * g


def _sigmoid(x):
    return 1.0 / (1.0 + jnp.exp(-x))


def _dot(a, b):
    return jnp.dot(a, b, preferred_element_type=F32)


def _dot_nt(a, b):
    return lax.dot_general(a, b, (((1,), (1,)), ((), ())), preferred_element_type=F32)


def _inproj_kernel(x_ref, g_ref, w_ref, o_ref, h_scr):
    @pl.when(pl.program_id(1) == 0)
    def _():
        h_scr[...] = _rmsnorm_f32(x_ref[...], g_ref[...]).astype(BF16)

    o_ref[...] = _dot(h_scr[...], w_ref[...]).astype(o_ref.dtype)


def _inproj(x2d, g, w_bf16, tm=1024, tn=1024):
    t, d = x2d.shape
    n = w_bf16.shape[1]
    return pl.pallas_call(
        _inproj_kernel,
        grid=(t // tm, n // tn),
        in_specs=[
            pl.BlockSpec((tm, d), lambda i, j: (i, 0)),
            pl.BlockSpec((1, d), lambda i, j: (0, 0)),
            pl.BlockSpec((d, tn), lambda i, j: (0, j)),
        ],
        out_specs=pl.BlockSpec((tm, tn), lambda i, j: (i, j)),
        out_shape=jax.ShapeDtypeStruct((t, n), BF16),
        scratch_shapes=[pltpu.VMEM((tm, d), BF16)],
        compiler_params=_params("parallel", "arbitrary"),
        name="inproj",
    )(x2d, g, w_bf16)


def _log_sigmoid(x):
    return jnp.minimum(x, 0.0) - jnp.log1p(jnp.exp(-jnp.abs(x)))


def _retention_kernel(decf_ref, decb_ref, q_ref, k_ref, v_ref, rg_ref, cos_ref, sin_ref, gn_ref, o_ref,
                      qs, ks, y_scr, st_scr):
    h = pl.program_id(1)
    c = RET_CHUNK
    seq = q_ref.shape[0]
    n_chunks = seq // c
    half = RET_QK_DIM // 2

    lgf = _log_sigmoid(jnp.full((1, 1), decf_ref[h], F32))
    lgb = _log_sigmoid(jnp.full((1, 1), decb_ref[h], F32))

    cos = cos_ref[...]
    sin = sin_ref[...]
    q = q_ref[...].astype(F32)
    qs[...] = ((q * cos + pltpu.roll(q, half, 1) * sin) * (RET_QK_DIM ** -0.5)).astype(BF16)
    k = k_ref[...].astype(F32)
    ks[...] = (k * cos + pltpu.roll(k, half, 1) * sin).astype(BF16)

    ii = lax.broadcasted_iota(jnp.int32, (c, c), 0)
    jj = lax.broadcasted_iota(jnp.int32, (c, c), 1)
    diff = (ii - jj).astype(F32)
    dmat = jnp.exp(jnp.where(diff >= 0, diff * lgf, (-diff) * lgb))
    pos = lax.broadcasted_iota(jnp.int32, (c, 1), 0).astype(F32)
    qd_f = jnp.exp((pos + 1.0) * lgf)
    kd_f = jnp.exp((c - 1.0 - pos) * lgf)
    cd_f = jnp.exp(c * lgf)
    qd_b = jnp.exp((c - pos) * lgb)
    kd_b = jnp.exp(pos * lgb)
    cd_b = jnp.exp(c * lgb)

    st_scr[...] = jnp.zeros_like(st_scr)

    def fwd(n):
        off = n * c
        qc = qs[pl.ds(off, c), :]
        kc = ks[pl.ds(off, c), :]
        vc = v_ref[pl.ds(off, c), :]
        s = _dot_nt(qc, kc) * dmat
        y = _dot(s.astype(BF16), vc)
        st = st_scr[...]
        y = y + _dot((qc.astype(F32) * qd_f).astype(BF16), st.astype(BF16))
        y_scr[pl.ds(off, c), :] = y
        kt = (kc.astype(F32) * kd_f).T.astype(BF16)
        st_scr[...] = st * cd_f + _dot(kt, vc)

    for n in range(n_chunks):
        fwd(n)

    st_scr[...] = jnp.zeros_like(st_scr)
    gn = gn_ref[...]

    def bwd(n):
        off = n * c
        qc = qs[pl.ds(off, c), :]
        kc = ks[pl.ds(off, c), :]
        vc = v_ref[pl.ds(off, c), :]
        st = st_scr[...]
        y = y_scr[pl.ds(off, c), :] + _dot((qc.astype(F32) * qd_b).astype(BF16), st.astype(BF16))
        kt = (kc.astype(F32) * kd_b).T.astype(BF16)
        st_scr[...] = st * cd_b + _dot(kt, vc)
        mu = jnp.mean(y, axis=-1, keepdims=True)
        yc = y - mu
        var = jnp.mean(yc * yc, axis=-1, keepdims=True)
        yn = yc * lax.rsqrt(var + GN_EPS) * gn
        rg = rg_ref[pl.ds(off, c), :].astype(F32)
        o_ref[pl.ds(off, c), :] = (rg * _sigmoid(rg) * yn).astype(o_ref.dtype)

    for n in reversed(range(n_chunks)):
        bwd(n)


def _retention(proj3, dec_f, dec_b, cos_t, sin_t, gn_gain):
    b, s, _ = proj3.shape
    qb, vb = RET_QK_DIM, RET_V_DIM
    smem = pl.BlockSpec(memory_space=pltpu.SMEM)
    return pl.pallas_call(
        _retention_kernel,
        grid=(b, RET_HEADS),
        in_specs=[
            smem,
            smem,
            pl.BlockSpec((None, s, qb), lambda i, h: (i, 0, OFF_RQ // qb + h)),
            pl.BlockSpec((None, s, qb), lambda i, h: (i, 0, OFF_RK // qb + h)),
            pl.BlockSpec((None, s, vb), lambda i, h: (i, 0, OFF_RV // vb + h)),
            pl.BlockSpec((None, s, vb), lambda i, h: (i, 0, OFF_RG // vb + h)),
            pl.BlockSpec((s, qb), lambda i, h: (0, 0)),
            pl.BlockSpec((s, qb), lambda i, h: (0, 0)),
            pl.BlockSpec((1, vb), lambda i, h: (0, h)),
        ],
        out_specs=pl.BlockSpec((None, s, vb), lambda i, h: (i, 0, h)),
        out_shape=jax.ShapeDtypeStruct((b, s, RET_V_WIDTH), BF16),
        scratch_shapes=[
            pltpu.VMEM((s, qb), BF16),
            pltpu.VMEM((s, qb), BF16),
            pltpu.VMEM((s, vb), F32),
            pltpu.VMEM((qb, vb), F32),
        ],
        compiler_params=_params("parallel", "parallel"),
        name="retention",
    )(dec_f, dec_b, proj3, proj3, proj3, proj3, cos_t, sin_t, gn_gain)


def _rope_tables(seq):
    half = RET_QK_DIM // 2
    inv_freq = ROPE_BASE ** (-jnp.arange(half, dtype=F32) / half)
    ang = jnp.arange(seq, dtype=F32)[:, None] * inv_freq[None, :]
    cos, sin = jnp.cos(ang), jnp.sin(ang)
    return jnp.concatenate([cos, cos], axis=-1), jnp.concatenate([-sin, sin], axis=-1)


def _na_bias_table(rpb):
    heads = rpb.shape[0]
    w = GRID_W
    cols = np.arange(w)
    col_start = np.clip(cols - NA_COLS // 2, 0, w - NA_COLS)
    col_off = cols[None, :] - col_start[:, None]
    col_mask = (col_off >= 0) & (col_off < NA_COLS)
    rel_c = np.clip(cols[None, :] - cols[:, None], -(NA_COLS - 1), NA_COLS - 1) + (NA_COLS - 1)
    onehot = (rel_c[:, :, None] == np.arange(2 * NA_COLS - 1)).astype(np.float32)
    toe = jnp.einsum('hrc,qkc->hrqk', rpb.astype(F32), onehot, precision=lax.Precision.HIGHEST)
    toe = jnp.where(col_mask[None, None], toe, -jnp.inf)
    tab = jnp.stack([toe[:, off:off + NA_ROWS] for off in range(NA_ROWS)], axis=1)
    tab = tab.transpose(0, 1, 3, 2, 4).reshape(heads // 2, 2, NA_ROWS, w, NA_ROWS * w)
    return tab.transpose(0, 2, 1, 3, 4).reshape(heads // 2, NA_ROWS, 2 * w, NA_ROWS * w)


NA_UNROLL = 4


def _na_kernel(q_ref, k_ref, v_ref, bias_ref, o_ref):
    w = GRID_W
    seq = q_ref.shape[0]
    rows_n = seq // w
    kr = NA_ROWS
    first = lax.broadcasted_iota(jnp.int32, (w, LANES), 1) < NA_HEAD_DIM
    scale = NA_HEAD_DIM ** -0.5

    def body(r, carry):
        rs = jnp.clip(r - kr // 2, 0, rows_n - kr)
        off = rs - r + (NA_ROWS - 1)
        qoff = pl.multiple_of(r * w, w)
        qr = q_ref[pl.ds(qoff, w), :] * scale
        zero = jnp.zeros_like(qr)
        q2 = jnp.concatenate([jnp.where(first, qr, zero), jnp.where(first, zero, qr)], axis=0)
        koff = pl.multiple_of(rs * w, w)
        kk = k_ref[pl.ds(koff, kr * w), :]
        vv = v_ref[pl.ds(koff, kr * w), :]
        s = _dot_nt(q2, kk) + bias_ref[off]
        m = jnp.max(s, axis=-1, keepdims=True)
        p = jnp.exp(s - m)
        l = jnp.sum(p, axis=-1, keepdims=True)
        o2 = _dot(p.astype(BF16), vv) / l
        o_ref[pl.ds(qoff, w), :] = jnp.where(first, o2[:w], o2[w:]).astype(o_ref.dtype)
        return carry

    lax.fori_loop(0, rows_n, body, 0, unroll=NA_UNROLL)


def _neighbourhood_attention(proj3, bias_tab):
    b, s, _ = proj3.shape
    pairs = NA_HEADS // 2
    return pl.pallas_call(
        _na_kernel,
        grid=(b, pairs),
        in_specs=[
            pl.BlockSpec((None, s, LANES), lambda i, p: (i, 0, OFF_NQ // LANES + p)),
            pl.BlockSpec((None, s, LANES), lambda i, p: (i, 0, OFF_NK // LANES + p)),
            pl.BlockSpec((None, s, LANES), lambda i, p: (i, 0, OFF_NV // LANES + p)),
            pl.BlockSpec((None, NA_ROWS, 2 * GRID_W, NA_ROWS * GRID_W), lambda i, p: (p, 0, 0, 0)),
        ],
        out_specs=pl.BlockSpec((None, s, LANES), lambda i, p: (i, 0, p)),
        out_shape=jax.ShapeDtypeStruct((b, s, NA_WIDTH), BF16),
        compiler_params=_params("parallel", "parallel"),
        name="nbr_attn",
    )(proj3, proj3, proj3, bias_tab)


def _xa_kernel(q_ref, mem_ref, g_ref, wkv_ref, o_ref, kv_scr):
    @pl.when(pl.program_id(1) == 0)
    def _():
        mn = _rmsnorm_f32(mem_ref[...], g_ref[...]).astype(BF16)
        kv_scr[...] = _dot(mn, wkv_ref[...]).astype(BF16)

    dh = XA_HEAD_DIM
    scale = dh ** -0.5
    for h in range(XA_HEADS):
        q = q_ref[:, h * dh:(h + 1) * dh]
        k = kv_scr[:, h * dh:(h + 1) * dh]
        v = kv_scr[:, XA_WIDTH + h * dh:XA_WIDTH + (h + 1) * dh]
        s = _dot_nt(q, k) * scale
        m = jnp.max(s, axis=-1, keepdims=True)
        p = jnp.exp(s - m)
        l = jnp.sum(p, axis=-1, keepdims=True)
        o_ref[:, h * dh:(h + 1) * dh] = (_dot(p.astype(BF16), v) / l).astype(o_ref.dtype)


def _memory_attention(proj3, mem, g_mem, wkv_bf16, ts=1024):
    b, s, _ = proj3.shape
    m, d = mem.shape[1], mem.shape[2]
    return pl.pallas_call(
        _xa_kernel,
        grid=(b, s // ts),
        in_specs=[
            pl.BlockSpec((None, ts, XA_WIDTH), lambda i, j: (i, j, OFF_XQ // XA_WIDTH)),
            pl.BlockSpec((None, m, d), lambda i, j: (i, 0, 0)),
            pl.BlockSpec((1, d), lambda i, j: (0, 0)),
            pl.BlockSpec((d, 2 * XA_WIDTH), lambda i, j: (0, 0)),
        ],
        out_specs=pl.BlockSpec((None, ts, XA_WIDTH), lambda i, j: (i, j, 0)),
        out_shape=jax.ShapeDtypeStruct((b, s, XA_WIDTH), BF16),
        scratch_shapes=[pltpu.VMEM((m, 2 * XA_WIDTH), BF16)],
        compiler_params=_params("parallel", "arbitrary"),
        name="mem_attn",
    )(proj3, mem, g_mem, wkv_bf16)


def _masked_lane_max(v, mask):
    return jnp.max(jnp.where(mask, v, -jnp.inf), axis=-1, keepdims=True)


def _first_lane_eq(v, target, mask, lane):
    return jnp.min(jnp.where(mask & (v == target), lane, float(LANES)), axis=-1, keepdims=True)


def _route(logits):
    g_n, e_n = MOE_GROUPS, MOE_EXPERTS_PER_GROUP
    lane = lax.broadcasted_iota(jnp.int32, logits.shape, 1).astype(F32)
    is_grp = lane < g_n
    gmax = _masked_lane_max(logits, is_grp)
    gsum = jnp.sum(jnp.where(is_grp, jnp.exp(logits - gmax), 0.0), axis=-1, keepdims=True)
    grp_w = 1.0 / gsum
    gidx = _first_lane_eq(logits, gmax, is_grp, lane)
    lo = ROUTER_EXPERT_LANE0 + gidx * e_n
    in_grp = (lane >= lo) & (lane < lo + e_n)
    emax = _masked_lane_max(logits, in_grp)
    ex = jnp.where(in_grp, jnp.exp(logits - emax), 0.0)
    prob = ex / jnp.sum(ex, axis=-1, keepdims=True)
    p1 = _masked_lane_max(prob, in_grp)
    i1 = _first_lane_eq(prob, p1, in_grp, lane)
    rest = in_grp & (lane != i1)
    p2 = _masked_lane_max(prob, rest)
    i2 = _first_lane_eq(prob, p2, rest, lane)
    tot = p1 + p2
    w1 = p1 / tot * grp_w
    w2 = p2 / tot * grp_w
    return gidx, jnp.where(lane == i1 - lo, w1, jnp.where(lane == i2 - lo, w2, 0.0))


SORT_TILE = 512
ROW_ALIGN = 16
SORTED_ROWS = 640
WIN_ROWS = SORT_TILE
SCR_ROWS = SORTED_ROWS + WIN_ROWS
EXPERT_TILE = 512


def _group_buf_rows(t):
    return t + 2 * WIN_ROWS


def _pad_rows(n):
    return (n + (ROW_ALIGN - 1)) // ROW_ALIGN * ROW_ALIGN


def _window_copies(hs, cs, hbuf, cbuf, sems, slot, kind, g, src_row, dst_row):
    return (
        pltpu.make_async_copy(hs.at[slot, pl.ds(src_row, WIN_ROWS)], hbuf.at[g, pl.ds(dst_row, WIN_ROWS)],
                              sems.at[slot, 2 * kind, g]),
        pltpu.make_async_copy(cs.at[slot, pl.ds(src_row, WIN_ROWS)], cbuf.at[g, pl.ds(dst_row, WIN_ROWS)],
                              sems.at[slot, 2 * kind + 1, g]),
    )


def _merge_kernel(yret_ref, yna_ref, yxa_ref, gr_ref, gn_ref, gx_ref, x_ref, wro_ref, wno_ref, wxo_ref, wout_ref,
                  gffn_ref, wr_ref, br_ref,
                  x2_ref, pos_ref, start_ref, cnt_ref, hbuf, cbuf,
                  hs, cs, run, sems):
    i = pl.program_id(0)
    n_tiles = pl.num_programs(0)
    slot = lax.rem(i, 2)
    tm = x_ref.shape[0]
    g_n = MOE_GROUPS

    @pl.when(i == 0)
    def _():
        for g in range(g_n):
            run[g] = 0
        hs[:, SORTED_ROWS:, :] = jnp.zeros((2, WIN_ROWS, hs.shape[2]), hs.dtype)
        cs[:, SORTED_ROWS:, :] = jnp.zeros((2, WIN_ROWS, cs.shape[2]), cs.dtype)

    y_ret = _dot(yret_ref[...], wro_ref[...])
    y_na = _dot(yna_ref[...], wno_ref[...])
    y_xa = _dot(yxa_ref[...], wxo_ref[...])
    mix = (_sigmoid(gr_ref[...].astype(F32)) * y_ret + _sigmoid(gn_ref[...].astype(F32)) * y_na
           + _sigmoid(gx_ref[...].astype(F32)) * y_xa)
    x2 = x_ref[...] + _dot(mix.astype(BF16), wout_ref[...])
    x2_ref[...] = x2
    h2 = _rmsnorm_f32(x2, gffn_ref[...])
    logits = jnp.dot(h2, wr_ref[...], preferred_element_type=F32, precision=lax.Precision.HIGHEST) + br_ref[...]
    gidx, w4 = _route(logits)

    lane = lax.broadcasted_iota(jnp.int32, (tm, LANES), 1).astype(F32)
    onehot = jnp.where(lane == gidx, 1.0, 0.0)
    before = (lax.broadcasted_iota(jnp.int32, (tm, tm), 1) < lax.broadcasted_iota(jnp.int32, (tm, tm), 0))
    rank = _dot(jnp.where(before, 1.0, 0.0).astype(BF16), onehot.astype(BF16))
    count_row = rank[tm - 1:tm, :] + onehot[tm - 1:tm, :]
    lane_row = lax.broadcasted_iota(jnp.int32, (1, LANES), 1)
    counts = [jnp.sum(jnp.where(lane_row == g, count_row, 0.0)).astype(jnp.int32) for g in range(g_n)]
    seg_start = []
    acc = jnp.int32(0)
    for g in range(g_n):
        seg_start.append(acc)
        acc = acc + _pad_rows(counts[g])
    start_row = jnp.zeros((1, LANES), F32)
    for g in range(g_n):
        start_row = jnp.where(lane_row == g, seg_start[g].astype(F32), start_row)
    pos = jnp.sum(onehot * (rank + start_row), axis=-1, keepdims=True)
    pos_ref[...] = jnp.broadcast_to(pos, (tm, LANES))

    pos_lanes = jnp.transpose(jnp.broadcast_to(pos, (tm, LANES)))[0:1, :]
    perm = jnp.where(lax.broadcasted_iota(jnp.int32, (SORTED_ROWS, tm), 0).astype(F32) == pos_lanes, 1.0, 0.0)
    perm = perm.astype(BF16)
    hs[slot, 0:SORTED_ROWS, :] = _dot(perm, h2.astype(BF16)).astype(hs.dtype)
    w_hi = w4.astype(BF16)
    r1 = w4 - w_hi.astype(F32)
    w_mid = r1.astype(BF16)
    w_lo = (r1 - w_mid.astype(F32)).astype(BF16)
    cs[slot, 0:SORTED_ROWS, :] = _dot(perm, w_hi) + _dot(perm, w_mid) + _dot(perm, w_lo)

    copies = functools.partial(_window_copies, hs, cs, hbuf, cbuf, sems)

    def wait_all(which_slot):
        for kind in range(2):
            for g in range(g_n):
                for cp in copies(which_slot, kind, g, 0, 0):
                    cp.wait()

    @pl.when(i > 0)
    def _():
        wait_all(1 - slot)

    @pl.when(i == 0)
    def _():
        top = hbuf.shape[1] - WIN_ROWS
        for g in range(g_n):
            for cp in copies(slot, 1, g, SORTED_ROWS, top):
                cp.start()
        for g in range(g_n):
            for cp in copies(slot, 1, g, 0, 0):
                cp.wait()

    tiles_left = n_tiles - 1 - i
    for g in range(g_n):
        c_g = run[g]
        c_next = c_g + _pad_rows(counts[g])
        start_ref[i, g] = c_g
        cnt_ref[i, g] = counts[g]
        run[g] = c_next
        for cp in copies(slot, 0, g, pl.multiple_of(seg_start[g], ROW_ALIGN), pl.multiple_of(c_g, ROW_ALIGN)):
            cp.start()
        dead = c_next + tiles_left * WIN_ROWS + WIN_ROWS
        for cp in copies(slot, 1, g, SORTED_ROWS, pl.multiple_of(dead, ROW_ALIGN)):
            cp.start()

    @pl.when(i == n_tiles - 1)
    def _():
        wait_all(slot)
        for g in range(g_n):
            for cp in copies(slot, 0, g, SORTED_ROWS, pl.multiple_of(run[g], ROW_ALIGN)):
                cp.start()
        for g in range(g_n):
            for cp in copies(slot, 0, g, 0, 0):
                cp.wait()


def _merge(y_ret, y_na, y_xa, proj, x2d, w_ret_o, w_na_o, w_xa_o, w_out, g_ffn, w_router, b_router):
    t, d = x2d.shape
    tm = SORT_TILE
    n_tiles = t // tm
    rows = _group_buf_rows(t)
    gate_blk = OFF_GATE // d
    full = lambda a: pl.BlockSpec(a.shape, lambda i: (0,) * a.ndim)
    smem = pl.BlockSpec(memory_space=pltpu.SMEM)
    hbm = pl.BlockSpec(memory_space=pl.ANY)
    return pl.pallas_call(
        _merge_kernel,
        grid=(n_tiles,),
        in_specs=[
            pl.BlockSpec((tm, RET_V_WIDTH), lambda i: (i, 0)),
            pl.BlockSpec((tm, NA_WIDTH), lambda i: (i, 0)),
            pl.BlockSpec((tm, XA_WIDTH), lambda i: (i, 0)),
            pl.BlockSpec((tm, d), lambda i: (i, gate_blk)),
            pl.BlockSpec((tm, d), lambda i: (i, gate_blk + 1)),
            pl.BlockSpec((tm, d), lambda i: (i, gate_blk + 2)),
            pl.BlockSpec((tm, d), lambda i: (i, 0)),
            full(w_ret_o), full(w_na_o), full(w_xa_o), full(w_out), full(g_ffn), full(w_router), full(b_router),
        ],
        out_specs=[
            pl.BlockSpec((tm, d), lambda i: (i, 0)),
            pl.BlockSpec((tm, LANES), lambda i: (i, 0)),
            smem, smem, hbm, hbm,
        ],
        out_shape=[
            jax.ShapeDtypeStruct((t, d), F32),
            jax.ShapeDtypeStruct((t, LANES), F32),
            jax.ShapeDtypeStruct((n_tiles, MOE_GROUPS), jnp.int32),
            jax.ShapeDtypeStruct((n_tiles, MOE_GROUPS), jnp.int32),
            jax.ShapeDtypeStruct((MOE_GROUPS, rows, d), BF16),
            jax.ShapeDtypeStruct((MOE_GROUPS, rows, LANES), F32),
        ],
        scratch_shapes=[
            pltpu.VMEM((2, SCR_ROWS, d), BF16),
            pltpu.VMEM((2, SCR_ROWS, LANES), F32),
            pltpu.SMEM((MOE_GROUPS,), jnp.int32),
            pltpu.SemaphoreType.DMA((2, 4, MOE_GROUPS)),
        ],
        compiler_params=_params("arbitrary"),
        name="merge_router",
    )(y_ret, y_na, y_xa, proj, proj, proj, x2d, w_ret_o, w_na_o, w_xa_o, w_out, g_ffn, w_router, b_router)


def _expert_slots(starts, counts, t):
    totals = starts[-1] + _pad_rows(counts[-1])
    nblk = (totals + EXPERT_TILE - 1) // EXPERT_TILE
    ends = jnp.cumsum(nblk)
    first = ends - nblk
    n_slots = (t + (t // SORT_TILE) * MOE_GROUPS * (ROW_ALIGN - 1)) // EXPERT_TILE + MOE_GROUPS + 1
    j = jnp.arange(n_slots, dtype=jnp.int32)
    valid = j < ends[-1]
    jc = jnp.minimum(j, ends[-1] - 1)
    grp = jnp.sum((jc[:, None] >= ends[None, :]).astype(jnp.int32), axis=1)
    blk = jc - first[grp]
    out_starts = first[None, :] * EXPERT_TILE + starts
    return (grp, blk, valid.astype(jnp.int32)), out_starts


def _experts_kernel(grp_ref, blk_ref, valid_ref, h_ref, c_ref, wg_ref, wu_ref, wd_ref, o_ref):
    j = pl.program_id(0)
    f = MOE_D_FF

    @pl.when(valid_ref[j] == 1)
    def _():
        h = h_ref[...]
        c = c_ref[...]
        lane = lax.broadcasted_iota(jnp.int32, c.shape, 1)
        out = None
        for e in range(MOE_EXPERTS_PER_GROUP):
            a = _dot(h, wg_ref[:, e * f:(e + 1) * f])
            u = _dot(h, wu_ref[:, e * f:(e + 1) * f])
            cw = jnp.sum(jnp.where(lane == e, c, 0.0), axis=-1, keepdims=True)
            hid = (a * _sigmoid(a) * u * cw).astype(BF16)
            part = _dot(hid, wd_ref[e * f:(e + 1) * f, :])
            out = part if out is None else out + part
        o_ref[...] = out.astype(o_ref.dtype)

    @pl.when(valid_ref[j] == 0)
    def _():
        o_ref[...] = jnp.zeros_like(o_ref)


def _experts(hbuf, cbuf, wg, wu, wd, slots):
    g_n, rows, d = hbuf.shape
    ef = wg.shape[2]
    tm = EXPERT_TILE
    n_slots = slots[0].shape[0]
    grid_spec = pltpu.PrefetchScalarGridSpec(
        num_scalar_prefetch=3,
        grid=(n_slots,),
        in_specs=[
            pl.BlockSpec((None, tm, d), lambda j, grp, blk, v: (grp[j], blk[j], 0)),
            pl.BlockSpec((None, tm, LANES), lambda j, grp, blk, v: (grp[j], blk[j], 0)),
            pl.BlockSpec((None, d, ef), lambda j, grp, blk, v: (grp[j], 0, 0)),
            pl.BlockSpec((None, d, ef), lambda j, grp, blk, v: (grp[j], 0, 0)),
            pl.BlockSpec((None, ef, d), lambda j, grp, blk, v: (grp[j], 0, 0)),
        ],
        out_specs=pl.BlockSpec((tm, d), lambda j, grp, blk, v: (j, 0)),
    )
    return pl.pallas_call(
        _experts_kernel,
        grid_spec=grid_spec,
        out_shape=jax.ShapeDtypeStruct((n_slots * tm, d), BF16),
        compiler_params=_params("arbitrary"),
        name="experts",
    )(*slots, hbuf, cbuf, wg, wu, wd)


def _finish_kernel(start_ref, cnt_ref, w0_ref, w1_ref, w2_ref, w3_ref, pos_ref, x2_ref, gfin_ref, o_ref, sorted_scr,
                   *, final_norm):
    i = pl.program_id(0)
    tm = x2_ref.shape[0]

    @pl.when(i == 0)
    def _():
        sorted_scr[...] = jnp.zeros_like(sorted_scr)

    seg = jnp.int32(0)
    for g, win in enumerate((w0_ref, w1_ref, w2_ref, w3_ref)):
        padded = _pad_rows(cnt_ref[i, g])
        seg_start = seg

        def copy_piece(k, carry, win=win, seg_start=seg_start):
            src = pl.multiple_of(k * ROW_ALIGN, ROW_ALIGN)
            dst = pl.multiple_of(seg_start + k * ROW_ALIGN, ROW_ALIGN)
            sorted_scr[pl.ds(dst, ROW_ALIGN), :] = win[pl.ds(src, ROW_ALIGN), :]
            return carry

        lax.fori_loop(0, padded // ROW_ALIGN, copy_piece, 0)
        seg = seg + padded

    pos = pos_ref[:, 0:1]
    unperm = jnp.where(lax.broadcasted_iota(jnp.int32, (tm, SORTED_ROWS), 1).astype(F32) == pos, 1.0, 0.0)
    y = x2_ref[...] + _dot(unperm.astype(BF16), sorted_scr[...])
    if final_norm:
        y = _rmsnorm_f32(y, gfin_ref[...])
    o_ref[...] = y


def _finish(mbuf, starts, counts, pos, x2, g_final, final_norm):
    t, d = x2.shape
    tm = SORT_TILE

    def window(g):
        return pl.BlockSpec((pl.Element(WIN_ROWS), pl.Element(d)),
                            lambda i, st, ct, g=g: (pl.multiple_of(st[i, g], ROW_ALIGN), 0))

    grid_spec = pltpu.PrefetchScalarGridSpec(
        num_scalar_prefetch=2,
        grid=(t // tm,),
        in_specs=[
            window(0), window(1), window(2), window(3),
            pl.BlockSpec((tm, LANES), lambda i, st, ct: (i, 0)),
            pl.BlockSpec((tm, d), lambda i, st, ct: (i, 0)),
            pl.BlockSpec((1, d), lambda i, st, ct: (0, 0)),
        ],
        out_specs=pl.BlockSpec((tm, d), lambda i, st, ct: (i, 0)),
        scratch_shapes=[pltpu.VMEM((SORTED_ROWS, d), BF16)],
    )
    return pl.pallas_call(
        functools.partial(_finish_kernel, final_norm=final_norm),
        grid_spec=grid_spec,
        out_shape=jax.ShapeDtypeStruct((t, d), F32),
        compiler_params=_params("arbitrary"),
        name="finish",
    )(starts, counts, mbuf, mbuf, mbuf, mbuf, pos, x2, g_final)


def kernel(x, mem, g_mix, w_in, ret_decay_fwd, ret_decay_bwd, ret_norm_gain, w_ret_o, na_rpb, w_na_o, g_mem, w_mem_kv, w_xa_o, w_out, g_ffn, w_router_group, b_router_group, w_router_expert, b_router_expert, w_exp_gate, w_exp_up, w_exp_down, g_final):
    b, s, d = x.shape
    depth = w_in.shape[0]
    t = b * s
    cos_t, sin_t = _rope_tables(s)
    row = lambda v: v.reshape(1, -1).astype(F32)
    x2d = x.reshape(t, d)
    for l in range(depth):
        proj = _inproj(x2d, row(g_mix[l]), w_in[l].astype(BF16))
        proj3 = proj.reshape(b, s, IN_WIDTH)
        y_ret = _retention(proj3, ret_decay_fwd[l].astype(F32), ret_decay_bwd[l].astype(F32), cos_t, sin_t,
                           row(ret_norm_gain[l]))
        y_na = _neighbourhood_attention(proj3, _na_bias_table(na_rpb[l]))
        y_xa = _memory_attention(proj3, mem, row(g_mem[l]), w_mem_kv[l].astype(BF16))
        n_r = MOE_GROUPS + N_EXPERTS
        w_router = jnp.pad(jnp.concatenate([w_router_group[l], w_router_expert[l]], axis=1).astype(F32),
                           ((0, 0), (0, LANES - n_r)))
        b_router = jnp.pad(jnp.concatenate([b_router_group[l], b_router_expert[l]]).astype(F32),
                           (0, LANES - n_r)).reshape(1, LANES)
        x2, pos, starts, counts, hbuf, cbuf = _merge(
            y_ret.reshape(t, -1), y_na.reshape(t, -1), y_xa.reshape(t, -1), proj, x2d,
            w_ret_o[l].astype(BF16), w_na_o[l].astype(BF16), w_xa_o[l].astype(BF16),
            w_out[l].astype(BF16), row(g_ffn[l]), w_router, b_router)
        g_n, e_n, f = MOE_GROUPS, MOE_EXPERTS_PER_GROUP, w_exp_gate.shape[-1]
        wg = w_exp_gate[l].astype(BF16).transpose(0, 2, 1, 3).reshape(g_n, d, e_n * f)
        wu = w_exp_up[l].astype(BF16).transpose(0, 2, 1, 3).reshape(g_n, d, e_n * f)
        wd = w_exp_down[l].astype(BF16).reshape(g_n, e_n * f, d)
        slots, out_starts = _expert_slots(starts, counts, t)
        mbuf = _experts(hbuf, cbuf, wg, wu, wd, slots)
        x2d = _finish(mbuf, out_starts, counts, pos, x2, row(g_final), final_norm=(l == depth - 1))
    return x2d.reshape(b, s, d)
```

```python
import functools

import jax
import jax.numpy as jnp
import numpy as np
from jax import lax
from jax.experimental import pallas as pl
from jax.experimental.pallas import tpu as pltpu

D_MODEL = 1024
GRID_W = 64
N_BRANCHES = 3
RET_HEADS = 4
RET_QK_DIM = 128
RET_V_DIM = 256
RET_CHUNK = 128
ROPE_BASE = 10000.0
NA_HEADS = 8
NA_HEAD_DIM = 64
NA_ROWS = 8
NA_COLS = 16
XA_HEADS = 4
XA_HEAD_DIM = 128
MOE_GROUPS = 4
MOE_EXPERTS_PER_GROUP = 4
MOE_TOP_K = 2
MOE_D_FF = 512
RMS_EPS = 1e-6
GN_EPS = 1e-5

RET_QK_WIDTH = RET_HEADS * RET_QK_DIM
RET_V_WIDTH = RET_HEADS * RET_V_DIM
NA_WIDTH = NA_HEADS * NA_HEAD_DIM
XA_WIDTH = XA_HEADS * XA_HEAD_DIM
IN_WIDTH = 2 * RET_QK_WIDTH + 2 * RET_V_WIDTH + 3 * NA_WIDTH + XA_WIDTH + N_BRANCHES * D_MODEL

OFF_RQ = 0
OFF_RK = OFF_RQ + RET_QK_WIDTH
OFF_RV = OFF_RK + RET_QK_WIDTH
OFF_RG = OFF_RV + RET_V_WIDTH
OFF_NQ = OFF_RG + RET_V_WIDTH
OFF_NK = OFF_NQ + NA_WIDTH
OFF_NV = OFF_NK + NA_WIDTH
OFF_XQ = OFF_NV + NA_WIDTH
OFF_GATE = OFF_XQ + XA_WIDTH

N_EXPERTS = MOE_GROUPS * MOE_EXPERTS_PER_GROUP
LANES = 128
ROUTER_EXPERT_LANE0 = MOE_GROUPS

VMEM_LIMIT = 48 * 1024 * 1024

F32 = jnp.float32
BF16 = jnp.bfloat16


def _params(*sem):
    return pltpu.CompilerParams(dimension_semantics=sem, vmem_limit_bytes=VMEM_LIMIT)


def _rmsnorm_f32(x, g):
    return x * lax.rsqrt(jnp.mean(x * x, axis=-1, keepdims=True) + RMS_EPS) * g


def _sigmoid(x):
    return 1.0 / (1.0 + jnp.exp(-x))


def _dot(a, b):
    return jnp.dot(a, b, preferred_element_type=F32)


def _dot_nt(a, b):
    return lax.dot_general(a, b, (((1,), (1,)), ((), ())), preferred_element_type=F32)


def _inproj_kernel(x_ref, g_ref, w_ref, o_ref, h_scr):
    @pl.when(pl.program_id(1) == 0)
    def _():
        h_scr[...] = _rmsnorm_f32(x_ref[...], g_ref[...]).astype(BF16)

    o_ref[...] = _dot(h_scr[...], w_ref[...]).astype(o_ref.dtype)


def _inproj(x2d, g, w_bf16, tm=1024, tn=2048):
    t, d = x2d.shape
    n = w_bf16.shape[1]
    return pl.pallas_call(
        _inproj_kernel,
        grid=(t // tm, n // tn),
        in_specs=[
            pl.BlockSpec((tm, d), lambda i, j: (i, 0)),
            pl.BlockSpec((1, d), lambda i, j: (0, 0)),
            pl.BlockSpec((d, tn), lambda i, j: (0, j)),
        ],
        out_specs=pl.BlockSpec((tm, tn), lambda i, j: (i, j)),
        out_shape=jax.ShapeDtypeStruct((t, n), BF16),
        scratch_shapes=[pltpu.VMEM((tm, d), BF16)],
        compiler_params=_params("parallel", "arbitrary"),
        name="inproj",
    )(x2d, g, w_bf16)


def _log_sigmoid(x):
    return jnp.minimum(x, 0.0) - jnp.log1p(jnp.exp(-jnp.abs(x)))


def _retention_kernel(decf_ref, decb_ref, q_ref, k_ref, v_ref, rg_ref, cos_ref, sin_ref, gn_ref, o_ref,
                      qs, ks, y_scr, st_scr):
    h = pl.program_id(1)
    c = RET_CHUNK
    seq = q_ref.shape[0]
    n_chunks = seq // c
    half = RET_QK_DIM // 2

    lgf = _log_sigmoid(jnp.full((1, 1), decf_ref[h], F32))
    lgb = _log_sigmoid(jnp.full((1, 1), decb_ref[h], F32))

    cos = cos_ref[...]
    sin = sin_ref[...]
    q = q_ref[...].astype(F32)
    qs[...] = ((q * cos + pltpu.roll(q, half, 1) * sin) * (RET_QK_DIM ** -0.5)).astype(BF16)
    k = k_ref[...].astype(F32)
    ks[...] = (k * cos + pltpu.roll(k, half, 1) * sin).astype(BF16)

    ii = lax.broadcasted_iota(jnp.int32, (c, c), 0)
    jj = lax.broadcasted_iota(jnp.int32, (c, c), 1)
    diff = (ii - jj).astype(F32)
    dmat = jnp.exp(jnp.where(diff >= 0, diff * lgf, (-diff) * lgb))
    pos = lax.broadcasted_iota(jnp.int32, (c, 1), 0).astype(F32)
    qd_f = jnp.exp((pos + 1.0) * lgf)
    kd_f = jnp.exp((c - 1.0 - pos) * lgf)
    cd_f = jnp.exp(c * lgf)
    qd_b = jnp.exp((c - pos) * lgb)
    kd_b = jnp.exp(pos * lgb)
    cd_b = jnp.exp(c * lgb)

    st_scr[...] = jnp.zeros_like(st_scr)

    def fwd(n):
        off = n * c
        qc = qs[pl.ds(off, c), :]
        kc = ks[pl.ds(off, c), :]
        vc = v_ref[pl.ds(off, c), :]
        s = _dot_nt(qc, kc) * dmat
        y = _dot(s.astype(BF16), vc)
        st = st_scr[...]
        y = y + _dot((qc.astype(F32) * qd_f).astype(BF16), st.astype(BF16))
        y_scr[pl.ds(off, c), :] = y
        kt = (kc.astype(F32) * kd_f).T.astype(BF16)
        st_scr[...] = st * cd_f + _dot(kt, vc)

    for n in range(n_chunks):
        fwd(n)

    st_scr[...] = jnp.zeros_like(st_scr)
    gn = gn_ref[...]

    def bwd(n):
        off = n * c
        qc = qs[pl.ds(off, c), :]
        kc = ks[pl.ds(off, c), :]
        vc = v_ref[pl.ds(off, c), :]
        st = st_scr[...]
        y = y_scr[pl.ds(off, c), :] + _dot((qc.astype(F32) * qd_b).astype(BF16), st.astype(BF16))
        kt = (kc.astype(F32) * kd_b).T.astype(BF16)
        st_scr[...] = st * cd_b + _dot(kt, vc)
        mu = jnp.mean(y, axis=-1, keepdims=True)
        yc = y - mu
        var = jnp.mean(yc * yc, axis=-1, keepdims=True)
        yn = yc * lax.rsqrt(var + GN_EPS) * gn
        rg = rg_ref[pl.ds(off, c), :].astype(F32)
        o_ref[pl.ds(off, c), :] = (rg * _sigmoid(rg) * yn).astype(o_ref.dtype)

    for n in reversed(range(n_chunks)):
        bwd(n)


def _retention(proj3, dec_f, dec_b, cos_t, sin_t, gn_gain):
    b, s, _ = proj3.shape
    qb, vb = RET_QK_DIM, RET_V_DIM
    smem = pl.BlockSpec(memory_space=pltpu.SMEM)
    return pl.pallas_call(
        _retention_kernel,
        grid=(b, RET_HEADS),
        in_specs=[
            smem,
            smem,
            pl.BlockSpec((None, s, qb), lambda i, h: (i, 0, OFF_RQ // qb + h)),
            pl.BlockSpec((None, s, qb), lambda i, h: (i, 0, OFF_RK // qb + h)),
            pl.BlockSpec((None, s, vb), lambda i, h: (i, 0, OFF_RV // vb + h)),
            pl.BlockSpec((None, s, vb), lambda i, h: (i, 0, OFF_RG // vb + h)),
            pl.BlockSpec((s, qb), lambda i, h: (0, 0)),
            pl.BlockSpec((s, qb), lambda i, h: (0, 0)),
            pl.BlockSpec((1, vb), lambda i, h: (0, h)),
        ],
        out_specs=pl.BlockSpec((None, s, vb), lambda i, h: (i, 0, h)),
        out_shape=jax.ShapeDtypeStruct((b, s, RET_V_WIDTH), BF16),
        scratch_shapes=[
            pltpu.VMEM((s, qb), BF16),
            pltpu.VMEM((s, qb), BF16),
            pltpu.VMEM((s, vb), F32),
            pltpu.VMEM((qb, vb), F32),
        ],
        compiler_params=_params("parallel", "parallel"),
        name="retention",
    )(dec_f, dec_b, proj3, proj3, proj3, proj3, cos_t, sin_t, gn_gain)


def _rope_tables(seq):
    half = RET_QK_DIM // 2
    inv_freq = ROPE_BASE ** (-jnp.arange(half, dtype=F32) / half)
    ang = jnp.arange(seq, dtype=F32)[:, None] * inv_freq[None, :]
    cos, sin = jnp.cos(ang), jnp.sin(ang)
    return jnp.concatenate([cos, cos], axis=-1), jnp.concatenate([-sin, sin], axis=-1)


def _na_bias_table(rpb):
    heads = rpb.shape[0]
    w = GRID_W
    cols = np.arange(w)
    col_start = np.clip(cols - NA_COLS // 2, 0, w - NA_COLS)
    col_off = cols[None, :] - col_start[:, None]
    col_mask = (col_off >= 0) & (col_off < NA_COLS)
    rel_c = np.clip(cols[None, :] - cols[:, None], -(NA_COLS - 1), NA_COLS - 1) + (NA_COLS - 1)
    onehot = (rel_c[:, :, None] == np.arange(2 * NA_COLS - 1)).astype(np.float32)
    toe = jnp.einsum('hrc,qkc->hrqk', rpb.astype(F32), onehot, precision=lax.Precision.HIGHEST)
    toe = jnp.where(col_mask[None, None], toe, -jnp.inf)
    tab = jnp.stack([toe[:, off:off + NA_ROWS] for off in range(NA_ROWS)], axis=1)
    tab = tab.transpose(0, 1, 3, 2, 4).reshape(heads // 2, 2, NA_ROWS, w, NA_ROWS * w)
    return tab.transpose(0, 2, 1, 3, 4).reshape(heads // 2, NA_ROWS, 2 * w, NA_ROWS * w)


NA_ROWS_PER_STEP = 8


def _na_kernel(q_ref, k_ref, v_ref, bias_ref, o_ref, s_scr, p_scr, l_scr):
    w = GRID_W
    seq = q_ref.shape[0]
    rows_n = seq // w
    kr = NA_ROWS
    first = lax.broadcasted_iota(jnp.int32, (w, LANES), 1) < NA_HEAD_DIM
    scale = NA_HEAD_DIM ** -0.5

    def window_row(r):
        return jnp.clip(r - kr // 2, 0, rows_n - kr)

    def body(it, carry):
        r0 = it * NA_ROWS_PER_STEP
        for u in range(NA_ROWS_PER_STEP):
            r = r0 + u
            rs = window_row(r)
            qr = q_ref[pl.ds(pl.multiple_of(r * w, w), w), :] * scale
            zero = jnp.zeros_like(qr)
            q2 = jnp.concatenate([jnp.where(first, qr, zero), jnp.where(first, zero, qr)], axis=0)
            kk = k_ref[pl.ds(pl.multiple_of(rs * w, w), kr * w), :]
            s_scr[u] = _dot_nt(q2, kk) + bias_ref[rs - r + (NA_ROWS - 1)]
        for u in range(NA_ROWS_PER_STEP):
            m = jnp.max(s_scr[u], axis=-1, keepdims=True)
            p = jnp.exp(s_scr[u] - m)
            l_scr[u] = jnp.broadcast_to(jnp.sum(p, axis=-1, keepdims=True), l_scr.shape[1:])
            p_scr[u] = p.astype(BF16)
        for u in range(NA_ROWS_PER_STEP):
            r = r0 + u
            vv = v_ref[pl.ds(pl.multiple_of(window_row(r) * w, w), kr * w), :]
            o2 = _dot(p_scr[u], vv) / l_scr[u]
            o_ref[pl.ds(pl.multiple_of(r * w, w), w), :] = jnp.where(first, o2[:w], o2[w:]).astype(o_ref.dtype)
        return carry

    lax.fori_loop(0, rows_n // NA_ROWS_PER_STEP, body, 0)


def _neighbourhood_attention(proj3, bias_tab):
    b, s, _ = proj3.shape
    pairs = NA_HEADS // 2
    return pl.pallas_call(
        _na_kernel,
        grid=(b, pairs),
        in_specs=[
            pl.BlockSpec((None, s, LANES), lambda i, p: (i, 0, OFF_NQ // LANES + p)),
            pl.BlockSpec((None, s, LANES), lambda i, p: (i, 0, OFF_NK // LANES + p)),
            pl.BlockSpec((None, s, LANES), lambda i, p: (i, 0, OFF_NV // LANES + p)),
            pl.BlockSpec((None, NA_ROWS, 2 * GRID_W, NA_ROWS * GRID_W), lambda i, p: (p, 0, 0, 0)),
        ],
        out_specs=pl.BlockSpec((None, s, LANES), lambda i, p: (i, 0, p)),
        out_shape=jax.ShapeDtypeStruct((b, s, NA_WIDTH), BF16),
        scratch_shapes=[
            pltpu.VMEM((NA_ROWS_PER_STEP, 2 * GRID_W, NA_ROWS * GRID_W), F32),
            pltpu.VMEM((NA_ROWS_PER_STEP, 2 * GRID_W, NA_ROWS * GRID_W), BF16),
            pltpu.VMEM((NA_ROWS_PER_STEP, 2 * GRID_W, LANES), F32),
        ],
        compiler_params=_params("parallel", "parallel"),
        name="nbr_attn",
    )(proj3, proj3, proj3, bias_tab)


def _xa_kernel(q_ref, mem_ref, g_ref, wkv_ref, o_ref, kv_scr):
    @pl.when(pl.program_id(1) == 0)
    def _():
        mn = _rmsnorm_f32(mem_ref[...], g_ref[...]).astype(BF16)
        kv_scr[...] = _dot(mn, wkv_ref[...]).astype(BF16)

    dh = XA_HEAD_DIM
    scale = dh ** -0.5
    for h in range(XA_HEADS):
        q = q_ref[:, h * dh:(h + 1) * dh]
        k = kv_scr[:, h * dh:(h + 1) * dh]
        v = kv_scr[:, XA_WIDTH + h * dh:XA_WIDTH + (h + 1) * dh]
        s = _dot_nt(q, k) * scale
        m = jnp.max(s, axis=-1, keepdims=True)
        p = jnp.exp(s - m)
        l = jnp.sum(p, axis=-1, keepdims=True)
        o_ref[:, h * dh:(h + 1) * dh] = (_dot(p.astype(BF16), v) / l).astype(o_ref.dtype)


def _memory_attention(proj3, mem, g_mem, wkv_bf16, ts=1024):
    b, s, _ = proj3.shape
    m, d = mem.shape[1], mem.shape[2]
    return pl.pallas_call(
        _xa_kernel,
        grid=(b, s // ts),
        in_specs=[
            pl.BlockSpec((None, ts, XA_WIDTH), lambda i, j: (i, j, OFF_XQ // XA_WIDTH)),
            pl.BlockSpec((None, m, d), lambda i, j: (i, 0, 0)),
            pl.BlockSpec((1, d), lambda i, j: (0, 0)),
            pl.BlockSpec((d, 2 * XA_WIDTH), lambda i, j: (0, 0)),
        ],
        out_specs=pl.BlockSpec((None, ts, XA_WIDTH), lambda i, j: (i, j, 0)),
        out_shape=jax.ShapeDtypeStruct((b, s, XA_WIDTH), BF16),
        scratch_shapes=[pltpu.VMEM((m, 2 * XA_WIDTH), BF16)],
        compiler_params=_params("parallel", "arbitrary"),
        name="mem_attn",
    )(proj3, mem, g_mem, wkv_bf16)


def _masked_lane_max(v, mask):
    return jnp.max(jnp.where(mask, v, -jnp.inf), axis=-1, keepdims=True)


def _first_lane_eq(v, target, mask, lane):
    return jnp.min(jnp.where(mask & (v == target), lane, float(LANES)), axis=-1, keepdims=True)


def _route(logits):
    g_n, e_n = MOE_GROUPS, MOE_EXPERTS_PER_GROUP
    lane = lax.broadcasted_iota(jnp.int32, logits.shape, 1).astype(F32)
    is_grp = lane < g_n
    gmax = _masked_lane_max(logits, is_grp)
    gsum = jnp.sum(jnp.where(is_grp, jnp.exp(logits - gmax), 0.0), axis=-1, keepdims=True)
    grp_w = 1.0 / gsum
    gidx = _first_lane_eq(logits, gmax, is_grp, lane)
    lo = ROUTER_EXPERT_LANE0 + gidx * e_n
    in_grp = (lane >= lo) & (lane < lo + e_n)
    emax = _masked_lane_max(logits, in_grp)
    ex = jnp.where(in_grp, jnp.exp(logits - emax), 0.0)
    prob = ex / jnp.sum(ex, axis=-1, keepdims=True)
    p1 = _masked_lane_max(prob, in_grp)
    i1 = _first_lane_eq(prob, p1, in_grp, lane)
    rest = in_grp & (lane != i1)
    p2 = _masked_lane_max(prob, rest)
    i2 = _first_lane_eq(prob, p2, rest, lane)
    tot = p1 + p2
    w1 = p1 / tot * grp_w
    w2 = p2 / tot * grp_w
    return gidx, jnp.where(lane == i1 - lo, w1, jnp.where(lane == i2 - lo, w2, 0.0))


SORT_TILE = 512
ROW_ALIGN = 16
SORTED_ROWS = 640
WIN_ROWS = SORT_TILE
HALF_WIN = WIN_ROWS // 2
SCR_ROWS = SORTED_ROWS + WIN_ROWS
EXPERT_TILE = 512


def _group_buf_rows(t):
    return t + 2 * WIN_ROWS


def _pad_rows(n):
    return (n + (ROW_ALIGN - 1)) // ROW_ALIGN * ROW_ALIGN


def _window_copies(hs, cs, hbuf, cbuf, sems, slot, kind, g, src_row, dst_row):
    return (
        pltpu.make_async_copy(hs.at[slot, pl.ds(src_row, WIN_ROWS)], hbuf.at[g, pl.ds(dst_row, WIN_ROWS)],
                              sems.at[slot, 2 * kind, g]),
        pltpu.make_async_copy(cs.at[slot, pl.ds(src_row, WIN_ROWS)], cbuf.at[g, pl.ds(dst_row, WIN_ROWS)],
                              sems.at[slot, 2 * kind + 1, g]),
    )


def _merge_kernel(yret_ref, yna_ref, yxa_ref, gr_ref, gn_ref, gx_ref, x_ref, wro_ref, wno_ref, wxo_ref, wout_ref,
                  gffn_ref, wr_ref, br_ref,
                  x2_ref, pos_ref, start_ref, cnt_ref, hbuf, cbuf,
                  hs, cs, run, sems):
    i = pl.program_id(0)
    n_tiles = pl.num_programs(0)
    slot = lax.rem(i, 2)
    tm = x_ref.shape[0]
    g_n = MOE_GROUPS

    @pl.when(i == 0)
    def _():
        for g in range(g_n):
            run[g] = 0
        hs[:, SORTED_ROWS:, :] = jnp.zeros((2, WIN_ROWS, hs.shape[2]), hs.dtype)
        cs[:, SORTED_ROWS:, :] = jnp.zeros((2, WIN_ROWS, cs.shape[2]), cs.dtype)

    y_ret = _dot(yret_ref[...], wro_ref[...])
    y_na = _dot(yna_ref[...], wno_ref[...])
    y_xa = _dot(yxa_ref[...], wxo_ref[...])
    mix = (_sigmoid(gr_ref[...].astype(F32)) * y_ret + _sigmoid(gn_ref[...].astype(F32)) * y_na
           + _sigmoid(gx_ref[...].astype(F32)) * y_xa)
    x2 = x_ref[...] + _dot(mix.astype(BF16), wout_ref[...])
    x2_ref[...] = x2
    h2 = _rmsnorm_f32(x2, gffn_ref[...])
    h_hi = h2.astype(BF16)
    h_lo = (h2 - h_hi.astype(F32)).astype(BF16)
    wr = wr_ref[...]
    wr_hi = wr.astype(BF16)
    wr_lo = (wr - wr_hi.astype(F32)).astype(BF16)
    logits = _dot(h_hi, wr_hi) + _dot(h_lo, wr_hi) + _dot(h_hi, wr_lo) + br_ref[...]
    gidx, w4 = _route(logits)

    lane = lax.broadcasted_iota(jnp.int32, (tm, LANES), 1).astype(F32)
    onehot = jnp.where(lane == gidx, 1.0, 0.0)
    before = (lax.broadcasted_iota(jnp.int32, (tm, tm), 1) < lax.broadcasted_iota(jnp.int32, (tm, tm), 0))
    rank = _dot(jnp.where(before, 1.0, 0.0).astype(BF16), onehot.astype(BF16))
    count_row = rank[tm - 1:tm, :] + onehot[tm - 1:tm, :]
    lane_row = lax.broadcasted_iota(jnp.int32, (1, LANES), 1)
    counts = [jnp.sum(jnp.where(lane_row == g, count_row, 0.0)).astype(jnp.int32) for g in range(g_n)]
    seg_start = []
    acc = jnp.int32(0)
    for g in range(g_n):
        seg_start.append(acc)
        acc = acc + _pad_rows(counts[g])
    start_row = jnp.zeros((1, LANES), F32)
    for g in range(g_n):
        start_row = jnp.where(lane_row == g, seg_start[g].astype(F32), start_row)
    pos = jnp.sum(onehot * (rank + start_row), axis=-1, keepdims=True)
    pos_ref[...] = jnp.broadcast_to(pos, (tm, LANES))

    pos_lanes = jnp.transpose(jnp.broadcast_to(pos, (tm, LANES)))[0:1, :]
    perm = jnp.where(lax.broadcasted_iota(jnp.int32, (SORTED_ROWS, tm), 0).astype(F32) == pos_lanes, 1.0, 0.0)
    perm = perm.astype(BF16)
    hs[slot, 0:SORTED_ROWS, :] = _dot(perm, h_hi).astype(hs.dtype)
    w_hi = w4.astype(BF16)
    r1 = w4 - w_hi.astype(F32)
    w_mid = r1.astype(BF16)
    w_lo = (r1 - w_mid.astype(F32)).astype(BF16)
    cs[slot, 0:SORTED_ROWS, :] = _dot(perm, w_hi) + _dot(perm, w_mid) + _dot(perm, w_lo)

    copies = functools.partial(_window_copies, hs, cs, hbuf, cbuf, sems)

    def wait_all(which_slot):
        for kind in range(2):
            for g in range(g_n):
                for cp in copies(which_slot, kind, g, 0, 0):
                    cp.wait()

    @pl.when(i > 0)
    def _():
        wait_all(1 - slot)

    @pl.when(i == 0)
    def _():
        top = hbuf.shape[1] - WIN_ROWS
        for g in range(g_n):
            for cp in copies(slot, 1, g, SORTED_ROWS, top):
                cp.start()
        for g in range(g_n):
            for cp in copies(slot, 1, g, 0, 0):
                cp.wait()

    tiles_left = n_tiles - 1 - i
    for g in range(g_n):
        c_g = run[g]
        c_next = c_g + _pad_rows(counts[g])
        start_ref[i, g] = c_g
        cnt_ref[i, g] = counts[g]
        run[g] = c_next
        for cp in copies(slot, 0, g, pl.multiple_of(seg_start[g], ROW_ALIGN), pl.multiple_of(c_g, ROW_ALIGN)):
            cp.start()
        dead = c_next + tiles_left * WIN_ROWS + WIN_ROWS
        for cp in copies(slot, 1, g, SORTED_ROWS, pl.multiple_of(dead, ROW_ALIGN)):
            cp.start()

    @pl.when(i == n_tiles - 1)
    def _():
        wait_all(slot)
        for g in range(g_n):
            for cp in copies(slot, 0, g, SORTED_ROWS, pl.multiple_of(run[g], ROW_ALIGN)):
                cp.start()
        for g in range(g_n):
            for cp in copies(slot, 0, g, 0, 0):
                cp.wait()


def _merge(y_ret, y_na, y_xa, proj, x2d, w_ret_o, w_na_o, w_xa_o, w_out, g_ffn, w_router, b_router):
    t, d = x2d.shape
    tm = SORT_TILE
    n_tiles = t // tm
    rows = _group_buf_rows(t)
    gate_blk = OFF_GATE // d
    full = lambda a: pl.BlockSpec(a.shape, lambda i: (0,) * a.ndim)
    smem = pl.BlockSpec(memory_space=pltpu.SMEM)
    hbm = pl.BlockSpec(memory_space=pl.ANY)
    return pl.pallas_call(
        _merge_kernel,
        grid=(n_tiles,),
        in_specs=[
            pl.BlockSpec((tm, RET_V_WIDTH), lambda i: (i, 0)),
            pl.BlockSpec((tm, NA_WIDTH), lambda i: (i, 0)),
            pl.BlockSpec((tm, XA_WIDTH), lambda i: (i, 0)),
            pl.BlockSpec((tm, d), lambda i: (i, gate_blk)),
            pl.BlockSpec((tm, d), lambda i: (i, gate_blk + 1)),
            pl.BlockSpec((tm, d), lambda i: (i, gate_blk + 2)),
            pl.BlockSpec((tm, d), lambda i: (i, 0)),
            full(w_ret_o), full(w_na_o), full(w_xa_o), full(w_out), full(g_ffn), full(w_router), full(b_router),
        ],
        out_specs=[
            pl.BlockSpec((tm, d), lambda i: (i, 0)),
            pl.BlockSpec((tm, LANES), lambda i: (i, 0)),
            smem, smem, hbm, hbm,
        ],
        out_shape=[
            jax.ShapeDtypeStruct((t, d), F32),
            jax.ShapeDtypeStruct((t, LANES), F32),
            jax.ShapeDtypeStruct((n_tiles, MOE_GROUPS), jnp.int32),
            jax.ShapeDtypeStruct((n_tiles, MOE_GROUPS), jnp.int32),
            jax.ShapeDtypeStruct((MOE_GROUPS, rows, d), BF16),
            jax.ShapeDtypeStruct((MOE_GROUPS, rows, LANES), F32),
        ],
        scratch_shapes=[
            pltpu.VMEM((2, SCR_ROWS, d), BF16),
            pltpu.VMEM((2, SCR_ROWS, LANES), F32),
            pltpu.SMEM((MOE_GROUPS,), jnp.int32),
            pltpu.SemaphoreType.DMA((2, 4, MOE_GROUPS)),
        ],
        compiler_params=_params("arbitrary"),
        name="merge_router",
    )(y_ret, y_na, y_xa, proj, proj, proj, x2d, w_ret_o, w_na_o, w_xa_o, w_out, g_ffn, w_router, b_router)


def _expert_slots(starts, counts, t):
    totals = starts[-1] + _pad_rows(counts[-1])
    nblk = (totals + EXPERT_TILE - 1) // EXPERT_TILE
    ends = jnp.cumsum(nblk)
    first = ends - nblk
    n_slots = (t + (t // SORT_TILE) * MOE_GROUPS * (ROW_ALIGN - 1)) // EXPERT_TILE + MOE_GROUPS + 1
    j = jnp.arange(n_slots, dtype=jnp.int32)
    valid = j < ends[-1]
    jc = jnp.clip(j, 0, jnp.maximum(ends[-1] - 1, 0))
    grp = jnp.sum((jc[:, None] >= ends[None, :]).astype(jnp.int32), axis=1)
    blk = jc - first[grp]
    out_starts = first[None, :] * EXPERT_TILE + starts
    return (grp, blk, valid.astype(jnp.int32)), out_starts


def _experts_kernel(grp_ref, blk_ref, valid_ref, h_ref, c_ref, wg_ref, wu_ref, wd_ref, o_ref):
    j = pl.program_id(0)
    f = MOE_D_FF

    @pl.when(valid_ref[j] == 1)
    def _():
        h = h_ref[...]
        c = c_ref[...]
        lane = lax.broadcasted_iota(jnp.int32, c.shape, 1)
        out = None
        for e in range(MOE_EXPERTS_PER_GROUP):
            a = _dot(h, wg_ref[:, e * f:(e + 1) * f])
            u = _dot(h, wu_ref[:, e * f:(e + 1) * f])
            cw = jnp.sum(jnp.where(lane == e, c, 0.0), axis=-1, keepdims=True)
            hid = (a * _sigmoid(a) * u * cw).astype(BF16)
            part = _dot(hid, wd_ref[e * f:(e + 1) * f, :])
            out = part if out is None else out + part
        o_ref[...] = out.astype(o_ref.dtype)

    @pl.when(valid_ref[j] == 0)
    def _():
        o_ref[...] = jnp.zeros_like(o_ref)


def _experts(hbuf, cbuf, wg, wu, wd, slots):
    g_n, rows, d = hbuf.shape
    ef = wg.shape[2]
    tm = EXPERT_TILE
    n_slots = slots[0].shape[0]
    grid_spec = pltpu.PrefetchScalarGridSpec(
        num_scalar_prefetch=3,
        grid=(n_slots,),
        in_specs=[
            pl.BlockSpec((None, tm, d), lambda j, grp, blk, v: (grp[j], blk[j], 0)),
            pl.BlockSpec((None, tm, LANES), lambda j, grp, blk, v: (grp[j], blk[j], 0)),
            pl.BlockSpec((None, d, ef), lambda j, grp, blk, v: (grp[j], 0, 0)),
            pl.BlockSpec((None, d, ef), lambda j, grp, blk, v: (grp[j], 0, 0)),
            pl.BlockSpec((None, ef, d), lambda j, grp, blk, v: (grp[j], 0, 0)),
        ],
        out_specs=pl.BlockSpec((tm, d), lambda j, grp, blk, v: (j, 0)),
    )
    return pl.pallas_call(
        _experts_kernel,
        grid_spec=grid_spec,
        out_shape=jax.ShapeDtypeStruct((n_slots * tm, d), BF16),
        compiler_params=_params("arbitrary"),
        name="experts",
    )(*slots, hbuf, cbuf, wg, wu, wd)


def _finish_kernel(start_ref, cnt_ref, *refs, final_norm):
    wins = refs[:2 * MOE_GROUPS]
    pos_ref, x2_ref, gfin_ref, o_ref, sorted_scr = refs[2 * MOE_GROUPS:]
    i = pl.program_id(0)
    tm = x2_ref.shape[0]
    half_pieces = HALF_WIN // ROW_ALIGN

    @pl.when(i == 0)
    def _():
        sorted_scr[...] = jnp.zeros_like(sorted_scr)

    seg = jnp.int32(0)
    for g in range(MOE_GROUPS):
        padded = _pad_rows(cnt_ref[i, g])
        pieces = padded // ROW_ALIGN
        for half, win in enumerate(wins[2 * g:2 * g + 2]):
            base = seg + half * HALF_WIN

            def copy_piece(k, carry, win=win, base=base):
                src = pl.multiple_of(k * ROW_ALIGN, ROW_ALIGN)
                dst = pl.multiple_of(base + k * ROW_ALIGN, ROW_ALIGN)
                sorted_scr[pl.ds(dst, ROW_ALIGN), :] = win[pl.ds(src, ROW_ALIGN), :]
                return carry

            lax.fori_loop(0, jnp.clip(pieces - half * half_pieces, 0, half_pieces), copy_piece, 0)
        seg = seg + padded

    pos = pos_ref[:, 0:1]
    unperm = jnp.where(lax.broadcasted_iota(jnp.int32, (tm, SORTED_ROWS), 1).astype(F32) == pos, 1.0, 0.0)
    y = x2_ref[...] + _dot(unperm.astype(BF16), sorted_scr[...])
    if final_norm:
        y = _rmsnorm_f32(y, gfin_ref[...])
    o_ref[...] = y


def _finish(mbuf, starts, counts, pos, x2, g_final, final_norm):
    t, d = x2.shape
    tm = SORT_TILE

    def window(g, half):
        def index(i, st, ct):
            row = st[i, g] + half * HALF_WIN
            if half:
                row = jnp.where(_pad_rows(ct[i, g]) > HALF_WIN, row, 0)
            return pl.multiple_of(row, ROW_ALIGN), 0

        return pl.BlockSpec((pl.Element(HALF_WIN), pl.Element(d)), index)

    grid_spec = pltpu.PrefetchScalarGridSpec(
        num_scalar_prefetch=2,
        grid=(t // tm,),
        in_specs=[
            *[window(g, half) for g in range(MOE_GROUPS) for half in range(2)],
            pl.BlockSpec((tm, LANES), lambda i, st, ct: (i, 0)),
            pl.BlockSpec((tm, d), lambda i, st, ct: (i, 0)),
            pl.BlockSpec((1, d), lambda i, st, ct: (0, 0)),
        ],
        out_specs=pl.BlockSpec((tm, d), lambda i, st, ct: (i, 0)),
        scratch_shapes=[pltpu.VMEM((SORTED_ROWS, d), BF16)],
    )
    return pl.pallas_call(
        functools.partial(_finish_kernel, final_norm=final_norm),
        grid_spec=grid_spec,
        out_shape=jax.ShapeDtypeStruct((t, d), F32),
        compiler_params=_params("arbitrary"),
        name="finish",
    )(starts, counts, *([mbuf] * (2 * MOE_GROUPS)), pos, x2, g_final)


def kernel(x, mem, g_mix, w_in, ret_decay_fwd, ret_decay_bwd, ret_norm_gain, w_ret_o, na_rpb, w_na_o, g_mem, w_mem_kv, w_xa_o, w_out, g_ffn, w_router_group, b_router_group, w_router_expert, b_router_expert, w_exp_gate, w_exp_up, w_exp_down, g_final):
    b, s, d = x.shape
    depth = w_in.shape[0]
    t = b * s
    cos_t, sin_t = _rope_tables(s)
    row = lambda v: v.reshape(1, -1).astype(F32)
    x2d = x.reshape(t, d)
    for l in range(depth):
        proj = _inproj(x2d, row(g_mix[l]), w_in[l].astype(BF16))
        proj3 = proj.reshape(b, s, IN_WIDTH)
        y_ret = _retention(proj3, ret_decay_fwd[l].astype(F32), ret_decay_bwd[l].astype(F32), cos_t, sin_t,
                           row(ret_norm_gain[l]))
        y_na = _neighbourhood_attention(proj3, _na_bias_table(na_rpb[l]))
        y_xa = _memory_attention(proj3, mem, row(g_mem[l]), w_mem_kv[l].astype(BF16))
        n_r = MOE_GROUPS + N_EXPERTS
        w_router = jnp.pad(jnp.concatenate([w_router_group[l], w_router_expert[l]], axis=1).astype(F32),
                           ((0, 0), (0, LANES - n_r)))
        b_router = jnp.pad(jnp.concatenate([b_router_group[l], b_router_expert[l]]).astype(F32),
                           (0, LANES - n_r)).reshape(1, LANES)
        x2, pos, starts, counts, hbuf, cbuf = _merge(
            y_ret.reshape(t, -1), y_na.reshape(t, -1), y_xa.reshape(t, -1), proj, x2d,
            w_ret_o[l].astype(BF16), w_na_o[l].astype(BF16), w_xa_o[l].astype(BF16),
            w_out[l].astype(BF16), row(g_ffn[l]), w_router, b_router)
        g_n, e_n, f = MOE_GROUPS, MOE_EXPERTS_PER_GROUP, w_exp_gate.shape[-1]
        wg = w_exp_gate[l].astype(BF16).transpose(0, 2, 1, 3).reshape(g_n, d, e_n * f)
        wu = w_exp_up[l].astype(BF16).transpose(0, 2, 1, 3).reshape(g_n, d, e_n * f)
        wd = w_exp_down[l].astype(BF16).reshape(g_n, e_n * f, d)
        slots, out_starts = _expert_slots(starts, counts, t)
        mbuf = _experts(hbuf, cbuf, wg, wu, wd, slots)
        x2d = _finish(mbuf, out_starts, counts, pos, x2, row(g_final), final_norm=(l == depth - 1))
    return x2d.reshape(b, s, d)
```

```python
import functools

import jax
import jax.numpy as jnp
import numpy as np
from jax import lax
from jax.experimental import pallas as pl
from jax.experimental.pallas import tpu as pltpu

D_MODEL = 1024
GRID_W = 64
N_BRANCHES = 3
RET_HEADS = 4
RET_QK_DIM = 128
RET_V_DIM = 256
RET_CHUNK = 128
ROPE_BASE = 10000.0
NA_HEADS = 8
NA_HEAD_DIM = 64
NA_ROWS = 8
NA_COLS = 16
XA_HEADS = 4
XA_HEAD_DIM = 128
MOE_GROUPS = 4
MOE_EXPERTS_PER_GROUP = 4
MOE_TOP_K = 2
MOE_D_FF = 512
RMS_EPS = 1e-6
GN_EPS = 1e-5

RET_QK_WIDTH = RET_HEADS * RET_QK_DIM
RET_V_WIDTH = RET_HEADS * RET_V_DIM
NA_WIDTH = NA_HEADS * NA_HEAD_DIM
XA_WIDTH = XA_HEADS * XA_HEAD_DIM
IN_WIDTH = 2 * RET_QK_WIDTH + 2 * RET_V_WIDTH + 3 * NA_WIDTH + XA_WIDTH + N_BRANCHES * D_MODEL

OFF_RQ = 0
OFF_RK = OFF_RQ + RET_QK_WIDTH
OFF_RV = OFF_RK + RET_QK_WIDTH
OFF_RG = OFF_RV + RET_V_WIDTH
OFF_NQ = OFF_RG + RET_V_WIDTH
OFF_NK = OFF_NQ + NA_WIDTH
OFF_NV = OFF_NK + NA_WIDTH
OFF_XQ = OFF_NV + NA_WIDTH
OFF_GATE = OFF_XQ + XA_WIDTH

N_EXPERTS = MOE_GROUPS * MOE_EXPERTS_PER_GROUP
LANES = 128
ROUTER_EXPERT_LANE0 = MOE_GROUPS

VMEM_LIMIT = 48 * 1024 * 1024
EXPERTS_VMEM_LIMIT = 56 * 1024 * 1024

F32 = jnp.float32
BF16 = jnp.bfloat16


def _params(*sem):
    return pltpu.CompilerParams(dimension_semantics=sem, vmem_limit_bytes=VMEM_LIMIT)


def _rmsnorm_f32(x, g):
    return x * lax.rsqrt(jnp.mean(x * x, axis=-1, keepdims=True) + RMS_EPS) * g


def _sigmoid(x):
    return 0.5 * jnp.tanh(0.5 * x) + 0.5


def _silu(x):
    h = 0.5 * x
    return h + h * jnp.tanh(h)


def _dot(a, b):
    return jnp.dot(a, b, preferred_element_type=F32)


def _dot_tn(a, b):
    return lax.dot_general(a, b, (((0,), (0,)), ((), ())), preferred_element_type=F32)


def _dot_nt(a, b):
    return lax.dot_general(a, b, (((1,), (1,)), ((), ())), preferred_element_type=F32)


def _inproj_kernel(x_ref, g_ref, w_ref, o_ref, h_scr):
    @pl.when(pl.program_id(1) == 0)
    def _():
        h_scr[...] = _rmsnorm_f32(x_ref[...], g_ref[...]).astype(BF16)

    o_ref[...] = _dot(h_scr[...], w_ref[...]).astype(o_ref.dtype)


def _inproj(x2d, g, w_bf16, tm=1024, tn=2048):
    t, d = x2d.shape
    n = w_bf16.shape[1]
    return pl.pallas_call(
        _inproj_kernel,
        grid=(t // tm, n // tn),
        in_specs=[
            pl.BlockSpec((tm, d), lambda i, j: (i, 0)),
            pl.BlockSpec((1, d), lambda i, j: (0, 0)),
            pl.BlockSpec((d, tn), lambda i, j: (0, j)),
        ],
        out_specs=pl.BlockSpec((tm, tn), lambda i, j: (i, j)),
        out_shape=jax.ShapeDtypeStruct((t, n), BF16),
        scratch_shapes=[pltpu.VMEM((tm, d), BF16)],
        compiler_params=_params("parallel", "arbitrary"),
        name="inproj",
    )(x2d, g, w_bf16)


def _log_sigmoid(x):
    return jnp.minimum(x, 0.0) - jnp.log1p(jnp.exp(-jnp.abs(x)))


def _retention_kernel(decf_ref, decb_ref, q_ref, k_ref, v_ref, rg_ref, cos_ref, sin_ref, gn_ref, o_ref,
                      qfs, qbs, y_scr, kvf_scr, kvb_scr, st_scr):
    h = pl.program_id(1)
    c = RET_CHUNK
    seq = q_ref.shape[0]
    n_chunks = seq // c
    half = RET_QK_DIM // 2

    lgf = _log_sigmoid(jnp.full((1, 1), decf_ref[h], F32))
    lgb = _log_sigmoid(jnp.full((1, 1), decb_ref[h], F32))

    ii = lax.broadcasted_iota(jnp.int32, (c, c), 0)
    jj = lax.broadcasted_iota(jnp.int32, (c, c), 1)
    diff = (ii - jj).astype(F32)
    dmat = jnp.exp(jnp.where(diff >= 0, diff * lgf, (-diff) * lgb))
    pos = lax.broadcasted_iota(jnp.int32, (c, 1), 0).astype(F32)
    qd_f = jnp.exp((pos + 1.0) * lgf)
    kd_f = jnp.exp((c - 1.0 - pos) * lgf)
    cd_f = jnp.exp(c * lgf)
    qd_b = jnp.exp((c - pos) * lgb)
    kd_b = jnp.exp(pos * lgb)
    cd_b = jnp.exp(c * lgb)

    def intra(n):
        rows = pl.ds(n * c, c)
        cos = cos_ref[rows, :]
        sin = sin_ref[rows, :]
        q = q_ref[rows, :].astype(F32)
        q = (q * cos + pltpu.roll(q, half, 1) * sin) * (RET_QK_DIM ** -0.5)
        qfs[rows, :] = (q * qd_f).astype(BF16)
        qbs[rows, :] = (q * qd_b).astype(BF16)
        k = k_ref[rows, :].astype(F32)
        k = k * cos + pltpu.roll(k, half, 1) * sin
        s = _dot_nt(q.astype(BF16), k.astype(BF16)) * dmat
        lhs = jnp.concatenate([s.astype(BF16), (k * kd_f).T.astype(BF16), (k * kd_b).T.astype(BF16)], axis=0)
        r = _dot(lhs, v_ref[rows, :])
        y_scr[rows, :] = r[:c]
        kvf_scr[n] = r[c:2 * c]
        kvb_scr[n] = r[2 * c:]

    for n in range(n_chunks):
        intra(n)

    st_scr[...] = jnp.zeros_like(st_scr)

    def fwd(n):
        rows = pl.ds(n * c, c)
        st = st_scr[...]
        y_scr[rows, :] += _dot(qfs[rows, :], st.astype(BF16))
        st_scr[...] = st * cd_f + kvf_scr[n]

    for n in range(n_chunks):
        fwd(n)

    st_scr[...] = jnp.zeros_like(st_scr)
    gn = gn_ref[...]

    def bwd(n):
        rows = pl.ds(n * c, c)
        st = st_scr[...]
        y = y_scr[rows, :] + _dot(qbs[rows, :], st.astype(BF16))
        st_scr[...] = st * cd_b + kvb_scr[n]
        mu = jnp.mean(y, axis=-1, keepdims=True)
        yc = y - mu
        var = jnp.mean(yc * yc, axis=-1, keepdims=True)
        yn = yc * lax.rsqrt(var + GN_EPS) * gn
        o_ref[rows, :] = _silu(rg_ref[rows, :]) * yn.astype(o_ref.dtype)

    for n in reversed(range(n_chunks)):
        bwd(n)


def _retention(proj3, dec_f, dec_b, cos_t, sin_t, gn_gain):
    b, s, _ = proj3.shape
    qb, vb = RET_QK_DIM, RET_V_DIM
    smem = pl.BlockSpec(memory_space=pltpu.SMEM)
    return pl.pallas_call(
        _retention_kernel,
        grid=(b, RET_HEADS),
        in_specs=[
            smem,
            smem,
            pl.BlockSpec((None, s, qb), lambda i, h: (i, 0, OFF_RQ // qb + h)),
            pl.BlockSpec((None, s, qb), lambda i, h: (i, 0, OFF_RK // qb + h)),
            pl.BlockSpec((None, s, vb), lambda i, h: (i, 0, OFF_RV // vb + h)),
            pl.BlockSpec((None, s, vb), lambda i, h: (i, 0, OFF_RG // vb + h)),
            pl.BlockSpec((s, qb), lambda i, h: (0, 0)),
            pl.BlockSpec((s, qb), lambda i, h: (0, 0)),
            pl.BlockSpec((1, vb), lambda i, h: (0, h)),
        ],
        out_specs=pl.BlockSpec((None, s, vb), lambda i, h: (i, 0, h)),
        out_shape=jax.ShapeDtypeStruct((b, s, RET_V_WIDTH), BF16),
        scratch_shapes=[
            pltpu.VMEM((s, qb), BF16),
            pltpu.VMEM((s, qb), BF16),
            pltpu.VMEM((s, vb), F32),
            pltpu.VMEM((s // RET_CHUNK, qb, vb), F32),
            pltpu.VMEM((s // RET_CHUNK, qb, vb), F32),
            pltpu.VMEM((qb, vb), F32),
        ],
        compiler_params=_params("parallel", "parallel"),
        name="retention",
    )(dec_f, dec_b, proj3, proj3, proj3, proj3, cos_t, sin_t, gn_gain)


def _rope_tables(seq):
    half = RET_QK_DIM // 2
    inv_freq = ROPE_BASE ** (-jnp.arange(half, dtype=F32) / half)
    ang = jnp.arange(seq, dtype=F32)[:, None] * inv_freq[None, :]
    cos, sin = jnp.cos(ang), jnp.sin(ang)
    return jnp.concatenate([cos, cos], axis=-1), jnp.concatenate([-sin, sin], axis=-1)


def _na_bias_table(rpb):
    heads = rpb.shape[0]
    w = GRID_W
    cols = np.arange(w)
    col_start = np.clip(cols - NA_COLS // 2, 0, w - NA_COLS)
    col_off = cols[None, :] - col_start[:, None]
    col_mask = (col_off >= 0) & (col_off < NA_COLS)
    rel_c = np.clip(cols[None, :] - cols[:, None], -(NA_COLS - 1), NA_COLS - 1) + (NA_COLS - 1)
    onehot = (rel_c[:, :, None] == np.arange(2 * NA_COLS - 1)).astype(np.float32)
    win = jnp.stack([rpb[:, off:off + NA_ROWS] for off in range(NA_ROWS)], axis=1).astype(F32)
    win = win.reshape(heads // 2, 2, NA_ROWS, NA_ROWS, 2 * NA_COLS - 1)
    tab = jnp.einsum('phowc,qkc->pohqwk', win, onehot, precision=lax.Precision.HIGHEST)
    tab = jnp.where(col_mask[None, None, None, :, None, :], tab, -jnp.inf)
    return tab.reshape(heads // 2, NA_ROWS, 2 * w, NA_ROWS * w)


NA_ROWS_PER_STEP = 8


def _na_kernel(q_ref, k_ref, v_ref, bias_ref, o_ref, s_scr, p_scr, l_scr):
    w = GRID_W
    seq = q_ref.shape[0]
    rows_n = seq // w
    kr = NA_ROWS
    first = lax.broadcasted_iota(jnp.int32, (w, LANES), 1) < NA_HEAD_DIM
    scale = NA_HEAD_DIM ** -0.5

    def window_row(r):
        return jnp.clip(r - kr // 2, 0, rows_n - kr)

    def body(it, carry):
        r0 = it * NA_ROWS_PER_STEP
        for u in range(NA_ROWS_PER_STEP):
            r = r0 + u
            rs = window_row(r)
            qr = q_ref[pl.ds(pl.multiple_of(r * w, w), w), :] * scale
            zero = jnp.zeros_like(qr)
            q2 = jnp.concatenate([jnp.where(first, qr, zero), jnp.where(first, zero, qr)], axis=0)
            kk = k_ref[pl.ds(pl.multiple_of(rs * w, w), kr * w), :]
            s_scr[u] = _dot_nt(q2, kk) + bias_ref[rs - r + (NA_ROWS - 1)]
        for u in range(NA_ROWS_PER_STEP):
            m = jnp.max(s_scr[u], axis=-1, keepdims=True)
            p = jnp.exp(s_scr[u] - m)
            l_scr[u] = jnp.broadcast_to(jnp.sum(p, axis=-1, keepdims=True), l_scr.shape[1:])
            p_scr[u] = p.astype(BF16)
        for u in range(NA_ROWS_PER_STEP):
            r = r0 + u
            vv = v_ref[pl.ds(pl.multiple_of(window_row(r) * w, w), kr * w), :]
            o2 = _dot(p_scr[u], vv) / l_scr[u]
            o_ref[pl.ds(pl.multiple_of(r * w, w), w), :] = jnp.where(first, o2[:w], o2[w:]).astype(o_ref.dtype)
        return carry

    lax.fori_loop(0, rows_n // NA_ROWS_PER_STEP, body, 0)


def _neighbourhood_attention(proj3, bias_tab):
    b, s, _ = proj3.shape
    pairs = NA_HEADS // 2
    return pl.pallas_call(
        _na_kernel,
        grid=(b, pairs),
        in_specs=[
            pl.BlockSpec((None, s, LANES), lambda i, p: (i, 0, OFF_NQ // LANES + p)),
            pl.BlockSpec((None, s, LANES), lambda i, p: (i, 0, OFF_NK // LANES + p)),
            pl.BlockSpec((None, s, LANES), lambda i, p: (i, 0, OFF_NV // LANES + p)),
            pl.BlockSpec((None, NA_ROWS, 2 * GRID_W, NA_ROWS * GRID_W), lambda i, p: (p, 0, 0, 0)),
        ],
        out_specs=pl.BlockSpec((None, s, LANES), lambda i, p: (i, 0, p)),
        out_shape=jax.ShapeDtypeStruct((b, s, NA_WIDTH), BF16),
        scratch_shapes=[
            pltpu.VMEM((NA_ROWS_PER_STEP, 2 * GRID_W, NA_ROWS * GRID_W), F32),
            pltpu.VMEM((NA_ROWS_PER_STEP, 2 * GRID_W, NA_ROWS * GRID_W), BF16),
            pltpu.VMEM((NA_ROWS_PER_STEP, 2 * GRID_W, LANES), F32),
        ],
        compiler_params=_params("parallel", "parallel"),
        name="nbr_attn",
    )(proj3, proj3, proj3, bias_tab)


def _xa_kernel(q_ref, mem_ref, g_ref, wkv_ref, o_ref, kv_scr):
    @pl.when(pl.program_id(1) == 0)
    def _():
        mn = _rmsnorm_f32(mem_ref[...], g_ref[...]).astype(BF16)
        kv_scr[...] = _dot(mn, wkv_ref[...]).astype(BF16)

    dh = XA_HEAD_DIM
    scale = dh ** -0.5
    for h in range(XA_HEADS):
        q = q_ref[:, h * dh:(h + 1) * dh]
        k = kv_scr[:, h * dh:(h + 1) * dh]
        v = kv_scr[:, XA_WIDTH + h * dh:XA_WIDTH + (h + 1) * dh]
        s = _dot_nt(q, k) * scale
        m = jnp.max(s, axis=-1, keepdims=True)
        p = jnp.exp(s - m)
        l = jnp.sum(p, axis=-1, keepdims=True)
        o_ref[:, h * dh:(h + 1) * dh] = (_dot(p.astype(BF16), v) / l).astype(o_ref.dtype)


def _memory_attention(proj3, mem, g_mem, wkv_bf16, ts=1024):
    b, s, _ = proj3.shape
    m, d = mem.shape[1], mem.shape[2]
    return pl.pallas_call(
        _xa_kernel,
        grid=(b, s // ts),
        in_specs=[
            pl.BlockSpec((None, ts, XA_WIDTH), lambda i, j: (i, j, OFF_XQ // XA_WIDTH)),
            pl.BlockSpec((None, m, d), lambda i, j: (i, 0, 0)),
            pl.BlockSpec((1, d), lambda i, j: (0, 0)),
            pl.BlockSpec((d, 2 * XA_WIDTH), lambda i, j: (0, 0)),
        ],
        out_specs=pl.BlockSpec((None, ts, XA_WIDTH), lambda i, j: (i, j, 0)),
        out_shape=jax.ShapeDtypeStruct((b, s, XA_WIDTH), BF16),
        scratch_shapes=[pltpu.VMEM((m, 2 * XA_WIDTH), BF16)],
        compiler_params=_params("parallel", "arbitrary"),
        name="mem_attn",
    )(proj3, mem, g_mem, wkv_bf16)


def _masked_lane_max(v, mask):
    return jnp.max(jnp.where(mask, v, -jnp.inf), axis=-1, keepdims=True)


def _first_lane_eq(v, target, mask, lane):
    return jnp.min(jnp.where(mask & (v == target), lane, float(LANES)), axis=-1, keepdims=True)


def _route(logits):
    g_n, e_n = MOE_GROUPS, MOE_EXPERTS_PER_GROUP
    lane = lax.broadcasted_iota(jnp.int32, logits.shape, 1).astype(F32)
    is_grp = lane < g_n
    gmax = _masked_lane_max(logits, is_grp)
    gsum = jnp.sum(jnp.where(is_grp, jnp.exp(logits - gmax), 0.0), axis=-1, keepdims=True)
    grp_w = 1.0 / gsum
    gidx = _first_lane_eq(logits, gmax, is_grp, lane)
    lo = ROUTER_EXPERT_LANE0 + gidx * e_n
    in_grp = (lane >= lo) & (lane < lo + e_n)
    emax = _masked_lane_max(logits, in_grp)
    ex = jnp.where(in_grp, jnp.exp(logits - emax), 0.0)
    prob = ex / jnp.sum(ex, axis=-1, keepdims=True)
    p1 = _masked_lane_max(prob, in_grp)
    i1 = _first_lane_eq(prob, p1, in_grp, lane)
    rest = in_grp & (lane != i1)
    p2 = _masked_lane_max(prob, rest)
    i2 = _first_lane_eq(prob, p2, rest, lane)
    tot = p1 + p2
    w1 = p1 / tot * grp_w
    w2 = p2 / tot * grp_w
    return gidx, jnp.where(lane == i1 - lo, w1, jnp.where(lane == i2 - lo, w2, 0.0))


SORT_TILE = 512
ROW_ALIGN = 16
SORTED_ROWS = 640
WIN_ROWS = SORT_TILE
HALF_WIN = WIN_ROWS // 2
SCR_ROWS = SORTED_ROWS + WIN_ROWS
EXPERT_TILE = 512


def _group_buf_rows(t):
    return t + 2 * WIN_ROWS


def _pad_rows(n):
    return (n + (ROW_ALIGN - 1)) // ROW_ALIGN * ROW_ALIGN


def _window_copies(hs, cs, hbuf, cbuf, sems, slot, kind, g, src_row, dst_row):
    return (
        pltpu.make_async_copy(hs.at[slot, pl.ds(src_row, WIN_ROWS)], hbuf.at[g, pl.ds(dst_row, WIN_ROWS)],
                              sems.at[slot, 2 * kind, g]),
        pltpu.make_async_copy(cs.at[slot, pl.ds(src_row, WIN_ROWS)], cbuf.at[g, pl.ds(dst_row, WIN_ROWS)],
                              sems.at[slot, 2 * kind + 1, g]),
    )


def _merge_kernel(yret_ref, yna_ref, yxa_ref, gr_ref, gn_ref, gx_ref, x_ref, wro_ref, wno_ref, wxo_ref, wout_ref,
                  gffn_ref, wr_ref, br_ref,
                  x2_ref, pos_ref, start_ref, cnt_ref, hbuf, cbuf,
                  hs, cs, run, sems):
    i = pl.program_id(0)
    n_tiles = pl.num_programs(0)
    slot = lax.rem(i, 2)
    tm = x_ref.shape[0]
    g_n = MOE_GROUPS

    @pl.when(i == 0)
    def _():
        for g in range(g_n):
            run[g] = 0
        hs[:, SORTED_ROWS:, :] = jnp.zeros((2, WIN_ROWS, hs.shape[2]), hs.dtype)
        cs[:, SORTED_ROWS:, :] = jnp.zeros((2, WIN_ROWS, cs.shape[2]), cs.dtype)

    y_ret = _dot(yret_ref[...], wro_ref[...])
    y_na = _dot(yna_ref[...], wno_ref[...])
    y_xa = _dot(yxa_ref[...], wxo_ref[...])
    mix = (_sigmoid(gr_ref[...].astype(F32)) * y_ret + _sigmoid(gn_ref[...].astype(F32)) * y_na
           + _sigmoid(gx_ref[...].astype(F32)) * y_xa)
    x2 = x_ref[...] + _dot(mix.astype(BF16), wout_ref[...])
    x2_ref[...] = x2
    h2 = _rmsnorm_f32(x2, gffn_ref[...])
    h_hi = h2.astype(BF16)
    h_lo = (h2 - h_hi.astype(F32)).astype(BF16)
    wr = wr_ref[...]
    wr_hi = wr.astype(BF16)
    wr_lo = (wr - wr_hi.astype(F32)).astype(BF16)
    hi_terms = _dot(h_hi, jnp.concatenate([wr_hi, wr_lo], axis=1))
    logits = hi_terms[:, :LANES] + hi_terms[:, LANES:] + _dot(h_lo, wr_hi) + br_ref[...]
    gidx, w4 = _route(logits)

    lane = lax.broadcasted_iota(jnp.int32, (tm, LANES), 1).astype(F32)
    onehot = jnp.where(lane == gidx, 1.0, 0.0)
    before = (lax.broadcasted_iota(jnp.int32, (tm, tm), 1) < lax.broadcasted_iota(jnp.int32, (tm, tm), 0))
    rank = _dot(jnp.where(before, 1.0, 0.0).astype(BF16), onehot.astype(BF16))
    count_row = rank[tm - 1:tm, :] + onehot[tm - 1:tm, :]
    lane_row = lax.broadcasted_iota(jnp.int32, (1, LANES), 1)
    counts = [jnp.sum(jnp.where(lane_row == g, count_row, 0.0)).astype(jnp.int32) for g in range(g_n)]
    seg_start = []
    acc = jnp.int32(0)
    for g in range(g_n):
        seg_start.append(acc)
        acc = acc + _pad_rows(counts[g])
    start_row = jnp.zeros((1, LANES), F32)
    for g in range(g_n):
        start_row = jnp.where(lane_row == g, seg_start[g].astype(F32), start_row)
    pos = jnp.sum(onehot * (rank + start_row), axis=-1, keepdims=True)
    pos_ref[...] = jnp.broadcast_to(pos, (tm, LANES))

    pos_lanes = jnp.transpose(jnp.broadcast_to(pos, (tm, LANES)))[0:1, :]
    perm = jnp.where(lax.broadcasted_iota(jnp.int32, (SORTED_ROWS, tm), 0).astype(F32) == pos_lanes, 1.0, 0.0)
    perm = perm.astype(BF16)
    hs[slot, 0:SORTED_ROWS, :] = _dot(perm, h_hi).astype(hs.dtype)
    e_n = MOE_EXPERTS_PER_GROUP
    w_hi = w4.astype(BF16).astype(F32)
    r1 = w4 - w_hi
    w_mid = r1.astype(BF16).astype(F32)
    w_lo = r1 - w_mid
    pieces = (w_hi + pltpu.roll(w_mid, e_n, 1) + pltpu.roll(w_lo, 2 * e_n, 1)).astype(BF16)
    sorted_pieces = _dot(perm, pieces)
    cs[slot, 0:SORTED_ROWS, :] = (sorted_pieces + pltpu.roll(sorted_pieces, LANES - e_n, 1)
                                  + pltpu.roll(sorted_pieces, LANES - 2 * e_n, 1))

    copies = functools.partial(_window_copies, hs, cs, hbuf, cbuf, sems)

    def wait_all(which_slot):
        for kind in range(2):
            for g in range(g_n):
                for cp in copies(which_slot, kind, g, 0, 0):
                    cp.wait()

    @pl.when(i > 0)
    def _():
        wait_all(1 - slot)

    @pl.when(i == 0)
    def _():
        top = hbuf.shape[1] - WIN_ROWS
        for g in range(g_n):
            for cp in copies(slot, 1, g, SORTED_ROWS, top):
                cp.start()
        for g in range(g_n):
            for cp in copies(slot, 1, g, 0, 0):
                cp.wait()

    tiles_left = n_tiles - 1 - i
    for g in range(g_n):
        c_g = run[g]
        c_next = c_g + _pad_rows(counts[g])
        start_ref[i, g] = c_g
        cnt_ref[i, g] = counts[g]
        run[g] = c_next
        for cp in copies(slot, 0, g, pl.multiple_of(seg_start[g], ROW_ALIGN), pl.multiple_of(c_g, ROW_ALIGN)):
            cp.start()
        dead = c_next + tiles_left * WIN_ROWS + WIN_ROWS
        for cp in copies(slot, 1, g, SORTED_ROWS, pl.multiple_of(dead, ROW_ALIGN)):
            cp.start()

    @pl.when(i == n_tiles - 1)
    def _():
        wait_all(slot)
        for g in range(g_n):
            for cp in copies(slot, 0, g, SORTED_ROWS, pl.multiple_of(run[g], ROW_ALIGN)):
                cp.start()
        for g in range(g_n):
            for cp in copies(slot, 0, g, 0, 0):
                cp.wait()


def _merge(y_ret, y_na, y_xa, proj, x2d, w_ret_o, w_na_o, w_xa_o, w_out, g_ffn, w_router, b_router):
    t, d = x2d.shape
    tm = SORT_TILE
    n_tiles = t // tm
    rows = _group_buf_rows(t)
    gate_blk = OFF_GATE // d
    full = lambda a: pl.BlockSpec(a.shape, lambda i: (0,) * a.ndim)
    smem = pl.BlockSpec(memory_space=pltpu.SMEM)
    hbm = pl.BlockSpec(memory_space=pl.ANY)
    return pl.pallas_call(
        _merge_kernel,
        grid=(n_tiles,),
        in_specs=[
            pl.BlockSpec((tm, RET_V_WIDTH), lambda i: (i, 0)),
            pl.BlockSpec((tm, NA_WIDTH), lambda i: (i, 0)),
            pl.BlockSpec((tm, XA_WIDTH), lambda i: (i, 0)),
            pl.BlockSpec((tm, d), lambda i: (i, gate_blk)),
            pl.BlockSpec((tm, d), lambda i: (i, gate_blk + 1)),
            pl.BlockSpec((tm, d), lambda i: (i, gate_blk + 2)),
            pl.BlockSpec((tm, d), lambda i: (i, 0)),
            full(w_ret_o), full(w_na_o), full(w_xa_o), full(w_out), full(g_ffn), full(w_router), full(b_router),
        ],
        out_specs=[
            pl.BlockSpec((tm, d), lambda i: (i, 0)),
            pl.BlockSpec((tm, LANES), lambda i: (i, 0)),
            smem, smem, hbm, hbm,
        ],
        out_shape=[
            jax.ShapeDtypeStruct((t, d), F32),
            jax.ShapeDtypeStruct((t, LANES), F32),
            jax.ShapeDtypeStruct((n_tiles, MOE_GROUPS), jnp.int32),
            jax.ShapeDtypeStruct((n_tiles, MOE_GROUPS), jnp.int32),
            jax.ShapeDtypeStruct((MOE_GROUPS, rows, d), BF16),
            jax.ShapeDtypeStruct((MOE_GROUPS, rows, LANES), F32),
        ],
        scratch_shapes=[
            pltpu.VMEM((2, SCR_ROWS, d), BF16),
            pltpu.VMEM((2, SCR_ROWS, LANES), F32),
            pltpu.SMEM((MOE_GROUPS,), jnp.int32),
            pltpu.SemaphoreType.DMA((2, 4, MOE_GROUPS)),
        ],
        compiler_params=_params("arbitrary"),
        name="merge_router",
    )(y_ret, y_na, y_xa, proj, proj, proj, x2d, w_ret_o, w_na_o, w_xa_o, w_out, g_ffn, w_router, b_router)


def _expert_slots(starts, counts, t):
    totals = starts[-1] + _pad_rows(counts[-1])
    nblk = (totals + EXPERT_TILE - 1) // EXPERT_TILE
    ends = jnp.cumsum(nblk)
    first = ends - nblk
    n_slots = (t + (t // SORT_TILE) * MOE_GROUPS * (ROW_ALIGN - 1)) // EXPERT_TILE + MOE_GROUPS + 1
    j = jnp.arange(n_slots, dtype=jnp.int32)
    valid = j < ends[-1]
    jc = jnp.clip(j, 0, jnp.maximum(ends[-1] - 1, 0))
    grp = jnp.sum((jc[:, None] >= ends[None, :]).astype(jnp.int32), axis=1)
    blk = jc - first[grp]
    out_starts = first[None, :] * EXPERT_TILE + starts
    group_first = (valid & (j == first[grp])).astype(jnp.int32)
    nxt = jnp.full((), -1, jnp.int32)
    next_of = []
    for g in reversed(range(MOE_GROUPS)):
        next_of.append(nxt)
        nxt = jnp.where(nblk[g] > 0, g, nxt)
    group_next = jnp.stack(next_of[::-1])[grp]
    return (grp, blk, valid.astype(jnp.int32), group_first, group_next), out_starts


CAST_ROWS = 64


def _experts_kernel(grp_ref, blk_ref, valid_ref, first_ref, next_ref, h_ref, c_ref, wg_hbm, wu_hbm, wd_hbm, o_ref,
                    stage_g, stage_u, stage_d, wg_ref, wu_ref, wd_ref, sems):
    j = pl.program_id(0)

    def weight_copies(g):
        return (pltpu.make_async_copy(wg_hbm.at[g], stage_g, sems.at[0]),
                pltpu.make_async_copy(wu_hbm.at[g], stage_u, sems.at[1]),
                pltpu.make_async_copy(wd_hbm.at[g], stage_d, sems.at[2]))

    @pl.when(j == 0)
    def _():
        for cp in weight_copies(grp_ref[0]):
            cp.start()

    @pl.when(first_ref[j] == 1)
    def _():
        for cp in weight_copies(grp_ref[j]):
            cp.wait()
        for stage, dst in ((stage_g, wg_ref), (stage_u, wu_ref), (stage_d, wd_ref)):
            for e in range(MOE_EXPERTS_PER_GROUP):
                def convert(i, carry, stage=stage, dst=dst, e=e):
                    rows = pl.ds(pl.multiple_of(i * CAST_ROWS, CAST_ROWS), CAST_ROWS)
                    dst[e, rows, :] = stage[e, rows, :].astype(dst.dtype)
                    return carry

                lax.fori_loop(0, stage.shape[1] // CAST_ROWS, convert, 0)

        @pl.when(next_ref[j] >= 0)
        def _():
            for cp in weight_copies(next_ref[j]):
                cp.start()

    @pl.when(valid_ref[j] == 1)
    def _():
        h = h_ref[...]
        c = c_ref[...]
        lane = lax.broadcasted_iota(jnp.int32, c.shape, 1)
        out = None
        for e in range(MOE_EXPERTS_PER_GROUP):
            a = _dot(h, wg_ref[e])
            u = _dot(h, wu_ref[e])
            cw = jnp.sum(jnp.where(lane == e, c, 0.0), axis=-1, keepdims=True)
            hid = (_silu(a) * u * cw).astype(BF16)
            part = _dot(hid, wd_ref[e])
            out = part if out is None else out + part
        o_ref[...] = out.astype(o_ref.dtype)

    @pl.when(valid_ref[j] == 0)
    def _():
        o_ref[...] = jnp.zeros_like(o_ref)


def _experts(hbuf, cbuf, wg, wu, wd, slots):
    g_n, rows, d = hbuf.shape
    e_n, f = wg.shape[1], wg.shape[3]
    tm = EXPERT_TILE
    n_slots = slots[0].shape[0]
    hbm = pl.BlockSpec(memory_space=pl.ANY)
    grid_spec = pltpu.PrefetchScalarGridSpec(
        num_scalar_prefetch=5,
        grid=(n_slots,),
        in_specs=[
            pl.BlockSpec((None, tm, d), lambda j, grp, blk, v, gf, gx: (grp[j], blk[j], 0)),
            pl.BlockSpec((None, tm, LANES), lambda j, grp, blk, v, gf, gx: (grp[j], blk[j], 0)),
            hbm, hbm, hbm,
        ],
        out_specs=pl.BlockSpec((tm, d), lambda j, grp, blk, v, gf, gx: (j, 0)),
        scratch_shapes=[
            pltpu.VMEM((e_n, d, f), wg.dtype),
            pltpu.VMEM((e_n, d, f), wu.dtype),
            pltpu.VMEM((e_n, f, d), wd.dtype),
            pltpu.VMEM((e_n, d, f), BF16),
            pltpu.VMEM((e_n, d, f), BF16),
            pltpu.VMEM((e_n, f, d), BF16),
            pltpu.SemaphoreType.DMA((3,)),
        ],
    )
    return pl.pallas_call(
        _experts_kernel,
        grid_spec=grid_spec,
        out_shape=jax.ShapeDtypeStruct((n_slots * tm, d), BF16),
        compiler_params=pltpu.CompilerParams(dimension_semantics=("arbitrary",),
                                             vmem_limit_bytes=EXPERTS_VMEM_LIMIT),
        name="experts",
    )(*slots, hbuf, cbuf, wg, wu, wd)


def _finish_kernel(start_ref, cnt_ref, *refs, final_norm):
    wins = refs[:2 * MOE_GROUPS]
    pos_ref, x2_ref, gfin_ref, o_ref, sorted_scr = refs[2 * MOE_GROUPS:]
    i = pl.program_id(0)
    tm = x2_ref.shape[0]
    half_pieces = HALF_WIN // ROW_ALIGN

    @pl.when(i == 0)
    def _():
        sorted_scr[...] = jnp.zeros_like(sorted_scr)

    seg = jnp.int32(0)
    for g in range(MOE_GROUPS):
        padded = _pad_rows(cnt_ref[i, g])
        pieces = padded // ROW_ALIGN
        for half, win in enumerate(wins[2 * g:2 * g + 2]):
            base = seg + half * HALF_WIN

            def copy_piece(k, carry, win=win, base=base):
                src = pl.multiple_of(k * ROW_ALIGN, ROW_ALIGN)
                dst = pl.multiple_of(base + k * ROW_ALIGN, ROW_ALIGN)
                sorted_scr[pl.ds(dst, ROW_ALIGN), :] = win[pl.ds(src, ROW_ALIGN), :]
                return carry

            lax.fori_loop(0, jnp.clip(pieces - half * half_pieces, 0, half_pieces), copy_piece, 0)
        seg = seg + padded

    pos = pos_ref[:, 0:1]
    unperm = jnp.where(lax.broadcasted_iota(jnp.int32, (tm, SORTED_ROWS), 1).astype(F32) == pos, 1.0, 0.0)
    y = x2_ref[...] + _dot(unperm.astype(BF16), sorted_scr[...])
    if final_norm:
        y = _rmsnorm_f32(y, gfin_ref[...])
    o_ref[...] = y


def _finish(mbuf, starts, counts, pos, x2, g_final, final_norm):
    t, d = x2.shape
    tm = SORT_TILE

    def window(g, half):
        def index(i, st, ct):
            row = st[i, g] + half * HALF_WIN
            if half:
                row = jnp.where(_pad_rows(ct[i, g]) > HALF_WIN, row, 0)
            return pl.multiple_of(row, ROW_ALIGN), 0

        return pl.BlockSpec((pl.Element(HALF_WIN), pl.Element(d)), index)

    grid_spec = pltpu.PrefetchScalarGridSpec(
        num_scalar_prefetch=2,
        grid=(t // tm,),
        in_specs=[
            *[window(g, half) for g in range(MOE_GROUPS) for half in range(2)],
            pl.BlockSpec((tm, LANES), lambda i, st, ct: (i, 0)),
            pl.BlockSpec((tm, d), lambda i, st, ct: (i, 0)),
            pl.BlockSpec((1, d), lambda i, st, ct: (0, 0)),
        ],
        out_specs=pl.BlockSpec((tm, d), lambda i, st, ct: (i, 0)),
        scratch_shapes=[pltpu.VMEM((SORTED_ROWS, d), BF16)],
    )
    return pl.pallas_call(
        functools.partial(_finish_kernel, final_norm=final_norm),
        grid_spec=grid_spec,
        out_shape=jax.ShapeDtypeStruct((t, d), F32),
        compiler_params=_params("arbitrary"),
        name="finish",
    )(starts, counts, *([mbuf] * (2 * MOE_GROUPS)), pos, x2, g_final)


def kernel(x, mem, g_mix, w_in, ret_decay_fwd, ret_decay_bwd, ret_norm_gain, w_ret_o, na_rpb, w_na_o, g_mem, w_mem_kv, w_xa_o, w_out, g_ffn, w_router_group, b_router_group, w_router_expert, b_router_expert, w_exp_gate, w_exp_up, w_exp_down, g_final):
    b, s, d = x.shape
    depth = w_in.shape[0]
    t = b * s
    cos_t, sin_t = _rope_tables(s)
    row = lambda v: v.reshape(1, -1).astype(F32)
    x2d = x.reshape(t, d)
    for l in range(depth):
        proj = _inproj(x2d, row(g_mix[l]), w_in[l].astype(BF16))
        proj3 = proj.reshape(b, s, IN_WIDTH)
        y_ret = _retention(proj3, ret_decay_fwd[l].astype(F32), ret_decay_bwd[l].astype(F32), cos_t, sin_t,
                           row(ret_norm_gain[l]))
        y_na = _neighbourhood_attention(proj3, _na_bias_table(na_rpb[l]))
        y_xa = _memory_attention(proj3, mem, row(g_mem[l]), w_mem_kv[l].astype(BF16))
        n_r = MOE_GROUPS + N_EXPERTS
        w_router = jnp.pad(jnp.concatenate([w_router_group[l], w_router_expert[l]], axis=1).astype(F32),
                           ((0, 0), (0, LANES - n_r)))
        b_router = jnp.pad(jnp.concatenate([b_router_group[l], b_router_expert[l]]).astype(F32),
                           (0, LANES - n_r)).reshape(1, LANES)
        x2, pos, starts, counts, hbuf, cbuf = _merge(
            y_ret.reshape(t, -1), y_na.reshape(t, -1), y_xa.reshape(t, -1), proj, x2d,
            w_ret_o[l].astype(BF16), w_na_o[l].astype(BF16), w_xa_o[l].astype(BF16),
            w_out[l].astype(BF16), row(g_ffn[l]), w_router, b_router)
        slots, out_starts = _expert_slots(starts, counts, t)
        mbuf = _experts(hbuf, cbuf, w_exp_gate[l], w_exp_up[l], w_exp_down[l], slots)
        x2d = _finish(mbuf, out_starts, counts, pos, x2, row(g_final), final_norm=(l == depth - 1))
    return x2d.reshape(b, s, d)
```

```python
import functools

import jax
import jax.numpy as jnp
import numpy as np
from jax import lax
from jax.experimental import pallas as pl
from jax.experimental.pallas import tpu as pltpu

D_MODEL = 1024
GRID_W = 64
N_BRANCHES = 3
RET_HEADS = 4
RET_QK_DIM = 128
RET_V_DIM = 256
RET_CHUNK = 128
ROPE_BASE = 10000.0
NA_HEADS = 8
NA_HEAD_DIM = 64
NA_ROWS = 8
NA_COLS = 16
XA_HEADS = 4
XA_HEAD_DIM = 128
MOE_GROUPS = 4
MOE_EXPERTS_PER_GROUP = 4
MOE_TOP_K = 2
MOE_D_FF = 512
RMS_EPS = 1e-6
GN_EPS = 1e-5

RET_QK_WIDTH = RET_HEADS * RET_QK_DIM
RET_V_WIDTH = RET_HEADS * RET_V_DIM
NA_WIDTH = NA_HEADS * NA_HEAD_DIM
XA_WIDTH = XA_HEADS * XA_HEAD_DIM
IN_WIDTH = 2 * RET_QK_WIDTH + 2 * RET_V_WIDTH + 3 * NA_WIDTH + XA_WIDTH + N_BRANCHES * D_MODEL

OFF_RQ = 0
OFF_RK = OFF_RQ + RET_QK_WIDTH
OFF_RV = OFF_RK + RET_QK_WIDTH
OFF_RG = OFF_RV + RET_V_WIDTH
OFF_NQ = OFF_RG + RET_V_WIDTH
OFF_NK = OFF_NQ + NA_WIDTH
OFF_NV = OFF_NK + NA_WIDTH
OFF_XQ = OFF_NV + NA_WIDTH
OFF_GATE = OFF_XQ + XA_WIDTH

N_EXPERTS = MOE_GROUPS * MOE_EXPERTS_PER_GROUP
LANES = 128
ROUTER_EXPERT_LANE0 = MOE_GROUPS

VMEM_LIMIT = 48 * 1024 * 1024
EXPERTS_VMEM_LIMIT = 56 * 1024 * 1024

F32 = jnp.float32
BF16 = jnp.bfloat16


def _params(*sem):
    return pltpu.CompilerParams(dimension_semantics=sem, vmem_limit_bytes=VMEM_LIMIT)


def _rmsnorm_f32(x, g):
    return x * lax.rsqrt(jnp.mean(x * x, axis=-1, keepdims=True) + RMS_EPS) * g


def _sigmoid(x):
    return 0.5 * jnp.tanh(0.5 * x) + 0.5


def _silu(x):
    h = 0.5 * x
    return h + h * jnp.tanh(h)


def _dot(a, b):
    return jnp.dot(a, b, preferred_element_type=F32)


def _dot_tn(a, b):
    return lax.dot_general(a, b, (((0,), (0,)), ((), ())), preferred_element_type=F32)


def _dot_nt(a, b):
    return lax.dot_general(a, b, (((1,), (1,)), ((), ())), preferred_element_type=F32)


def _inproj_kernel(x_ref, g_ref, w_ref, o_ref, h_scr):
    @pl.when(pl.program_id(1) == 0)
    def _():
        h_scr[...] = _rmsnorm_f32(x_ref[...], g_ref[...]).astype(BF16)

    o_ref[...] = _dot(h_scr[...], w_ref[...]).astype(o_ref.dtype)


def _inproj(x2d, g, w_bf16, tm=1024, tn=2048):
    t, d = x2d.shape
    n = w_bf16.shape[1]
    return pl.pallas_call(
        _inproj_kernel,
        grid=(t // tm, n // tn),
        in_specs=[
            pl.BlockSpec((tm, d), lambda i, j: (i, 0)),
            pl.BlockSpec((1, d), lambda i, j: (0, 0)),
            pl.BlockSpec((d, tn), lambda i, j: (0, j)),
        ],
        out_specs=pl.BlockSpec((tm, tn), lambda i, j: (i, j)),
        out_shape=jax.ShapeDtypeStruct((t, n), BF16),
        scratch_shapes=[pltpu.VMEM((tm, d), BF16)],
        compiler_params=_params("parallel", "arbitrary"),
        name="inproj",
    )(x2d, g, w_bf16)


def _log_sigmoid(x):
    return jnp.minimum(x, 0.0) - jnp.log1p(jnp.exp(-jnp.abs(x)))


RET_HEADS_PER_STEP = 2


def _retention_kernel(decf_ref, decb_ref, q_ref, k_ref, v_ref, rg_ref, cos_ref, sin_ref, gn_ref, o_ref, *scratch):
    qk, dv = RET_QK_DIM, RET_V_DIM
    for hh in range(RET_HEADS_PER_STEP):
        qcols = slice(hh * qk, (hh + 1) * qk)
        vcols = slice(hh * dv, (hh + 1) * dv)
        _retention_head(pl.program_id(1) * RET_HEADS_PER_STEP + hh, decf_ref, decb_ref,
                        q_ref.at[:, qcols], k_ref.at[:, qcols], v_ref.at[:, vcols], rg_ref.at[:, vcols],
                        cos_ref, sin_ref, gn_ref.at[:, vcols], o_ref.at[:, vcols], *[s.at[hh] for s in scratch])


def _retention_head(h, decf_ref, decb_ref, q_ref, k_ref, v_ref, rg_ref, cos_ref, sin_ref, gn_ref, o_ref,
                    qfs, qbs, y_scr, kvf_scr, kvb_scr, st_scr):
    c = RET_CHUNK
    seq = q_ref.shape[0]
    n_chunks = seq // c
    half = RET_QK_DIM // 2

    lgf = _log_sigmoid(jnp.full((1, 1), decf_ref[h], F32))
    lgb = _log_sigmoid(jnp.full((1, 1), decb_ref[h], F32))

    ii = lax.broadcasted_iota(jnp.int32, (c, c), 0)
    jj = lax.broadcasted_iota(jnp.int32, (c, c), 1)
    diff = (ii - jj).astype(F32)
    dmat = jnp.exp(jnp.where(diff >= 0, diff * lgf, (-diff) * lgb))
    pos = lax.broadcasted_iota(jnp.int32, (c, 1), 0).astype(F32)
    qd_f = jnp.exp((pos + 1.0) * lgf)
    kd_f = jnp.exp((c - 1.0 - pos) * lgf)
    cd_f = jnp.exp(c * lgf)
    qd_b = jnp.exp((c - pos) * lgb)
    kd_b = jnp.exp(pos * lgb)
    cd_b = jnp.exp(c * lgb)

    def intra(n):
        rows = pl.ds(n * c, c)
        cos = cos_ref[rows, :]
        sin = sin_ref[rows, :]
        q = q_ref[rows, :].astype(F32)
        q = (q * cos + pltpu.roll(q, half, 1) * sin) * (RET_QK_DIM ** -0.5)
        qfs[rows, :] = (q * qd_f).astype(BF16)
        qbs[rows, :] = (q * qd_b).astype(BF16)
        k = k_ref[rows, :].astype(F32)
        k = k * cos + pltpu.roll(k, half, 1) * sin
        s = _dot_nt(q.astype(BF16), k.astype(BF16)) * dmat
        lhs = jnp.concatenate([s.astype(BF16), (k * kd_f).T.astype(BF16), (k * kd_b).T.astype(BF16)], axis=0)
        r = _dot(lhs, v_ref[rows, :])
        y_scr[rows, :] = r[:c]
        kvf_scr[n] = r[c:2 * c]
        kvb_scr[n] = r[2 * c:]

    for n in range(n_chunks):
        intra(n)

    st_scr[...] = jnp.zeros_like(st_scr)

    def fwd(n):
        rows = pl.ds(n * c, c)
        st = st_scr[...]
        y_scr[rows, :] += _dot(qfs[rows, :], st.astype(BF16))
        st_scr[...] = st * cd_f + kvf_scr[n]

    for n in range(n_chunks):
        fwd(n)

    st_scr[...] = jnp.zeros_like(st_scr)
    gn = gn_ref[...]

    def bwd(n):
        rows = pl.ds(n * c, c)
        st = st_scr[...]
        y = y_scr[rows, :] + _dot(qbs[rows, :], st.astype(BF16))
        st_scr[...] = st * cd_b + kvb_scr[n]
        mu = jnp.mean(y, axis=-1, keepdims=True)
        yc = y - mu
        var = jnp.mean(yc * yc, axis=-1, keepdims=True)
        yn = yc * lax.rsqrt(var + GN_EPS) * gn
        o_ref[rows, :] = _silu(rg_ref[rows, :]) * yn.astype(o_ref.dtype)

    for n in reversed(range(n_chunks)):
        bwd(n)


def _retention(proj3, dec_f, dec_b, cos_t, sin_t, gn_gain):
    b, s, _ = proj3.shape
    hps = RET_HEADS_PER_STEP
    dk, dv = RET_QK_DIM, RET_V_DIM
    qb, vb = hps * dk, hps * dv
    smem = pl.BlockSpec(memory_space=pltpu.SMEM)
    return pl.pallas_call(
        _retention_kernel,
        grid=(b, RET_HEADS // hps),
        in_specs=[
            smem,
            smem,
            pl.BlockSpec((None, s, qb), lambda i, h: (i, 0, OFF_RQ // qb + h)),
            pl.BlockSpec((None, s, qb), lambda i, h: (i, 0, OFF_RK // qb + h)),
            pl.BlockSpec((None, s, vb), lambda i, h: (i, 0, OFF_RV // vb + h)),
            pl.BlockSpec((None, s, vb), lambda i, h: (i, 0, OFF_RG // vb + h)),
            pl.BlockSpec((s, dk), lambda i, h: (0, 0)),
            pl.BlockSpec((s, dk), lambda i, h: (0, 0)),
            pl.BlockSpec((1, vb), lambda i, h: (0, h)),
        ],
        out_specs=pl.BlockSpec((None, s, vb), lambda i, h: (i, 0, h)),
        out_shape=jax.ShapeDtypeStruct((b, s, RET_V_WIDTH), BF16),
        scratch_shapes=[
            pltpu.VMEM((hps, s, dk), BF16),
            pltpu.VMEM((hps, s, dk), BF16),
            pltpu.VMEM((hps, s, dv), F32),
            pltpu.VMEM((hps, s // RET_CHUNK, dk, dv), F32),
            pltpu.VMEM((hps, s // RET_CHUNK, dk, dv), F32),
            pltpu.VMEM((hps, dk, dv), F32),
        ],
        compiler_params=_params("parallel", "parallel"),
        name="retention",
    )(dec_f, dec_b, proj3, proj3, proj3, proj3, cos_t, sin_t, gn_gain)


def _rope_tables(seq):
    half = RET_QK_DIM // 2
    inv_freq = ROPE_BASE ** (-jnp.arange(half, dtype=F32) / half)
    ang = jnp.arange(seq, dtype=F32)[:, None] * inv_freq[None, :]
    cos, sin = jnp.cos(ang), jnp.sin(ang)
    return jnp.concatenate([cos, cos], axis=-1), jnp.concatenate([-sin, sin], axis=-1)


def _na_bias_table(rpb):
    heads = rpb.shape[0]
    w = GRID_W
    cols = np.arange(w)
    col_start = np.clip(cols - NA_COLS // 2, 0, w - NA_COLS)
    col_off = cols[None, :] - col_start[:, None]
    col_mask = (col_off >= 0) & (col_off < NA_COLS)
    rel_c = np.clip(cols[None, :] - cols[:, None], -(NA_COLS - 1), NA_COLS - 1) + (NA_COLS - 1)
    onehot = (rel_c[:, :, None] == np.arange(2 * NA_COLS - 1)).astype(np.float32)
    toe = jnp.einsum('hrc,qkc->hrqk', rpb.astype(F32), onehot, precision=lax.Precision.HIGHEST)
    toe = jnp.where(col_mask[None, None], toe, -jnp.inf)
    two = jnp.concatenate([toe[:, :-1], toe[:, 1:]], axis=-1)
    two = two.reshape(heads // 2, 2, 2 * NA_ROWS - 2, w, 2 * w)
    return two.transpose(0, 2, 1, 3, 4).reshape(heads // 2, 2 * NA_ROWS - 2, 2 * w, 2 * w)


NA_ROWS_PER_STEP = 8


def _na_kernel(q_ref, k_ref, v_ref, bias_ref, o_ref, s_scr, p_scr):
    w = GRID_W
    seq = q_ref.shape[0]
    rows_n = seq // w
    kr = NA_ROWS
    first = lax.broadcasted_iota(jnp.int32, (w, LANES), 1) < NA_HEAD_DIM
    scale = NA_HEAD_DIM ** -0.5

    def window_row(r):
        return jnp.clip(r - kr // 2, 0, rows_n - kr)

    def body(it, carry):
        r0 = it * NA_ROWS_PER_STEP
        for u in range(NA_ROWS_PER_STEP):
            r = r0 + u
            rs = window_row(r)
            qr = q_ref[pl.ds(pl.multiple_of(r * w, w), w), :] * scale
            zero = jnp.zeros_like(qr)
            q2 = jnp.concatenate([jnp.where(first, qr, zero), jnp.where(first, zero, qr)], axis=0)
            kk = k_ref[pl.ds(pl.multiple_of(rs * w, w), kr * w), :]
            off = rs - r + (NA_ROWS - 1)
            bias = jnp.concatenate([bias_ref[off + 2 * i] for i in range(kr // 2)], axis=1)
            s_scr[u] = _dot_nt(q2, kk) + bias
        for u in range(NA_ROWS_PER_STEP):
            m = jnp.max(s_scr[u], axis=-1, keepdims=True)
            p_scr[u] = jnp.exp(s_scr[u] - m).astype(BF16)
        for u in range(NA_ROWS_PER_STEP):
            r = r0 + u
            vv = v_ref[pl.ds(pl.multiple_of(window_row(r) * w, w), kr * w), :]
            o2 = _dot(p_scr[u], jnp.concatenate([vv, jnp.ones_like(vv)], axis=1))
            o2 = o2[:, :LANES] / o2[:, LANES:]
            o_ref[pl.ds(pl.multiple_of(r * w, w), w), :] = jnp.where(first, o2[:w], o2[w:]).astype(o_ref.dtype)
        return carry

    lax.fori_loop(0, rows_n // NA_ROWS_PER_STEP, body, 0)


def _neighbourhood_attention(proj3, bias_tab):
    b, s, _ = proj3.shape
    pairs = NA_HEADS // 2
    return pl.pallas_call(
        _na_kernel,
        grid=(b, pairs),
        in_specs=[
            pl.BlockSpec((None, s, LANES), lambda i, p: (i, 0, OFF_NQ // LANES + p)),
            pl.BlockSpec((None, s, LANES), lambda i, p: (i, 0, OFF_NK // LANES + p)),
            pl.BlockSpec((None, s, LANES), lambda i, p: (i, 0, OFF_NV // LANES + p)),
            pl.BlockSpec((None, 2 * NA_ROWS - 2, 2 * GRID_W, 2 * GRID_W), lambda i, p: (p, 0, 0, 0)),
        ],
        out_specs=pl.BlockSpec((None, s, LANES), lambda i, p: (i, 0, p)),
        out_shape=jax.ShapeDtypeStruct((b, s, NA_WIDTH), BF16),
        scratch_shapes=[
            pltpu.VMEM((NA_ROWS_PER_STEP, 2 * GRID_W, NA_ROWS * GRID_W), F32),
            pltpu.VMEM((NA_ROWS_PER_STEP, 2 * GRID_W, NA_ROWS * GRID_W), BF16),
        ],
        compiler_params=_params("parallel", "parallel"),
        name="nbr_attn",
    )(proj3, proj3, proj3, bias_tab)


def _xa_kernel(q_ref, mem_ref, g_ref, wkv_ref, o_ref, kv_scr):
    @pl.when(pl.program_id(1) == 0)
    def _():
        mn = _rmsnorm_f32(mem_ref[...], g_ref[...]).astype(BF16)
        kv_scr[...] = _dot(mn, wkv_ref[...]).astype(BF16)

    dh = XA_HEAD_DIM
    scale = dh ** -0.5
    for h in range(XA_HEADS):
        q = q_ref[:, h * dh:(h + 1) * dh]
        k = kv_scr[:, h * dh:(h + 1) * dh]
        v = kv_scr[:, XA_WIDTH + h * dh:XA_WIDTH + (h + 1) * dh]
        s = _dot_nt(q, k) * scale
        m = jnp.max(s, axis=-1, keepdims=True)
        p = jnp.exp(s - m).astype(BF16)
        o2 = _dot(p, jnp.concatenate([v, jnp.ones_like(v)], axis=1))
        o_ref[:, h * dh:(h + 1) * dh] = (o2[:, :dh] / o2[:, dh:]).astype(o_ref.dtype)


def _memory_attention(proj3, mem, g_mem, wkv_bf16, ts=1024):
    b, s, _ = proj3.shape
    m, d = mem.shape[1], mem.shape[2]
    return pl.pallas_call(
        _xa_kernel,
        grid=(b, s // ts),
        in_specs=[
            pl.BlockSpec((None, ts, XA_WIDTH), lambda i, j: (i, j, OFF_XQ // XA_WIDTH)),
            pl.BlockSpec((None, m, d), lambda i, j: (i, 0, 0)),
            pl.BlockSpec((1, d), lambda i, j: (0, 0)),
            pl.BlockSpec((d, 2 * XA_WIDTH), lambda i, j: (0, 0)),
        ],
        out_specs=pl.BlockSpec((None, ts, XA_WIDTH), lambda i, j: (i, j, 0)),
        out_shape=jax.ShapeDtypeStruct((b, s, XA_WIDTH), BF16),
        scratch_shapes=[pltpu.VMEM((m, 2 * XA_WIDTH), BF16)],
        compiler_params=_params("parallel", "arbitrary"),
        name="mem_attn",
    )(proj3, mem, g_mem, wkv_bf16)


def _masked_lane_max(v, mask):
    return jnp.max(jnp.where(mask, v, -jnp.inf), axis=-1, keepdims=True)


def _first_lane_eq(v, target, mask, lane):
    return jnp.min(jnp.where(mask & (v == target), lane, float(LANES)), axis=-1, keepdims=True)


def _route(logits):
    g_n, e_n = MOE_GROUPS, MOE_EXPERTS_PER_GROUP
    lane = lax.broadcasted_iota(jnp.int32, logits.shape, 1).astype(F32)
    is_grp = lane < g_n
    gmax = _masked_lane_max(logits, is_grp)
    gsum = jnp.sum(jnp.where(is_grp, jnp.exp(logits - gmax), 0.0), axis=-1, keepdims=True)
    grp_w = 1.0 / gsum
    gidx = _first_lane_eq(logits, gmax, is_grp, lane)
    lo = ROUTER_EXPERT_LANE0 + gidx * e_n
    in_grp = (lane >= lo) & (lane < lo + e_n)
    emax = _masked_lane_max(logits, in_grp)
    ex = jnp.where(in_grp, jnp.exp(logits - emax), 0.0)
    prob = ex / jnp.sum(ex, axis=-1, keepdims=True)
    p1 = _masked_lane_max(prob, in_grp)
    i1 = _first_lane_eq(prob, p1, in_grp, lane)
    rest = in_grp & (lane != i1)
    p2 = _masked_lane_max(prob, rest)
    i2 = _first_lane_eq(prob, p2, rest, lane)
    tot = p1 + p2
    w1 = p1 / tot * grp_w
    w2 = p2 / tot * grp_w
    return gidx, jnp.where(lane == i1 - lo, w1, jnp.where(lane == i2 - lo, w2, 0.0))


SORT_TILE = 512
ROW_ALIGN = 16
SORTED_ROWS = 640
WIN_ROWS = SORT_TILE
HALF_WIN = WIN_ROWS // 2
SCR_ROWS = SORTED_ROWS + WIN_ROWS
EXPERT_TILE = 512


def _group_buf_rows(t):
    return t + 2 * WIN_ROWS


def _pad_rows(n):
    return (n + (ROW_ALIGN - 1)) // ROW_ALIGN * ROW_ALIGN


def _window_copies(hs, cs, hbuf, cbuf, sems, slot, kind, g, src_row, dst_row):
    return (
        pltpu.make_async_copy(hs.at[slot, pl.ds(src_row, WIN_ROWS)], hbuf.at[g, pl.ds(dst_row, WIN_ROWS)],
                              sems.at[slot, 2 * kind, g]),
        pltpu.make_async_copy(cs.at[slot, pl.ds(src_row, WIN_ROWS)], cbuf.at[g, pl.ds(dst_row, WIN_ROWS)],
                              sems.at[slot, 2 * kind + 1, g]),
    )


def _merge_kernel(yret_ref, yna_ref, yxa_ref, gr_ref, gn_ref, gx_ref, x_ref, wro_ref, wno_ref, wxo_ref, wout_ref,
                  gffn_ref, wr_ref, br_ref,
                  x2_ref, pos_ref, start_ref, cnt_ref, hbuf, cbuf,
                  hs, cs, run, sems):
    i = pl.program_id(0)
    n_tiles = pl.num_programs(0)
    slot = lax.rem(i, 2)
    tm = x_ref.shape[0]
    g_n = MOE_GROUPS

    @pl.when(i == 0)
    def _():
        for g in range(g_n):
            run[g] = 0
        hs[:, SORTED_ROWS:, :] = jnp.zeros((2, WIN_ROWS, hs.shape[2]), hs.dtype)
        cs[:, SORTED_ROWS:, :] = jnp.zeros((2, WIN_ROWS, cs.shape[2]), cs.dtype)

    y_ret = _dot(yret_ref[...], wro_ref[...])
    y_na = _dot(yna_ref[...], wno_ref[...])
    y_xa = _dot(yxa_ref[...], wxo_ref[...])
    mix = (_sigmoid(gr_ref[...].astype(F32)) * y_ret + _sigmoid(gn_ref[...].astype(F32)) * y_na
           + _sigmoid(gx_ref[...].astype(F32)) * y_xa)
    x2 = x_ref[...] + _dot(mix.astype(BF16), wout_ref[...])
    x2_ref[...] = x2
    h2 = _rmsnorm_f32(x2, gffn_ref[...])
    h_hi = h2.astype(BF16)
    h_lo = (h2 - h_hi.astype(F32)).astype(BF16)
    wr = wr_ref[...]
    wr_hi = wr.astype(BF16)
    wr_lo = (wr - wr_hi.astype(F32)).astype(BF16)
    hi_terms = _dot(h_hi, jnp.concatenate([wr_hi, wr_lo], axis=1))
    logits = hi_terms[:, :LANES] + hi_terms[:, LANES:] + _dot(h_lo, wr_hi) + br_ref[...]
    gidx, w4 = _route(logits)

    lane = lax.broadcasted_iota(jnp.int32, (tm, LANES), 1).astype(F32)
    onehot = jnp.where(lane == gidx, 1.0, 0.0)
    before = (lax.broadcasted_iota(jnp.int32, (tm, tm), 1) < lax.broadcasted_iota(jnp.int32, (tm, tm), 0))
    rank = _dot(jnp.where(before, 1.0, 0.0).astype(BF16), onehot.astype(BF16))
    count_row = rank[tm - 1:tm, :] + onehot[tm - 1:tm, :]
    lane_row = lax.broadcasted_iota(jnp.int32, (1, LANES), 1)
    counts = [jnp.sum(jnp.where(lane_row == g, count_row, 0.0)).astype(jnp.int32) for g in range(g_n)]
    seg_start = []
    acc = jnp.int32(0)
    for g in range(g_n):
        seg_start.append(acc)
        acc = acc + _pad_rows(counts[g])
    start_row = jnp.zeros((1, LANES), F32)
    for g in range(g_n):
        start_row = jnp.where(lane_row == g, seg_start[g].astype(F32), start_row)
    pos = jnp.sum(onehot * (rank + start_row), axis=-1, keepdims=True)
    pos_ref[...] = jnp.broadcast_to(pos, (tm, LANES))

    pos_lanes = jnp.transpose(jnp.broadcast_to(pos, (tm, LANES)))[0:1, :]
    perm = jnp.where(lax.broadcasted_iota(jnp.int32, (SORTED_ROWS, tm), 0).astype(F32) == pos_lanes, 1.0, 0.0)
    perm = perm.astype(BF16)
    hs[slot, 0:SORTED_ROWS, :] = _dot(perm, h_hi).astype(hs.dtype)
    e_n = MOE_EXPERTS_PER_GROUP
    w_hi = w4.astype(BF16).astype(F32)
    r1 = w4 - w_hi
    w_mid = r1.astype(BF16).astype(F32)
    w_lo = r1 - w_mid
    pieces = (w_hi + pltpu.roll(w_mid, e_n, 1) + pltpu.roll(w_lo, 2 * e_n, 1)).astype(BF16)
    sorted_pieces = _dot(perm, pieces)
    cs[slot, 0:SORTED_ROWS, :] = (sorted_pieces + pltpu.roll(sorted_pieces, LANES - e_n, 1)
                                  + pltpu.roll(sorted_pieces, LANES - 2 * e_n, 1))

    copies = functools.partial(_window_copies, hs, cs, hbuf, cbuf, sems)

    def wait_all(which_slot):
        for kind in range(2):
            for g in range(g_n):
                for cp in copies(which_slot, kind, g, 0, 0):
                    cp.wait()

    @pl.when(i > 0)
    def _():
        wait_all(1 - slot)

    @pl.when(i == 0)
    def _():
        top = hbuf.shape[1] - WIN_ROWS
        for g in range(g_n):
            for cp in copies(slot, 1, g, SORTED_ROWS, top):
                cp.start()
        for g in range(g_n):
            for cp in copies(slot, 1, g, 0, 0):
                cp.wait()

    tiles_left = n_tiles - 1 - i
    for g in range(g_n):
        c_g = run[g]
        c_next = c_g + _pad_rows(counts[g])
        start_ref[i, g] = c_g
        cnt_ref[i, g] = counts[g]
        run[g] = c_next
        for cp in copies(slot, 0, g, pl.multiple_of(seg_start[g], ROW_ALIGN), pl.multiple_of(c_g, ROW_ALIGN)):
            cp.start()
        dead = c_next + tiles_left * WIN_ROWS + WIN_ROWS
        for cp in copies(slot, 1, g, SORTED_ROWS, pl.multiple_of(dead, ROW_ALIGN)):
            cp.start()

    @pl.when(i == n_tiles - 1)
    def _():
        wait_all(slot)
        for g in range(g_n):
            for cp in copies(slot, 0, g, SORTED_ROWS, pl.multiple_of(run[g], ROW_ALIGN)):
                cp.start()
        for g in range(g_n):
            for cp in copies(slot, 0, g, 0, 0):
                cp.wait()


def _merge(y_ret, y_na, y_xa, proj, x2d, w_ret_o, w_na_o, w_xa_o, w_out, g_ffn, w_router, b_router):
    t, d = x2d.shape
    tm = SORT_TILE
    n_tiles = t // tm
    rows = _group_buf_rows(t)
    gate_blk = OFF_GATE // d
    full = lambda a: pl.BlockSpec(a.shape, lambda i: (0,) * a.ndim)
    smem = pl.BlockSpec(memory_space=pltpu.SMEM)
    hbm = pl.BlockSpec(memory_space=pl.ANY)
    return pl.pallas_call(
        _merge_kernel,
        grid=(n_tiles,),
        in_specs=[
            pl.BlockSpec((tm, RET_V_WIDTH), lambda i: (i, 0)),
            pl.BlockSpec((tm, NA_WIDTH), lambda i: (i, 0)),
            pl.BlockSpec((tm, XA_WIDTH), lambda i: (i, 0)),
            pl.BlockSpec((tm, d), lambda i: (i, gate_blk)),
            pl.BlockSpec((tm, d), lambda i: (i, gate_blk + 1)),
            pl.BlockSpec((tm, d), lambda i: (i, gate_blk + 2)),
            pl.BlockSpec((tm, d), lambda i: (i, 0)),
            full(w_ret_o), full(w_na_o), full(w_xa_o), full(w_out), full(g_ffn), full(w_router), full(b_router),
        ],
        out_specs=[
            pl.BlockSpec((tm, d), lambda i: (i, 0)),
            pl.BlockSpec((tm, LANES), lambda i: (i, 0)),
            smem, smem, hbm, hbm,
        ],
        out_shape=[
            jax.ShapeDtypeStruct((t, d), F32),
            jax.ShapeDtypeStruct((t, LANES), F32),
            jax.ShapeDtypeStruct((n_tiles, MOE_GROUPS), jnp.int32),
            jax.ShapeDtypeStruct((n_tiles, MOE_GROUPS), jnp.int32),
            jax.ShapeDtypeStruct((MOE_GROUPS, rows, d), BF16),
            jax.ShapeDtypeStruct((MOE_GROUPS, rows, LANES), F32),
        ],
        scratch_shapes=[
            pltpu.VMEM((2, SCR_ROWS, d), BF16),
            pltpu.VMEM((2, SCR_ROWS, LANES), F32),
            pltpu.SMEM((MOE_GROUPS,), jnp.int32),
            pltpu.SemaphoreType.DMA((2, 4, MOE_GROUPS)),
        ],
        compiler_params=_params("arbitrary"),
        name="merge_router",
    )(y_ret, y_na, y_xa, proj, proj, proj, x2d, w_ret_o, w_na_o, w_xa_o, w_out, g_ffn, w_router, b_router)


def _expert_slots(starts, counts, t):
    totals = starts[-1] + _pad_rows(counts[-1])
    nblk = (totals + EXPERT_TILE - 1) // EXPERT_TILE
    ends = jnp.cumsum(nblk)
    first = ends - nblk
    n_slots = (t + (t // SORT_TILE) * MOE_GROUPS * (ROW_ALIGN - 1)) // EXPERT_TILE + MOE_GROUPS + 1
    j = jnp.arange(n_slots, dtype=jnp.int32)
    valid = j < ends[-1]
    jc = jnp.clip(j, 0, jnp.maximum(ends[-1] - 1, 0))
    grp = jnp.sum((jc[:, None] >= ends[None, :]).astype(jnp.int32), axis=1)
    blk = jc - first[grp]
    out_starts = first[None, :] * EXPERT_TILE + starts
    group_first = (valid & (j == first[grp])).astype(jnp.int32)
    nxt = jnp.full((), -1, jnp.int32)
    next_of = []
    for g in reversed(range(MOE_GROUPS)):
        next_of.append(nxt)
        nxt = jnp.where(nblk[g] > 0, g, nxt)
    group_next = jnp.stack(next_of[::-1])[grp]
    return (grp, blk, valid.astype(jnp.int32), group_first, group_next), out_starts


CAST_ROWS = 64


def _experts_kernel(grp_ref, blk_ref, valid_ref, first_ref, next_ref, h_ref, c_ref, wg_hbm, wu_hbm, wd_hbm, o_ref,
                    stage_g, stage_u, stage_d, wg_ref, wu_ref, wd_ref, sems):
    j = pl.program_id(0)

    def weight_copies(g):
        return (pltpu.make_async_copy(wg_hbm.at[g], stage_g, sems.at[0]),
                pltpu.make_async_copy(wu_hbm.at[g], stage_u, sems.at[1]),
                pltpu.make_async_copy(wd_hbm.at[g], stage_d, sems.at[2]))

    @pl.when(j == 0)
    def _():
        for cp in weight_copies(grp_ref[0]):
            cp.start()

    @pl.when(first_ref[j] == 1)
    def _():
        for cp in weight_copies(grp_ref[j]):
            cp.wait()
        for stage, dst in ((stage_g, wg_ref), (stage_u, wu_ref), (stage_d, wd_ref)):
            for e in range(MOE_EXPERTS_PER_GROUP):
                def convert(i, carry, stage=stage, dst=dst, e=e):
                    rows = pl.ds(pl.multiple_of(i * CAST_ROWS, CAST_ROWS), CAST_ROWS)
                    dst[e, rows, :] = stage[e, rows, :].astype(dst.dtype)
                    return carry

                lax.fori_loop(0, stage.shape[1] // CAST_ROWS, convert, 0)

        @pl.when(next_ref[j] >= 0)
        def _():
            for cp in weight_copies(next_ref[j]):
                cp.start()

    @pl.when(valid_ref[j] == 1)
    def _():
        h = h_ref[...]
        c = c_ref[...]
        lane = lax.broadcasted_iota(jnp.int32, c.shape, 1)
        out = None
        for e in range(MOE_EXPERTS_PER_GROUP):
            a = _dot(h, wg_ref[e])
            u = _dot(h, wu_ref[e])
            cw = jnp.sum(jnp.where(lane == e, c, 0.0), axis=-1, keepdims=True)
            hid = (_silu(a) * u * cw).astype(BF16)
            part = _dot(hid, wd_ref[e])
            out = part if out is None else out + part
        o_ref[...] = out.astype(o_ref.dtype)

    @pl.when(valid_ref[j] == 0)
    def _():
        o_ref[...] = jnp.zeros_like(o_ref)


def _experts(hbuf, cbuf, wg, wu, wd, slots):
    g_n, rows, d = hbuf.shape
    e_n, f = wg.shape[1], wg.shape[3]
    tm = EXPERT_TILE
    n_slots = slots[0].shape[0]
    hbm = pl.BlockSpec(memory_space=pl.ANY)
    grid_spec = pltpu.PrefetchScalarGridSpec(
        num_scalar_prefetch=5,
        grid=(n_slots,),
        in_specs=[
            pl.BlockSpec((None, tm, d), lambda j, grp, blk, v, gf, gx: (grp[j], blk[j], 0)),
            pl.BlockSpec((None, tm, LANES), lambda j, grp, blk, v, gf, gx: (grp[j], blk[j], 0)),
            hbm, hbm, hbm,
        ],
        out_specs=pl.BlockSpec((tm, d), lambda j, grp, blk, v, gf, gx: (j, 0)),
        scratch_shapes=[
            pltpu.VMEM((e_n, d, f), wg.dtype),
            pltpu.VMEM((e_n, d, f), wu.dtype),
            pltpu.VMEM((e_n, f, d), wd.dtype),
            pltpu.VMEM((e_n, d, f), BF16),
            pltpu.VMEM((e_n, d, f), BF16),
            pltpu.VMEM((e_n, f, d), BF16),
            pltpu.SemaphoreType.DMA((3,)),
        ],
    )
    return pl.pallas_call(
        _experts_kernel,
        grid_spec=grid_spec,
        out_shape=jax.ShapeDtypeStruct((n_slots * tm, d), BF16),
        compiler_params=pltpu.CompilerParams(dimension_semantics=("arbitrary",),
                                             vmem_limit_bytes=EXPERTS_VMEM_LIMIT),
        name="experts",
    )(*slots, hbuf, cbuf, wg, wu, wd)


def _finish_kernel(start_ref, cnt_ref, *refs, final_norm):
    wins = refs[:2 * MOE_GROUPS]
    pos_ref, x2_ref, gfin_ref, o_ref, sorted_scr = refs[2 * MOE_GROUPS:]
    i = pl.program_id(0)
    tm = x2_ref.shape[0]
    half_pieces = HALF_WIN // ROW_ALIGN

    @pl.when(i == 0)
    def _():
        sorted_scr[...] = jnp.zeros_like(sorted_scr)

    seg = jnp.int32(0)
    for g in range(MOE_GROUPS):
        padded = _pad_rows(cnt_ref[i, g])
        pieces = padded // ROW_ALIGN
        for half, win in enumerate(wins[2 * g:2 * g + 2]):
            base = seg + half * HALF_WIN

            def copy_piece(k, carry, win=win, base=base):
                src = pl.multiple_of(k * ROW_ALIGN, ROW_ALIGN)
                dst = pl.multiple_of(base + k * ROW_ALIGN, ROW_ALIGN)
                sorted_scr[pl.ds(dst, ROW_ALIGN), :] = win[pl.ds(src, ROW_ALIGN), :]
                return carry

            lax.fori_loop(0, jnp.clip(pieces - half * half_pieces, 0, half_pieces), copy_piece, 0)
        seg = seg + padded

    pos = pos_ref[:, 0:1]
    unperm = jnp.where(lax.broadcasted_iota(jnp.int32, (tm, SORTED_ROWS), 1).astype(F32) == pos, 1.0, 0.0)
    y = x2_ref[...] + _dot(unperm.astype(BF16), sorted_scr[...])
    if final_norm:
        y = _rmsnorm_f32(y, gfin_ref[...])
    o_ref[...] = y


def _finish(mbuf, starts, counts, pos, x2, g_final, final_norm):
    t, d = x2.shape
    tm = SORT_TILE

    def window(g, half):
        def index(i, st, ct):
            row = st[i, g] + half * HALF_WIN
            if half:
                row = jnp.where(_pad_rows(ct[i, g]) > HALF_WIN, row, 0)
            return pl.multiple_of(row, ROW_ALIGN), 0

        return pl.BlockSpec((pl.Element(HALF_WIN), pl.Element(d)), index)

    grid_spec = pltpu.PrefetchScalarGridSpec(
        num_scalar_prefetch=2,
        grid=(t // tm,),
        in_specs=[
            *[window(g, half) for g in range(MOE_GROUPS) for half in range(2)],
            pl.BlockSpec((tm, LANES), lambda i, st, ct: (i, 0)),
            pl.BlockSpec((tm, d), lambda i, st, ct: (i, 0)),
            pl.BlockSpec((1, d), lambda i, st, ct: (0, 0)),
        ],
        out_specs=pl.BlockSpec((tm, d), lambda i, st, ct: (i, 0)),
        scratch_shapes=[pltpu.VMEM((SORTED_ROWS, d), BF16)],
    )
    return pl.pallas_call(
        functools.partial(_finish_kernel, final_norm=final_norm),
        grid_spec=grid_spec,
        out_shape=jax.ShapeDtypeStruct((t, d), F32),
        compiler_params=_params("arbitrary"),
        name="finish",
    )(starts, counts, *([mbuf] * (2 * MOE_GROUPS)), pos, x2, g_final)


def kernel(x, mem, g_mix, w_in, ret_decay_fwd, ret_decay_bwd, ret_norm_gain, w_ret_o, na_rpb, w_na_o, g_mem, w_mem_kv, w_xa_o, w_out, g_ffn, w_router_group, b_router_group, w_router_expert, b_router_expert, w_exp_gate, w_exp_up, w_exp_down, g_final):
    b, s, d = x.shape
    depth = w_in.shape[0]
    t = b * s
    cos_t, sin_t = _rope_tables(s)
    row = lambda v: v.reshape(1, -1).astype(F32)
    x2d = x.reshape(t, d)
    for l in range(depth):
        proj = _inproj(x2d, row(g_mix[l]), w_in[l].astype(BF16))
        proj3 = proj.reshape(b, s, IN_WIDTH)
        y_ret = _retention(proj3, ret_decay_fwd[l].astype(F32), ret_decay_bwd[l].astype(F32), cos_t, sin_t,
                           row(ret_norm_gain[l]))
        y_na = _neighbourhood_attention(proj3, _na_bias_table(na_rpb[l]))
        y_xa = _memory_attention(proj3, mem, row(g_mem[l]), w_mem_kv[l].astype(BF16))
        n_r = MOE_GROUPS + N_EXPERTS
        w_router = jnp.pad(jnp.concatenate([w_router_group[l], w_router_expert[l]], axis=1).astype(F32),
                           ((0, 0), (0, LANES - n_r)))
        b_router = jnp.pad(jnp.concatenate([b_router_group[l], b_router_expert[l]]).astype(F32),
                           (0, LANES - n_r)).reshape(1, LANES)
        x2, pos, starts, counts, hbuf, cbuf = _merge(
            y_ret.reshape(t, -1), y_na.reshape(t, -1), y_xa.reshape(t, -1), proj, x2d,
            w_ret_o[l].astype(BF16), w_na_o[l].astype(BF16), w_xa_o[l].astype(BF16),
            w_out[l].astype(BF16), row(g_ffn[l]), w_router, b_router)
        slots, out_starts = _expert_slots(starts, counts, t)
        mbuf = _experts(hbuf, cbuf, w_exp_gate[l], w_exp_up[l], w_exp_down[l], slots)
        x2d = _finish(mbuf, out_starts, counts, pos, x2, row(g_final), final_norm=(l == depth - 1))
    return x2d.reshape(b, s, d)
```

```python
import functools

import jax
import jax.numpy as jnp
import numpy as np
from jax import lax
from jax.experimental import pallas as pl
from jax.experimental.pallas import tpu as pltpu

D_MODEL = 1024
GRID_W = 64
N_BRANCHES = 3
RET_HEADS = 4
RET_QK_DIM = 128
RET_V_DIM = 256
RET_CHUNK = 128
ROPE_BASE = 10000.0
NA_HEADS = 8
NA_HEAD_DIM = 64
NA_ROWS = 8
NA_COLS = 16
XA_HEADS = 4
XA_HEAD_DIM = 128
MOE_GROUPS = 4
MOE_EXPERTS_PER_GROUP = 4
MOE_TOP_K = 2
MOE_D_FF = 512
RMS_EPS = 1e-6
GN_EPS = 1e-5

RET_QK_WIDTH = RET_HEADS * RET_QK_DIM
RET_V_WIDTH = RET_HEADS * RET_V_DIM
NA_WIDTH = NA_HEADS * NA_HEAD_DIM
XA_WIDTH = XA_HEADS * XA_HEAD_DIM
IN_WIDTH = 2 * RET_QK_WIDTH + 2 * RET_V_WIDTH + 3 * NA_WIDTH + XA_WIDTH + N_BRANCHES * D_MODEL

OFF_RQ = 0
OFF_RK = OFF_RQ + RET_QK_WIDTH
OFF_RV = OFF_RK + RET_QK_WIDTH
OFF_RG = OFF_RV + RET_V_WIDTH
OFF_NQ = OFF_RG + RET_V_WIDTH
OFF_NK = OFF_NQ + NA_WIDTH
OFF_NV = OFF_NK + NA_WIDTH
OFF_XQ = OFF_NV + NA_WIDTH
OFF_GATE = OFF_XQ + XA_WIDTH

N_EXPERTS = MOE_GROUPS * MOE_EXPERTS_PER_GROUP
LANES = 128
ROUTER_EXPERT_LANE0 = MOE_GROUPS

VMEM_LIMIT = 48 * 1024 * 1024
STAGING_VMEM_LIMIT = 56 * 1024 * 1024
CAST_ROWS = 64

F32 = jnp.float32
BF16 = jnp.bfloat16


def _params(*sem):
    return pltpu.CompilerParams(dimension_semantics=sem, vmem_limit_bytes=VMEM_LIMIT)


def _rmsnorm_f32(x, g):
    return x * lax.rsqrt(jnp.mean(x * x, axis=-1, keepdims=True) + RMS_EPS) * g


def _sigmoid(x):
    return 0.5 * jnp.tanh(0.5 * x) + 0.5


def _silu(x):
    h = 0.5 * x
    return h + h * jnp.tanh(h)


def _dot(a, b):
    return jnp.dot(a, b, preferred_element_type=F32)


def _dot_tn(a, b):
    return lax.dot_general(a, b, (((0,), (0,)), ((), ())), preferred_element_type=F32)


def _dot_nt(a, b):
    return lax.dot_general(a, b, (((1,), (1,)), ((), ())), preferred_element_type=F32)


INPROJ_STAGE_COLS = 1024


def _inproj_kernel(x_ref, g_ref, w_hbm, o_ref, h_scr, w_scr, stage, sems):
    i = pl.program_id(0)
    j = pl.program_id(1)
    n_col = pl.num_programs(1)
    tn = o_ref.shape[1]
    per_block = tn // INPROJ_STAGE_COLS

    def chunk_copy(col_block, k):
        cols = pl.ds(pl.multiple_of((col_block * per_block + k) * INPROJ_STAGE_COLS, INPROJ_STAGE_COLS),
                     INPROJ_STAGE_COLS)
        return pltpu.make_async_copy(w_hbm.at[:, cols], stage.at[k], sems.at[k])

    @pl.when(j == 0)
    def _():
        h_scr[...] = _rmsnorm_f32(x_ref[...], g_ref[...]).astype(BF16)

    @pl.when((i == 0) & (j == 0))
    def _():
        for k in range(per_block):
            chunk_copy(0, k).start()

    @pl.when(i == 0)
    def _():
        for k in range(per_block):
            chunk_copy(j, k).wait()

            def convert(r, carry, k=k):
                rows = pl.ds(pl.multiple_of(r * CAST_ROWS, CAST_ROWS), CAST_ROWS)
                w_scr[j, rows, k * INPROJ_STAGE_COLS:(k + 1) * INPROJ_STAGE_COLS] = stage[k, rows, :].astype(BF16)
                return carry

            lax.fori_loop(0, stage.shape[1] // CAST_ROWS, convert, 0)

        @pl.when(j + 1 < n_col)
        def _():
            for k in range(per_block):
                chunk_copy(j + 1, k).start()

    o_ref[...] = _dot(h_scr[...], w_scr[j]).astype(o_ref.dtype)


def _inproj(x2d, g, w, tm=1024, tn=2048):
    t, d = x2d.shape
    n = w.shape[1]
    per_block = tn // INPROJ_STAGE_COLS
    return pl.pallas_call(
        _inproj_kernel,
        grid=(t // tm, n // tn),
        in_specs=[
            pl.BlockSpec((tm, d), lambda i, j: (i, 0)),
            pl.BlockSpec((1, d), lambda i, j: (0, 0)),
            pl.BlockSpec(memory_space=pl.ANY),
        ],
        out_specs=pl.BlockSpec((tm, tn), lambda i, j: (i, j)),
        out_shape=jax.ShapeDtypeStruct((t, n), BF16),
        scratch_shapes=[
            pltpu.VMEM((tm, d), BF16),
            pltpu.VMEM((n // tn, d, tn), BF16),
            pltpu.VMEM((per_block, d, INPROJ_STAGE_COLS), w.dtype),
            pltpu.SemaphoreType.DMA((per_block,)),
        ],
        compiler_params=pltpu.CompilerParams(dimension_semantics=("arbitrary", "arbitrary"),
                                             vmem_limit_bytes=STAGING_VMEM_LIMIT),
        name="inproj",
    )(x2d, g, w)


def _log_sigmoid(x):
    return jnp.minimum(x, 0.0) - jnp.log1p(jnp.exp(-jnp.abs(x)))


RET_HEADS_PER_STEP = 2


def _retention_kernel(decf_ref, decb_ref, q_ref, k_ref, v_ref, rg_ref, cos_ref, sin_ref, gn_ref, o_ref, *scratch):
    qk, dv = RET_QK_DIM, RET_V_DIM
    for hh in range(RET_HEADS_PER_STEP):
        qcols = slice(hh * qk, (hh + 1) * qk)
        vcols = slice(hh * dv, (hh + 1) * dv)
        _retention_head(pl.program_id(1) * RET_HEADS_PER_STEP + hh, decf_ref, decb_ref,
                        q_ref.at[:, qcols], k_ref.at[:, qcols], v_ref.at[:, vcols], rg_ref.at[:, vcols],
                        cos_ref, sin_ref, gn_ref.at[:, vcols], o_ref.at[:, vcols], *[s.at[hh] for s in scratch])


def _retention_head(h, decf_ref, decb_ref, q_ref, k_ref, v_ref, rg_ref, cos_ref, sin_ref, gn_ref, o_ref,
                    qfs, qbs, y_scr, kvf_scr, kvb_scr, st_scr):
    c = RET_CHUNK
    seq = q_ref.shape[0]
    n_chunks = seq // c
    half = RET_QK_DIM // 2

    lgf = _log_sigmoid(jnp.full((1, 1), decf_ref[h], F32))
    lgb = _log_sigmoid(jnp.full((1, 1), decb_ref[h], F32))

    ii = lax.broadcasted_iota(jnp.int32, (c, c), 0)
    jj = lax.broadcasted_iota(jnp.int32, (c, c), 1)
    diff = (ii - jj).astype(F32)
    dmat = jnp.exp(jnp.where(diff >= 0, diff * lgf, (-diff) * lgb))
    pos = lax.broadcasted_iota(jnp.int32, (c, 1), 0).astype(F32)
    qd_f = jnp.exp((pos + 1.0) * lgf)
    kd_f = jnp.exp((c - 1.0 - pos) * lgf)
    cd_f = jnp.exp(c * lgf)
    qd_b = jnp.exp((c - pos) * lgb)
    kd_b = jnp.exp(pos * lgb)
    cd_b = jnp.exp(c * lgb)

    def intra(n):
        rows = pl.ds(n * c, c)
        cos = cos_ref[rows, :]
        sin = sin_ref[rows, :]
        q = q_ref[rows, :].astype(F32)
        q = (q * cos + pltpu.roll(q, half, 1) * sin) * (RET_QK_DIM ** -0.5)
        qfs[rows, :] = (q * qd_f).astype(BF16)
        qbs[rows, :] = (q * qd_b).astype(BF16)
        k = k_ref[rows, :].astype(F32)
        k = k * cos + pltpu.roll(k, half, 1) * sin
        s = _dot_nt(q.astype(BF16), k.astype(BF16)) * dmat
        lhs = jnp.concatenate([s.astype(BF16), (k * kd_f).T.astype(BF16), (k * kd_b).T.astype(BF16)], axis=0)
        r = _dot(lhs, v_ref[rows, :])
        y_scr[rows, :] = r[:c]
        kvf_scr[n] = r[c:2 * c]
        kvb_scr[n] = r[2 * c:]

    for n in range(n_chunks):
        intra(n)

    st_scr[...] = jnp.zeros_like(st_scr)

    def fwd(n):
        rows = pl.ds(n * c, c)
        st = st_scr[...]
        y_scr[rows, :] += _dot(qfs[rows, :], st.astype(BF16))
        st_scr[...] = st * cd_f + kvf_scr[n]

    for n in range(n_chunks):
        fwd(n)

    st_scr[...] = jnp.zeros_like(st_scr)
    gn = gn_ref[...]

    def bwd(n):
        rows = pl.ds(n * c, c)
        st = st_scr[...]
        y = y_scr[rows, :] + _dot(qbs[rows, :], st.astype(BF16))
        st_scr[...] = st * cd_b + kvb_scr[n]
        mu = jnp.mean(y, axis=-1, keepdims=True)
        yc = y - mu
        var = jnp.mean(yc * yc, axis=-1, keepdims=True)
        yn = yc * lax.rsqrt(var + GN_EPS) * gn
        o_ref[rows, :] = _silu(rg_ref[rows, :]) * yn.astype(o_ref.dtype)

    for n in reversed(range(n_chunks)):
        bwd(n)


def _retention(proj3, dec_f, dec_b, cos_t, sin_t, gn_gain):
    b, s, _ = proj3.shape
    hps = RET_HEADS_PER_STEP
    dk, dv = RET_QK_DIM, RET_V_DIM
    qb, vb = hps * dk, hps * dv
    smem = pl.BlockSpec(memory_space=pltpu.SMEM)
    return pl.pallas_call(
        _retention_kernel,
        grid=(b, RET_HEADS // hps),
        in_specs=[
            smem,
            smem,
            pl.BlockSpec((None, s, qb), lambda i, h: (i, 0, OFF_RQ // qb + h)),
            pl.BlockSpec((None, s, qb), lambda i, h: (i, 0, OFF_RK // qb + h)),
            pl.BlockSpec((None, s, vb), lambda i, h: (i, 0, OFF_RV // vb + h)),
            pl.BlockSpec((None, s, vb), lambda i, h: (i, 0, OFF_RG // vb + h)),
            pl.BlockSpec((s, dk), lambda i, h: (0, 0)),
            pl.BlockSpec((s, dk), lambda i, h: (0, 0)),
            pl.BlockSpec((1, vb), lambda i, h: (0, h)),
        ],
        out_specs=pl.BlockSpec((None, s, vb), lambda i, h: (i, 0, h)),
        out_shape=jax.ShapeDtypeStruct((b, s, RET_V_WIDTH), BF16),
        scratch_shapes=[
            pltpu.VMEM((hps, s, dk), BF16),
            pltpu.VMEM((hps, s, dk), BF16),
            pltpu.VMEM((hps, s, dv), F32),
            pltpu.VMEM((hps, s // RET_CHUNK, dk, dv), F32),
            pltpu.VMEM((hps, s // RET_CHUNK, dk, dv), F32),
            pltpu.VMEM((hps, dk, dv), F32),
        ],
        compiler_params=_params("parallel", "parallel"),
        name="retention",
    )(dec_f, dec_b, proj3, proj3, proj3, proj3, cos_t, sin_t, gn_gain)


def _rope_tables(seq):
    half = RET_QK_DIM // 2
    inv_freq = ROPE_BASE ** (-jnp.arange(half, dtype=F32) / half)
    ang = jnp.arange(seq, dtype=F32)[:, None] * inv_freq[None, :]
    cos, sin = jnp.cos(ang), jnp.sin(ang)
    return jnp.concatenate([cos, cos], axis=-1), jnp.concatenate([-sin, sin], axis=-1)


def _na_bias_table(rpb):
    heads = rpb.shape[0]
    w = GRID_W
    cols = np.arange(w)
    col_start = np.clip(cols - NA_COLS // 2, 0, w - NA_COLS)
    col_off = cols[None, :] - col_start[:, None]
    col_mask = (col_off >= 0) & (col_off < NA_COLS)
    rel_c = np.clip(cols[None, :] - cols[:, None], -(NA_COLS - 1), NA_COLS - 1) + (NA_COLS - 1)
    onehot = (rel_c[:, :, None] == np.arange(2 * NA_COLS - 1)).astype(np.float32)
    toe = jnp.einsum('hrc,qkc->hrqk', rpb.astype(F32), onehot, precision=lax.Precision.HIGHEST)
    toe = jnp.where(col_mask[None, None], toe, -jnp.inf)
    two = jnp.concatenate([toe[:, :-1], toe[:, 1:]], axis=-1)
    two = two.reshape(heads // 2, 2, 2 * NA_ROWS - 2, w, 2 * w)
    return two.transpose(0, 2, 1, 3, 4).reshape(heads // 2, 2 * NA_ROWS - 2, 2 * w, 2 * w)


NA_ROWS_PER_STEP = 8


def _na_kernel(q_ref, k_ref, v_ref, bias_ref, o_ref, s_scr, p_scr):
    w = GRID_W
    seq = q_ref.shape[0]
    rows_n = seq // w
    kr = NA_ROWS
    first = lax.broadcasted_iota(jnp.int32, (w, LANES), 1) < NA_HEAD_DIM
    scale = NA_HEAD_DIM ** -0.5

    def window_row(r):
        return jnp.clip(r - kr // 2, 0, rows_n - kr)

    def body(it, carry):
        r0 = it * NA_ROWS_PER_STEP
        for u in range(NA_ROWS_PER_STEP):
            r = r0 + u
            rs = window_row(r)
            qr = q_ref[pl.ds(pl.multiple_of(r * w, w), w), :] * scale
            zero = jnp.zeros_like(qr)
            q2 = jnp.concatenate([jnp.where(first, qr, zero), jnp.where(first, zero, qr)], axis=0)
            kk = k_ref[pl.ds(pl.multiple_of(rs * w, w), kr * w), :]
            off = rs - r + (NA_ROWS - 1)
            bias = jnp.concatenate([bias_ref[off + 2 * i] for i in range(kr // 2)], axis=1)
            s_scr[u] = _dot_nt(q2, kk) + bias
        for u in range(NA_ROWS_PER_STEP):
            m = jnp.max(s_scr[u], axis=-1, keepdims=True)
            p_scr[u] = jnp.exp(s_scr[u] - m).astype(BF16)
        for u in range(NA_ROWS_PER_STEP):
            r = r0 + u
            vv = v_ref[pl.ds(pl.multiple_of(window_row(r) * w, w), kr * w), :]
            o2 = _dot(p_scr[u], jnp.concatenate([vv, jnp.ones_like(vv)], axis=1))
            o2 = o2[:, :LANES] / o2[:, LANES:]
            o_ref[pl.ds(pl.multiple_of(r * w, w), w), :] = jnp.where(first, o2[:w], o2[w:]).astype(o_ref.dtype)
        return carry

    lax.fori_loop(0, rows_n // NA_ROWS_PER_STEP, body, 0)


def _neighbourhood_attention(proj3, bias_tab):
    b, s, _ = proj3.shape
    pairs = NA_HEADS // 2
    return pl.pallas_call(
        _na_kernel,
        grid=(b, pairs),
        in_specs=[
            pl.BlockSpec((None, s, LANES), lambda i, p: (i, 0, OFF_NQ // LANES + p)),
            pl.BlockSpec((None, s, LANES), lambda i, p: (i, 0, OFF_NK // LANES + p)),
            pl.BlockSpec((None, s, LANES), lambda i, p: (i, 0, OFF_NV // LANES + p)),
            pl.BlockSpec((None, 2 * NA_ROWS - 2, 2 * GRID_W, 2 * GRID_W), lambda i, p: (p, 0, 0, 0)),
        ],
        out_specs=pl.BlockSpec((None, s, LANES), lambda i, p: (i, 0, p)),
        out_shape=jax.ShapeDtypeStruct((b, s, NA_WIDTH), BF16),
        scratch_shapes=[
            pltpu.VMEM((NA_ROWS_PER_STEP, 2 * GRID_W, NA_ROWS * GRID_W), F32),
            pltpu.VMEM((NA_ROWS_PER_STEP, 2 * GRID_W, NA_ROWS * GRID_W), BF16),
        ],
        compiler_params=_params("parallel", "parallel"),
        name="nbr_attn",
    )(proj3, proj3, proj3, bias_tab)


def _xa_kernel(q_ref, mem_ref, g_ref, wkv_ref, o_ref, kv_scr):
    @pl.when(pl.program_id(1) == 0)
    def _():
        mn = _rmsnorm_f32(mem_ref[...], g_ref[...]).astype(BF16)
        kv_scr[...] = _dot(mn, wkv_ref[...]).astype(BF16)

    dh = XA_HEAD_DIM
    scale = dh ** -0.5
    for h in range(XA_HEADS):
        q = q_ref[:, h * dh:(h + 1) * dh]
        k = kv_scr[:, h * dh:(h + 1) * dh]
        v = kv_scr[:, XA_WIDTH + h * dh:XA_WIDTH + (h + 1) * dh]
        s = _dot_nt(q, k) * scale
        m = jnp.max(s, axis=-1, keepdims=True)
        p = jnp.exp(s - m).astype(BF16)
        o2 = _dot(p, jnp.concatenate([v, jnp.ones_like(v)], axis=1))
        o_ref[:, h * dh:(h + 1) * dh] = (o2[:, :dh] / o2[:, dh:]).astype(o_ref.dtype)


def _memory_attention(proj3, mem, g_mem, wkv_bf16, ts=1024):
    b, s, _ = proj3.shape
    m, d = mem.shape[1], mem.shape[2]
    return pl.pallas_call(
        _xa_kernel,
        grid=(b, s // ts),
        in_specs=[
            pl.BlockSpec((None, ts, XA_WIDTH), lambda i, j: (i, j, OFF_XQ // XA_WIDTH)),
            pl.BlockSpec((None, m, d), lambda i, j: (i, 0, 0)),
            pl.BlockSpec((1, d), lambda i, j: (0, 0)),
            pl.BlockSpec((d, 2 * XA_WIDTH), lambda i, j: (0, 0)),
        ],
        out_specs=pl.BlockSpec((None, ts, XA_WIDTH), lambda i, j: (i, j, 0)),
        out_shape=jax.ShapeDtypeStruct((b, s, XA_WIDTH), BF16),
        scratch_shapes=[pltpu.VMEM((m, 2 * XA_WIDTH), BF16)],
        compiler_params=_params("parallel", "arbitrary"),
        name="mem_attn",
    )(proj3, mem, g_mem, wkv_bf16)


def _masked_lane_max(v, mask):
    return jnp.max(jnp.where(mask, v, -jnp.inf), axis=-1, keepdims=True)


def _first_lane_eq(v, target, mask, lane):
    return jnp.min(jnp.where(mask & (v == target), lane, float(LANES)), axis=-1, keepdims=True)


def _route(logits):
    g_n, e_n = MOE_GROUPS, MOE_EXPERTS_PER_GROUP
    lane = lax.broadcasted_iota(jnp.int32, logits.shape, 1).astype(F32)
    is_grp = lane < g_n
    gmax = _masked_lane_max(logits, is_grp)
    gsum = jnp.sum(jnp.where(is_grp, jnp.exp(logits - gmax), 0.0), axis=-1, keepdims=True)
    grp_w = 1.0 / gsum
    gidx = _first_lane_eq(logits, gmax, is_grp, lane)
    lo = ROUTER_EXPERT_LANE0 + gidx * e_n
    in_grp = (lane >= lo) & (lane < lo + e_n)
    emax = _masked_lane_max(logits, in_grp)
    ex = jnp.where(in_grp, jnp.exp(logits - emax), 0.0)
    prob = ex / jnp.sum(ex, axis=-1, keepdims=True)
    p1 = _masked_lane_max(prob, in_grp)
    i1 = _first_lane_eq(prob, p1, in_grp, lane)
    rest = in_grp & (lane != i1)
    p2 = _masked_lane_max(prob, rest)
    i2 = _first_lane_eq(prob, p2, rest, lane)
    tot = p1 + p2
    w1 = p1 / tot * grp_w
    w2 = p2 / tot * grp_w
    return gidx, jnp.where(lane == i1 - lo, w1, jnp.where(lane == i2 - lo, w2, 0.0))


SORT_TILE = 512
ROW_ALIGN = 16
SORTED_ROWS = 640
WIN_ROWS = SORT_TILE
HALF_WIN = WIN_ROWS // 2
SCR_ROWS = SORTED_ROWS + WIN_ROWS
EXPERT_TILE = 512


def _group_buf_rows(t):
    return t + 2 * WIN_ROWS


def _pad_rows(n):
    return (n + (ROW_ALIGN - 1)) // ROW_ALIGN * ROW_ALIGN


def _window_copies(hs, cs, hbuf, cbuf, sems, slot, kind, g, src_row, dst_row):
    return (
        pltpu.make_async_copy(hs.at[slot, pl.ds(src_row, WIN_ROWS)], hbuf.at[g, pl.ds(dst_row, WIN_ROWS)],
                              sems.at[slot, 2 * kind, g]),
        pltpu.make_async_copy(cs.at[slot, pl.ds(src_row, WIN_ROWS)], cbuf.at[g, pl.ds(dst_row, WIN_ROWS)],
                              sems.at[slot, 2 * kind + 1, g]),
    )


def _merge_kernel(*refs):
    hs, cs, hh_scr, lg_scr, run = refs[-6:-1]
    i = pl.program_id(0)

    @pl.when(i == 0)
    def _():
        for g in range(MOE_GROUPS):
            run[g] = 0
        hs[:, SORTED_ROWS:, :] = jnp.zeros((2, WIN_ROWS, hs.shape[2]), hs.dtype)
        cs[:, SORTED_ROWS:, :] = jnp.zeros((2, WIN_ROWS, cs.shape[2]), cs.dtype)
        hh_scr[...] = jnp.zeros_like(hh_scr)
        lg_scr[...] = jnp.zeros_like(lg_scr)

    for slot in range(2):
        pl.when(lax.rem(i, 2) == slot)(functools.partial(_merge_step, slot, *refs))


def _merge_step(slot, yret_ref, yna_ref, yxa_ref, gr_ref, gn_ref, gx_ref, x_ref, wro_ref, wno_ref, wxo_ref, wout_ref,
                gffn_ref, wr_ref, br_ref,
                x2_ref, pos_ref, start_ref, cnt_ref, hbuf, cbuf,
                hs, cs, hh_scr, lg_scr, run, sems):
    i = pl.program_id(0)
    n_tiles = pl.num_programs(0) - 1
    tm = x_ref.shape[0]
    g_n = MOE_GROUPS

    y_ret = _dot(yret_ref[...], wro_ref[...])
    y_na = _dot(yna_ref[...], wno_ref[...])
    y_xa = _dot(yxa_ref[...], wxo_ref[...])
    mix = (_sigmoid(gr_ref[...].astype(F32)) * y_ret + _sigmoid(gn_ref[...].astype(F32)) * y_na
           + _sigmoid(gx_ref[...].astype(F32)) * y_xa)
    x2 = x_ref[...] + _dot(mix.astype(BF16), wout_ref[...])
    x2_ref[...] = x2
    h2 = _rmsnorm_f32(x2, gffn_ref[...])
    h2_hi = h2.astype(BF16)
    h2_lo = (h2 - h2_hi.astype(F32)).astype(BF16)
    wr = wr_ref[...]
    wr_hi = wr.astype(BF16)
    wr_lo = (wr - wr_hi.astype(F32)).astype(BF16)
    hi_terms = _dot(h2_hi, jnp.concatenate([wr_hi, wr_lo], axis=1))
    hh_scr[slot] = h2_hi
    lg_scr[slot] = hi_terms[:, :LANES] + hi_terms[:, LANES:] + _dot(h2_lo, wr_hi) + br_ref[...]

    h_hi = hh_scr[1 - slot]
    gidx, w4 = _route(lg_scr[1 - slot])

    lane = lax.broadcasted_iota(jnp.int32, (tm, LANES), 1).astype(F32)
    live = jnp.where(i > 0, 1.0, 0.0)
    onehot = jnp.where(lane == gidx, live, 0.0)
    before = (lax.broadcasted_iota(jnp.int32, (tm, tm), 1) < lax.broadcasted_iota(jnp.int32, (tm, tm), 0))
    rank = _dot(jnp.where(before, 1.0, 0.0).astype(BF16), onehot.astype(BF16))
    count_row = rank[tm - 1:tm, :] + onehot[tm - 1:tm, :]
    lane_row = lax.broadcasted_iota(jnp.int32, (1, LANES), 1)
    counts = [jnp.sum(jnp.where(lane_row == g, count_row, 0.0)).astype(jnp.int32) for g in range(g_n)]
    seg_start = []
    acc = jnp.int32(0)
    for g in range(g_n):
        seg_start.append(acc)
        acc = acc + _pad_rows(counts[g])
    start_row = jnp.zeros((1, LANES), F32)
    for g in range(g_n):
        start_row = jnp.where(lane_row == g, seg_start[g].astype(F32), start_row)
    pos = jnp.sum(onehot * (rank + start_row), axis=-1, keepdims=True)
    pos_ref[...] = jnp.broadcast_to(pos, (tm, LANES))

    pos_lanes = jnp.transpose(jnp.broadcast_to(pos, (tm, LANES)))[0:1, :]
    perm = jnp.where(lax.broadcasted_iota(jnp.int32, (SORTED_ROWS, tm), 0).astype(F32) == pos_lanes, 1.0, 0.0)
    perm = perm.astype(BF16)
    hs[slot, 0:SORTED_ROWS, :] = _dot(perm, h_hi).astype(hs.dtype)
    e_n = MOE_EXPERTS_PER_GROUP
    w_hi = w4.astype(BF16).astype(F32)
    r1 = w4 - w_hi
    w_mid = r1.astype(BF16).astype(F32)
    w_lo = r1 - w_mid
    pieces = (w_hi + pltpu.roll(w_mid, e_n, 1) + pltpu.roll(w_lo, 2 * e_n, 1)).astype(BF16)
    sorted_pieces = _dot(perm, pieces)
    cs[slot, 0:SORTED_ROWS, :] = (sorted_pieces + pltpu.roll(sorted_pieces, LANES - e_n, 1)
                                  + pltpu.roll(sorted_pieces, LANES - 2 * e_n, 1))

    copies = functools.partial(_window_copies, hs, cs, hbuf, cbuf, sems)

    def wait_all(which_slot):
        for kind in range(2):
            for g in range(g_n):
                for cp in copies(which_slot, kind, g, 0, 0):
                    cp.wait()

    @pl.when(i > 0)
    def _():
        wait_all(1 - slot)

    sorted_tile = jnp.maximum(i - 1, 0)
    tiles_left = n_tiles - i
    for g in range(g_n):
        c_g = run[g]
        c_next = c_g + _pad_rows(counts[g])
        start_ref[sorted_tile, g] = c_g
        cnt_ref[sorted_tile, g] = counts[g]
        run[g] = c_next
        for cp in copies(slot, 0, g, pl.multiple_of(seg_start[g], ROW_ALIGN), pl.multiple_of(c_g, ROW_ALIGN)):
            cp.start()
        dead = c_next + tiles_left * WIN_ROWS + WIN_ROWS
        for cp in copies(slot, 1, g, SORTED_ROWS, pl.multiple_of(dead, ROW_ALIGN)):
            cp.start()

    @pl.when(i == n_tiles)
    def _():
        wait_all(slot)
        for g in range(g_n):
            for cp in copies(slot, 0, g, SORTED_ROWS, pl.multiple_of(run[g], ROW_ALIGN)):
                cp.start()
        for g in range(g_n):
            for cp in copies(slot, 0, g, 0, 0):
                cp.wait()


def _merge(y_ret, y_na, y_xa, proj, x2d, w_ret_o, w_na_o, w_xa_o, w_out, g_ffn, w_router, b_router):
    t, d = x2d.shape
    tm = SORT_TILE
    n_tiles = t // tm
    rows = _group_buf_rows(t)
    gate_blk = OFF_GATE // d
    full = lambda a: pl.BlockSpec(a.shape, lambda i: (0,) * a.ndim)
    smem = pl.BlockSpec(memory_space=pltpu.SMEM)
    hbm = pl.BlockSpec(memory_space=pl.ANY)
    cur = lambda i: jnp.minimum(i, n_tiles - 1)
    prev = lambda i: jnp.maximum(i - 1, 0)
    return pl.pallas_call(
        _merge_kernel,
        grid=(n_tiles + 1,),
        in_specs=[
            pl.BlockSpec((tm, RET_V_WIDTH), lambda i: (cur(i), 0)),
            pl.BlockSpec((tm, NA_WIDTH), lambda i: (cur(i), 0)),
            pl.BlockSpec((tm, XA_WIDTH), lambda i: (cur(i), 0)),
            pl.BlockSpec((tm, d), lambda i: (cur(i), gate_blk)),
            pl.BlockSpec((tm, d), lambda i: (cur(i), gate_blk + 1)),
            pl.BlockSpec((tm, d), lambda i: (cur(i), gate_blk + 2)),
            pl.BlockSpec((tm, d), lambda i: (cur(i), 0)),
            full(w_ret_o), full(w_na_o), full(w_xa_o), full(w_out), full(g_ffn), full(w_router), full(b_router),
        ],
        out_specs=[
            pl.BlockSpec((tm, d), lambda i: (cur(i), 0)),
            pl.BlockSpec((tm, LANES), lambda i: (prev(i), 0)),
            smem, smem, hbm, hbm,
        ],
        out_shape=[
            jax.ShapeDtypeStruct((t, d), F32),
            jax.ShapeDtypeStruct((t, LANES), F32),
            jax.ShapeDtypeStruct((n_tiles, MOE_GROUPS), jnp.int32),
            jax.ShapeDtypeStruct((n_tiles, MOE_GROUPS), jnp.int32),
            jax.ShapeDtypeStruct((MOE_GROUPS, rows, d), BF16),
            jax.ShapeDtypeStruct((MOE_GROUPS, rows, LANES), F32),
        ],
        scratch_shapes=[
            pltpu.VMEM((2, SCR_ROWS, d), BF16),
            pltpu.VMEM((2, SCR_ROWS, LANES), F32),
            pltpu.VMEM((2, tm, d), BF16),
            pltpu.VMEM((2, tm, LANES), F32),
            pltpu.SMEM((MOE_GROUPS,), jnp.int32),
            pltpu.SemaphoreType.DMA((2, 4, MOE_GROUPS)),
        ],
        compiler_params=_params("arbitrary"),
        name="merge_router",
    )(y_ret, y_na, y_xa, proj, proj, proj, x2d, w_ret_o, w_na_o, w_xa_o, w_out, g_ffn, w_router, b_router)


def _expert_slots(starts, counts, t):
    totals = starts[-1] + _pad_rows(counts[-1])
    nblk = (totals + EXPERT_TILE - 1) // EXPERT_TILE
    ends = jnp.cumsum(nblk)
    first = ends - nblk
    n_slots = (t + (t // SORT_TILE) * MOE_GROUPS * (ROW_ALIGN - 1)) // EXPERT_TILE + MOE_GROUPS + 1
    j = jnp.arange(n_slots, dtype=jnp.int32)
    valid = j < ends[-1]
    jc = jnp.clip(j, 0, jnp.maximum(ends[-1] - 1, 0))
    grp = jnp.sum((jc[:, None] >= ends[None, :]).astype(jnp.int32), axis=1)
    blk = jc - first[grp]
    out_starts = first[None, :] * EXPERT_TILE + starts
    group_first = (valid & (j == first[grp])).astype(jnp.int32)
    nxt = jnp.full((), -1, jnp.int32)
    next_of = []
    for g in reversed(range(MOE_GROUPS)):
        next_of.append(nxt)
        nxt = jnp.where(nblk[g] > 0, g, nxt)
    group_next = jnp.stack(next_of[::-1])[grp]
    return (grp, blk, valid.astype(jnp.int32), group_first, group_next), out_starts


def _experts_kernel(grp_ref, blk_ref, valid_ref, first_ref, next_ref, h_ref, c_ref, wg_hbm, wu_hbm, wd_hbm, o_ref,
                    stage_g, stage_u, stage_d, wg_ref, wu_ref, wd_ref, sems):
    j = pl.program_id(0)

    def weight_copies(g):
        return (pltpu.make_async_copy(wg_hbm.at[g], stage_g, sems.at[0]),
                pltpu.make_async_copy(wu_hbm.at[g], stage_u, sems.at[1]),
                pltpu.make_async_copy(wd_hbm.at[g], stage_d, sems.at[2]))

    @pl.when(j == 0)
    def _():
        for cp in weight_copies(grp_ref[0]):
            cp.start()

    @pl.when(first_ref[j] == 1)
    def _():
        for cp in weight_copies(grp_ref[j]):
            cp.wait()
        for stage, dst in ((stage_g, wg_ref), (stage_u, wu_ref), (stage_d, wd_ref)):
            for e in range(MOE_EXPERTS_PER_GROUP):
                def convert(i, carry, stage=stage, dst=dst, e=e):
                    rows = pl.ds(pl.multiple_of(i * CAST_ROWS, CAST_ROWS), CAST_ROWS)
                    dst[e, rows, :] = stage[e, rows, :].astype(dst.dtype)
                    return carry

                lax.fori_loop(0, stage.shape[1] // CAST_ROWS, convert, 0)

        @pl.when(next_ref[j] >= 0)
        def _():
            for cp in weight_copies(next_ref[j]):
                cp.start()

    @pl.when(valid_ref[j] == 1)
    def _():
        h = h_ref[...]
        c = c_ref[...]
        lane = lax.broadcasted_iota(jnp.int32, c.shape, 1)
        out = None
        for e in range(MOE_EXPERTS_PER_GROUP):
            a = _dot(h, wg_ref[e])
            u = _dot(h, wu_ref[e])
            cw = jnp.sum(jnp.where(lane == e, c, 0.0), axis=-1, keepdims=True)
            hid = (_silu(a) * u * cw).astype(BF16)
            part = _dot(hid, wd_ref[e])
            out = part if out is None else out + part
        o_ref[...] = out.astype(o_ref.dtype)

    @pl.when(valid_ref[j] == 0)
    def _():
        o_ref[...] = jnp.zeros_like(o_ref)


def _experts(hbuf, cbuf, wg, wu, wd, slots):
    g_n, rows, d = hbuf.shape
    e_n, f = wg.shape[1], wg.shape[3]
    tm = EXPERT_TILE
    n_slots = slots[0].shape[0]
    hbm = pl.BlockSpec(memory_space=pl.ANY)
    grid_spec = pltpu.PrefetchScalarGridSpec(
        num_scalar_prefetch=5,
        grid=(n_slots,),
        in_specs=[
            pl.BlockSpec((None, tm, d), lambda j, grp, blk, v, gf, gx: (grp[j], blk[j], 0)),
            pl.BlockSpec((None, tm, LANES), lambda j, grp, blk, v, gf, gx: (grp[j], blk[j], 0)),
            hbm, hbm, hbm,
        ],
        out_specs=pl.BlockSpec((tm, d), lambda j, grp, blk, v, gf, gx: (j, 0)),
        scratch_shapes=[
            pltpu.VMEM((e_n, d, f), wg.dtype),
            pltpu.VMEM((e_n, d, f), wu.dtype),
            pltpu.VMEM((e_n, f, d), wd.dtype),
            pltpu.VMEM((e_n, d, f), BF16),
            pltpu.VMEM((e_n, d, f), BF16),
            pltpu.VMEM((e_n, f, d), BF16),
            pltpu.SemaphoreType.DMA((3,)),
        ],
    )
    return pl.pallas_call(
        _experts_kernel,
        grid_spec=grid_spec,
        out_shape=jax.ShapeDtypeStruct((n_slots * tm, d), BF16),
        compiler_params=pltpu.CompilerParams(dimension_semantics=("arbitrary",),
                                             vmem_limit_bytes=STAGING_VMEM_LIMIT),
        name="experts",
    )(*slots, hbuf, cbuf, wg, wu, wd)


def _finish_kernel(start_ref, cnt_ref, *refs, final_norm):
    wins = refs[:2 * MOE_GROUPS]
    pos_ref, x2_ref, gfin_ref, o_ref, sorted_scr = refs[2 * MOE_GROUPS:]
    i = pl.program_id(0)
    tm = x2_ref.shape[0]
    half_pieces = HALF_WIN // ROW_ALIGN

    @pl.when(i == 0)
    def _():
        sorted_scr[...] = jnp.zeros_like(sorted_scr)

    seg = jnp.int32(0)
    for g in range(MOE_GROUPS):
        padded = _pad_rows(cnt_ref[i, g])
        pieces = padded // ROW_ALIGN
        for half, win in enumerate(wins[2 * g:2 * g + 2]):
            base = seg + half * HALF_WIN

            def copy_piece(k, carry, win=win, base=base):
                src = pl.multiple_of(k * ROW_ALIGN, ROW_ALIGN)
                dst = pl.multiple_of(base + k * ROW_ALIGN, ROW_ALIGN)
                sorted_scr[pl.ds(dst, ROW_ALIGN), :] = win[pl.ds(src, ROW_ALIGN), :]
                return carry

            lax.fori_loop(0, jnp.clip(pieces - half * half_pieces, 0, half_pieces), copy_piece, 0)
        seg = seg + padded

    pos = pos_ref[:, 0:1]
    unperm = jnp.where(lax.broadcasted_iota(jnp.int32, (tm, SORTED_ROWS), 1).astype(F32) == pos, 1.0, 0.0)
    y = x2_ref[...] + _dot(unperm.astype(BF16), sorted_scr[...])
    if final_norm:
        y = _rmsnorm_f32(y, gfin_ref[...])
    o_ref[...] = y


def _finish(mbuf, starts, counts, pos, x2, g_final, final_norm):
    t, d = x2.shape
    tm = SORT_TILE

    def window(g, half):
        def index(i, st, ct):
            row = st[i, g] + half * HALF_WIN
            if half:
                row = jnp.where(_pad_rows(ct[i, g]) > HALF_WIN, row, 0)
            return pl.multiple_of(row, ROW_ALIGN), 0

        return pl.BlockSpec((pl.Element(HALF_WIN), pl.Element(d)), index)

    grid_spec = pltpu.PrefetchScalarGridSpec(
        num_scalar_prefetch=2,
        grid=(t // tm,),
        in_specs=[
            *[window(g, half) for g in range(MOE_GROUPS) for half in range(2)],
            pl.BlockSpec((tm, LANES), lambda i, st, ct: (i, 0)),
            pl.BlockSpec((tm, d), lambda i, st, ct: (i, 0)),
            pl.BlockSpec((1, d), lambda i, st, ct: (0, 0)),
        ],
        out_specs=pl.BlockSpec((tm, d), lambda i, st, ct: (i, 0)),
        scratch_shapes=[pltpu.VMEM((SORTED_ROWS, d), BF16)],
    )
    return pl.pallas_call(
        functools.partial(_finish_kernel, final_norm=final_norm),
        grid_spec=grid_spec,
        out_shape=jax.ShapeDtypeStruct((t, d), F32),
        compiler_params=_params("arbitrary"),
        name="finish",
    )(starts, counts, *([mbuf] * (2 * MOE_GROUPS)), pos, x2, g_final)


def kernel(x, mem, g_mix, w_in, ret_decay_fwd, ret_decay_bwd, ret_norm_gain, w_ret_o, na_rpb, w_na_o, g_mem, w_mem_kv, w_xa_o, w_out, g_ffn, w_router_group, b_router_group, w_router_expert, b_router_expert, w_exp_gate, w_exp_up, w_exp_down, g_final):
    b, s, d = x.shape
    depth = w_in.shape[0]
    t = b * s
    cos_t, sin_t = _rope_tables(s)
    row = lambda v: v.reshape(1, -1).astype(F32)
    x2d = x.reshape(t, d)
    for l in range(depth):
        proj = _inproj(x2d, row(g_mix[l]), w_in[l])
        proj3 = proj.reshape(b, s, IN_WIDTH)
        y_ret = _retention(proj3, ret_decay_fwd[l].astype(F32), ret_decay_bwd[l].astype(F32), cos_t, sin_t,
                           row(ret_norm_gain[l]))
        y_na = _neighbourhood_attention(proj3, _na_bias_table(na_rpb[l]))
        y_xa = _memory_attention(proj3, mem, row(g_mem[l]), w_mem_kv[l].astype(BF16))
        n_r = MOE_GROUPS + N_EXPERTS
        w_router = jnp.pad(jnp.concatenate([w_router_group[l], w_router_expert[l]], axis=1).astype(F32),
                           ((0, 0), (0, LANES - n_r)))
        b_router = jnp.pad(jnp.concatenate([b_router_group[l], b_router_expert[l]]).astype(F32),
                           (0, LANES - n_r)).reshape(1, LANES)
        x2, pos, starts, counts, hbuf, cbuf = _merge(
            y_ret.reshape(t, -1), y_na.reshape(t, -1), y_xa.reshape(t, -1), proj, x2d,
            w_ret_o[l].astype(BF16), w_na_o[l].astype(BF16), w_xa_o[l].astype(BF16),
            w_out[l].astype(BF16), row(g_ffn[l]), w_router, b_router)
        slots, out_starts = _expert_slots(starts, counts, t)
        mbuf = _experts(hbuf, cbuf, w_exp_gate[l], w_exp_up[l], w_exp_down[l], slots)
        x2d = _finish(mbuf, out_starts, counts, pos, x2, row(g_final), final_norm=(l == depth - 1))
    return x2d.reshape(b, s, d)
```

```python
import functools

import jax
import jax.numpy as jnp
import numpy as np
from jax import lax
from jax.experimental import pallas as pl
from jax.experimental.pallas import tpu as pltpu

D_MODEL = 1024
GRID_W = 64
N_BRANCHES = 3
RET_HEADS = 4
RET_QK_DIM = 128
RET_V_DIM = 256
RET_CHUNK = 128
ROPE_BASE = 10000.0
NA_HEADS = 8
NA_HEAD_DIM = 64
NA_ROWS = 8
NA_COLS = 16
XA_HEADS = 4
XA_HEAD_DIM = 128
MOE_GROUPS = 4
MOE_EXPERTS_PER_GROUP = 4
MOE_TOP_K = 2
MOE_D_FF = 512
RMS_EPS = 1e-6
GN_EPS = 1e-5

RET_QK_WIDTH = RET_HEADS * RET_QK_DIM
RET_V_WIDTH = RET_HEADS * RET_V_DIM
NA_WIDTH = NA_HEADS * NA_HEAD_DIM
XA_WIDTH = XA_HEADS * XA_HEAD_DIM
IN_WIDTH = 2 * RET_QK_WIDTH + 2 * RET_V_WIDTH + 3 * NA_WIDTH + XA_WIDTH + N_BRANCHES * D_MODEL

OFF_RQ = 0
OFF_RK = OFF_RQ + RET_QK_WIDTH
OFF_RV = OFF_RK + RET_QK_WIDTH
OFF_RG = OFF_RV + RET_V_WIDTH
OFF_NQ = OFF_RG + RET_V_WIDTH
OFF_NK = OFF_NQ + NA_WIDTH
OFF_NV = OFF_NK + NA_WIDTH
OFF_XQ = OFF_NV + NA_WIDTH
OFF_GATE = OFF_XQ + XA_WIDTH

N_EXPERTS = MOE_GROUPS * MOE_EXPERTS_PER_GROUP
LANES = 128
ROUTER_EXPERT_LANE0 = MOE_GROUPS

VMEM_LIMIT = 48 * 1024 * 1024
STAGING_VMEM_LIMIT = 56 * 1024 * 1024
CAST_ROWS = 64

F32 = jnp.float32
BF16 = jnp.bfloat16


def _params(*sem):
    return pltpu.CompilerParams(dimension_semantics=sem, vmem_limit_bytes=VMEM_LIMIT)


def _rmsnorm_f32(x, g):
    return x * lax.rsqrt(jnp.mean(x * x, axis=-1, keepdims=True) + RMS_EPS) * g


def _sigmoid(x):
    return 0.5 * jnp.tanh(0.5 * x) + 0.5


def _silu(x):
    h = 0.5 * x
    return h + h * jnp.tanh(h)


def _dot(a, b):
    return jnp.dot(a, b, preferred_element_type=F32)


def _dot_tn(a, b):
    return lax.dot_general(a, b, (((0,), (0,)), ((), ())), preferred_element_type=F32)


def _dot_nt(a, b):
    return lax.dot_general(a, b, (((1,), (1,)), ((), ())), preferred_element_type=F32)


INPROJ_STAGE_COLS = 1024


def _inproj_kernel(x_ref, g_ref, w_hbm, o_ref, h_scr, w_scr, stage, sems):
    i = pl.program_id(0)
    j = pl.program_id(1)
    n_col = pl.num_programs(1)
    tn = o_ref.shape[1]
    per_block = tn // INPROJ_STAGE_COLS

    def chunk_copy(col_block, k):
        cols = pl.ds(pl.multiple_of((col_block * per_block + k) * INPROJ_STAGE_COLS, INPROJ_STAGE_COLS),
                     INPROJ_STAGE_COLS)
        return pltpu.make_async_copy(w_hbm.at[:, cols], stage.at[k], sems.at[k])

    @pl.when(j == 0)
    def _():
        h_scr[...] = _rmsnorm_f32(x_ref[...], g_ref[...]).astype(BF16)

    @pl.when((i == 0) & (j == 0))
    def _():
        for k in range(per_block):
            chunk_copy(0, k).start()

    @pl.when(i == 0)
    def _():
        for k in range(per_block):
            chunk_copy(j, k).wait()

            def convert(r, carry, k=k):
                rows = pl.ds(pl.multiple_of(r * CAST_ROWS, CAST_ROWS), CAST_ROWS)
                w_scr[j, rows, k * INPROJ_STAGE_COLS:(k + 1) * INPROJ_STAGE_COLS] = stage[k, rows, :].astype(BF16)
                return carry

            lax.fori_loop(0, stage.shape[1] // CAST_ROWS, convert, 0)

        @pl.when(j + 1 < n_col)
        def _():
            for k in range(per_block):
                chunk_copy(j + 1, k).start()

    o_ref[...] = _dot(h_scr[...], w_scr[j]).astype(o_ref.dtype)


def _inproj(x2d, g, w, tm=1024, tn=2048):
    t, d = x2d.shape
    n = w.shape[1]
    per_block = tn // INPROJ_STAGE_COLS
    return pl.pallas_call(
        _inproj_kernel,
        grid=(t // tm, n // tn),
        in_specs=[
            pl.BlockSpec((tm, d), lambda i, j: (i, 0)),
            pl.BlockSpec((1, d), lambda i, j: (0, 0)),
            pl.BlockSpec(memory_space=pl.ANY),
        ],
        out_specs=pl.BlockSpec((tm, tn), lambda i, j: (i, j)),
        out_shape=jax.ShapeDtypeStruct((t, n), BF16),
        scratch_shapes=[
            pltpu.VMEM((tm, d), BF16),
            pltpu.VMEM((n // tn, d, tn), BF16),
            pltpu.VMEM((per_block, d, INPROJ_STAGE_COLS), w.dtype),
            pltpu.SemaphoreType.DMA((per_block,)),
        ],
        compiler_params=pltpu.CompilerParams(dimension_semantics=("arbitrary", "arbitrary"),
                                             vmem_limit_bytes=STAGING_VMEM_LIMIT),
        name="inproj",
    )(x2d, g, w)


def _log_sigmoid(x):
    return jnp.minimum(x, 0.0) - jnp.log1p(jnp.exp(-jnp.abs(x)))


RET_HEADS_PER_STEP = 2


def _retention_kernel(decf_ref, decb_ref, q_ref, k_ref, v_ref, rg_ref, cos_ref, sin_ref, gn_ref, o_ref, *scratch):
    qk, dv = RET_QK_DIM, RET_V_DIM
    for hh in range(RET_HEADS_PER_STEP):
        qcols = slice(hh * qk, (hh + 1) * qk)
        vcols = slice(hh * dv, (hh + 1) * dv)
        _retention_head(pl.program_id(1) * RET_HEADS_PER_STEP + hh, decf_ref, decb_ref,
                        q_ref.at[:, qcols], k_ref.at[:, qcols], v_ref.at[:, vcols], rg_ref.at[:, vcols],
                        cos_ref, sin_ref, gn_ref.at[:, vcols], o_ref.at[:, vcols], *[s.at[hh] for s in scratch])


def _retention_head(h, decf_ref, decb_ref, q_ref, k_ref, v_ref, rg_ref, cos_ref, sin_ref, gn_ref, o_ref,
                    qfs, qbs, y_scr, kvf_scr, kvb_scr, st_scr):
    c = RET_CHUNK
    seq = q_ref.shape[0]
    n_chunks = seq // c
    half = RET_QK_DIM // 2

    lgf = _log_sigmoid(jnp.full((1, 1), decf_ref[h], F32))
    lgb = _log_sigmoid(jnp.full((1, 1), decb_ref[h], F32))

    ii = lax.broadcasted_iota(jnp.int32, (c, c), 0)
    jj = lax.broadcasted_iota(jnp.int32, (c, c), 1)
    diff = (ii - jj).astype(F32)
    dmat = jnp.exp(jnp.where(diff >= 0, diff * lgf, (-diff) * lgb))
    pos = lax.broadcasted_iota(jnp.int32, (c, 1), 0).astype(F32)
    qd_f = jnp.exp((pos + 1.0) * lgf)
    kd_f = jnp.exp((c - 1.0 - pos) * lgf)
    cd_f = jnp.exp(c * lgf)
    qd_b = jnp.exp((c - pos) * lgb)
    kd_b = jnp.exp(pos * lgb)
    cd_b = jnp.exp(c * lgb)

    def intra(n):
        rows = pl.ds(n * c, c)
        cos = cos_ref[rows, :]
        sin = sin_ref[rows, :]
        q = q_ref[rows, :].astype(F32)
        q = (q * cos + pltpu.roll(q, half, 1) * sin) * (RET_QK_DIM ** -0.5)
        qfs[rows, :] = (q * qd_f).astype(BF16)
        qbs[rows, :] = (q * qd_b).astype(BF16)
        k = k_ref[rows, :].astype(F32)
        k = k * cos + pltpu.roll(k, half, 1) * sin
        s = _dot_nt(q.astype(BF16), k.astype(BF16)) * dmat
        lhs = jnp.concatenate([s.astype(BF16), (k * kd_f).T.astype(BF16), (k * kd_b).T.astype(BF16)], axis=0)
        r = _dot(lhs, v_ref[rows, :])
        y_scr[rows, :] = r[:c]
        kvf_scr[n] = r[c:2 * c]
        kvb_scr[n] = r[2 * c:]

    for n in range(n_chunks):
        intra(n)

    st_scr[...] = jnp.zeros_like(st_scr)

    def fwd(n):
        rows = pl.ds(n * c, c)
        st = st_scr[...]
        y_scr[rows, :] += _dot(qfs[rows, :], st.astype(BF16))
        st_scr[...] = st * cd_f + kvf_scr[n]

    for n in range(n_chunks):
        fwd(n)

    st_scr[...] = jnp.zeros_like(st_scr)
    gn = gn_ref[...]

    def bwd(n):
        rows = pl.ds(n * c, c)
        st = st_scr[...]
        y = y_scr[rows, :] + _dot(qbs[rows, :], st.astype(BF16))
        st_scr[...] = st * cd_b + kvb_scr[n]
        mu = jnp.mean(y, axis=-1, keepdims=True)
        yc = y - mu
        var = jnp.mean(yc * yc, axis=-1, keepdims=True)
        yn = yc * lax.rsqrt(var + GN_EPS) * gn
        o_ref[rows, :] = _silu(rg_ref[rows, :]) * yn.astype(o_ref.dtype)

    for n in reversed(range(n_chunks)):
        bwd(n)


def _retention(proj3, dec_f, dec_b, cos_t, sin_t, gn_gain):
    b, s, _ = proj3.shape
    hps = RET_HEADS_PER_STEP
    dk, dv = RET_QK_DIM, RET_V_DIM
    qb, vb = hps * dk, hps * dv
    smem = pl.BlockSpec(memory_space=pltpu.SMEM)
    return pl.pallas_call(
        _retention_kernel,
        grid=(b, RET_HEADS // hps),
        in_specs=[
            smem,
            smem,
            pl.BlockSpec((None, s, qb), lambda i, h: (i, 0, OFF_RQ // qb + h)),
            pl.BlockSpec((None, s, qb), lambda i, h: (i, 0, OFF_RK // qb + h)),
            pl.BlockSpec((None, s, vb), lambda i, h: (i, 0, OFF_RV // vb + h)),
            pl.BlockSpec((None, s, vb), lambda i, h: (i, 0, OFF_RG // vb + h)),
            pl.BlockSpec((s, dk), lambda i, h: (0, 0)),
            pl.BlockSpec((s, dk), lambda i, h: (0, 0)),
            pl.BlockSpec((1, vb), lambda i, h: (0, h)),
        ],
        out_specs=pl.BlockSpec((None, s, vb), lambda i, h: (i, 0, h)),
        out_shape=jax.ShapeDtypeStruct((b, s, RET_V_WIDTH), BF16),
        scratch_shapes=[
            pltpu.VMEM((hps, s, dk), BF16),
            pltpu.VMEM((hps, s, dk), BF16),
            pltpu.VMEM((hps, s, dv), F32),
            pltpu.VMEM((hps, s // RET_CHUNK, dk, dv), F32),
            pltpu.VMEM((hps, s // RET_CHUNK, dk, dv), F32),
            pltpu.VMEM((hps, dk, dv), F32),
        ],
        compiler_params=_params("parallel", "parallel"),
        name="retention",
    )(dec_f, dec_b, proj3, proj3, proj3, proj3, cos_t, sin_t, gn_gain)


def _rope_tables(seq):
    half = RET_QK_DIM // 2
    inv_freq = ROPE_BASE ** (-np.arange(half, dtype=np.float64) / half)
    ang = np.arange(seq, dtype=np.float64)[:, None] * inv_freq[None, :]
    cos, sin = np.cos(ang), np.sin(ang)
    return (jnp.asarray(np.concatenate([cos, cos], axis=-1), F32),
            jnp.asarray(np.concatenate([-sin, sin], axis=-1), F32))


def _na_bias_table(rpb):
    heads = rpb.shape[0]
    w = GRID_W
    cols = np.arange(w)
    col_start = np.clip(cols - NA_COLS // 2, 0, w - NA_COLS)
    col_off = cols[None, :] - col_start[:, None]
    col_mask = (col_off >= 0) & (col_off < NA_COLS)
    rel_c = np.clip(cols[None, :] - cols[:, None], -(NA_COLS - 1), NA_COLS - 1) + (NA_COLS - 1)
    onehot = (rel_c[:, :, None] == np.arange(2 * NA_COLS - 1)).astype(np.float32)
    toe = jnp.einsum('hrc,qkc->hrqk', rpb.astype(F32), onehot, precision=lax.Precision.HIGHEST)
    toe = jnp.where(col_mask[None, None], toe, -jnp.inf)
    two = jnp.concatenate([toe[:, :-1], toe[:, 1:]], axis=-1)
    two = two.reshape(heads // 2, 2, 2 * NA_ROWS - 2, w, 2 * w)
    return two.transpose(0, 2, 1, 3, 4).reshape(heads // 2, 2 * NA_ROWS - 2, 2 * w, 2 * w)


NA_ROWS_PER_STEP = 16


def _na_kernel(q_ref, k_ref, v_ref, bias_ref, o_ref, s_scr, p_scr):
    w = GRID_W
    seq = q_ref.shape[0]
    rows_n = seq // w
    kr = NA_ROWS
    first = lax.broadcasted_iota(jnp.int32, (w, LANES), 1) < NA_HEAD_DIM
    scale = NA_HEAD_DIM ** -0.5

    def window_row(r):
        return jnp.clip(r - kr // 2, 0, rows_n - kr)

    def body(it, carry):
        r0 = it * NA_ROWS_PER_STEP
        for u in range(NA_ROWS_PER_STEP):
            r = r0 + u
            rs = window_row(r)
            qr = q_ref[pl.ds(pl.multiple_of(r * w, w), w), :] * scale
            zero = jnp.zeros_like(qr)
            q2 = jnp.concatenate([jnp.where(first, qr, zero), jnp.where(first, zero, qr)], axis=0)
            kk = k_ref[pl.ds(pl.multiple_of(rs * w, w), kr * w), :]
            off = rs - r + (NA_ROWS - 1)
            bias = jnp.concatenate([bias_ref[off + 2 * i] for i in range(kr // 2)], axis=1)
            s_scr[u] = _dot_nt(q2, kk) + bias
        for u in range(NA_ROWS_PER_STEP):
            m = jnp.max(s_scr[u], axis=-1, keepdims=True)
            p_scr[u] = jnp.exp(s_scr[u] - m).astype(BF16)
        for u in range(NA_ROWS_PER_STEP):
            r = r0 + u
            vv = v_ref[pl.ds(pl.multiple_of(window_row(r) * w, w), kr * w), :]
            o2 = _dot(p_scr[u], jnp.concatenate([vv, jnp.ones_like(vv)], axis=1))
            o2 = o2[:, :LANES] / o2[:, LANES:]
            o_ref[pl.ds(pl.multiple_of(r * w, w), w), :] = jnp.where(first, o2[:w], o2[w:]).astype(o_ref.dtype)
        return carry

    lax.fori_loop(0, rows_n // NA_ROWS_PER_STEP, body, 0)


def _neighbourhood_attention(proj3, bias_tab):
    b, s, _ = proj3.shape
    pairs = NA_HEADS // 2
    return pl.pallas_call(
        _na_kernel,
        grid=(b, pairs),
        in_specs=[
            pl.BlockSpec((None, s, LANES), lambda i, p: (i, 0, OFF_NQ // LANES + p)),
            pl.BlockSpec((None, s, LANES), lambda i, p: (i, 0, OFF_NK // LANES + p)),
            pl.BlockSpec((None, s, LANES), lambda i, p: (i, 0, OFF_NV // LANES + p)),
            pl.BlockSpec((None, 2 * NA_ROWS - 2, 2 * GRID_W, 2 * GRID_W), lambda i, p: (p, 0, 0, 0)),
        ],
        out_specs=pl.BlockSpec((None, s, LANES), lambda i, p: (i, 0, p)),
        out_shape=jax.ShapeDtypeStruct((b, s, NA_WIDTH), BF16),
        scratch_shapes=[
            pltpu.VMEM((NA_ROWS_PER_STEP, 2 * GRID_W, NA_ROWS * GRID_W), F32),
            pltpu.VMEM((NA_ROWS_PER_STEP, 2 * GRID_W, NA_ROWS * GRID_W), BF16),
        ],
        compiler_params=_params("parallel", "parallel"),
        name="nbr_attn",
    )(proj3, proj3, proj3, bias_tab)


def _xa_kernel(q_ref, mem_ref, g_ref, wkv_ref, o_ref, kv_scr):
    @pl.when(pl.program_id(1) == 0)
    def _():
        mn = _rmsnorm_f32(mem_ref[...], g_ref[...]).astype(BF16)
        kv_scr[...] = _dot(mn, wkv_ref[...]).astype(BF16)

    dh = XA_HEAD_DIM
    scale = dh ** -0.5
    for h in range(XA_HEADS):
        q = q_ref[:, h * dh:(h + 1) * dh]
        k = kv_scr[:, h * dh:(h + 1) * dh]
        v = kv_scr[:, XA_WIDTH + h * dh:XA_WIDTH + (h + 1) * dh]
        s = _dot_nt(q, k) * scale
        m = jnp.max(s, axis=-1, keepdims=True)
        p = jnp.exp(s - m).astype(BF16)
        o2 = _dot(p, jnp.concatenate([v, jnp.ones_like(v)], axis=1))
        o_ref[:, h * dh:(h + 1) * dh] = (o2[:, :dh] / o2[:, dh:]).astype(o_ref.dtype)


def _memory_attention(proj3, mem, g_mem, wkv_bf16, ts=1024):
    b, s, _ = proj3.shape
    m, d = mem.shape[1], mem.shape[2]
    return pl.pallas_call(
        _xa_kernel,
        grid=(b, s // ts),
        in_specs=[
            pl.BlockSpec((None, ts, XA_WIDTH), lambda i, j: (i, j, OFF_XQ // XA_WIDTH)),
            pl.BlockSpec((None, m, d), lambda i, j: (i, 0, 0)),
            pl.BlockSpec((1, d), lambda i, j: (0, 0)),
            pl.BlockSpec((d, 2 * XA_WIDTH), lambda i, j: (0, 0)),
        ],
        out_specs=pl.BlockSpec((None, ts, XA_WIDTH), lambda i, j: (i, j, 0)),
        out_shape=jax.ShapeDtypeStruct((b, s, XA_WIDTH), BF16),
        scratch_shapes=[pltpu.VMEM((m, 2 * XA_WIDTH), BF16)],
        compiler_params=_params("parallel", "arbitrary"),
        name="mem_attn",
    )(proj3, mem, g_mem, wkv_bf16)


def _masked_lane_max(v, mask):
    return jnp.max(jnp.where(mask, v, -jnp.inf), axis=-1, keepdims=True)


def _first_lane_eq(v, target, mask, lane):
    return jnp.min(jnp.where(mask & (v == target), lane, float(LANES)), axis=-1, keepdims=True)


def _route(logits):
    g_n, e_n = MOE_GROUPS, MOE_EXPERTS_PER_GROUP
    lane = lax.broadcasted_iota(jnp.int32, logits.shape, 1).astype(F32)
    is_grp = lane < g_n
    gmax = _masked_lane_max(logits, is_grp)
    gsum = jnp.sum(jnp.where(is_grp, jnp.exp(logits - gmax), 0.0), axis=-1, keepdims=True)
    grp_w = 1.0 / gsum
    gidx = _first_lane_eq(logits, gmax, is_grp, lane)
    lo = ROUTER_EXPERT_LANE0 + gidx * e_n
    in_grp = (lane >= lo) & (lane < lo + e_n)
    emax = _masked_lane_max(logits, in_grp)
    ex = jnp.where(in_grp, jnp.exp(logits - emax), 0.0)
    prob = ex / jnp.sum(ex, axis=-1, keepdims=True)
    p1 = _masked_lane_max(prob, in_grp)
    i1 = _first_lane_eq(prob, p1, in_grp, lane)
    rest = in_grp & (lane != i1)
    p2 = _masked_lane_max(prob, rest)
    i2 = _first_lane_eq(prob, p2, rest, lane)
    tot = p1 + p2
    w1 = p1 / tot * grp_w
    w2 = p2 / tot * grp_w
    return gidx, jnp.where(lane == i1 - lo, w1, jnp.where(lane == i2 - lo, w2, 0.0))


SORT_TILE = 512
ROW_ALIGN = 16
SORTED_ROWS = 640
WIN_ROWS = SORT_TILE
HALF_WIN = WIN_ROWS // 2
SCR_ROWS = SORTED_ROWS + WIN_ROWS
EXPERT_TILE = 512


def _group_buf_rows(t):
    return t + 2 * WIN_ROWS


def _pad_rows(n):
    return (n + (ROW_ALIGN - 1)) // ROW_ALIGN * ROW_ALIGN


def _window_copies(hs, cs, hbuf, cbuf, sems, slot, kind, g, src_row, dst_row):
    return (
        pltpu.make_async_copy(hs.at[slot, pl.ds(src_row, WIN_ROWS)], hbuf.at[g, pl.ds(dst_row, WIN_ROWS)],
                              sems.at[slot, 2 * kind, g]),
        pltpu.make_async_copy(cs.at[slot, pl.ds(src_row, WIN_ROWS)], cbuf.at[g, pl.ds(dst_row, WIN_ROWS)],
                              sems.at[slot, 2 * kind + 1, g]),
    )


def _merge_kernel(yret_ref, yna_ref, yxa_ref, gr_ref, gn_ref, gx_ref, x_ref, wro_ref, wno_ref, wxo_ref, wout_ref,
                  gffn_ref, wr_ref, br_ref,
                  x2_ref, pos_ref, start_ref, cnt_ref, hbuf, cbuf,
                  hs, cs, run, sems):
    i = pl.program_id(0)
    n_tiles = pl.num_programs(0)
    slot = lax.rem(i, 2)
    tm = x_ref.shape[0]
    g_n = MOE_GROUPS

    @pl.when(i == 0)
    def _():
        for g in range(g_n):
            run[g] = 0
        hs[:, SORTED_ROWS:, :] = jnp.zeros((2, WIN_ROWS, hs.shape[2]), hs.dtype)
        cs[:, SORTED_ROWS:, :] = jnp.zeros((2, WIN_ROWS, cs.shape[2]), cs.dtype)

    y_ret = _dot(yret_ref[...], wro_ref[...])
    y_na = _dot(yna_ref[...], wno_ref[...])
    y_xa = _dot(yxa_ref[...], wxo_ref[...])
    mix = (_sigmoid(gr_ref[...].astype(F32)) * y_ret + _sigmoid(gn_ref[...].astype(F32)) * y_na
           + _sigmoid(gx_ref[...].astype(F32)) * y_xa)
    x2 = x_ref[...] + _dot(mix.astype(BF16), wout_ref[...])
    x2_ref[...] = x2
    h2 = _rmsnorm_f32(x2, gffn_ref[...])
    h_hi = h2.astype(BF16)
    h_lo = (h2 - h_hi.astype(F32)).astype(BF16)
    wr = wr_ref[...]
    wr_hi = wr.astype(BF16)
    wr_lo = (wr - wr_hi.astype(F32)).astype(BF16)
    hi_terms = _dot(h_hi, jnp.concatenate([wr_hi, wr_lo], axis=1))
    logits = hi_terms[:, :LANES] + hi_terms[:, LANES:] + _dot(h_lo, wr_hi) + br_ref[...]

    gidx, w4 = _route(logits)

    lane = lax.broadcasted_iota(jnp.int32, (tm, LANES), 1).astype(F32)
    onehot = jnp.where(lane == gidx, 1.0, 0.0)
    before = (lax.broadcasted_iota(jnp.int32, (tm, tm), 1) < lax.broadcasted_iota(jnp.int32, (tm, tm), 0))
    rank = _dot(jnp.where(before, 1.0, 0.0).astype(BF16), onehot.astype(BF16))
    count_row = rank[tm - 1:tm, :] + onehot[tm - 1:tm, :]
    lane_row = lax.broadcasted_iota(jnp.int32, (1, LANES), 1)
    counts = [jnp.sum(jnp.where(lane_row == g, count_row, 0.0)).astype(jnp.int32) for g in range(g_n)]
    seg_start = []
    acc = jnp.int32(0)
    for g in range(g_n):
        seg_start.append(acc)
        acc = acc + _pad_rows(counts[g])
    start_row = jnp.zeros((1, LANES), F32)
    for g in range(g_n):
        start_row = jnp.where(lane_row == g, seg_start[g].astype(F32), start_row)
    pos = jnp.sum(onehot * (rank + start_row), axis=-1, keepdims=True)
    pos_ref[...] = jnp.broadcast_to(pos, (tm, LANES))

    pos_lanes = jnp.transpose(jnp.broadcast_to(pos, (tm, LANES)))[0:1, :]
    perm = jnp.where(lax.broadcasted_iota(jnp.int32, (SORTED_ROWS, tm), 0).astype(F32) == pos_lanes, 1.0, 0.0)
    perm = perm.astype(BF16)
    hs[slot, 0:SORTED_ROWS, :] = _dot(perm, h_hi).astype(hs.dtype)
    e_n = MOE_EXPERTS_PER_GROUP
    w_hi = w4.astype(BF16).astype(F32)
    r1 = w4 - w_hi
    w_mid = r1.astype(BF16).astype(F32)
    w_lo = r1 - w_mid
    pieces = (w_hi + pltpu.roll(w_mid, e_n, 1) + pltpu.roll(w_lo, 2 * e_n, 1)).astype(BF16)
    sorted_pieces = _dot(perm, pieces)
    cs[slot, 0:SORTED_ROWS, :] = (sorted_pieces + pltpu.roll(sorted_pieces, LANES - e_n, 1)
                                  + pltpu.roll(sorted_pieces, LANES - 2 * e_n, 1))

    copies = functools.partial(_window_copies, hs, cs, hbuf, cbuf, sems)

    def wait_all(which_slot):
        for kind in range(2):
            for g in range(g_n):
                for cp in copies(which_slot, kind, g, 0, 0):
                    cp.wait()

    @pl.when(i > 0)
    def _():
        wait_all(1 - slot)

    @pl.when(i == 0)
    def _():
        top = hbuf.shape[1] - WIN_ROWS
        for g in range(g_n):
            for cp in copies(slot, 1, g, SORTED_ROWS, top):
                cp.start()
        for g in range(g_n):
            for cp in copies(slot, 1, g, 0, 0):
                cp.wait()

    tiles_left = n_tiles - 1 - i
    for g in range(g_n):
        c_g = run[g]
        c_next = c_g + _pad_rows(counts[g])
        start_ref[i, g] = c_g
        cnt_ref[i, g] = counts[g]
        run[g] = c_next
        for cp in copies(slot, 0, g, pl.multiple_of(seg_start[g], ROW_ALIGN), pl.multiple_of(c_g, ROW_ALIGN)):
            cp.start()
        dead = c_next + tiles_left * WIN_ROWS + WIN_ROWS
        for cp in copies(slot, 1, g, SORTED_ROWS, pl.multiple_of(dead, ROW_ALIGN)):
            cp.start()

    @pl.when(i == n_tiles - 1)
    def _():
        wait_all(slot)
        for g in range(g_n):
            for cp in copies(slot, 0, g, SORTED_ROWS, pl.multiple_of(run[g], ROW_ALIGN)):
                cp.start()
        for g in range(g_n):
            for cp in copies(slot, 0, g, 0, 0):
                cp.wait()


def _merge(y_ret, y_na, y_xa, proj, x2d, w_ret_o, w_na_o, w_xa_o, w_out, g_ffn, w_router, b_router):
    t, d = x2d.shape
    tm = SORT_TILE
    n_tiles = t // tm
    rows = _group_buf_rows(t)
    gate_blk = OFF_GATE // d
    full = lambda a: pl.BlockSpec(a.shape, lambda i: (0,) * a.ndim)
    smem = pl.BlockSpec(memory_space=pltpu.SMEM)
    hbm = pl.BlockSpec(memory_space=pl.ANY)
    return pl.pallas_call(
        _merge_kernel,
        grid=(n_tiles,),
        in_specs=[
            pl.BlockSpec((tm, RET_V_WIDTH), lambda i: (i, 0)),
            pl.BlockSpec((tm, NA_WIDTH), lambda i: (i, 0)),
            pl.BlockSpec((tm, XA_WIDTH), lambda i: (i, 0)),
            pl.BlockSpec((tm, d), lambda i: (i, gate_blk)),
            pl.BlockSpec((tm, d), lambda i: (i, gate_blk + 1)),
            pl.BlockSpec((tm, d), lambda i: (i, gate_blk + 2)),
            pl.BlockSpec((tm, d), lambda i: (i, 0)),
            full(w_ret_o), full(w_na_o), full(w_xa_o), full(w_out), full(g_ffn), full(w_router), full(b_router),
        ],
        out_specs=[
            pl.BlockSpec((tm, d), lambda i: (i, 0)),
            pl.BlockSpec((tm, LANES), lambda i: (i, 0)),
            smem, smem, hbm, hbm,
        ],
        out_shape=[
            jax.ShapeDtypeStruct((t, d), F32),
            jax.ShapeDtypeStruct((t, LANES), F32),
            jax.ShapeDtypeStruct((n_tiles, MOE_GROUPS), jnp.int32),
            jax.ShapeDtypeStruct((n_tiles, MOE_GROUPS), jnp.int32),
            jax.ShapeDtypeStruct((MOE_GROUPS, rows, d), BF16),
            jax.ShapeDtypeStruct((MOE_GROUPS, rows, LANES), F32),
        ],
        scratch_shapes=[
            pltpu.VMEM((2, SCR_ROWS, d), BF16),
            pltpu.VMEM((2, SCR_ROWS, LANES), F32),
            pltpu.SMEM((MOE_GROUPS,), jnp.int32),
            pltpu.SemaphoreType.DMA((2, 4, MOE_GROUPS)),
        ],
        compiler_params=_params("arbitrary"),
        name="merge_router",
    )(y_ret, y_na, y_xa, proj, proj, proj, x2d, w_ret_o, w_na_o, w_xa_o, w_out, g_ffn, w_router, b_router)


def _expert_slots(starts, counts, t):
    totals = starts[-1] + _pad_rows(counts[-1])
    nblk = (totals + EXPERT_TILE - 1) // EXPERT_TILE
    ends = jnp.cumsum(nblk)
    first = ends - nblk
    n_slots = (t + (t // SORT_TILE) * MOE_GROUPS * (ROW_ALIGN - 1)) // EXPERT_TILE + MOE_GROUPS + 1
    j = jnp.arange(n_slots, dtype=jnp.int32)
    valid = j < ends[-1]
    jc = jnp.clip(j, 0, jnp.maximum(ends[-1] - 1, 0))
    grp = jnp.sum((jc[:, None] >= ends[None, :]).astype(jnp.int32), axis=1)
    blk = jc - first[grp]
    out_starts = first[None, :] * EXPERT_TILE + starts
    group_first = (valid & (j == first[grp])).astype(jnp.int32)
    nxt = jnp.full((), -1, jnp.int32)
    next_of = []
    for g in reversed(range(MOE_GROUPS)):
        next_of.append(nxt)
        nxt = jnp.where(nblk[g] > 0, g, nxt)
    group_next = jnp.stack(next_of[::-1])[grp]
    return (grp, blk, valid.astype(jnp.int32), group_first, group_next), out_starts


def _experts_kernel(grp_ref, blk_ref, valid_ref, first_ref, next_ref, h_ref, c_ref, wg_hbm, wu_hbm, wd_hbm, o_ref,
                    stage_g, stage_u, stage_d, wg_ref, wu_ref, wd_ref, sems):
    j = pl.program_id(0)

    def weight_copies(g):
        return (pltpu.make_async_copy(wg_hbm.at[g], stage_g, sems.at[0]),
                pltpu.make_async_copy(wu_hbm.at[g], stage_u, sems.at[1]),
                pltpu.make_async_copy(wd_hbm.at[g], stage_d, sems.at[2]))

    @pl.when(j == 0)
    def _():
        for cp in weight_copies(grp_ref[0]):
            cp.start()

    @pl.when(first_ref[j] == 1)
    def _():
        for cp in weight_copies(grp_ref[j]):
            cp.wait()
        for stage, dst in ((stage_g, wg_ref), (stage_u, wu_ref), (stage_d, wd_ref)):
            for e in range(MOE_EXPERTS_PER_GROUP):
                def convert(i, carry, stage=stage, dst=dst, e=e):
                    rows = pl.ds(pl.multiple_of(i * CAST_ROWS, CAST_ROWS), CAST_ROWS)
                    dst[e, rows, :] = stage[e, rows, :].astype(dst.dtype)
                    return carry

                lax.fori_loop(0, stage.shape[1] // CAST_ROWS, convert, 0)

        @pl.when(next_ref[j] >= 0)
        def _():
            for cp in weight_copies(next_ref[j]):
                cp.start()

    @pl.when(valid_ref[j] == 1)
    def _():
        h = h_ref[...]
        c = c_ref[...]
        lane = lax.broadcasted_iota(jnp.int32, c.shape, 1)
        out = None
        for e in range(MOE_EXPERTS_PER_GROUP):
            a = _dot(h, wg_ref[e])
            u = _dot(h, wu_ref[e])
            cw = jnp.sum(jnp.where(lane == e, c, 0.0), axis=-1, keepdims=True)
            hid = (_silu(a) * u * cw).astype(BF16)
            part = _dot(hid, wd_ref[e])
            out = part if out is None else out + part
        o_ref[...] = out.astype(o_ref.dtype)

    @pl.when(valid_ref[j] == 0)
    def _():
        o_ref[...] = jnp.zeros_like(o_ref)


def _experts(hbuf, cbuf, wg, wu, wd, slots):
    g_n, rows, d = hbuf.shape
    e_n, f = wg.shape[1], wg.shape[3]
    tm = EXPERT_TILE
    n_slots = slots[0].shape[0]
    hbm = pl.BlockSpec(memory_space=pl.ANY)
    grid_spec = pltpu.PrefetchScalarGridSpec(
        num_scalar_prefetch=5,
        grid=(n_slots,),
        in_specs=[
            pl.BlockSpec((None, tm, d), lambda j, grp, blk, v, gf, gx: (grp[j], blk[j], 0)),
            pl.BlockSpec((None, tm, LANES), lambda j, grp, blk, v, gf, gx: (grp[j], blk[j], 0)),
            hbm, hbm, hbm,
        ],
        out_specs=pl.BlockSpec((tm, d), lambda j, grp, blk, v, gf, gx: (j, 0)),
        scratch_shapes=[
            pltpu.VMEM((e_n, d, f), wg.dtype),
            pltpu.VMEM((e_n, d, f), wu.dtype),
            pltpu.VMEM((e_n, f, d), wd.dtype),
            pltpu.VMEM((e_n, d, f), BF16),
            pltpu.VMEM((e_n, d, f), BF16),
            pltpu.VMEM((e_n, f, d), BF16),
            pltpu.SemaphoreType.DMA((3,)),
        ],
    )
    return pl.pallas_call(
        _experts_kernel,
        grid_spec=grid_spec,
        out_shape=jax.ShapeDtypeStruct((n_slots * tm, d), BF16),
        compiler_params=pltpu.CompilerParams(dimension_semantics=("arbitrary",),
                                             vmem_limit_bytes=STAGING_VMEM_LIMIT),
        name="experts",
    )(*slots, hbuf, cbuf, wg, wu, wd)


def _finish_kernel(start_ref, cnt_ref, *refs, final_norm):
    wins = refs[:2 * MOE_GROUPS]
    pos_ref, x2_ref, gfin_ref, o_ref, sorted_scr = refs[2 * MOE_GROUPS:]
    i = pl.program_id(0)
    tm = x2_ref.shape[0]
    half_pieces = HALF_WIN // ROW_ALIGN

    @pl.when(i == 0)
    def _():
        sorted_scr[...] = jnp.zeros_like(sorted_scr)

    seg = jnp.int32(0)
    for g in range(MOE_GROUPS):
        padded = _pad_rows(cnt_ref[i, g])
        pieces = padded // ROW_ALIGN
        for half, win in enumerate(wins[2 * g:2 * g + 2]):
            base = seg + half * HALF_WIN

            def copy_piece(k, carry, win=win, base=base):
                src = pl.multiple_of(k * ROW_ALIGN, ROW_ALIGN)
                dst = pl.multiple_of(base + k * ROW_ALIGN, ROW_ALIGN)
                sorted_scr[pl.ds(dst, ROW_ALIGN), :] = win[pl.ds(src, ROW_ALIGN), :]
                return carry

            lax.fori_loop(0, jnp.clip(pieces - half * half_pieces, 0, half_pieces), copy_piece, 0)
        seg = seg + padded

    pos = pos_ref[:, 0:1]
    unperm = jnp.where(lax.broadcasted_iota(jnp.int32, (tm, SORTED_ROWS), 1).astype(F32) == pos, 1.0, 0.0)
    y = x2_ref[...] + _dot(unperm.astype(BF16), sorted_scr[...])
    if final_norm:
        y = _rmsnorm_f32(y, gfin_ref[...])
    o_ref[...] = y


def _finish(mbuf, starts, counts, pos, x2, g_final, final_norm):
    t, d = x2.shape
    tm = SORT_TILE

    def window(g, half):
        def index(i, st, ct):
            row = st[i, g] + half * HALF_WIN
            if half:
                row = jnp.where(_pad_rows(ct[i, g]) > HALF_WIN, row, 0)
            return pl.multiple_of(row, ROW_ALIGN), 0

        return pl.BlockSpec((pl.Element(HALF_WIN), pl.Element(d)), index)

    grid_spec = pltpu.PrefetchScalarGridSpec(
        num_scalar_prefetch=2,
        grid=(t // tm,),
        in_specs=[
            *[window(g, half) for g in range(MOE_GROUPS) for half in range(2)],
            pl.BlockSpec((tm, LANES), lambda i, st, ct: (i, 0)),
            pl.BlockSpec((tm, d), lambda i, st, ct: (i, 0)),
            pl.BlockSpec((1, d), lambda i, st, ct: (0, 0)),
        ],
        out_specs=pl.BlockSpec((tm, d), lambda i, st, ct: (i, 0)),
        scratch_shapes=[pltpu.VMEM((SORTED_ROWS, d), BF16)],
    )
    return pl.pallas_call(
        functools.partial(_finish_kernel, final_norm=final_norm),
        grid_spec=grid_spec,
        out_shape=jax.ShapeDtypeStruct((t, d), F32),
        compiler_params=_params("arbitrary"),
        name="finish",
    )(starts, counts, *([mbuf] * (2 * MOE_GROUPS)), pos, x2, g_final)


def kernel(x, mem, g_mix, w_in, ret_decay_fwd, ret_decay_bwd, ret_norm_gain, w_ret_o, na_rpb, w_na_o, g_mem, w_mem_kv, w_xa_o, w_out, g_ffn, w_router_group, b_router_group, w_router_expert, b_router_expert, w_exp_gate, w_exp_up, w_exp_down, g_final):
    b, s, d = x.shape
    depth = w_in.shape[0]
    t = b * s
    cos_t, sin_t = _rope_tables(s)
    row = lambda v: v.reshape(1, -1).astype(F32)
    x2d = x.reshape(t, d)
    for l in range(depth):
        proj = _inproj(x2d, row(g_mix[l]), w_in[l])
        proj3 = proj.reshape(b, s, IN_WIDTH)
        y_ret = _retention(proj3, ret_decay_fwd[l].astype(F32), ret_decay_bwd[l].astype(F32), cos_t, sin_t,
                           row(ret_norm_gain[l]))
        y_na = _neighbourhood_attention(proj3, _na_bias_table(na_rpb[l]))
        y_xa = _memory_attention(proj3, mem, row(g_mem[l]), w_mem_kv[l].astype(BF16))
        n_r = MOE_GROUPS + N_EXPERTS
        w_router = jnp.pad(jnp.concatenate([w_router_group[l], w_router_expert[l]], axis=1).astype(F32),
                           ((0, 0), (0, LANES - n_r)))
        b_router = jnp.pad(jnp.concatenate([b_router_group[l], b_router_expert[l]]).astype(F32),
                           (0, LANES - n_r)).reshape(1, LANES)
        x2, pos, starts, counts, hbuf, cbuf = _merge(
            y_ret.reshape(t, -1), y_na.reshape(t, -1), y_xa.reshape(t, -1), proj, x2d,
            w_ret_o[l].astype(BF16), w_na_o[l].astype(BF16), w_xa_o[l].astype(BF16),
            w_out[l].astype(BF16), row(g_ffn[l]), w_router, b_router)
        slots, out_starts = _expert_slots(starts, counts, t)
        mbuf = _experts(hbuf, cbuf, w_exp_gate[l], w_exp_up[l], w_exp_down[l], slots)
        x2d = _finish(mbuf, out_starts, counts, pos, x2, row(g_final), final_norm=(l == depth - 1))
    return x2d.reshape(b, s, d)
```

```python
import functools

import jax
import jax.numpy as jnp
import numpy as np
from jax import lax
from jax.experimental import pallas as pl
from jax.experimental.pallas import tpu as pltpu

D_MODEL = 1024
GRID_W = 64
N_BRANCHES = 3
RET_HEADS = 4
RET_QK_DIM = 128
RET_V_DIM = 256
RET_CHUNK = 128
ROPE_BASE = 10000.0
NA_HEADS = 8
NA_HEAD_DIM = 64
NA_ROWS = 8
NA_COLS = 16
XA_HEADS = 4
XA_HEAD_DIM = 128
MOE_GROUPS = 4
MOE_EXPERTS_PER_GROUP = 4
MOE_TOP_K = 2
MOE_D_FF = 512
RMS_EPS = 1e-6
GN_EPS = 1e-5

RET_QK_WIDTH = RET_HEADS * RET_QK_DIM
RET_V_WIDTH = RET_HEADS * RET_V_DIM
NA_WIDTH = NA_HEADS * NA_HEAD_DIM
XA_WIDTH = XA_HEADS * XA_HEAD_DIM
IN_WIDTH = 2 * RET_QK_WIDTH + 2 * RET_V_WIDTH + 3 * NA_WIDTH + XA_WIDTH + N_BRANCHES * D_MODEL

OFF_RQ = 0
OFF_RK = OFF_RQ + RET_QK_WIDTH
OFF_RV = OFF_RK + RET_QK_WIDTH
OFF_RG = OFF_RV + RET_V_WIDTH
OFF_NQ = OFF_RG + RET_V_WIDTH
OFF_NK = OFF_NQ + NA_WIDTH
OFF_NV = OFF_NK + NA_WIDTH
OFF_XQ = OFF_NV + NA_WIDTH
OFF_GATE = OFF_XQ + XA_WIDTH

N_EXPERTS = MOE_GROUPS * MOE_EXPERTS_PER_GROUP
LANES = 128
ROUTER_EXPERT_LANE0 = MOE_GROUPS

VMEM_LIMIT = 48 * 1024 * 1024
STAGING_VMEM_LIMIT = 56 * 1024 * 1024
CAST_ROWS = 64

F32 = jnp.float32
BF16 = jnp.bfloat16


def _params(*sem):
    return pltpu.CompilerParams(dimension_semantics=sem, vmem_limit_bytes=VMEM_LIMIT)


def _rmsnorm_f32(x, g):
    return x * lax.rsqrt(jnp.mean(x * x, axis=-1, keepdims=True) + RMS_EPS) * g


def _sigmoid(x):
    return 0.5 * jnp.tanh(0.5 * x) + 0.5


def _silu(x):
    h = 0.5 * x
    return h + h * jnp.tanh(h)


def _dot(a, b):
    return jnp.dot(a, b, preferred_element_type=F32)


def _dot_tn(a, b):
    return lax.dot_general(a, b, (((0,), (0,)), ((), ())), preferred_element_type=F32)


def _dot_nt(a, b):
    return lax.dot_general(a, b, (((1,), (1,)), ((), ())), preferred_element_type=F32)


INPROJ_STAGE_COLS = 1024


def _inproj_kernel(x_ref, g_ref, w_hbm, o_ref, h_scr, w_scr, stage, sems):
    i = pl.program_id(0)
    j = pl.program_id(1)
    n_col = pl.num_programs(1)
    tn = o_ref.shape[1]
    per_block = tn // INPROJ_STAGE_COLS

    def chunk_copy(col_block, k):
        cols = pl.ds(pl.multiple_of((col_block * per_block + k) * INPROJ_STAGE_COLS, INPROJ_STAGE_COLS),
                     INPROJ_STAGE_COLS)
        return pltpu.make_async_copy(w_hbm.at[:, cols], stage.at[k], sems.at[k])

    @pl.when(j == 0)
    def _():
        h_scr[...] = _rmsnorm_f32(x_ref[...], g_ref[...]).astype(BF16)

    @pl.when((i == 0) & (j == 0))
    def _():
        for k in range(per_block):
            chunk_copy(0, k).start()

    @pl.when(i == 0)
    def _():
        for k in range(per_block):
            chunk_copy(j, k).wait()

            def convert(r, carry, k=k):
                rows = pl.ds(pl.multiple_of(r * CAST_ROWS, CAST_ROWS), CAST_ROWS)
                w_scr[j, rows, k * INPROJ_STAGE_COLS:(k + 1) * INPROJ_STAGE_COLS] = stage[k, rows, :].astype(BF16)
                return carry

            lax.fori_loop(0, stage.shape[1] // CAST_ROWS, convert, 0)

        @pl.when(j + 1 < n_col)
        def _():
            for k in range(per_block):
                chunk_copy(j + 1, k).start()

    o_ref[...] = _dot(h_scr[...], w_scr[j]).astype(o_ref.dtype)


def _inproj(x2d, g, w, tm=1024, tn=2048):
    t, d = x2d.shape
    n = w.shape[1]
    per_block = tn // INPROJ_STAGE_COLS
    return pl.pallas_call(
        _inproj_kernel,
        grid=(t // tm, n // tn),
        in_specs=[
            pl.BlockSpec((tm, d), lambda i, j: (i, 0)),
            pl.BlockSpec((1, d), lambda i, j: (0, 0)),
            pl.BlockSpec(memory_space=pl.ANY),
        ],
        out_specs=pl.BlockSpec((tm, tn), lambda i, j: (i, j)),
        out_shape=jax.ShapeDtypeStruct((t, n), BF16),
        scratch_shapes=[
            pltpu.VMEM((tm, d), BF16),
            pltpu.VMEM((n // tn, d, tn), BF16),
            pltpu.VMEM((per_block, d, INPROJ_STAGE_COLS), w.dtype),
            pltpu.SemaphoreType.DMA((per_block,)),
        ],
        compiler_params=pltpu.CompilerParams(dimension_semantics=("arbitrary", "arbitrary"),
                                             vmem_limit_bytes=STAGING_VMEM_LIMIT),
        name="inproj",
    )(x2d, g, w)


def _log_sigmoid(x):
    return jnp.minimum(x, 0.0) - jnp.log1p(jnp.exp(-jnp.abs(x)))


RET_HEADS_PER_STEP = 2


def _retention_kernel(decf_ref, decb_ref, q_ref, k_ref, v_ref, rg_ref, cos_ref, sin_ref, gn_ref, o_ref, *scratch):
    qk, dv = RET_QK_DIM, RET_V_DIM
    for hh in range(RET_HEADS_PER_STEP):
        qcols = slice(hh * qk, (hh + 1) * qk)
        vcols = slice(hh * dv, (hh + 1) * dv)
        _retention_head(pl.program_id(1) * RET_HEADS_PER_STEP + hh, decf_ref, decb_ref,
                        q_ref.at[:, qcols], k_ref.at[:, qcols], v_ref.at[:, vcols], rg_ref.at[:, vcols],
                        cos_ref, sin_ref, gn_ref.at[:, vcols], o_ref.at[:, vcols], *[s.at[hh] for s in scratch])


def _retention_head(h, decf_ref, decb_ref, q_ref, k_ref, v_ref, rg_ref, cos_ref, sin_ref, gn_ref, o_ref,
                    qfs, qbs, y_scr, kvf_scr, kvb_scr, st_scr):
    c = RET_CHUNK
    seq = q_ref.shape[0]
    n_chunks = seq // c
    half = RET_QK_DIM // 2

    lgf = _log_sigmoid(jnp.full((1, 1), decf_ref[h], F32))
    lgb = _log_sigmoid(jnp.full((1, 1), decb_ref[h], F32))

    ii = lax.broadcasted_iota(jnp.int32, (c, c), 0)
    jj = lax.broadcasted_iota(jnp.int32, (c, c), 1)
    diff = (ii - jj).astype(F32)
    dmat = jnp.exp(jnp.where(diff >= 0, diff * lgf, (-diff) * lgb))
    pos = lax.broadcasted_iota(jnp.int32, (c, 1), 0).astype(F32)
    qd_f = jnp.exp((pos + 1.0) * lgf)
    kd_f = jnp.exp((c - 1.0 - pos) * lgf)
    cd_f = jnp.exp(c * lgf)
    qd_b = jnp.exp((c - pos) * lgb)
    kd_b = jnp.exp(pos * lgb)
    cd_b = jnp.exp(c * lgb)

    def intra(n):
        rows = pl.ds(n * c, c)
        cos = cos_ref[rows, :]
        sin = sin_ref[rows, :]
        q = q_ref[rows, :].astype(F32)
        q = (q * cos + pltpu.roll(q, half, 1) * sin) * (RET_QK_DIM ** -0.5)
        qfs[rows, :] = (q * qd_f).astype(BF16)
        qbs[rows, :] = (q * qd_b).astype(BF16)
        k = k_ref[rows, :].astype(F32)
        k = k * cos + pltpu.roll(k, half, 1) * sin
        s = _dot_nt(q.astype(BF16), k.astype(BF16)) * dmat
        lhs = jnp.concatenate([s.astype(BF16), (k * kd_f).T.astype(BF16), (k * kd_b).T.astype(BF16)], axis=0)
        r = _dot(lhs, v_ref[rows, :])
        y_scr[rows, :] = r[:c]
        kvf_scr[n] = r[c:2 * c]
        kvb_scr[n] = r[2 * c:]

    for n in range(n_chunks):
        intra(n)

    st_scr[...] = jnp.zeros_like(st_scr)

    def fwd(n):
        rows = pl.ds(n * c, c)
        st = st_scr[...]
        y_scr[rows, :] += _dot(qfs[rows, :], st.astype(BF16))
        st_scr[...] = st * cd_f + kvf_scr[n]

    for n in range(n_chunks):
        fwd(n)

    st_scr[...] = jnp.zeros_like(st_scr)
    gn = gn_ref[...]

    def bwd(n):
        rows = pl.ds(n * c, c)
        st = st_scr[...]
        y = y_scr[rows, :] + _dot(qbs[rows, :], st.astype(BF16))
        st_scr[...] = st * cd_b + kvb_scr[n]
        mu = jnp.mean(y, axis=-1, keepdims=True)
        yc = y - mu
        var = jnp.mean(yc * yc, axis=-1, keepdims=True)
        yn = yc * lax.rsqrt(var + GN_EPS) * gn
        o_ref[rows, :] = _silu(rg_ref[rows, :]) * yn.astype(o_ref.dtype)

    for n in reversed(range(n_chunks)):
        bwd(n)


def _retention(proj3, dec_f, dec_b, cos_t, sin_t, gn_gain):
    b, s, _ = proj3.shape
    hps = RET_HEADS_PER_STEP
    dk, dv = RET_QK_DIM, RET_V_DIM
    qb, vb = hps * dk, hps * dv
    smem = pl.BlockSpec(memory_space=pltpu.SMEM)
    return pl.pallas_call(
        _retention_kernel,
        grid=(b, RET_HEADS // hps),
        in_specs=[
            smem,
            smem,
            pl.BlockSpec((None, s, qb), lambda i, h: (i, 0, OFF_RQ // qb + h)),
            pl.BlockSpec((None, s, qb), lambda i, h: (i, 0, OFF_RK // qb + h)),
            pl.BlockSpec((None, s, vb), lambda i, h: (i, 0, OFF_RV // vb + h)),
            pl.BlockSpec((None, s, vb), lambda i, h: (i, 0, OFF_RG // vb + h)),
            pl.BlockSpec((s, dk), lambda i, h: (0, 0)),
            pl.BlockSpec((s, dk), lambda i, h: (0, 0)),
            pl.BlockSpec((1, vb), lambda i, h: (0, h)),
        ],
        out_specs=pl.BlockSpec((None, s, vb), lambda i, h: (i, 0, h)),
        out_shape=jax.ShapeDtypeStruct((b, s, RET_V_WIDTH), BF16),
        scratch_shapes=[
            pltpu.VMEM((hps, s, dk), BF16),
            pltpu.VMEM((hps, s, dk), BF16),
            pltpu.VMEM((hps, s, dv), F32),
            pltpu.VMEM((hps, s // RET_CHUNK, dk, dv), F32),
            pltpu.VMEM((hps, s // RET_CHUNK, dk, dv), F32),
            pltpu.VMEM((hps, dk, dv), F32),
        ],
        compiler_params=_params("parallel", "parallel"),
        name="retention",
    )(dec_f, dec_b, proj3, proj3, proj3, proj3, cos_t, sin_t, gn_gain)


def _rope_tables(seq):
    half = RET_QK_DIM // 2
    inv_freq = ROPE_BASE ** (-np.arange(half, dtype=np.float64) / half)
    ang = np.arange(seq, dtype=np.float64)[:, None] * inv_freq[None, :]
    cos, sin = np.cos(ang), np.sin(ang)
    return (jnp.asarray(np.concatenate([cos, cos], axis=-1), F32),
            jnp.asarray(np.concatenate([-sin, sin], axis=-1), F32))


def _na_bias_table(rpb):
    heads = rpb.shape[0]
    w = GRID_W
    cols = np.arange(w)
    col_start = np.clip(cols - NA_COLS // 2, 0, w - NA_COLS)
    col_off = cols[None, :] - col_start[:, None]
    col_mask = (col_off >= 0) & (col_off < NA_COLS)
    rel_c = np.clip(cols[None, :] - cols[:, None], -(NA_COLS - 1), NA_COLS - 1) + (NA_COLS - 1)
    onehot = (rel_c[:, :, None] == np.arange(2 * NA_COLS - 1)).astype(np.float32)
    toe = jnp.einsum('hrc,qkc->hrqk', rpb.astype(F32), onehot, precision=lax.Precision.HIGHEST)
    toe = jnp.where(col_mask[None, None], toe, -jnp.inf)
    two = jnp.concatenate([toe[:, :-1], toe[:, 1:]], axis=-1)
    two = two.reshape(heads // 2, 2, 2 * NA_ROWS - 2, w, 2 * w)
    return two.transpose(0, 2, 1, 3, 4).reshape(heads // 2, 2 * NA_ROWS - 2, 2 * w, 2 * w)


NA_ROWS_PER_STEP = 16


def _na_kernel(q_ref, k_ref, v_ref, bias_ref, o_ref, s_scr, p_scr):
    w = GRID_W
    seq = q_ref.shape[0]
    rows_n = seq // w
    kr = NA_ROWS
    first = lax.broadcasted_iota(jnp.int32, (w, LANES), 1) < NA_HEAD_DIM
    scale = NA_HEAD_DIM ** -0.5

    def window_row(r):
        return jnp.clip(r - kr // 2, 0, rows_n - kr)

    def body(it, carry):
        r0 = it * NA_ROWS_PER_STEP
        for u in range(NA_ROWS_PER_STEP):
            r = r0 + u
            rs = window_row(r)
            qr = q_ref[pl.ds(pl.multiple_of(r * w, w), w), :] * scale
            zero = jnp.zeros_like(qr)
            q2 = jnp.concatenate([jnp.where(first, qr, zero), jnp.where(first, zero, qr)], axis=0)
            kk = k_ref[pl.ds(pl.multiple_of(rs * w, w), kr * w), :]
            off = rs - r + (NA_ROWS - 1)
            bias = jnp.concatenate([bias_ref[off + 2 * i] for i in range(kr // 2)], axis=1)
            s_scr[u] = _dot_nt(q2, kk) + bias
        for u in range(NA_ROWS_PER_STEP):
            m = jnp.max(s_scr[u], axis=-1, keepdims=True)
            p_scr[u] = jnp.exp(s_scr[u] - m).astype(BF16)
        for u in range(NA_ROWS_PER_STEP):
            r = r0 + u
            vv = v_ref[pl.ds(pl.multiple_of(window_row(r) * w, w), kr * w), :]
            o2 = _dot(p_scr[u], jnp.concatenate([vv, jnp.ones_like(vv)], axis=1))
            o2 = o2[:, :LANES] / o2[:, LANES:]
            o_ref[pl.ds(pl.multiple_of(r * w, w), w), :] = jnp.where(first, o2[:w], o2[w:]).astype(o_ref.dtype)
        return carry

    lax.fori_loop(0, rows_n // NA_ROWS_PER_STEP, body, 0)


def _neighbourhood_attention(proj3, bias_tab):
    b, s, _ = proj3.shape
    pairs = NA_HEADS // 2
    return pl.pallas_call(
        _na_kernel,
        grid=(b, pairs),
        in_specs=[
            pl.BlockSpec((None, s, LANES), lambda i, p: (i, 0, OFF_NQ // LANES + p)),
            pl.BlockSpec((None, s, LANES), lambda i, p: (i, 0, OFF_NK // LANES + p)),
            pl.BlockSpec((None, s, LANES), lambda i, p: (i, 0, OFF_NV // LANES + p)),
            pl.BlockSpec((None, 2 * NA_ROWS - 2, 2 * GRID_W, 2 * GRID_W), lambda i, p: (p, 0, 0, 0)),
        ],
        out_specs=pl.BlockSpec((None, s, LANES), lambda i, p: (i, 0, p)),
        out_shape=jax.ShapeDtypeStruct((b, s, NA_WIDTH), BF16),
        scratch_shapes=[
            pltpu.VMEM((NA_ROWS_PER_STEP, 2 * GRID_W, NA_ROWS * GRID_W), F32),
            pltpu.VMEM((NA_ROWS_PER_STEP, 2 * GRID_W, NA_ROWS * GRID_W), BF16),
        ],
        compiler_params=_params("parallel", "parallel"),
        name="nbr_attn",
    )(proj3, proj3, proj3, bias_tab)


def _xa_kernel(q_ref, mem_ref, g_ref, wkv_ref, o_ref, kv_scr):
    @pl.when(pl.program_id(1) == 0)
    def _():
        mn = _rmsnorm_f32(mem_ref[...], g_ref[...]).astype(BF16)
        kv_scr[...] = _dot(mn, wkv_ref[...]).astype(BF16)

    dh = XA_HEAD_DIM
    scale = dh ** -0.5
    for h in range(XA_HEADS):
        q = q_ref[:, h * dh:(h + 1) * dh]
        k = kv_scr[:, h * dh:(h + 1) * dh]
        v = kv_scr[:, XA_WIDTH + h * dh:XA_WIDTH + (h + 1) * dh]
        s = _dot_nt(q, k) * scale
        m = jnp.max(s, axis=-1, keepdims=True)
        p = jnp.exp(s - m).astype(BF16)
        o2 = _dot(p, jnp.concatenate([v, jnp.ones_like(v)], axis=1))
        o_ref[:, h * dh:(h + 1) * dh] = (o2[:, :dh] / o2[:, dh:]).astype(o_ref.dtype)


def _memory_attention(proj3, mem, g_mem, wkv_bf16, ts=1024):
    b, s, _ = proj3.shape
    m, d = mem.shape[1], mem.shape[2]
    return pl.pallas_call(
        _xa_kernel,
        grid=(b, s // ts),
        in_specs=[
            pl.BlockSpec((None, ts, XA_WIDTH), lambda i, j: (i, j, OFF_XQ // XA_WIDTH)),
            pl.BlockSpec((None, m, d), lambda i, j: (i, 0, 0)),
            pl.BlockSpec((1, d), lambda i, j: (0, 0)),
            pl.BlockSpec((d, 2 * XA_WIDTH), lambda i, j: (0, 0)),
        ],
        out_specs=pl.BlockSpec((None, ts, XA_WIDTH), lambda i, j: (i, j, 0)),
        out_shape=jax.ShapeDtypeStruct((b, s, XA_WIDTH), BF16),
        scratch_shapes=[pltpu.VMEM((m, 2 * XA_WIDTH), BF16)],
        compiler_params=_params("parallel", "arbitrary"),
        name="mem_attn",
    )(proj3, mem, g_mem, wkv_bf16)


def _masked_lane_max(v, mask):
    return jnp.max(jnp.where(mask, v, -jnp.inf), axis=-1, keepdims=True)


def _first_lane_eq(v, target, mask, lane):
    return jnp.min(jnp.where(mask & (v == target), lane, float(LANES)), axis=-1, keepdims=True)


def _route(logits):
    g_n, e_n = MOE_GROUPS, MOE_EXPERTS_PER_GROUP
    lane = lax.broadcasted_iota(jnp.int32, logits.shape, 1).astype(F32)
    is_grp = lane < g_n
    gmax = _masked_lane_max(logits, is_grp)
    gsum = jnp.sum(jnp.where(is_grp, jnp.exp(logits - gmax), 0.0), axis=-1, keepdims=True)
    grp_w = 1.0 / gsum
    gidx = _first_lane_eq(logits, gmax, is_grp, lane)
    lo = ROUTER_EXPERT_LANE0 + gidx * e_n
    in_grp = (lane >= lo) & (lane < lo + e_n)
    emax = _masked_lane_max(logits, in_grp)
    ex = jnp.where(in_grp, jnp.exp(logits - emax), 0.0)
    prob = ex / jnp.sum(ex, axis=-1, keepdims=True)
    p1 = _masked_lane_max(prob, in_grp)
    i1 = _first_lane_eq(prob, p1, in_grp, lane)
    rest = in_grp & (lane != i1)
    p2 = _masked_lane_max(prob, rest)
    i2 = _first_lane_eq(prob, p2, rest, lane)
    tot = p1 + p2
    w1 = p1 / tot * grp_w
    w2 = p2 / tot * grp_w
    return gidx, jnp.where(lane == i1 - lo, w1, jnp.where(lane == i2 - lo, w2, 0.0))


SORT_TILE = 512
ROW_ALIGN = 16
SORTED_ROWS = 640
WIN_ROWS = SORT_TILE
HALF_WIN = WIN_ROWS // 2
SCR_ROWS = SORTED_ROWS + WIN_ROWS
EXPERT_TILE = 512


def _group_buf_rows(t):
    return t + 2 * WIN_ROWS


def _pad_rows(n):
    return (n + (ROW_ALIGN - 1)) // ROW_ALIGN * ROW_ALIGN


def _window_copies(hs, cs, hbuf, cbuf, sems, slot, kind, g, src_row, dst_row):
    return (
        pltpu.make_async_copy(hs.at[slot, pl.ds(src_row, WIN_ROWS)], hbuf.at[g, pl.ds(dst_row, WIN_ROWS)],
                              sems.at[slot, 2 * kind, g]),
        pltpu.make_async_copy(cs.at[slot, pl.ds(src_row, WIN_ROWS)], cbuf.at[g, pl.ds(dst_row, WIN_ROWS)],
                              sems.at[slot, 2 * kind + 1, g]),
    )


def _merge_kernel(yret_ref, yna_ref, yxa_ref, gr_ref, gn_ref, gx_ref, x_ref, wro_ref, wno_ref, wxo_ref, wout_ref,
                  gffn_ref, wr_ref, br_ref, before_ref,
                  x2_ref, pos_ref, cnt_ref, out_start_ref,
                  slot_grp, slot_blk, slot_valid, slot_first, slot_next, hbuf, cbuf,
                  hs, cs, run, start_ref, sems):
    i = pl.program_id(0)
    n_tiles = pl.num_programs(0)
    slot = lax.rem(i, 2)
    tm = x_ref.shape[0]
    g_n = MOE_GROUPS

    @pl.when(i == 0)
    def _():
        for g in range(g_n):
            run[g] = 0
        hs[:, SORTED_ROWS:, :] = jnp.zeros((2, WIN_ROWS, hs.shape[2]), hs.dtype)
        cs[:, SORTED_ROWS:, :] = jnp.zeros((2, WIN_ROWS, cs.shape[2]), cs.dtype)

    y_ret = _dot(yret_ref[...], wro_ref[...])
    y_na = _dot(yna_ref[...], wno_ref[...])
    y_xa = _dot(yxa_ref[...], wxo_ref[...])
    mix = (_sigmoid(gr_ref[...].astype(F32)) * y_ret + _sigmoid(gn_ref[...].astype(F32)) * y_na
           + _sigmoid(gx_ref[...].astype(F32)) * y_xa)
    x2 = x_ref[...] + _dot(mix.astype(BF16), wout_ref[...])
    x2_ref[...] = x2
    h2 = _rmsnorm_f32(x2, gffn_ref[...])
    h_hi = h2.astype(BF16)
    h_lo = (h2 - h_hi.astype(F32)).astype(BF16)
    wr = wr_ref[...]
    wr_hi = wr.astype(BF16)
    wr_lo = (wr - wr_hi.astype(F32)).astype(BF16)
    hi_terms = _dot(h_hi, jnp.concatenate([wr_hi, wr_lo], axis=1))
    logits = hi_terms[:, :LANES] + hi_terms[:, LANES:] + _dot(h_lo, wr_hi) + br_ref[...]

    gidx, w4 = _route(logits)

    lane = lax.broadcasted_iota(jnp.int32, (tm, LANES), 1).astype(F32)
    onehot = jnp.where(lane == gidx, 1.0, 0.0)
    rank = _dot(before_ref[...], onehot.astype(BF16))
    count_row = rank[tm - 1:tm, :] + onehot[tm - 1:tm, :]
    lane_row = lax.broadcasted_iota(jnp.int32, (1, LANES), 1)
    counts = [jnp.sum(jnp.where(lane_row == g, count_row, 0.0)).astype(jnp.int32) for g in range(g_n)]
    seg_start = []
    acc = jnp.int32(0)
    for g in range(g_n):
        seg_start.append(acc)
        acc = acc + _pad_rows(counts[g])
    start_row = jnp.zeros((1, LANES), F32)
    for g in range(g_n):
        start_row = jnp.where(lane_row == g, seg_start[g].astype(F32), start_row)
    pos = jnp.sum(onehot * (rank + start_row), axis=-1, keepdims=True)
    pos_ref[...] = jnp.broadcast_to(pos, (tm, LANES))

    pos_lanes = jnp.transpose(jnp.broadcast_to(pos, (tm, LANES)))[0:1, :]
    perm = jnp.where(lax.broadcasted_iota(jnp.int32, (SORTED_ROWS, tm), 0).astype(F32) == pos_lanes, 1.0, 0.0)
    perm = perm.astype(BF16)
    hs[slot, 0:SORTED_ROWS, :] = _dot(perm, h_hi).astype(hs.dtype)
    e_n = MOE_EXPERTS_PER_GROUP
    w_hi = w4.astype(BF16).astype(F32)
    r1 = w4 - w_hi
    w_mid = r1.astype(BF16).astype(F32)
    w_lo = r1 - w_mid
    pieces = (w_hi + pltpu.roll(w_mid, e_n, 1) + pltpu.roll(w_lo, 2 * e_n, 1)).astype(BF16)
    sorted_pieces = _dot(perm, pieces)
    cs[slot, 0:SORTED_ROWS, :] = (sorted_pieces + pltpu.roll(sorted_pieces, LANES - e_n, 1)
                                  + pltpu.roll(sorted_pieces, LANES - 2 * e_n, 1))

    copies = functools.partial(_window_copies, hs, cs, hbuf, cbuf, sems)

    def wait_all(which_slot):
        for kind in range(2):
            for g in range(g_n):
                for cp in copies(which_slot, kind, g, 0, 0):
                    cp.wait()

    @pl.when(i > 0)
    def _():
        wait_all(1 - slot)

    @pl.when(i == 0)
    def _():
        top = hbuf.shape[1] - WIN_ROWS
        for g in range(g_n):
            for cp in copies(slot, 1, g, SORTED_ROWS, top):
                cp.start()
        for g in range(g_n):
            for cp in copies(slot, 1, g, 0, 0):
                cp.wait()

    tiles_left = n_tiles - 1 - i
    for g in range(g_n):
        c_g = run[g]
        c_next = c_g + _pad_rows(counts[g])
        start_ref[i, g] = c_g
        cnt_ref[i, g] = counts[g]
        run[g] = c_next
        for cp in copies(slot, 0, g, pl.multiple_of(seg_start[g], ROW_ALIGN), pl.multiple_of(c_g, ROW_ALIGN)):
            cp.start()
        dead = c_next + tiles_left * WIN_ROWS + WIN_ROWS
        for cp in copies(slot, 1, g, SORTED_ROWS, pl.multiple_of(dead, ROW_ALIGN)):
            cp.start()

    @pl.when(i == n_tiles - 1)
    def _():
        wait_all(slot)
        for g in range(g_n):
            for cp in copies(slot, 0, g, SORTED_ROWS, pl.multiple_of(run[g], ROW_ALIGN)):
                cp.start()
        for g in range(g_n):
            for cp in copies(slot, 0, g, 0, 0):
                cp.wait()
        _write_slot_tables(run, start_ref, (slot_grp, slot_blk, slot_valid, slot_first, slot_next), out_start_ref)


def _merge(y_ret, y_na, y_xa, proj, x2d, w_ret_o, w_na_o, w_xa_o, w_out, g_ffn, w_router, b_router):
    t, d = x2d.shape
    tm = SORT_TILE
    n_tiles = t // tm
    rows = _group_buf_rows(t)
    gate_blk = OFF_GATE // d
    before = jnp.asarray(np.tril(np.ones((tm, tm), np.float32), -1), BF16)
    full = lambda a: pl.BlockSpec(a.shape, lambda i: (0,) * a.ndim)
    smem = pl.BlockSpec(memory_space=pltpu.SMEM)
    hbm = pl.BlockSpec(memory_space=pl.ANY)
    return pl.pallas_call(
        _merge_kernel,
        grid=(n_tiles,),
        in_specs=[
            pl.BlockSpec((tm, RET_V_WIDTH), lambda i: (i, 0)),
            pl.BlockSpec((tm, NA_WIDTH), lambda i: (i, 0)),
            pl.BlockSpec((tm, XA_WIDTH), lambda i: (i, 0)),
            pl.BlockSpec((tm, d), lambda i: (i, gate_blk)),
            pl.BlockSpec((tm, d), lambda i: (i, gate_blk + 1)),
            pl.BlockSpec((tm, d), lambda i: (i, gate_blk + 2)),
            pl.BlockSpec((tm, d), lambda i: (i, 0)),
            full(w_ret_o), full(w_na_o), full(w_xa_o), full(w_out), full(g_ffn), full(w_router), full(b_router),
            full(before),
        ],
        out_specs=[
            pl.BlockSpec((tm, d), lambda i: (i, 0)),
            pl.BlockSpec((tm, LANES), lambda i: (i, 0)),
            smem, smem, *([smem] * 5), hbm, hbm,
        ],
        out_shape=[
            jax.ShapeDtypeStruct((t, d), F32),
            jax.ShapeDtypeStruct((t, LANES), F32),
            jax.ShapeDtypeStruct((n_tiles, MOE_GROUPS), jnp.int32),
            jax.ShapeDtypeStruct((n_tiles, MOE_GROUPS), jnp.int32),
            *([jax.ShapeDtypeStruct((_expert_slot_count(t),), jnp.int32)] * 5),
            jax.ShapeDtypeStruct((MOE_GROUPS, rows, d), BF16),
            jax.ShapeDtypeStruct((MOE_GROUPS, rows, LANES), F32),
        ],
        scratch_shapes=[
            pltpu.VMEM((2, SCR_ROWS, d), BF16),
            pltpu.VMEM((2, SCR_ROWS, LANES), F32),
            pltpu.SMEM((MOE_GROUPS,), jnp.int32),
            pltpu.SMEM((n_tiles, MOE_GROUPS), jnp.int32),
            pltpu.SemaphoreType.DMA((2, 4, MOE_GROUPS)),
        ],
        compiler_params=_params("arbitrary"),
        name="merge_router",
    )(y_ret, y_na, y_xa, proj, proj, proj, x2d, w_ret_o, w_na_o, w_xa_o, w_out, g_ffn, w_router, b_router, before)


def _expert_slot_count(t):
    return (t + (t // SORT_TILE) * MOE_GROUPS * (ROW_ALIGN - 1)) // EXPERT_TILE + MOE_GROUPS + 1


def _write_slot_tables(run, start_ref, slot_refs, out_start_ref):
    nblk, first = [], []
    total = jnp.int32(0)
    for g in range(MOE_GROUPS):
        n = (run[g] + EXPERT_TILE - 1) // EXPERT_TILE
        first.append(total)
        nblk.append(n)
        total = total + n

    def fill_slot(j, carry):
        for ref, value in zip(slot_refs, _expert_slot(j, nblk, first, total)):
            ref[j] = value.astype(jnp.int32)
        return carry

    lax.fori_loop(0, slot_refs[0].shape[0], fill_slot, 0)

    def fill_tile(i, carry):
        for g in range(MOE_GROUPS):
            out_start_ref[i, g] = first[g] * EXPERT_TILE + start_ref[i, g]
        return carry

    lax.fori_loop(0, out_start_ref.shape[0], fill_tile, 0)


def _expert_slot(j, nblk, first, total):
    valid = j < total
    jc = jnp.clip(j, 0, jnp.maximum(total - 1, 0))
    grp = jnp.int32(0)
    for g in range(MOE_GROUPS):
        grp = grp + (jc >= first[g] + nblk[g]).astype(jnp.int32)
    grp = jnp.minimum(grp, MOE_GROUPS - 1)
    first_blk = jnp.int32(0)
    nxt = jnp.int32(-1)
    for g in reversed(range(MOE_GROUPS)):
        first_blk = jnp.where(grp == g, first[g], first_blk)
        nxt = jnp.where((g > grp) & (nblk[g] > 0), g, nxt)
    return grp, jc - first_blk, valid, valid & (jc == first_blk), nxt


def _experts_kernel(grp_ref, blk_ref, valid_ref, first_ref, next_ref, h_ref, c_ref, wg_hbm, wu_hbm, wd_hbm, o_ref,
                    stage_g, stage_u, stage_d, wg_ref, wu_ref, wd_ref, sems):
    j = pl.program_id(0)
    grp, valid, is_first, nxt = grp_ref[j], valid_ref[j] == 1, first_ref[j] == 1, next_ref[j]

    def weight_copies(g):
        return (pltpu.make_async_copy(wg_hbm.at[g], stage_g, sems.at[0]),
                pltpu.make_async_copy(wu_hbm.at[g], stage_u, sems.at[1]),
                pltpu.make_async_copy(wd_hbm.at[g], stage_d, sems.at[2]))

    @pl.when(j == 0)
    def _():
        for cp in weight_copies(grp):
            cp.start()

    @pl.when(is_first)
    def _():
        for cp in weight_copies(grp):
            cp.wait()
        for stage, dst in ((stage_g, wg_ref), (stage_u, wu_ref), (stage_d, wd_ref)):
            for e in range(MOE_EXPERTS_PER_GROUP):
                def convert(i, carry, stage=stage, dst=dst, e=e):
                    rows = pl.ds(pl.multiple_of(i * CAST_ROWS, CAST_ROWS), CAST_ROWS)
                    dst[e, rows, :] = stage[e, rows, :].astype(dst.dtype)
                    return carry

                lax.fori_loop(0, stage.shape[1] // CAST_ROWS, convert, 0)

        @pl.when(nxt >= 0)
        def _():
            for cp in weight_copies(nxt):
                cp.start()

    @pl.when(valid)
    def _():
        h = h_ref[...]
        c = c_ref[...]
        lane = lax.broadcasted_iota(jnp.int32, c.shape, 1)
        out = None
        for e in range(MOE_EXPERTS_PER_GROUP):
            a = _dot(h, wg_ref[e])
            u = _dot(h, wu_ref[e])
            cw = jnp.sum(jnp.where(lane == e, c, 0.0), axis=-1, keepdims=True)
            hid = (_silu(a) * u * cw).astype(BF16)
            part = _dot(hid, wd_ref[e])
            out = part if out is None else out + part
        o_ref[...] = out.astype(o_ref.dtype)

    @pl.when(jnp.logical_not(valid))
    def _():
        o_ref[...] = jnp.zeros_like(o_ref)


def _experts(hbuf, cbuf, wg, wu, wd, slots):
    g_n, rows, d = hbuf.shape
    e_n, f = wg.shape[1], wg.shape[3]
    tm = EXPERT_TILE
    n_slots = slots[0].shape[0]
    hbm = pl.BlockSpec(memory_space=pl.ANY)
    grid_spec = pltpu.PrefetchScalarGridSpec(
        num_scalar_prefetch=5,
        grid=(n_slots,),
        in_specs=[
            pl.BlockSpec((None, tm, d), lambda j, grp, blk, v, gf, gx: (grp[j], blk[j], 0)),
            pl.BlockSpec((None, tm, LANES), lambda j, grp, blk, v, gf, gx: (grp[j], blk[j], 0)),
            hbm, hbm, hbm,
        ],
        out_specs=pl.BlockSpec((tm, d), lambda j, grp, blk, v, gf, gx: (j, 0)),
        scratch_shapes=[
            pltpu.VMEM((e_n, d, f), wg.dtype),
            pltpu.VMEM((e_n, d, f), wu.dtype),
            pltpu.VMEM((e_n, f, d), wd.dtype),
            pltpu.VMEM((e_n, d, f), BF16),
            pltpu.VMEM((e_n, d, f), BF16),
            pltpu.VMEM((e_n, f, d), BF16),
            pltpu.SemaphoreType.DMA((3,)),
        ],
    )
    return pl.pallas_call(
        _experts_kernel,
        grid_spec=grid_spec,
        out_shape=jax.ShapeDtypeStruct((n_slots * tm, d), BF16),
        compiler_params=pltpu.CompilerParams(dimension_semantics=("arbitrary",),
                                             vmem_limit_bytes=STAGING_VMEM_LIMIT),
        name="experts",
    )(*slots, hbuf, cbuf, wg, wu, wd)


def _finish_kernel(start_ref, cnt_ref, *refs, final_norm):
    wins = refs[:2 * MOE_GROUPS]
    pos_ref, x2_ref, gfin_ref, o_ref, sorted_scr = refs[2 * MOE_GROUPS:]
    i = pl.program_id(0)
    tm = x2_ref.shape[0]
    half_pieces = HALF_WIN // ROW_ALIGN

    @pl.when(i == 0)
    def _():
        sorted_scr[...] = jnp.zeros_like(sorted_scr)

    seg = jnp.int32(0)
    for g in range(MOE_GROUPS):
        padded = _pad_rows(cnt_ref[i, g])
        pieces = padded // ROW_ALIGN
        for half, win in enumerate(wins[2 * g:2 * g + 2]):
            base = seg + half * HALF_WIN

            def copy_piece(k, carry, win=win, base=base):
                src = pl.multiple_of(k * ROW_ALIGN, ROW_ALIGN)
                dst = pl.multiple_of(base + k * ROW_ALIGN, ROW_ALIGN)
                sorted_scr[pl.ds(dst, ROW_ALIGN), :] = win[pl.ds(src, ROW_ALIGN), :]
                return carry

            lax.fori_loop(0, jnp.clip(pieces - half * half_pieces, 0, half_pieces), copy_piece, 0)
        seg = seg + padded

    pos = pos_ref[:, 0:1]
    unperm = jnp.where(lax.broadcasted_iota(jnp.int32, (tm, SORTED_ROWS), 1).astype(F32) == pos, 1.0, 0.0)
    y = x2_ref[...] + _dot(unperm.astype(BF16), sorted_scr[...])
    if final_norm:
        y = _rmsnorm_f32(y, gfin_ref[...])
    o_ref[...] = y


def _finish(mbuf, starts, counts, pos, x2, g_final, final_norm):
    t, d = x2.shape
    tm = SORT_TILE

    def window(g, half):
        def index(i, st, ct):
            row = st[i, g] + half * HALF_WIN
            if half:
                row = jnp.where(_pad_rows(ct[i, g]) > HALF_WIN, row, 0)
            return pl.multiple_of(row, ROW_ALIGN), 0

        return pl.BlockSpec((pl.Element(HALF_WIN), pl.Element(d)), index)

    grid_spec = pltpu.PrefetchScalarGridSpec(
        num_scalar_prefetch=2,
        grid=(t // tm,),
        in_specs=[
            *[window(g, half) for g in range(MOE_GROUPS) for half in range(2)],
            pl.BlockSpec((tm, LANES), lambda i, st, ct: (i, 0)),
            pl.BlockSpec((tm, d), lambda i, st, ct: (i, 0)),
            pl.BlockSpec((1, d), lambda i, st, ct: (0, 0)),
        ],
        out_specs=pl.BlockSpec((tm, d), lambda i, st, ct: (i, 0)),
        scratch_shapes=[pltpu.VMEM((SORTED_ROWS, d), BF16)],
    )
    return pl.pallas_call(
        functools.partial(_finish_kernel, final_norm=final_norm),
        grid_spec=grid_spec,
        out_shape=jax.ShapeDtypeStruct((t, d), F32),
        compiler_params=_params("arbitrary"),
        name="finish",
    )(starts, counts, *([mbuf] * (2 * MOE_GROUPS)), pos, x2, g_final)


def kernel(x, mem, g_mix, w_in, ret_decay_fwd, ret_decay_bwd, ret_norm_gain, w_ret_o, na_rpb, w_na_o, g_mem, w_mem_kv, w_xa_o, w_out, g_ffn, w_router_group, b_router_group, w_router_expert, b_router_expert, w_exp_gate, w_exp_up, w_exp_down, g_final):
    b, s, d = x.shape
    depth = w_in.shape[0]
    t = b * s
    cos_t, sin_t = _rope_tables(s)
    row = lambda v: v.reshape(1, -1).astype(F32)
    x2d = x.reshape(t, d)
    for l in range(depth):
        proj = _inproj(x2d, row(g_mix[l]), w_in[l])
        proj3 = proj.reshape(b, s, IN_WIDTH)
        y_ret = _retention(proj3, ret_decay_fwd[l].astype(F32), ret_decay_bwd[l].astype(F32), cos_t, sin_t,
                           row(ret_norm_gain[l]))
        y_na = _neighbourhood_attention(proj3, _na_bias_table(na_rpb[l]))
        y_xa = _memory_attention(proj3, mem, row(g_mem[l]), w_mem_kv[l].astype(BF16))
        n_r = MOE_GROUPS + N_EXPERTS
        w_router = jnp.pad(jnp.concatenate([w_router_group[l], w_router_expert[l]], axis=1).astype(F32),
                           ((0, 0), (0, LANES - n_r)))
        b_router = jnp.pad(jnp.concatenate([b_router_group[l], b_router_expert[l]]).astype(F32),
                           (0, LANES - n_r)).reshape(1, LANES)
        x2, pos, counts, out_starts, *slots, hbuf, cbuf = _merge(
            y_ret.reshape(t, -1), y_na.reshape(t, -1), y_xa.reshape(t, -1), proj, x2d,
            w_ret_o[l].astype(BF16), w_na_o[l].astype(BF16), w_xa_o[l].astype(BF16),
            w_out[l].astype(BF16), row(g_ffn[l]), w_router, b_router)
        mbuf = _experts(hbuf, cbuf, w_exp_gate[l], w_exp_up[l], w_exp_down[l], slots)
        x2d = _finish(mbuf, out_starts, counts, pos, x2, row(g_final), final_norm=(l == depth - 1))
    return x2d.reshape(b, s, d)
```

```python
import functools

import jax
import jax.numpy as jnp
import numpy as np
from jax import lax
from jax.experimental import pallas as pl
from jax.experimental.pallas import tpu as pltpu

D_MODEL = 1024
GRID_W = 64
N_BRANCHES = 3
RET_HEADS = 4
RET_QK_DIM = 128
RET_V_DIM = 256
RET_CHUNK = 128
ROPE_BASE = 10000.0
NA_HEADS = 8
NA_HEAD_DIM = 64
NA_ROWS = 8
NA_COLS = 16
XA_HEADS = 4
XA_HEAD_DIM = 128
MOE_GROUPS = 4
MOE_EXPERTS_PER_GROUP = 4
MOE_TOP_K = 2
MOE_D_FF = 512
RMS_EPS = 1e-6
GN_EPS = 1e-5

RET_QK_WIDTH = RET_HEADS * RET_QK_DIM
RET_V_WIDTH = RET_HEADS * RET_V_DIM
NA_WIDTH = NA_HEADS * NA_HEAD_DIM
XA_WIDTH = XA_HEADS * XA_HEAD_DIM
IN_WIDTH = 2 * RET_QK_WIDTH + 2 * RET_V_WIDTH + 3 * NA_WIDTH + XA_WIDTH + N_BRANCHES * D_MODEL

OFF_RQ = 0
OFF_RK = OFF_RQ + RET_QK_WIDTH
OFF_RV = OFF_RK + RET_QK_WIDTH
OFF_RG = OFF_RV + RET_V_WIDTH
OFF_NQ = OFF_RG + RET_V_WIDTH
OFF_NK = OFF_NQ + NA_WIDTH
OFF_NV = OFF_NK + NA_WIDTH
OFF_XQ = OFF_NV + NA_WIDTH
OFF_GATE = OFF_XQ + XA_WIDTH

N_EXPERTS = MOE_GROUPS * MOE_EXPERTS_PER_GROUP
LANES = 128
ROUTER_EXPERT_LANE0 = MOE_GROUPS

VMEM_LIMIT = 48 * 1024 * 1024
STAGING_VMEM_LIMIT = 56 * 1024 * 1024
CAST_ROWS = 64

F32 = jnp.float32
BF16 = jnp.bfloat16


def _params(*sem):
    return pltpu.CompilerParams(dimension_semantics=sem, vmem_limit_bytes=VMEM_LIMIT)


def _rmsnorm_f32(x, g):
    return x * lax.rsqrt(jnp.mean(x * x, axis=-1, keepdims=True) + RMS_EPS) * g


def _sigmoid(x):
    return 0.5 * jnp.tanh(0.5 * x) + 0.5


def _silu(x):
    h = 0.5 * x
    return h + h * jnp.tanh(h)


def _dot(a, b):
    return jnp.dot(a, b, preferred_element_type=F32)


def _dot_tn(a, b):
    return lax.dot_general(a, b, (((0,), (0,)), ((), ())), preferred_element_type=F32)


def _dot_nt(a, b):
    return lax.dot_general(a, b, (((1,), (1,)), ((), ())), preferred_element_type=F32)


INPROJ_STAGE_COLS = 1024


def _inproj_kernel(x_ref, g_ref, cos_ref, sin_ref, w_hbm, o_ref, w_scr, stage, sems):
    i = pl.program_id(0)
    j = pl.program_id(1)
    n_col = pl.num_programs(1)
    tn = o_ref.shape[1]
    per_block = tn // INPROJ_STAGE_COLS

    def chunk_copy(col_block, k):
        cols = pl.ds(pl.multiple_of((col_block * per_block + k) * INPROJ_STAGE_COLS, INPROJ_STAGE_COLS),
                     INPROJ_STAGE_COLS)
        return pltpu.make_async_copy(w_hbm.at[:, cols], stage.at[k], sems.at[k])

    @pl.when((i == 0) & (j == 0))
    def _():
        for k in range(per_block):
            chunk_copy(0, k).start()

    @pl.when(i == 0)
    def _():
        for k in range(per_block):
            chunk_copy(j, k).wait()

            def convert(r, carry, k=k):
                rows = pl.ds(pl.multiple_of(r * CAST_ROWS, CAST_ROWS), CAST_ROWS)
                w_scr[j, rows, k * INPROJ_STAGE_COLS:(k + 1) * INPROJ_STAGE_COLS] = stage[k, rows, :].astype(BF16)
                return carry

            lax.fori_loop(0, stage.shape[1] // CAST_ROWS, convert, 0)

        @pl.when(j + 1 < n_col)
        def _():
            for k in range(per_block):
                chunk_copy(j + 1, k).start()

    half = RET_QK_DIM // 2
    for jj in range(w_scr.shape[0]):
        @pl.when(j == jj)
        def _(jj=jj):
            h = _rmsnorm_f32(x_ref[...], g_ref[...]).astype(BF16)
            res = _dot(h, w_scr[jj])
            for c in range(0, tn, RET_QK_DIM):
                a = res[:, c:c + RET_QK_DIM]
                col = jj * tn + c
                if OFF_RQ <= col < OFF_RV:
                    a = a * cos_ref[...] + pltpu.roll(a, half, 1) * sin_ref[...]
                    if col < OFF_RK:
                        a = a * (RET_QK_DIM ** -0.5)
                elif OFF_RG <= col < OFF_NQ:
                    a = _silu(a)
                o_ref[:, c:c + RET_QK_DIM] = a.astype(o_ref.dtype)


def _inproj(x2d, g, w, cos_t, sin_t, tm=1024, tn=2048):
    t, d = x2d.shape
    n = w.shape[1]
    per_block = tn // INPROJ_STAGE_COLS
    seq_tiles = cos_t.shape[0] // tm
    return pl.pallas_call(
        _inproj_kernel,
        grid=(t // tm, n // tn),
        in_specs=[
            pl.BlockSpec((tm, d), lambda i, j: (i, 0)),
            pl.BlockSpec((1, d), lambda i, j: (0, 0)),
            pl.BlockSpec((tm, RET_QK_DIM), lambda i, j: (i % seq_tiles, 0)),
            pl.BlockSpec((tm, RET_QK_DIM), lambda i, j: (i % seq_tiles, 0)),
            pl.BlockSpec(memory_space=pl.ANY),
        ],
        out_specs=pl.BlockSpec((tm, tn), lambda i, j: (i, j)),
        out_shape=jax.ShapeDtypeStruct((t, n), BF16),
        scratch_shapes=[
            pltpu.VMEM((n // tn, d, tn), BF16),
            pltpu.VMEM((per_block, d, INPROJ_STAGE_COLS), w.dtype),
            pltpu.SemaphoreType.DMA((per_block,)),
        ],
        compiler_params=pltpu.CompilerParams(dimension_semantics=("arbitrary", "arbitrary"),
                                             vmem_limit_bytes=STAGING_VMEM_LIMIT),
        name="inproj",
    )(x2d, g, cos_t, sin_t, w)


def _log_sigmoid(x):
    return jnp.minimum(x, 0.0) - jnp.log1p(jnp.exp(-jnp.abs(x)))


RET_HEADS_PER_STEP = 2


def _retention_kernel(decf_ref, decb_ref, q_ref, k_ref, v_ref, rg_ref, gn_ref, o_ref, *scratch):
    qk, dv = RET_QK_DIM, RET_V_DIM
    for hh in range(RET_HEADS_PER_STEP):
        qcols = slice(hh * qk, (hh + 1) * qk)
        vcols = slice(hh * dv, (hh + 1) * dv)
        _retention_head(pl.program_id(1) * RET_HEADS_PER_STEP + hh, decf_ref, decb_ref,
                        q_ref.at[:, qcols], k_ref.at[:, qcols], v_ref.at[:, vcols], rg_ref.at[:, vcols],
                        gn_ref.at[:, vcols], o_ref.at[:, vcols], *[s.at[hh] for s in scratch])


def _retention_head(h, decf_ref, decb_ref, q_ref, k_ref, v_ref, rg_ref, gn_ref, o_ref,
                    qfs, qbs, y_scr, kvf_scr, kvb_scr, st_scr):
    c = RET_CHUNK
    seq = q_ref.shape[0]
    n_chunks = seq // c

    lgf = _log_sigmoid(jnp.full((1, 1), decf_ref[h], F32))
    lgb = _log_sigmoid(jnp.full((1, 1), decb_ref[h], F32))

    ii = lax.broadcasted_iota(jnp.int32, (c, c), 0)
    jj = lax.broadcasted_iota(jnp.int32, (c, c), 1)
    diff = (ii - jj).astype(F32)
    dmat = jnp.exp(jnp.where(diff >= 0, diff * lgf, (-diff) * lgb))
    pos = lax.broadcasted_iota(jnp.int32, (c, 1), 0).astype(F32)
    qd_f = jnp.exp((pos + 1.0) * lgf)
    kd_f = jnp.exp((c - 1.0 - pos) * lgf)
    cd_f = jnp.exp(c * lgf)
    qd_b = jnp.exp((c - pos) * lgb)
    kd_b = jnp.exp(pos * lgb)
    cd_b = jnp.exp(c * lgb)

    def intra(n):
        rows = pl.ds(n * c, c)
        q = q_ref[rows, :].astype(F32)
        qfs[rows, :] = (q * qd_f).astype(BF16)
        qbs[rows, :] = (q * qd_b).astype(BF16)
        k = k_ref[rows, :].astype(F32)
        s = _dot_nt(q_ref[rows, :], k_ref[rows, :]) * dmat
        lhs = jnp.concatenate([s.astype(BF16), (k * kd_f).T.astype(BF16), (k * kd_b).T.astype(BF16)], axis=0)
        r = _dot(lhs, v_ref[rows, :])
        y_scr[rows, :] = r[:c]
        kvf_scr[n] = r[c:2 * c]
        kvb_scr[n] = r[2 * c:]

    for n in range(n_chunks):
        intra(n)

    st_scr[...] = jnp.zeros_like(st_scr)

    def fwd(n):
        rows = pl.ds(n * c, c)
        st = st_scr[...]
        y_scr[rows, :] += _dot(qfs[rows, :], st.astype(BF16))
        st_scr[...] = st * cd_f + kvf_scr[n]

    for n in range(n_chunks):
        fwd(n)

    st_scr[...] = jnp.zeros_like(st_scr)
    gn = gn_ref[...]

    def bwd(n):
        rows = pl.ds(n * c, c)
        st = st_scr[...]
        y = y_scr[rows, :] + _dot(qbs[rows, :], st.astype(BF16))
        st_scr[...] = st * cd_b + kvb_scr[n]
        mu = jnp.mean(y, axis=-1, keepdims=True)
        yc = y - mu
        var = jnp.mean(yc * yc, axis=-1, keepdims=True)
        yn = yc * lax.rsqrt(var + GN_EPS) * gn
        o_ref[rows, :] = rg_ref[rows, :] * yn.astype(o_ref.dtype)

    for n in reversed(range(n_chunks)):
        bwd(n)


def _retention(proj3, dec_f, dec_b, gn_gain):
    b, s, _ = proj3.shape
    hps = RET_HEADS_PER_STEP
    dk, dv = RET_QK_DIM, RET_V_DIM
    qb, vb = hps * dk, hps * dv
    smem = pl.BlockSpec(memory_space=pltpu.SMEM)
    return pl.pallas_call(
        _retention_kernel,
        grid=(b, RET_HEADS // hps),
        in_specs=[
            smem,
            smem,
            pl.BlockSpec((None, s, qb), lambda i, h: (i, 0, OFF_RQ // qb + h)),
            pl.BlockSpec((None, s, qb), lambda i, h: (i, 0, OFF_RK // qb + h)),
            pl.BlockSpec((None, s, vb), lambda i, h: (i, 0, OFF_RV // vb + h)),
            pl.BlockSpec((None, s, vb), lambda i, h: (i, 0, OFF_RG // vb + h)),
            pl.BlockSpec((1, vb), lambda i, h: (0, h)),
        ],
        out_specs=pl.BlockSpec((None, s, vb), lambda i, h: (i, 0, h)),
        out_shape=jax.ShapeDtypeStruct((b, s, RET_V_WIDTH), BF16),
        scratch_shapes=[
            pltpu.VMEM((hps, s, dk), BF16),
            pltpu.VMEM((hps, s, dk), BF16),
            pltpu.VMEM((hps, s, dv), F32),
            pltpu.VMEM((hps, s // RET_CHUNK, dk, dv), F32),
            pltpu.VMEM((hps, s // RET_CHUNK, dk, dv), F32),
            pltpu.VMEM((hps, dk, dv), F32),
        ],
        compiler_params=_params("parallel", "parallel"),
        name="retention",
    )(dec_f, dec_b, proj3, proj3, proj3, proj3, gn_gain)


def _rope_tables(seq):
    half = RET_QK_DIM // 2
    inv_freq = ROPE_BASE ** (-np.arange(half, dtype=np.float64) / half)
    ang = np.arange(seq, dtype=np.float64)[:, None] * inv_freq[None, :]
    cos, sin = np.cos(ang), np.sin(ang)
    return (jnp.asarray(np.concatenate([cos, cos], axis=-1), F32),
            jnp.asarray(np.concatenate([-sin, sin], axis=-1), F32))


def _na_bias_table(rpb):
    heads = rpb.shape[0]
    w = GRID_W
    cols = np.arange(w)
    col_start = np.clip(cols - NA_COLS // 2, 0, w - NA_COLS)
    col_off = cols[None, :] - col_start[:, None]
    col_mask = (col_off >= 0) & (col_off < NA_COLS)
    rel_c = np.clip(cols[None, :] - cols[:, None], -(NA_COLS - 1), NA_COLS - 1) + (NA_COLS - 1)
    onehot = (rel_c[:, :, None] == np.arange(2 * NA_COLS - 1)).astype(np.float32)
    toe = jnp.einsum('hrc,qkc->hrqk', rpb.astype(F32), onehot, precision=lax.Precision.HIGHEST)
    toe = jnp.where(col_mask[None, None], toe, -jnp.inf)
    two = jnp.concatenate([toe[:, :-1], toe[:, 1:]], axis=-1)
    two = two.reshape(heads // 2, 2, 2 * NA_ROWS - 2, w, 2 * w)
    return two.transpose(0, 2, 1, 3, 4).reshape(heads // 2, 2 * NA_ROWS - 2, 2 * w, 2 * w)


NA_ROWS_PER_STEP = 16


def _na_kernel(q_ref, k_ref, v_ref, bias_ref, o_ref, s_scr, p_scr):
    w = GRID_W
    seq = q_ref.shape[0]
    rows_n = seq // w
    kr = NA_ROWS
    first = lax.broadcasted_iota(jnp.int32, (w, LANES), 1) < NA_HEAD_DIM
    scale = NA_HEAD_DIM ** -0.5

    def window_row(r):
        return jnp.clip(r - kr // 2, 0, rows_n - kr)

    def body(it, carry):
        r0 = it * NA_ROWS_PER_STEP
        for u in range(NA_ROWS_PER_STEP):
            r = r0 + u
            rs = window_row(r)
            qr = q_ref[pl.ds(pl.multiple_of(r * w, w), w), :] * scale
            zero = jnp.zeros_like(qr)
            q2 = jnp.concatenate([jnp.where(first, qr, zero), jnp.where(first, zero, qr)], axis=0)
            kk = k_ref[pl.ds(pl.multiple_of(rs * w, w), kr * w), :]
            off = rs - r + (NA_ROWS - 1)
            bias = jnp.concatenate([bias_ref[off + 2 * i] for i in range(kr // 2)], axis=1)
            s_scr[u] = _dot_nt(q2, kk) + bias
        for u in range(NA_ROWS_PER_STEP):
            m = jnp.max(s_scr[u], axis=-1, keepdims=True)
            p_scr[u] = jnp.exp(s_scr[u] - m).astype(BF16)
        for u in range(NA_ROWS_PER_STEP):
            r = r0 + u
            vv = v_ref[pl.ds(pl.multiple_of(window_row(r) * w, w), kr * w), :]
            o2 = _dot(p_scr[u], jnp.concatenate([vv, jnp.ones_like(vv)], axis=1))
            o2 = o2[:, :LANES] / o2[:, LANES:]
            o_ref[pl.ds(pl.multiple_of(r * w, w), w), :] = jnp.where(first, o2[:w], o2[w:]).astype(o_ref.dtype)
        return carry

    lax.fori_loop(0, rows_n // NA_ROWS_PER_STEP, body, 0)


def _neighbourhood_attention(proj3, bias_tab):
    b, s, _ = proj3.shape
    pairs = NA_HEADS // 2
    return pl.pallas_call(
        _na_kernel,
        grid=(b, pairs),
        in_specs=[
            pl.BlockSpec((None, s, LANES), lambda i, p: (i, 0, OFF_NQ // LANES + p)),
            pl.BlockSpec((None, s, LANES), lambda i, p: (i, 0, OFF_NK // LANES + p)),
            pl.BlockSpec((None, s, LANES), lambda i, p: (i, 0, OFF_NV // LANES + p)),
            pl.BlockSpec((None, 2 * NA_ROWS - 2, 2 * GRID_W, 2 * GRID_W), lambda i, p: (p, 0, 0, 0)),
        ],
        out_specs=pl.BlockSpec((None, s, LANES), lambda i, p: (i, 0, p)),
        out_shape=jax.ShapeDtypeStruct((b, s, NA_WIDTH), BF16),
        scratch_shapes=[
            pltpu.VMEM((NA_ROWS_PER_STEP, 2 * GRID_W, NA_ROWS * GRID_W), F32),
            pltpu.VMEM((NA_ROWS_PER_STEP, 2 * GRID_W, NA_ROWS * GRID_W), BF16),
        ],
        compiler_params=_params("parallel", "parallel"),
        name="nbr_attn",
    )(proj3, proj3, proj3, bias_tab)


def _xa_kernel(q_ref, mem_ref, g_ref, wkv_ref, o_ref, kv_scr):
    @pl.when(pl.program_id(1) == 0)
    def _():
        mn = _rmsnorm_f32(mem_ref[...], g_ref[...]).astype(BF16)
        kv_scr[...] = _dot(mn, wkv_ref[...]).astype(BF16)

    dh = XA_HEAD_DIM
    scale = dh ** -0.5
    for h in range(XA_HEADS):
        q = q_ref[:, h * dh:(h + 1) * dh]
        k = kv_scr[:, h * dh:(h + 1) * dh]
        v = kv_scr[:, XA_WIDTH + h * dh:XA_WIDTH + (h + 1) * dh]
        s = _dot_nt(q, k) * scale
        m = jnp.max(s, axis=-1, keepdims=True)
        p = jnp.exp(s - m).astype(BF16)
        o2 = _dot(p, jnp.concatenate([v, jnp.ones_like(v)], axis=1))
        o_ref[:, h * dh:(h + 1) * dh] = (o2[:, :dh] / o2[:, dh:]).astype(o_ref.dtype)


def _memory_attention(proj3, mem, g_mem, wkv_bf16, ts=1024):
    b, s, _ = proj3.shape
    m, d = mem.shape[1], mem.shape[2]
    return pl.pallas_call(
        _xa_kernel,
        grid=(b, s // ts),
        in_specs=[
            pl.BlockSpec((None, ts, XA_WIDTH), lambda i, j: (i, j, OFF_XQ // XA_WIDTH)),
            pl.BlockSpec((None, m, d), lambda i, j: (i, 0, 0)),
            pl.BlockSpec((1, d), lambda i, j: (0, 0)),
            pl.BlockSpec((d, 2 * XA_WIDTH), lambda i, j: (0, 0)),
        ],
        out_specs=pl.BlockSpec((None, ts, XA_WIDTH), lambda i, j: (i, j, 0)),
        out_shape=jax.ShapeDtypeStruct((b, s, XA_WIDTH), BF16),
        scratch_shapes=[pltpu.VMEM((m, 2 * XA_WIDTH), BF16)],
        compiler_params=_params("parallel", "arbitrary"),
        name="mem_attn",
    )(proj3, mem, g_mem, wkv_bf16)


def _masked_lane_max(v, mask):
    return jnp.max(jnp.where(mask, v, -jnp.inf), axis=-1, keepdims=True)


def _first_lane_eq(v, target, mask, lane):
    return jnp.min(jnp.where(mask & (v == target), lane, float(LANES)), axis=-1, keepdims=True)


def _route(logits):
    g_n, e_n = MOE_GROUPS, MOE_EXPERTS_PER_GROUP
    lane = lax.broadcasted_iota(jnp.int32, logits.shape, 1).astype(F32)
    is_grp = lane < g_n
    gmax = _masked_lane_max(logits, is_grp)
    gsum = jnp.sum(jnp.where(is_grp, jnp.exp(logits - gmax), 0.0), axis=-1, keepdims=True)
    grp_w = 1.0 / gsum
    gidx = _first_lane_eq(logits, gmax, is_grp, lane)
    lo = ROUTER_EXPERT_LANE0 + gidx * e_n
    in_grp = (lane >= lo) & (lane < lo + e_n)
    emax = _masked_lane_max(logits, in_grp)
    ex = jnp.where(in_grp, jnp.exp(logits - emax), 0.0)
    prob = ex / jnp.sum(ex, axis=-1, keepdims=True)
    p1 = _masked_lane_max(prob, in_grp)
    i1 = _first_lane_eq(prob, p1, in_grp, lane)
    rest = in_grp & (lane != i1)
    p2 = _masked_lane_max(prob, rest)
    i2 = _first_lane_eq(prob, p2, rest, lane)
    tot = p1 + p2
    w1 = p1 / tot * grp_w
    w2 = p2 / tot * grp_w
    return gidx, jnp.where(lane == i1 - lo, w1, jnp.where(lane == i2 - lo, w2, 0.0))


SORT_TILE = 512
ROW_ALIGN = 16
SORTED_ROWS = 640
WIN_ROWS = SORT_TILE
HALF_WIN = WIN_ROWS // 2
SCR_ROWS = SORTED_ROWS + WIN_ROWS
EXPERT_TILE = 512


def _group_buf_rows(t):
    return t + 2 * WIN_ROWS


def _pad_rows(n):
    return (n + (ROW_ALIGN - 1)) // ROW_ALIGN * ROW_ALIGN


def _window_copies(hs, cs, hbuf, cbuf, sems, slot, kind, g, src_row, dst_row):
    return (
        pltpu.make_async_copy(hs.at[slot, pl.ds(src_row, WIN_ROWS)], hbuf.at[g, pl.ds(dst_row, WIN_ROWS)],
                              sems.at[slot, 2 * kind, g]),
        pltpu.make_async_copy(cs.at[slot, pl.ds(src_row, WIN_ROWS)], cbuf.at[g, pl.ds(dst_row, WIN_ROWS)],
                              sems.at[slot, 2 * kind + 1, g]),
    )


def _merge_kernel(yret_ref, yna_ref, yxa_ref, gr_ref, gn_ref, gx_ref, x_ref, wro_ref, wno_ref, wxo_ref, wout_ref,
                  gffn_ref, wr_ref, br_ref, before_ref,
                  x2_ref, pos_ref, cnt_ref, out_start_ref,
                  slot_grp, slot_blk, slot_valid, slot_first, slot_next, hbuf, cbuf,
                  hs, cs, run, start_ref, sems):
    i = pl.program_id(0)
    n_tiles = pl.num_programs(0)
    slot = lax.rem(i, 2)
    tm = x_ref.shape[0]
    g_n = MOE_GROUPS

    @pl.when(i == 0)
    def _():
        for g in range(g_n):
            run[g] = 0
        hs[:, SORTED_ROWS:, :] = jnp.zeros((2, WIN_ROWS, hs.shape[2]), hs.dtype)
        cs[:, SORTED_ROWS:, :] = jnp.zeros((2, WIN_ROWS, cs.shape[2]), cs.dtype)

    y_ret = _dot(yret_ref[...], wro_ref[...])
    y_na = _dot(yna_ref[...], wno_ref[...])
    y_xa = _dot(yxa_ref[...], wxo_ref[...])
    mix = (_sigmoid(gr_ref[...].astype(F32)) * y_ret + _sigmoid(gn_ref[...].astype(F32)) * y_na
           + _sigmoid(gx_ref[...].astype(F32)) * y_xa)
    x2 = x_ref[...] + _dot(mix.astype(BF16), wout_ref[...])
    x2_ref[...] = x2
    h2 = _rmsnorm_f32(x2, gffn_ref[...])
    h_hi = h2.astype(BF16)
    h_lo = (h2 - h_hi.astype(F32)).astype(BF16)
    wr = wr_ref[...]
    wr_hi = wr.astype(BF16)
    wr_lo = (wr - wr_hi.astype(F32)).astype(BF16)
    hi_terms = _dot(h_hi, jnp.concatenate([wr_hi, wr_lo], axis=1))
    logits = hi_terms[:, :LANES] + hi_terms[:, LANES:] + _dot(h_lo, wr_hi) + br_ref[...]

    gidx, w4 = _route(logits)

    lane = lax.broadcasted_iota(jnp.int32, (tm, LANES), 1).astype(F32)
    onehot = jnp.where(lane == gidx, 1.0, 0.0)
    rank = _dot(before_ref[...], onehot.astype(BF16))
    count_row = rank[tm - 1:tm, :] + onehot[tm - 1:tm, :]
    lane_row = lax.broadcasted_iota(jnp.int32, (1, LANES), 1)
    counts = [jnp.sum(jnp.where(lane_row == g, count_row, 0.0)).astype(jnp.int32) for g in range(g_n)]
    seg_start = []
    acc = jnp.int32(0)
    for g in range(g_n):
        seg_start.append(acc)
        acc = acc + _pad_rows(counts[g])
    start_row = jnp.zeros((1, LANES), F32)
    for g in range(g_n):
        start_row = jnp.where(lane_row == g, seg_start[g].astype(F32), start_row)
    pos = jnp.sum(onehot * (rank + start_row), axis=-1, keepdims=True)
    pos_ref[...] = jnp.broadcast_to(pos, (tm, LANES))

    pos_lanes = jnp.transpose(jnp.broadcast_to(pos, (tm, LANES)))[0:1, :]
    perm = jnp.where(lax.broadcasted_iota(jnp.int32, (SORTED_ROWS, tm), 0).astype(F32) == pos_lanes, 1.0, 0.0)
    perm = perm.astype(BF16)
    hs[slot, 0:SORTED_ROWS, :] = _dot(perm, h_hi).astype(hs.dtype)
    e_n = MOE_EXPERTS_PER_GROUP
    w_hi = w4.astype(BF16).astype(F32)
    r1 = w4 - w_hi
    w_mid = r1.astype(BF16).astype(F32)
    w_lo = r1 - w_mid
    pieces = (w_hi + pltpu.roll(w_mid, e_n, 1) + pltpu.roll(w_lo, 2 * e_n, 1)).astype(BF16)
    sorted_pieces = _dot(perm, pieces)
    cs[slot, 0:SORTED_ROWS, :] = (sorted_pieces + pltpu.roll(sorted_pieces, LANES - e_n, 1)
                                  + pltpu.roll(sorted_pieces, LANES - 2 * e_n, 1))

    copies = functools.partial(_window_copies, hs, cs, hbuf, cbuf, sems)

    def wait_all(which_slot):
        for kind in range(2):
            for g in range(g_n):
                for cp in copies(which_slot, kind, g, 0, 0):
                    cp.wait()

    @pl.when(i > 0)
    def _():
        wait_all(1 - slot)

    @pl.when(i == 0)
    def _():
        top = hbuf.shape[1] - WIN_ROWS
        for g in range(g_n):
            for cp in copies(slot, 1, g, SORTED_ROWS, top):
                cp.start()
        for g in range(g_n):
            for cp in copies(slot, 1, g, 0, 0):
                cp.wait()

    tiles_left = n_tiles - 1 - i
    for g in range(g_n):
        c_g = run[g]
        c_next = c_g + _pad_rows(counts[g])
        start_ref[i, g] = c_g
        cnt_ref[i, g] = counts[g]
        run[g] = c_next
        for cp in copies(slot, 0, g, pl.multiple_of(seg_start[g], ROW_ALIGN), pl.multiple_of(c_g, ROW_ALIGN)):
            cp.start()
        dead = c_next + tiles_left * WIN_ROWS + WIN_ROWS
        for cp in copies(slot, 1, g, SORTED_ROWS, pl.multiple_of(dead, ROW_ALIGN)):
            cp.start()

    @pl.when(i == n_tiles - 1)
    def _():
        wait_all(slot)
        for g in range(g_n):
            for cp in copies(slot, 0, g, SORTED_ROWS, pl.multiple_of(run[g], ROW_ALIGN)):
                cp.start()
        for g in range(g_n):
            for cp in copies(slot, 0, g, 0, 0):
                cp.wait()
        _write_slot_tables(run, start_ref, (slot_grp, slot_blk, slot_valid, slot_first, slot_next), out_start_ref)


def _merge(y_ret, y_na, y_xa, proj, x2d, w_ret_o, w_na_o, w_xa_o, w_out, g_ffn, w_router, b_router):
    t, d = x2d.shape
    tm = SORT_TILE
    n_tiles = t // tm
    rows = _group_buf_rows(t)
    gate_blk = OFF_GATE // d
    before = jnp.asarray(np.tril(np.ones((tm, tm), np.float32), -1), BF16)
    full = lambda a: pl.BlockSpec(a.shape, lambda i: (0,) * a.ndim)
    smem = pl.BlockSpec(memory_space=pltpu.SMEM)
    hbm = pl.BlockSpec(memory_space=pl.ANY)
    return pl.pallas_call(
        _merge_kernel,
        grid=(n_tiles,),
        in_specs=[
            pl.BlockSpec((tm, RET_V_WIDTH), lambda i: (i, 0)),
            pl.BlockSpec((tm, NA_WIDTH), lambda i: (i, 0)),
            pl.BlockSpec((tm, XA_WIDTH), lambda i: (i, 0)),
            pl.BlockSpec((tm, d), lambda i: (i, gate_blk)),
            pl.BlockSpec((tm, d), lambda i: (i, gate_blk + 1)),
            pl.BlockSpec((tm, d), lambda i: (i, gate_blk + 2)),
            pl.BlockSpec((tm, d), lambda i: (i, 0)),
            full(w_ret_o), full(w_na_o), full(w_xa_o), full(w_out), full(g_ffn), full(w_router), full(b_router),
            full(before),
        ],
        out_specs=[
            pl.BlockSpec((tm, d), lambda i: (i, 0)),
            pl.BlockSpec((tm, LANES), lambda i: (i, 0)),
            smem, smem, *([smem] * 5), hbm, hbm,
        ],
        out_shape=[
            jax.ShapeDtypeStruct((t, d), F32),
            jax.ShapeDtypeStruct((t, LANES), F32),
            jax.ShapeDtypeStruct((n_tiles, MOE_GROUPS), jnp.int32),
            jax.ShapeDtypeStruct((n_tiles, MOE_GROUPS), jnp.int32),
            *([jax.ShapeDtypeStruct((_expert_slot_count(t),), jnp.int32)] * 5),
            jax.ShapeDtypeStruct((MOE_GROUPS, rows, d), BF16),
            jax.ShapeDtypeStruct((MOE_GROUPS, rows, LANES), F32),
        ],
        scratch_shapes=[
            pltpu.VMEM((2, SCR_ROWS, d), BF16),
            pltpu.VMEM((2, SCR_ROWS, LANES), F32),
            pltpu.SMEM((MOE_GROUPS,), jnp.int32),
            pltpu.SMEM((n_tiles, MOE_GROUPS), jnp.int32),
            pltpu.SemaphoreType.DMA((2, 4, MOE_GROUPS)),
        ],
        compiler_params=_params("arbitrary"),
        name="merge_router",
    )(y_ret, y_na, y_xa, proj, proj, proj, x2d, w_ret_o, w_na_o, w_xa_o, w_out, g_ffn, w_router, b_router, before)


def _expert_slot_count(t):
    return (t + (t // SORT_TILE) * MOE_GROUPS * (ROW_ALIGN - 1)) // EXPERT_TILE + MOE_GROUPS + 1


def _write_slot_tables(run, start_ref, slot_refs, out_start_ref):
    nblk, first = [], []
    total = jnp.int32(0)
    for g in range(MOE_GROUPS):
        n = (run[g] + EXPERT_TILE - 1) // EXPERT_TILE
        first.append(total)
        nblk.append(n)
        total = total + n

    def fill_slot(j, carry):
        for ref, value in zip(slot_refs, _expert_slot(j, nblk, first, total)):
            ref[j] = value.astype(jnp.int32)
        return carry

    lax.fori_loop(0, slot_refs[0].shape[0], fill_slot, 0)

    def fill_tile(i, carry):
        for g in range(MOE_GROUPS):
            out_start_ref[i, g] = first[g] * EXPERT_TILE + start_ref[i, g]
        return carry

    lax.fori_loop(0, out_start_ref.shape[0], fill_tile, 0)


def _expert_slot(j, nblk, first, total):
    valid = j < total
    jc = jnp.clip(j, 0, jnp.maximum(total - 1, 0))
    grp = jnp.int32(0)
    for g in range(MOE_GROUPS):
        grp = grp + (jc >= first[g] + nblk[g]).astype(jnp.int32)
    grp = jnp.minimum(grp, MOE_GROUPS - 1)
    first_blk = jnp.int32(0)
    nxt = jnp.int32(-1)
    for g in reversed(range(MOE_GROUPS)):
        first_blk = jnp.where(grp == g, first[g], first_blk)
        nxt = jnp.where((g > grp) & (nblk[g] > 0), g, nxt)
    return grp, jc - first_blk, valid, valid & (jc == first_blk), nxt


def _experts_kernel(grp_ref, blk_ref, valid_ref, first_ref, next_ref, h_ref, c_ref, wg_hbm, wu_hbm, wd_hbm, o_ref,
                    stage_g, stage_u, stage_d, wg_ref, wu_ref, wd_ref, sems):
    j = pl.program_id(0)
    grp, valid, is_first, nxt = grp_ref[j], valid_ref[j] == 1, first_ref[j] == 1, next_ref[j]

    def weight_copies(g):
        return (pltpu.make_async_copy(wg_hbm.at[g], stage_g, sems.at[0]),
                pltpu.make_async_copy(wu_hbm.at[g], stage_u, sems.at[1]),
                pltpu.make_async_copy(wd_hbm.at[g], stage_d, sems.at[2]))

    @pl.when(j == 0)
    def _():
        for cp in weight_copies(grp):
            cp.start()

    @pl.when(is_first)
    def _():
        for cp in weight_copies(grp):
            cp.wait()
        for stage, dst in ((stage_g, wg_ref), (stage_u, wu_ref), (stage_d, wd_ref)):
            for e in range(MOE_EXPERTS_PER_GROUP):
                def convert(i, carry, stage=stage, dst=dst, e=e):
                    rows = pl.ds(pl.multiple_of(i * CAST_ROWS, CAST_ROWS), CAST_ROWS)
                    dst[e, rows, :] = stage[e, rows, :].astype(dst.dtype)
                    return carry

                lax.fori_loop(0, stage.shape[1] // CAST_ROWS, convert, 0)

        @pl.when(nxt >= 0)
        def _():
            for cp in weight_copies(nxt):
                cp.start()

    @pl.when(valid)
    def _():
        h = h_ref[...]
        c = c_ref[...]
        lane = lax.broadcasted_iota(jnp.int32, c.shape, 1)
        out = None
        for e in range(MOE_EXPERTS_PER_GROUP):
            a = _dot(h, wg_ref[e])
            u = _dot(h, wu_ref[e])
            cw = jnp.sum(jnp.where(lane == e, c, 0.0), axis=-1, keepdims=True)
            hid = (_silu(a) * u * cw).astype(BF16)
            part = _dot(hid, wd_ref[e])
            out = part if out is None else out + part
        o_ref[...] = out.astype(o_ref.dtype)

    @pl.when(jnp.logical_not(valid))
    def _():
        o_ref[...] = jnp.zeros_like(o_ref)


def _experts(hbuf, cbuf, wg, wu, wd, slots):
    g_n, rows, d = hbuf.shape
    e_n, f = wg.shape[1], wg.shape[3]
    tm = EXPERT_TILE
    n_slots = slots[0].shape[0]
    hbm = pl.BlockSpec(memory_space=pl.ANY)
    grid_spec = pltpu.PrefetchScalarGridSpec(
        num_scalar_prefetch=5,
        grid=(n_slots,),
        in_specs=[
            pl.BlockSpec((None, tm, d), lambda j, grp, blk, v, gf, gx: (grp[j], blk[j], 0)),
            pl.BlockSpec((None, tm, LANES), lambda j, grp, blk, v, gf, gx: (grp[j], blk[j], 0)),
            hbm, hbm, hbm,
        ],
        out_specs=pl.BlockSpec((tm, d), lambda j, grp, blk, v, gf, gx: (j, 0)),
        scratch_shapes=[
            pltpu.VMEM((e_n, d, f), wg.dtype),
            pltpu.VMEM((e_n, d, f), wu.dtype),
            pltpu.VMEM((e_n, f, d), wd.dtype),
            pltpu.VMEM((e_n, d, f), BF16),
            pltpu.VMEM((e_n, d, f), BF16),
            pltpu.VMEM((e_n, f, d), BF16),
            pltpu.SemaphoreType.DMA((3,)),
        ],
    )
    return pl.pallas_call(
        _experts_kernel,
        grid_spec=grid_spec,
        out_shape=jax.ShapeDtypeStruct((n_slots * tm, d), BF16),
        compiler_params=pltpu.CompilerParams(dimension_semantics=("arbitrary",),
                                             vmem_limit_bytes=STAGING_VMEM_LIMIT),
        name="experts",
    )(*slots, hbuf, cbuf, wg, wu, wd)


def _finish_kernel(start_ref, cnt_ref, *refs, final_norm):
    wins = refs[:2 * MOE_GROUPS]
    pos_ref, x2_ref, gfin_ref, o_ref, sorted_scr = refs[2 * MOE_GROUPS:]
    i = pl.program_id(0)
    tm = x2_ref.shape[0]
    half_pieces = HALF_WIN // ROW_ALIGN

    @pl.when(i == 0)
    def _():
        sorted_scr[...] = jnp.zeros_like(sorted_scr)

    seg = jnp.int32(0)
    for g in range(MOE_GROUPS):
        padded = _pad_rows(cnt_ref[i, g])
        pieces = padded // ROW_ALIGN
        for half, win in enumerate(wins[2 * g:2 * g + 2]):
            base = seg + half * HALF_WIN

            def copy_piece(k, carry, win=win, base=base):
                src = pl.multiple_of(k * ROW_ALIGN, ROW_ALIGN)
                dst = pl.multiple_of(base + k * ROW_ALIGN, ROW_ALIGN)
                sorted_scr[pl.ds(dst, ROW_ALIGN), :] = win[pl.ds(src, ROW_ALIGN), :]
                return carry

            lax.fori_loop(0, jnp.clip(pieces - half * half_pieces, 0, half_pieces), copy_piece, 0)
        seg = seg + padded

    pos = pos_ref[:, 0:1]
    unperm = jnp.where(lax.broadcasted_iota(jnp.int32, (tm, SORTED_ROWS), 1).astype(F32) == pos, 1.0, 0.0)
    y = x2_ref[...] + _dot(unperm.astype(BF16), sorted_scr[...])
    if final_norm:
        y = _rmsnorm_f32(y, gfin_ref[...])
    o_ref[...] = y


def _finish(mbuf, starts, counts, pos, x2, g_final, final_norm):
    t, d = x2.shape
    tm = SORT_TILE

    def window(g, half):
        def index(i, st, ct):
            row = st[i, g] + half * HALF_WIN
            if half:
                row = jnp.where(_pad_rows(ct[i, g]) > HALF_WIN, row, 0)
            return pl.multiple_of(row, ROW_ALIGN), 0

        return pl.BlockSpec((pl.Element(HALF_WIN), pl.Element(d)), index)

    grid_spec = pltpu.PrefetchScalarGridSpec(
        num_scalar_prefetch=2,
        grid=(t // tm,),
        in_specs=[
            *[window(g, half) for g in range(MOE_GROUPS) for half in range(2)],
            pl.BlockSpec((tm, LANES), lambda i, st, ct: (i, 0)),
            pl.BlockSpec((tm, d), lambda i, st, ct: (i, 0)),
            pl.BlockSpec((1, d), lambda i, st, ct: (0, 0)),
        ],
        out_specs=pl.BlockSpec((tm, d), lambda i, st, ct: (i, 0)),
        scratch_shapes=[pltpu.VMEM((SORTED_ROWS, d), BF16)],
    )
    return pl.pallas_call(
        functools.partial(_finish_kernel, final_norm=final_norm),
        grid_spec=grid_spec,
        out_shape=jax.ShapeDtypeStruct((t, d), F32),
        compiler_params=_params("arbitrary"),
        name="finish",
    )(starts, counts, *([mbuf] * (2 * MOE_GROUPS)), pos, x2, g_final)


def kernel(x, mem, g_mix, w_in, ret_decay_fwd, ret_decay_bwd, ret_norm_gain, w_ret_o, na_rpb, w_na_o, g_mem, w_mem_kv, w_xa_o, w_out, g_ffn, w_router_group, b_router_group, w_router_expert, b_router_expert, w_exp_gate, w_exp_up, w_exp_down, g_final):
    b, s, d = x.shape
    depth = w_in.shape[0]
    t = b * s
    cos_t, sin_t = _rope_tables(s)
    row = lambda v: v.reshape(1, -1).astype(F32)
    x2d = x.reshape(t, d)
    for l in range(depth):
        proj = _inproj(x2d, row(g_mix[l]), w_in[l], cos_t, sin_t)
        proj3 = proj.reshape(b, s, IN_WIDTH)
        y_ret = _retention(proj3, ret_decay_fwd[l].astype(F32), ret_decay_bwd[l].astype(F32),
                           row(ret_norm_gain[l]))
        y_na = _neighbourhood_attention(proj3, _na_bias_table(na_rpb[l]))
        y_xa = _memory_attention(proj3, mem, row(g_mem[l]), w_mem_kv[l].astype(BF16))
        n_r = MOE_GROUPS + N_EXPERTS
        w_router = jnp.pad(jnp.concatenate([w_router_group[l], w_router_expert[l]], axis=1).astype(F32),
                           ((0, 0), (0, LANES - n_r)))
        b_router = jnp.pad(jnp.concatenate([b_router_group[l], b_router_expert[l]]).astype(F32),
                           (0, LANES - n_r)).reshape(1, LANES)
        x2, pos, counts, out_starts, *slots, hbuf, cbuf = _merge(
            y_ret.reshape(t, -1), y_na.reshape(t, -1), y_xa.reshape(t, -1), proj, x2d,
            w_ret_o[l].astype(BF16), w_na_o[l].astype(BF16), w_xa_o[l].astype(BF16),
            w_out[l].astype(BF16), row(g_ffn[l]), w_router, b_router)
        mbuf = _experts(hbuf, cbuf, w_exp_gate[l], w_exp_up[l], w_exp_down[l], slots)
        x2d = _finish(mbuf, out_starts, counts, pos, x2, row(g_final), final_norm=(l == depth - 1))
    return x2d.reshape(b, s, d)
```

```python
import functools

import jax
import jax.numpy as jnp
import numpy as np
from jax import lax
from jax.experimental import pallas as pl
from jax.experimental.pallas import tpu as pltpu

D_MODEL = 1024
GRID_W = 64
N_BRANCHES = 3
RET_HEADS = 4
RET_QK_DIM = 128
RET_V_DIM = 256
RET_CHUNK = 128
ROPE_BASE = 10000.0
NA_HEADS = 8
NA_HEAD_DIM = 64
NA_ROWS = 8
NA_COLS = 16
XA_HEADS = 4
XA_HEAD_DIM = 128
MOE_GROUPS = 4
MOE_EXPERTS_PER_GROUP = 4
MOE_TOP_K = 2
MOE_D_FF = 512
RMS_EPS = 1e-6
GN_EPS = 1e-5

RET_QK_WIDTH = RET_HEADS * RET_QK_DIM
RET_V_WIDTH = RET_HEADS * RET_V_DIM
NA_WIDTH = NA_HEADS * NA_HEAD_DIM
XA_WIDTH = XA_HEADS * XA_HEAD_DIM
IN_WIDTH = 2 * RET_QK_WIDTH + 2 * RET_V_WIDTH + 3 * NA_WIDTH + XA_WIDTH + N_BRANCHES * D_MODEL

OFF_RQ = 0
OFF_RK = OFF_RQ + RET_QK_WIDTH
OFF_RV = OFF_RK + RET_QK_WIDTH
OFF_RG = OFF_RV + RET_V_WIDTH
OFF_NQ = OFF_RG + RET_V_WIDTH
OFF_NK = OFF_NQ + NA_WIDTH
OFF_NV = OFF_NK + NA_WIDTH
OFF_XQ = OFF_NV + NA_WIDTH
OFF_GATE = OFF_XQ + XA_WIDTH

N_EXPERTS = MOE_GROUPS * MOE_EXPERTS_PER_GROUP
LANES = 128
ROUTER_EXPERT_LANE0 = MOE_GROUPS

VMEM_LIMIT = 48 * 1024 * 1024
STAGING_VMEM_LIMIT = 56 * 1024 * 1024
CAST_ROWS = 64

F32 = jnp.float32
BF16 = jnp.bfloat16


def _params(*sem):
    return pltpu.CompilerParams(dimension_semantics=sem, vmem_limit_bytes=VMEM_LIMIT)


def _rmsnorm_f32(x, g):
    return x * lax.rsqrt(jnp.mean(x * x, axis=-1, keepdims=True) + RMS_EPS) * g


def _sigmoid(x):
    return 0.5 * jnp.tanh(0.5 * x) + 0.5


def _silu(x):
    h = 0.5 * x
    return h + h * jnp.tanh(h)


def _dot(a, b):
    return jnp.dot(a, b, preferred_element_type=F32)


def _dot_tn(a, b):
    return lax.dot_general(a, b, (((0,), (0,)), ((), ())), preferred_element_type=F32)


def _dot_nt(a, b):
    return lax.dot_general(a, b, (((1,), (1,)), ((), ())), preferred_element_type=F32)


INPROJ_STAGE_COLS = 1024


def _inproj_kernel(x_ref, g_ref, cos_ref, sin_ref, w_hbm, o_ref, h_scr, w_scr, stage, sems):
    i = pl.program_id(0)
    j = pl.program_id(1)
    n_col = pl.num_programs(1)
    tn = o_ref.shape[1]
    per_block = tn // INPROJ_STAGE_COLS

    def chunk_copy(col_block, k):
        cols = pl.ds(pl.multiple_of((col_block * per_block + k) * INPROJ_STAGE_COLS, INPROJ_STAGE_COLS),
                     INPROJ_STAGE_COLS)
        return pltpu.make_async_copy(w_hbm.at[:, cols], stage.at[k], sems.at[k])

    @pl.when((i == 0) & (j == 0))
    def _():
        for k in range(per_block):
            chunk_copy(0, k).start()

    @pl.when(i == 0)
    def _():
        for k in range(per_block):
            chunk_copy(j, k).wait()

            def convert(r, carry, k=k):
                rows = pl.ds(pl.multiple_of(r * CAST_ROWS, CAST_ROWS), CAST_ROWS)
                w_scr[j, rows, k * INPROJ_STAGE_COLS:(k + 1) * INPROJ_STAGE_COLS] = stage[k, rows, :].astype(BF16)
                return carry

            lax.fori_loop(0, stage.shape[1] // CAST_ROWS, convert, 0)

        @pl.when(j + 1 < n_col)
        def _():
            for k in range(per_block):
                chunk_copy(j + 1, k).start()

    @pl.when(j == 0)
    def _():
        h_scr[...] = _rmsnorm_f32(x_ref[...], g_ref[...]).astype(BF16)

    half = RET_QK_DIM // 2
    for jj in range(w_scr.shape[0]):
        @pl.when(j == jj)
        def _(jj=jj):
            res = _dot(h_scr[...], w_scr[jj])
            for c in range(0, tn, RET_QK_DIM):
                a = res[:, c:c + RET_QK_DIM]
                col = jj * tn + c
                if OFF_RQ <= col < OFF_RV:
                    a = a * cos_ref[...] + pltpu.roll(a, half, 1) * sin_ref[...]
                    if col < OFF_RK:
                        a = a * (RET_QK_DIM ** -0.5)
                elif OFF_RG <= col < OFF_NQ:
                    a = _silu(a)
                o_ref[:, c:c + RET_QK_DIM] = a.astype(o_ref.dtype)


def _inproj(x2d, g, w, cos_t, sin_t, tm=1024, tn=2048):
    t, d = x2d.shape
    n = w.shape[1]
    per_block = tn // INPROJ_STAGE_COLS
    seq_tiles = cos_t.shape[0] // tm
    return pl.pallas_call(
        _inproj_kernel,
        grid=(t // tm, n // tn),
        in_specs=[
            pl.BlockSpec((tm, d), lambda i, j: (i, 0)),
            pl.BlockSpec((1, d), lambda i, j: (0, 0)),
            pl.BlockSpec((tm, RET_QK_DIM), lambda i, j: (i % seq_tiles, 0)),
            pl.BlockSpec((tm, RET_QK_DIM), lambda i, j: (i % seq_tiles, 0)),
            pl.BlockSpec(memory_space=pl.ANY),
        ],
        out_specs=pl.BlockSpec((tm, tn), lambda i, j: (i, j)),
        out_shape=jax.ShapeDtypeStruct((t, n), BF16),
        scratch_shapes=[
            pltpu.VMEM((tm, d), BF16),
            pltpu.VMEM((n // tn, d, tn), BF16),
            pltpu.VMEM((per_block, d, INPROJ_STAGE_COLS), w.dtype),
            pltpu.SemaphoreType.DMA((per_block,)),
        ],
        compiler_params=pltpu.CompilerParams(dimension_semantics=("arbitrary", "arbitrary"),
                                             vmem_limit_bytes=STAGING_VMEM_LIMIT),
        name="inproj",
    )(x2d, g, cos_t, sin_t, w)


def _log_sigmoid(x):
    return jnp.minimum(x, 0.0) - jnp.log1p(jnp.exp(-jnp.abs(x)))


RET_HEADS_PER_STEP = 2


def _retention_kernel(decf_ref, decb_ref, q_ref, k_ref, v_ref, rg_ref, gn_ref, o_ref, *scratch):
    qk, dv = RET_QK_DIM, RET_V_DIM
    for hh in range(RET_HEADS_PER_STEP):
        qcols = slice(hh * qk, (hh + 1) * qk)
        vcols = slice(hh * dv, (hh + 1) * dv)
        _retention_head(pl.program_id(1) * RET_HEADS_PER_STEP + hh, decf_ref, decb_ref,
                        q_ref.at[:, qcols], k_ref.at[:, qcols], v_ref.at[:, vcols], rg_ref.at[:, vcols],
                        gn_ref.at[:, vcols], o_ref.at[:, vcols], *[s.at[hh] for s in scratch])


def _retention_head(h, decf_ref, decb_ref, q_ref, k_ref, v_ref, rg_ref, gn_ref, o_ref,
                    qfs, qbs, y_scr, kvf_scr, kvb_scr, st_scr):
    c = RET_CHUNK
    seq = q_ref.shape[0]
    n_chunks = seq // c

    lgf = _log_sigmoid(jnp.full((1, 1), decf_ref[h], F32))
    lgb = _log_sigmoid(jnp.full((1, 1), decb_ref[h], F32))

    ii = lax.broadcasted_iota(jnp.int32, (c, c), 0)
    jj = lax.broadcasted_iota(jnp.int32, (c, c), 1)
    diff = (ii - jj).astype(F32)
    dmat = jnp.exp(jnp.where(diff >= 0, diff * lgf, (-diff) * lgb))
    pos = lax.broadcasted_iota(jnp.int32, (c, 1), 0).astype(F32)
    qd_f = jnp.exp((pos + 1.0) * lgf)
    kd_f = jnp.exp((c - 1.0 - pos) * lgf)
    cd_f = jnp.exp(c * lgf)
    qd_b = jnp.exp((c - pos) * lgb)
    kd_b = jnp.exp(pos * lgb)
    cd_b = jnp.exp(c * lgb)

    def intra(n):
        rows = pl.ds(n * c, c)
        q = q_ref[rows, :].astype(F32)
        qfs[rows, :] = (q * qd_f).astype(BF16)
        qbs[rows, :] = (q * qd_b).astype(BF16)
        k = k_ref[rows, :].astype(F32)
        s = _dot_nt(q_ref[rows, :], k_ref[rows, :]) * dmat
        lhs = jnp.concatenate([s.astype(BF16), (k * kd_f).T.astype(BF16), (k * kd_b).T.astype(BF16)], axis=0)
        r = _dot(lhs, v_ref[rows, :])
        y_scr[rows, :] = r[:c]
        kvf_scr[n] = r[c:2 * c]
        kvb_scr[n] = r[2 * c:]

    for n in range(n_chunks):
        intra(n)

    st_scr[...] = jnp.zeros_like(st_scr)

    def fwd(n):
        rows = pl.ds(n * c, c)
        st = st_scr[...]
        y_scr[rows, :] += _dot(qfs[rows, :], st.astype(BF16))
        st_scr[...] = st * cd_f + kvf_scr[n]

    for n in range(n_chunks):
        fwd(n)

    st_scr[...] = jnp.zeros_like(st_scr)
    gn = gn_ref[...]

    def bwd(n):
        rows = pl.ds(n * c, c)
        st = st_scr[...]
        y = y_scr[rows, :] + _dot(qbs[rows, :], st.astype(BF16))
        st_scr[...] = st * cd_b + kvb_scr[n]
        mu = jnp.mean(y, axis=-1, keepdims=True)
        yc = y - mu
        var = jnp.mean(yc * yc, axis=-1, keepdims=True)
        yn = yc * lax.rsqrt(var + GN_EPS) * gn
        o_ref[rows, :] = rg_ref[rows, :] * yn.astype(o_ref.dtype)

    for n in reversed(range(n_chunks)):
        bwd(n)


def _retention(proj3, dec_f, dec_b, gn_gain):
    b, s, _ = proj3.shape
    hps = RET_HEADS_PER_STEP
    dk, dv = RET_QK_DIM, RET_V_DIM
    qb, vb = hps * dk, hps * dv
    smem = pl.BlockSpec(memory_space=pltpu.SMEM)
    return pl.pallas_call(
        _retention_kernel,
        grid=(b, RET_HEADS // hps),
        in_specs=[
            smem,
            smem,
            pl.BlockSpec((None, s, qb), lambda i, h: (i, 0, OFF_RQ // qb + h)),
            pl.BlockSpec((None, s, qb), lambda i, h: (i, 0, OFF_RK // qb + h)),
            pl.BlockSpec((None, s, vb), lambda i, h: (i, 0, OFF_RV // vb + h)),
            pl.BlockSpec((None, s, vb), lambda i, h: (i, 0, OFF_RG // vb + h)),
            pl.BlockSpec((1, vb), lambda i, h: (0, h)),
        ],
        out_specs=pl.BlockSpec((None, s, vb), lambda i, h: (i, 0, h)),
        out_shape=jax.ShapeDtypeStruct((b, s, RET_V_WIDTH), BF16),
        scratch_shapes=[
            pltpu.VMEM((hps, s, dk), BF16),
            pltpu.VMEM((hps, s, dk), BF16),
            pltpu.VMEM((hps, s, dv), F32),
            pltpu.VMEM((hps, s // RET_CHUNK, dk, dv), F32),
            pltpu.VMEM((hps, s // RET_CHUNK, dk, dv), F32),
            pltpu.VMEM((hps, dk, dv), F32),
        ],
        compiler_params=_params("parallel", "parallel"),
        name="retention",
    )(dec_f, dec_b, proj3, proj3, proj3, proj3, gn_gain)


def _rope_tables(seq):
    half = RET_QK_DIM // 2
    inv_freq = ROPE_BASE ** (-np.arange(half, dtype=np.float64) / half)
    ang = np.arange(seq, dtype=np.float64)[:, None] * inv_freq[None, :]
    cos, sin = np.cos(ang), np.sin(ang)
    return (jnp.asarray(np.concatenate([cos, cos], axis=-1), F32),
            jnp.asarray(np.concatenate([-sin, sin], axis=-1), F32))


def _na_bias_table(rpb):
    heads = rpb.shape[0]
    w = GRID_W
    cols = np.arange(w)
    col_start = np.clip(cols - NA_COLS // 2, 0, w - NA_COLS)
    col_off = cols[None, :] - col_start[:, None]
    col_mask = (col_off >= 0) & (col_off < NA_COLS)
    rel_c = np.clip(cols[None, :] - cols[:, None], -(NA_COLS - 1), NA_COLS - 1) + (NA_COLS - 1)
    onehot = (rel_c[:, :, None] == np.arange(2 * NA_COLS - 1)).astype(np.float32)
    toe = jnp.einsum('hrc,qkc->hrqk', rpb.astype(F32), onehot, precision=lax.Precision.HIGHEST)
    toe = jnp.where(col_mask[None, None], toe, -jnp.inf)
    two = jnp.concatenate([toe[:, :-1], toe[:, 1:]], axis=-1)
    two = two.reshape(heads // 2, 2, 2 * NA_ROWS - 2, w, 2 * w)
    return two.transpose(0, 2, 1, 3, 4).reshape(heads // 2, 2 * NA_ROWS - 2, 2 * w, 2 * w)


NA_ROWS_PER_STEP = 16


def _na_kernel(q_ref, k_ref, v_ref, bias_ref, o_ref, s_scr, p_scr):
    w = GRID_W
    seq = q_ref.shape[0]
    rows_n = seq // w
    kr = NA_ROWS
    first = lax.broadcasted_iota(jnp.int32, (w, LANES), 1) < NA_HEAD_DIM
    scale = NA_HEAD_DIM ** -0.5

    def window_row(r):
        return jnp.clip(r - kr // 2, 0, rows_n - kr)

    def body(it, carry):
        r0 = it * NA_ROWS_PER_STEP
        for u in range(NA_ROWS_PER_STEP):
            r = r0 + u
            rs = window_row(r)
            qr = q_ref[pl.ds(pl.multiple_of(r * w, w), w), :] * scale
            zero = jnp.zeros_like(qr)
            q2 = jnp.concatenate([jnp.where(first, qr, zero), jnp.where(first, zero, qr)], axis=0)
            kk = k_ref[pl.ds(pl.multiple_of(rs * w, w), kr * w), :]
            off = rs - r + (NA_ROWS - 1)
            bias = jnp.concatenate([bias_ref[off + 2 * i] for i in range(kr // 2)], axis=1)
            s_scr[u] = _dot_nt(q2, kk) + bias
        for u in range(NA_ROWS_PER_STEP):
            m = jnp.max(s_scr[u], axis=-1, keepdims=True)
            p_scr[u] = jnp.exp(s_scr[u] - m).astype(BF16)
        for u in range(NA_ROWS_PER_STEP):
            r = r0 + u
            vv = v_ref[pl.ds(pl.multiple_of(window_row(r) * w, w), kr * w), :]
            o2 = _dot(p_scr[u], jnp.concatenate([vv, jnp.ones_like(vv)], axis=1))
            o2 = o2[:, :LANES] / o2[:, LANES:]
            o_ref[pl.ds(pl.multiple_of(r * w, w), w), :] = jnp.where(first, o2[:w], o2[w:]).astype(o_ref.dtype)
        return carry

    lax.fori_loop(0, rows_n // NA_ROWS_PER_STEP, body, 0)


def _neighbourhood_attention(proj3, bias_tab):
    b, s, _ = proj3.shape
    pairs = NA_HEADS // 2
    return pl.pallas_call(
        _na_kernel,
        grid=(b, pairs),
        in_specs=[
            pl.BlockSpec((None, s, LANES), lambda i, p: (i, 0, OFF_NQ // LANES + p)),
            pl.BlockSpec((None, s, LANES), lambda i, p: (i, 0, OFF_NK // LANES + p)),
            pl.BlockSpec((None, s, LANES), lambda i, p: (i, 0, OFF_NV // LANES + p)),
            pl.BlockSpec((None, 2 * NA_ROWS - 2, 2 * GRID_W, 2 * GRID_W), lambda i, p: (p, 0, 0, 0)),
        ],
        out_specs=pl.BlockSpec((None, s, LANES), lambda i, p: (i, 0, p)),
        out_shape=jax.ShapeDtypeStruct((b, s, NA_WIDTH), BF16),
        scratch_shapes=[
            pltpu.VMEM((NA_ROWS_PER_STEP, 2 * GRID_W, NA_ROWS * GRID_W), F32),
            pltpu.VMEM((NA_ROWS_PER_STEP, 2 * GRID_W, NA_ROWS * GRID_W), BF16),
        ],
        compiler_params=_params("parallel", "parallel"),
        name="nbr_attn",
    )(proj3, proj3, proj3, bias_tab)


def _xa_kernel(q_ref, mem_ref, g_ref, wkv_ref, o_ref, kv_scr):
    @pl.when(pl.program_id(1) == 0)
    def _():
        mn = _rmsnorm_f32(mem_ref[...], g_ref[...]).astype(BF16)
        kv_scr[...] = _dot(mn, wkv_ref[...]).astype(BF16)

    dh = XA_HEAD_DIM
    scale = dh ** -0.5
    for h in range(XA_HEADS):
        q = q_ref[:, h * dh:(h + 1) * dh]
        k = kv_scr[:, h * dh:(h + 1) * dh]
        v = kv_scr[:, XA_WIDTH + h * dh:XA_WIDTH + (h + 1) * dh]
        s = _dot_nt(q, k) * scale
        m = jnp.max(s, axis=-1, keepdims=True)
        p = jnp.exp(s - m).astype(BF16)
        o2 = _dot(p, jnp.concatenate([v, jnp.ones_like(v)], axis=1))
        o_ref[:, h * dh:(h + 1) * dh] = (o2[:, :dh] / o2[:, dh:]).astype(o_ref.dtype)


def _memory_attention(proj3, mem, g_mem, wkv_bf16, ts=2048):
    b, s, _ = proj3.shape
    m, d = mem.shape[1], mem.shape[2]
    return pl.pallas_call(
        _xa_kernel,
        grid=(b, s // ts),
        in_specs=[
            pl.BlockSpec((None, ts, XA_WIDTH), lambda i, j: (i, j, OFF_XQ // XA_WIDTH)),
            pl.BlockSpec((None, m, d), lambda i, j: (i, 0, 0)),
            pl.BlockSpec((1, d), lambda i, j: (0, 0)),
            pl.BlockSpec((d, 2 * XA_WIDTH), lambda i, j: (0, 0)),
        ],
        out_specs=pl.BlockSpec((None, ts, XA_WIDTH), lambda i, j: (i, j, 0)),
        out_shape=jax.ShapeDtypeStruct((b, s, XA_WIDTH), BF16),
        scratch_shapes=[pltpu.VMEM((m, 2 * XA_WIDTH), BF16)],
        compiler_params=_params("parallel", "arbitrary"),
        name="mem_attn",
    )(proj3, mem, g_mem, wkv_bf16)


def _masked_lane_max(v, mask):
    return jnp.max(jnp.where(mask, v, -jnp.inf), axis=-1, keepdims=True)


def _first_lane_eq(v, target, mask, lane):
    return jnp.min(jnp.where(mask & (v == target), lane, float(LANES)), axis=-1, keepdims=True)


def _route(logits):
    g_n, e_n = MOE_GROUPS, MOE_EXPERTS_PER_GROUP
    lane = lax.broadcasted_iota(jnp.int32, logits.shape, 1).astype(F32)
    is_grp = lane < g_n
    gmax = _masked_lane_max(logits, is_grp)
    gsum = jnp.sum(jnp.where(is_grp, jnp.exp(logits - gmax), 0.0), axis=-1, keepdims=True)
    grp_w = 1.0 / gsum
    gidx = _first_lane_eq(logits, gmax, is_grp, lane)
    lo = ROUTER_EXPERT_LANE0 + gidx * e_n
    in_grp = (lane >= lo) & (lane < lo + e_n)
    emax = _masked_lane_max(logits, in_grp)
    ex = jnp.where(in_grp, jnp.exp(logits - emax), 0.0)
    prob = ex / jnp.sum(ex, axis=-1, keepdims=True)
    p1 = _masked_lane_max(prob, in_grp)
    i1 = _first_lane_eq(prob, p1, in_grp, lane)
    rest = in_grp & (lane != i1)
    p2 = _masked_lane_max(prob, rest)
    i2 = _first_lane_eq(prob, p2, rest, lane)
    tot = p1 + p2
    w1 = p1 / tot * grp_w
    w2 = p2 / tot * grp_w
    return gidx, jnp.where(lane == i1 - lo, w1, jnp.where(lane == i2 - lo, w2, 0.0))


SORT_TILE = 512
ROW_ALIGN = 16
SORTED_ROWS = 640
WIN_ROWS = SORT_TILE
HALF_WIN = WIN_ROWS // 2
SCR_ROWS = SORTED_ROWS + WIN_ROWS
EXPERT_TILE = 512


def _group_buf_rows(t):
    return t + 2 * WIN_ROWS


def _pad_rows(n):
    return (n + (ROW_ALIGN - 1)) // ROW_ALIGN * ROW_ALIGN


def _window_copies(hs, cs, hbuf, cbuf, sems, slot, kind, g, src_row, dst_row):
    return (
        pltpu.make_async_copy(hs.at[slot, pl.ds(src_row, WIN_ROWS)], hbuf.at[g, pl.ds(dst_row, WIN_ROWS)],
                              sems.at[slot, 2 * kind, g]),
        pltpu.make_async_copy(cs.at[slot, pl.ds(src_row, WIN_ROWS)], cbuf.at[g, pl.ds(dst_row, WIN_ROWS)],
                              sems.at[slot, 2 * kind + 1, g]),
    )


def _merge_kernel(yret_ref, yna_ref, yxa_ref, gr_ref, gn_ref, gx_ref, x_ref, wro_ref, wno_ref, wxo_ref, wout_ref,
                  gffn_ref, wr_ref, br_ref, before_ref,
                  x2_ref, pos_ref, cnt_ref, out_start_ref,
                  slot_grp, slot_blk, slot_valid, slot_first, slot_next, hbuf, cbuf,
                  hs, cs, run, start_ref, sems):
    i = pl.program_id(0)
    n_tiles = pl.num_programs(0)
    slot = lax.rem(i, 2)
    tm = x_ref.shape[0]
    g_n = MOE_GROUPS

    @pl.when(i == 0)
    def _():
        for g in range(g_n):
            run[g] = 0
        hs[:, SORTED_ROWS:, :] = jnp.zeros((2, WIN_ROWS, hs.shape[2]), hs.dtype)
        cs[:, SORTED_ROWS:, :] = jnp.zeros((2, WIN_ROWS, cs.shape[2]), cs.dtype)

    y_ret = _dot(yret_ref[...], wro_ref[...])
    y_na = _dot(yna_ref[...], wno_ref[...])
    y_xa = _dot(yxa_ref[...], wxo_ref[...])
    mix = (_sigmoid(gr_ref[...].astype(F32)) * y_ret + _sigmoid(gn_ref[...].astype(F32)) * y_na
           + _sigmoid(gx_ref[...].astype(F32)) * y_xa)
    x2 = x_ref[...] + _dot(mix.astype(BF16), wout_ref[...])
    x2_ref[...] = x2
    h2 = _rmsnorm_f32(x2, gffn_ref[...])
    h_hi = h2.astype(BF16)
    h_lo = (h2 - h_hi.astype(F32)).astype(BF16)
    wr = wr_ref[...]
    wr_hi = wr.astype(BF16)
    wr_lo = (wr - wr_hi.astype(F32)).astype(BF16)
    hi_terms = _dot(h_hi, jnp.concatenate([wr_hi, wr_lo], axis=1))
    logits = hi_terms[:, :LANES] + hi_terms[:, LANES:] + _dot(h_lo, wr_hi) + br_ref[...]

    gidx, w4 = _route(logits)

    lane = lax.broadcasted_iota(jnp.int32, (tm, LANES), 1).astype(F32)
    onehot = jnp.where(lane == gidx, 1.0, 0.0)
    rank = _dot(before_ref[...], onehot.astype(BF16))
    count_row = rank[tm - 1:tm, :] + onehot[tm - 1:tm, :]
    lane_row = lax.broadcasted_iota(jnp.int32, (1, LANES), 1)
    counts = [jnp.sum(jnp.where(lane_row == g, count_row, 0.0)).astype(jnp.int32) for g in range(g_n)]
    seg_start = []
    acc = jnp.int32(0)
    for g in range(g_n):
        seg_start.append(acc)
        acc = acc + _pad_rows(counts[g])
    start_row = jnp.zeros((1, LANES), F32)
    for g in range(g_n):
        start_row = jnp.where(lane_row == g, seg_start[g].astype(F32), start_row)
    pos = jnp.sum(onehot * (rank + start_row), axis=-1, keepdims=True)
    pos_ref[...] = jnp.broadcast_to(pos, (tm, LANES))

    pos_lanes = jnp.transpose(jnp.broadcast_to(pos, (tm, LANES)))[0:1, :]
    perm = jnp.where(lax.broadcasted_iota(jnp.int32, (SORTED_ROWS, tm), 0).astype(F32) == pos_lanes, 1.0, 0.0)
    perm = perm.astype(BF16)
    hs[slot, 0:SORTED_ROWS, :] = _dot(perm, h_hi).astype(hs.dtype)
    e_n = MOE_EXPERTS_PER_GROUP
    w_hi = w4.astype(BF16).astype(F32)
    r1 = w4 - w_hi
    w_mid = r1.astype(BF16).astype(F32)
    w_lo = r1 - w_mid
    pieces = (w_hi + pltpu.roll(w_mid, e_n, 1) + pltpu.roll(w_lo, 2 * e_n, 1)).astype(BF16)
    sorted_pieces = _dot(perm, pieces)
    cs[slot, 0:SORTED_ROWS, :] = (sorted_pieces + pltpu.roll(sorted_pieces, LANES - e_n, 1)
                                  + pltpu.roll(sorted_pieces, LANES - 2 * e_n, 1))

    copies = functools.partial(_window_copies, hs, cs, hbuf, cbuf, sems)

    def wait_all(which_slot):
        for kind in range(2):
            for g in range(g_n):
                for cp in copies(which_slot, kind, g, 0, 0):
                    cp.wait()

    @pl.when(i > 0)
    def _():
        wait_all(1 - slot)

    @pl.when(i == 0)
    def _():
        top = hbuf.shape[1] - WIN_ROWS
        for g in range(g_n):
            for cp in copies(slot, 1, g, SORTED_ROWS, top):
                cp.start()
        for g in range(g_n):
            for cp in copies(slot, 1, g, 0, 0):
                cp.wait()

    tiles_left = n_tiles - 1 - i
    for g in range(g_n):
        c_g = run[g]
        c_next = c_g + _pad_rows(counts[g])
        start_ref[i, g] = c_g
        cnt_ref[i, g] = counts[g]
        run[g] = c_next
        for cp in copies(slot, 0, g, pl.multiple_of(seg_start[g], ROW_ALIGN), pl.multiple_of(c_g, ROW_ALIGN)):
            cp.start()
        dead = c_next + tiles_left * WIN_ROWS + WIN_ROWS
        for cp in copies(slot, 1, g, SORTED_ROWS, pl.multiple_of(dead, ROW_ALIGN)):
            cp.start()

    @pl.when(i == n_tiles - 1)
    def _():
        wait_all(slot)
        for g in range(g_n):
            for cp in copies(slot, 0, g, SORTED_ROWS, pl.multiple_of(run[g], ROW_ALIGN)):
                cp.start()
        for g in range(g_n):
            for cp in copies(slot, 0, g, 0, 0):
                cp.wait()
        _write_slot_tables(run, start_ref, (slot_grp, slot_blk, slot_valid, slot_first, slot_next), out_start_ref)


def _merge(y_ret, y_na, y_xa, proj, x2d, w_ret_o, w_na_o, w_xa_o, w_out, g_ffn, w_router, b_router):
    t, d = x2d.shape
    tm = SORT_TILE
    n_tiles = t // tm
    rows = _group_buf_rows(t)
    gate_blk = OFF_GATE // d
    before = jnp.asarray(np.tril(np.ones((tm, tm), np.float32), -1), BF16)
    full = lambda a: pl.BlockSpec(a.shape, lambda i: (0,) * a.ndim)
    smem = pl.BlockSpec(memory_space=pltpu.SMEM)
    hbm = pl.BlockSpec(memory_space=pl.ANY)
    return pl.pallas_call(
        _merge_kernel,
        grid=(n_tiles,),
        in_specs=[
            pl.BlockSpec((tm, RET_V_WIDTH), lambda i: (i, 0)),
            pl.BlockSpec((tm, NA_WIDTH), lambda i: (i, 0)),
            pl.BlockSpec((tm, XA_WIDTH), lambda i: (i, 0)),
            pl.BlockSpec((tm, d), lambda i: (i, gate_blk)),
            pl.BlockSpec((tm, d), lambda i: (i, gate_blk + 1)),
            pl.BlockSpec((tm, d), lambda i: (i, gate_blk + 2)),
            pl.BlockSpec((tm, d), lambda i: (i, 0)),
            full(w_ret_o), full(w_na_o), full(w_xa_o), full(w_out), full(g_ffn), full(w_router), full(b_router),
            full(before),
        ],
        out_specs=[
            pl.BlockSpec((tm, d), lambda i: (i, 0)),
            pl.BlockSpec((tm, LANES), lambda i: (i, 0)),
            smem, smem, *([smem] * 5), hbm, hbm,
        ],
        out_shape=[
            jax.ShapeDtypeStruct((t, d), F32),
            jax.ShapeDtypeStruct((t, LANES), F32),
            jax.ShapeDtypeStruct((n_tiles, MOE_GROUPS), jnp.int32),
            jax.ShapeDtypeStruct((n_tiles, MOE_GROUPS), jnp.int32),
            *([jax.ShapeDtypeStruct((_expert_slot_count(t),), jnp.int32)] * 5),
            jax.ShapeDtypeStruct((MOE_GROUPS, rows, d), BF16),
            jax.ShapeDtypeStruct((MOE_GROUPS, rows, LANES), F32),
        ],
        scratch_shapes=[
            pltpu.VMEM((2, SCR_ROWS, d), BF16),
            pltpu.VMEM((2, SCR_ROWS, LANES), F32),
            pltpu.SMEM((MOE_GROUPS,), jnp.int32),
            pltpu.SMEM((n_tiles, MOE_GROUPS), jnp.int32),
            pltpu.SemaphoreType.DMA((2, 4, MOE_GROUPS)),
        ],
        compiler_params=_params("arbitrary"),
        name="merge_router",
    )(y_ret, y_na, y_xa, proj, proj, proj, x2d, w_ret_o, w_na_o, w_xa_o, w_out, g_ffn, w_router, b_router, before)


def _expert_slot_count(t):
    return (t + (t // SORT_TILE) * MOE_GROUPS * (ROW_ALIGN - 1)) // EXPERT_TILE + MOE_GROUPS + 1


def _write_slot_tables(run, start_ref, slot_refs, out_start_ref):
    nblk, first = [], []
    total = jnp.int32(0)
    for g in range(MOE_GROUPS):
        n = (run[g] + EXPERT_TILE - 1) // EXPERT_TILE
        first.append(total)
        nblk.append(n)
        total = total + n

    def fill_slot(j, carry):
        for ref, value in zip(slot_refs, _expert_slot(j, nblk, first, total)):
            ref[j] = value.astype(jnp.int32)
        return carry

    lax.fori_loop(0, slot_refs[0].shape[0], fill_slot, 0)

    def fill_tile(i, carry):
        for g in range(MOE_GROUPS):
            out_start_ref[i, g] = first[g] * EXPERT_TILE + start_ref[i, g]
        return carry

    lax.fori_loop(0, out_start_ref.shape[0], fill_tile, 0)


def _expert_slot(j, nblk, first, total):
    valid = j < total
    jc = jnp.clip(j, 0, jnp.maximum(total - 1, 0))
    grp = jnp.int32(0)
    for g in range(MOE_GROUPS):
        grp = grp + (jc >= first[g] + nblk[g]).astype(jnp.int32)
    grp = jnp.minimum(grp, MOE_GROUPS - 1)
    first_blk = jnp.int32(0)
    nxt = jnp.int32(-1)
    for g in reversed(range(MOE_GROUPS)):
        first_blk = jnp.where(grp == g, first[g], first_blk)
        nxt = jnp.where((g > grp) & (nblk[g] > 0), g, nxt)
    return grp, jc - first_blk, valid, valid & (jc == first_blk), nxt


def _experts_kernel(grp_ref, blk_ref, valid_ref, first_ref, next_ref, h_ref, c_ref, wg_hbm, wu_hbm, wd_hbm, o_ref,
                    stage_g, stage_u, stage_d, wg_ref, wu_ref, wd_ref, sems):
    j = pl.program_id(0)
    grp, valid, is_first, nxt = grp_ref[j], valid_ref[j] == 1, first_ref[j] == 1, next_ref[j]

    def weight_copies(g):
        return (pltpu.make_async_copy(wg_hbm.at[g], stage_g, sems.at[0]),
                pltpu.make_async_copy(wu_hbm.at[g], stage_u, sems.at[1]),
                pltpu.make_async_copy(wd_hbm.at[g], stage_d, sems.at[2]))

    @pl.when(j == 0)
    def _():
        for cp in weight_copies(grp):
            cp.start()

    @pl.when(is_first)
    def _():
        for cp in weight_copies(grp):
            cp.wait()
        for stage, dst in ((stage_g, wg_ref), (stage_u, wu_ref), (stage_d, wd_ref)):
            for e in range(MOE_EXPERTS_PER_GROUP):
                def convert(i, carry, stage=stage, dst=dst, e=e):
                    rows = pl.ds(pl.multiple_of(i * CAST_ROWS, CAST_ROWS), CAST_ROWS)
                    dst[e, rows, :] = stage[e, rows, :].astype(dst.dtype)
                    return carry

                lax.fori_loop(0, stage.shape[1] // CAST_ROWS, convert, 0)

        @pl.when(nxt >= 0)
        def _():
            for cp in weight_copies(nxt):
                cp.start()

    @pl.when(valid)
    def _():
        h = h_ref[...]
        c = c_ref[...]
        lane = lax.broadcasted_iota(jnp.int32, c.shape, 1)
        out = None
        for e in range(MOE_EXPERTS_PER_GROUP):
            a = _dot(h, wg_ref[e])
            u = _dot(h, wu_ref[e])
            cw = jnp.sum(jnp.where(lane == e, c, 0.0), axis=-1, keepdims=True)
            hid = (_silu(a) * u * cw).astype(BF16)
            part = _dot(hid, wd_ref[e])
            out = part if out is None else out + part
        o_ref[...] = out.astype(o_ref.dtype)

    @pl.when(jnp.logical_not(valid))
    def _():
        o_ref[...] = jnp.zeros_like(o_ref)


def _experts(hbuf, cbuf, wg, wu, wd, slots):
    g_n, rows, d = hbuf.shape
    e_n, f = wg.shape[1], wg.shape[3]
    tm = EXPERT_TILE
    n_slots = slots[0].shape[0]
    hbm = pl.BlockSpec(memory_space=pl.ANY)
    grid_spec = pltpu.PrefetchScalarGridSpec(
        num_scalar_prefetch=5,
        grid=(n_slots,),
        in_specs=[
            pl.BlockSpec((None, tm, d), lambda j, grp, blk, v, gf, gx: (grp[j], blk[j], 0)),
            pl.BlockSpec((None, tm, LANES), lambda j, grp, blk, v, gf, gx: (grp[j], blk[j], 0)),
            hbm, hbm, hbm,
        ],
        out_specs=pl.BlockSpec((tm, d), lambda j, grp, blk, v, gf, gx: (j, 0)),
        scratch_shapes=[
            pltpu.VMEM((e_n, d, f), wg.dtype),
            pltpu.VMEM((e_n, d, f), wu.dtype),
            pltpu.VMEM((e_n, f, d), wd.dtype),
            pltpu.VMEM((e_n, d, f), BF16),
            pltpu.VMEM((e_n, d, f), BF16),
            pltpu.VMEM((e_n, f, d), BF16),
            pltpu.SemaphoreType.DMA((3,)),
        ],
    )
    return pl.pallas_call(
        _experts_kernel,
        grid_spec=grid_spec,
        out_shape=jax.ShapeDtypeStruct((n_slots * tm, d), BF16),
        compiler_params=pltpu.CompilerParams(dimension_semantics=("arbitrary",),
                                             vmem_limit_bytes=STAGING_VMEM_LIMIT),
        name="experts",
    )(*slots, hbuf, cbuf, wg, wu, wd)


def _finish_kernel(start_ref, cnt_ref, *refs, final_norm):
    wins = refs[:2 * MOE_GROUPS]
    pos_ref, x2_ref, gfin_ref, o_ref, sorted_scr = refs[2 * MOE_GROUPS:]
    i = pl.program_id(0)
    tm = x2_ref.shape[0]
    half_pieces = HALF_WIN // ROW_ALIGN

    @pl.when(i == 0)
    def _():
        sorted_scr[...] = jnp.zeros_like(sorted_scr)

    seg = jnp.int32(0)
    for g in range(MOE_GROUPS):
        padded = _pad_rows(cnt_ref[i, g])
        pieces = padded // ROW_ALIGN
        for half, win in enumerate(wins[2 * g:2 * g + 2]):
            base = seg + half * HALF_WIN

            def copy_piece(k, carry, win=win, base=base):
                src = pl.multiple_of(k * ROW_ALIGN, ROW_ALIGN)
                dst = pl.multiple_of(base + k * ROW_ALIGN, ROW_ALIGN)
                sorted_scr[pl.ds(dst, ROW_ALIGN), :] = win[pl.ds(src, ROW_ALIGN), :]
                return carry

            lax.fori_loop(0, jnp.clip(pieces - half * half_pieces, 0, half_pieces), copy_piece, 0)
        seg = seg + padded

    pos = pos_ref[:, 0:1]
    unperm = jnp.where(lax.broadcasted_iota(jnp.int32, (tm, SORTED_ROWS), 1).astype(F32) == pos, 1.0, 0.0)
    y = x2_ref[...] + _dot(unperm.astype(BF16), sorted_scr[...])
    if final_norm:
        y = _rmsnorm_f32(y, gfin_ref[...])
    o_ref[...] = y


def _finish(mbuf, starts, counts, pos, x2, g_final, final_norm):
    t, d = x2.shape
    tm = SORT_TILE

    def window(g, half):
        def index(i, st, ct):
            row = st[i, g] + half * HALF_WIN
            if half:
                row = jnp.where(_pad_rows(ct[i, g]) > HALF_WIN, row, 0)
            return pl.multiple_of(row, ROW_ALIGN), 0

        return pl.BlockSpec((pl.Element(HALF_WIN), pl.Element(d)), index)

    grid_spec = pltpu.PrefetchScalarGridSpec(
        num_scalar_prefetch=2,
        grid=(t // tm,),
        in_specs=[
            *[window(g, half) for g in range(MOE_GROUPS) for half in range(2)],
            pl.BlockSpec((tm, LANES), lambda i, st, ct: (i, 0)),
            pl.BlockSpec((tm, d), lambda i, st, ct: (i, 0)),
            pl.BlockSpec((1, d), lambda i, st, ct: (0, 0)),
        ],
        out_specs=pl.BlockSpec((tm, d), lambda i, st, ct: (i, 0)),
        scratch_shapes=[pltpu.VMEM((SORTED_ROWS, d), BF16)],
    )
    return pl.pallas_call(
        functools.partial(_finish_kernel, final_norm=final_norm),
        grid_spec=grid_spec,
        out_shape=jax.ShapeDtypeStruct((t, d), F32),
        compiler_params=_params("arbitrary"),
        name="finish",
    )(starts, counts, *([mbuf] * (2 * MOE_GROUPS)), pos, x2, g_final)


def kernel(x, mem, g_mix, w_in, ret_decay_fwd, ret_decay_bwd, ret_norm_gain, w_ret_o, na_rpb, w_na_o, g_mem, w_mem_kv, w_xa_o, w_out, g_ffn, w_router_group, b_router_group, w_router_expert, b_router_expert, w_exp_gate, w_exp_up, w_exp_down, g_final):
    b, s, d = x.shape
    depth = w_in.shape[0]
    t = b * s
    cos_t, sin_t = _rope_tables(s)
    row = lambda v: v.reshape(1, -1).astype(F32)
    x2d = x.reshape(t, d)
    for l in range(depth):
        proj = _inproj(x2d, row(g_mix[l]), w_in[l], cos_t, sin_t)
        proj3 = proj.reshape(b, s, IN_WIDTH)
        y_ret = _retention(proj3, ret_decay_fwd[l].astype(F32), ret_decay_bwd[l].astype(F32),
                           row(ret_norm_gain[l]))
        y_na = _neighbourhood_attention(proj3, _na_bias_table(na_rpb[l]))
        y_xa = _memory_attention(proj3, mem, row(g_mem[l]), w_mem_kv[l].astype(BF16))
        n_r = MOE_GROUPS + N_EXPERTS
        w_router = jnp.pad(jnp.concatenate([w_router_group[l], w_router_expert[l]], axis=1).astype(F32),
                           ((0, 0), (0, LANES - n_r)))
        b_router = jnp.pad(jnp.concatenate([b_router_group[l], b_router_expert[l]]).astype(F32),
                           (0, LANES - n_r)).reshape(1, LANES)
        x2, pos, counts, out_starts, *slots, hbuf, cbuf = _merge(
            y_ret.reshape(t, -1), y_na.reshape(t, -1), y_xa.reshape(t, -1), proj, x2d,
            w_ret_o[l].astype(BF16), w_na_o[l].astype(BF16), w_xa_o[l].astype(BF16),
            w_out[l].astype(BF16), row(g_ffn[l]), w_router, b_router)
        mbuf = _experts(hbuf, cbuf, w_exp_gate[l], w_exp_up[l], w_exp_down[l], slots)
        x2d = _finish(mbuf, out_starts, counts, pos, x2, row(g_final), final_norm=(l == depth - 1))
    return x2d.reshape(b, s, d)
```

```python
import functools

import jax
import jax.numpy as jnp
import numpy as np
from jax import lax
from jax.experimental import pallas as pl
from jax.experimental.pallas import tpu as pltpu

D_MODEL = 1024
GRID_W = 64
N_BRANCHES = 3
RET_HEADS = 4
RET_QK_DIM = 128
RET_V_DIM = 256
RET_CHUNK = 128
ROPE_BASE = 10000.0
NA_HEADS = 8
NA_HEAD_DIM = 64
NA_ROWS = 8
NA_COLS = 16
XA_HEADS = 4
XA_HEAD_DIM = 128
MOE_GROUPS = 4
MOE_EXPERTS_PER_GROUP = 4
MOE_TOP_K = 2
MOE_D_FF = 512
RMS_EPS = 1e-6
GN_EPS = 1e-5

RET_QK_WIDTH = RET_HEADS * RET_QK_DIM
RET_V_WIDTH = RET_HEADS * RET_V_DIM
NA_WIDTH = NA_HEADS * NA_HEAD_DIM
XA_WIDTH = XA_HEADS * XA_HEAD_DIM
IN_WIDTH = 2 * RET_QK_WIDTH + 2 * RET_V_WIDTH + 3 * NA_WIDTH + XA_WIDTH + N_BRANCHES * D_MODEL

OFF_RQ = 0
OFF_RK = OFF_RQ + RET_QK_WIDTH
OFF_RV = OFF_RK + RET_QK_WIDTH
OFF_RG = OFF_RV + RET_V_WIDTH
OFF_NQ = OFF_RG + RET_V_WIDTH
OFF_NK = OFF_NQ + NA_WIDTH
OFF_NV = OFF_NK + NA_WIDTH
OFF_XQ = OFF_NV + NA_WIDTH
OFF_GATE = OFF_XQ + XA_WIDTH

N_EXPERTS = MOE_GROUPS * MOE_EXPERTS_PER_GROUP
LANES = 128
ROUTER_EXPERT_LANE0 = MOE_GROUPS

VMEM_LIMIT = 48 * 1024 * 1024
STAGING_VMEM_LIMIT = 56 * 1024 * 1024
CAST_ROWS = 64

F32 = jnp.float32
BF16 = jnp.bfloat16


def _params(*sem):
    return pltpu.CompilerParams(dimension_semantics=sem, vmem_limit_bytes=VMEM_LIMIT)


def _rmsnorm_f32(x, g):
    return x * lax.rsqrt(jnp.mean(x * x, axis=-1, keepdims=True) + RMS_EPS) * g


def _sigmoid(x):
    return 0.5 * jnp.tanh(0.5 * x) + 0.5


def _silu(x):
    h = 0.5 * x
    return h + h * jnp.tanh(h)


def _dot(a, b):
    return jnp.dot(a, b, preferred_element_type=F32)


def _dot_tn(a, b):
    return lax.dot_general(a, b, (((0,), (0,)), ((), ())), preferred_element_type=F32)


def _dot_nt(a, b):
    return lax.dot_general(a, b, (((1,), (1,)), ((), ())), preferred_element_type=F32)


INPROJ_STAGE_COLS = 1024


def _inproj_kernel(x_ref, g_ref, cos_ref, sin_ref, w_hbm, o_ref, h_scr, w_scr, stage, sems):
    i = pl.program_id(0)
    j = pl.program_id(1)
    n_col = pl.num_programs(1)
    tn = o_ref.shape[1]
    per_block = tn // INPROJ_STAGE_COLS

    def chunk_copy(col_block, k):
        cols = pl.ds(pl.multiple_of((col_block * per_block + k) * INPROJ_STAGE_COLS, INPROJ_STAGE_COLS),
                     INPROJ_STAGE_COLS)
        return pltpu.make_async_copy(w_hbm.at[:, cols], stage.at[k], sems.at[k])

    @pl.when((i == 0) & (j == 0))
    def _():
        for k in range(per_block):
            chunk_copy(0, k).start()

    @pl.when(i == 0)
    def _():
        for k in range(per_block):
            chunk_copy(j, k).wait()

            def convert(r, carry, k=k):
                rows = pl.ds(pl.multiple_of(r * CAST_ROWS, CAST_ROWS), CAST_ROWS)
                w_scr[j, rows, k * INPROJ_STAGE_COLS:(k + 1) * INPROJ_STAGE_COLS] = stage[k, rows, :].astype(BF16)
                return carry

            lax.fori_loop(0, stage.shape[1] // CAST_ROWS, convert, 0)

        @pl.when(j + 1 < n_col)
        def _():
            for k in range(per_block):
                chunk_copy(j + 1, k).start()

    @pl.when(j == 0)
    def _():
        h_scr[...] = _rmsnorm_f32(x_ref[...], g_ref[...]).astype(BF16)

    half = RET_QK_DIM // 2
    for jj in range(w_scr.shape[0]):
        @pl.when(j == jj)
        def _(jj=jj):
            res = _dot(h_scr[...], w_scr[jj])
            for c in range(0, tn, RET_QK_DIM):
                a = res[:, c:c + RET_QK_DIM]
                col = jj * tn + c
                if OFF_RQ <= col < OFF_RV:
                    a = a * cos_ref[...] + pltpu.roll(a, half, 1) * sin_ref[...]
                    if col < OFF_RK:
                        a = a * (RET_QK_DIM ** -0.5)
                elif OFF_RG <= col < OFF_NQ:
                    a = _silu(a)
                o_ref[:, c:c + RET_QK_DIM] = a.astype(o_ref.dtype)


def _inproj(x2d, g, w, cos_t, sin_t, tm=1024, tn=2048):
    t, d = x2d.shape
    n = w.shape[1]
    per_block = tn // INPROJ_STAGE_COLS
    seq_tiles = cos_t.shape[0] // tm
    return pl.pallas_call(
        _inproj_kernel,
        grid=(t // tm, n // tn),
        in_specs=[
            pl.BlockSpec((tm, d), lambda i, j: (i, 0)),
            pl.BlockSpec((1, d), lambda i, j: (0, 0)),
            pl.BlockSpec((tm, RET_QK_DIM), lambda i, j: (i % seq_tiles, 0)),
            pl.BlockSpec((tm, RET_QK_DIM), lambda i, j: (i % seq_tiles, 0)),
            pl.BlockSpec(memory_space=pl.ANY),
        ],
        out_specs=pl.BlockSpec((tm, tn), lambda i, j: (i, j)),
        out_shape=jax.ShapeDtypeStruct((t, n), BF16),
        scratch_shapes=[
            pltpu.VMEM((tm, d), BF16),
            pltpu.VMEM((n // tn, d, tn), BF16),
            pltpu.VMEM((per_block, d, INPROJ_STAGE_COLS), w.dtype),
            pltpu.SemaphoreType.DMA((per_block,)),
        ],
        compiler_params=pltpu.CompilerParams(dimension_semantics=("arbitrary", "arbitrary"),
                                             vmem_limit_bytes=STAGING_VMEM_LIMIT),
        name="inproj",
    )(x2d, g, cos_t, sin_t, w)


def _log_sigmoid(x):
    return jnp.minimum(x, 0.0) - jnp.log1p(jnp.exp(-jnp.abs(x)))


RET_HEADS_PER_STEP = 2


def _retention_kernel(decf_ref, decb_ref, q_ref, k_ref, v_ref, rg_ref, gn_ref, o_ref, *scratch):
    qk, dv = RET_QK_DIM, RET_V_DIM
    for hh in range(RET_HEADS_PER_STEP):
        qcols = slice(hh * qk, (hh + 1) * qk)
        vcols = slice(hh * dv, (hh + 1) * dv)
        _retention_head(pl.program_id(1) * RET_HEADS_PER_STEP + hh, decf_ref, decb_ref,
                        q_ref.at[:, qcols], k_ref.at[:, qcols], v_ref.at[:, vcols], rg_ref.at[:, vcols],
                        gn_ref.at[:, vcols], o_ref.at[:, vcols], *[s.at[hh] for s in scratch])


def _retention_head(h, decf_ref, decb_ref, q_ref, k_ref, v_ref, rg_ref, gn_ref, o_ref,
                    qfs, qbs, y_scr, kvf_scr, kvb_scr, st_scr):
    c = RET_CHUNK
    seq = q_ref.shape[0]
    n_chunks = seq // c

    lgf = _log_sigmoid(jnp.full((1, 1), decf_ref[h], F32))
    lgb = _log_sigmoid(jnp.full((1, 1), decb_ref[h], F32))

    ii = lax.broadcasted_iota(jnp.int32, (c, c), 0)
    jj = lax.broadcasted_iota(jnp.int32, (c, c), 1)
    diff = (ii - jj).astype(F32)
    dmat = jnp.exp(jnp.where(diff >= 0, diff * lgf, (-diff) * lgb))
    pos = lax.broadcasted_iota(jnp.int32, (c, 1), 0).astype(F32)
    qd_f = jnp.exp((pos + 1.0) * lgf)
    kd_f = jnp.exp((c - 1.0 - pos) * lgf)
    cd_f = jnp.exp(c * lgf)
    qd_b = jnp.exp((c - pos) * lgb)
    kd_b = jnp.exp(pos * lgb)
    cd_b = jnp.exp(c * lgb)

    def intra(n):
        rows = pl.ds(n * c, c)
        q = q_ref[rows, :].astype(F32)
        qfs[rows, :] = (q * qd_f).astype(BF16)
        qbs[rows, :] = (q * qd_b).astype(BF16)
        k = k_ref[rows, :].astype(F32)
        s = _dot_nt(q_ref[rows, :], k_ref[rows, :]) * dmat
        lhs = jnp.concatenate([s.astype(BF16), (k * kd_f).T.astype(BF16), (k * kd_b).T.astype(BF16)], axis=0)
        r = _dot(lhs, v_ref[rows, :])
        y_scr[rows, :] = r[:c]
        kvf_scr[n] = r[c:2 * c]
        kvb_scr[n] = r[2 * c:]

    for n in range(n_chunks):
        intra(n)

    st_scr[...] = jnp.zeros_like(st_scr)

    def fwd(n):
        rows = pl.ds(n * c, c)
        st = st_scr[...]
        y_scr[rows, :] += _dot(qfs[rows, :], st.astype(BF16))
        st_scr[...] = st * cd_f + kvf_scr[n]

    for n in range(n_chunks):
        fwd(n)

    st_scr[...] = jnp.zeros_like(st_scr)
    gn = gn_ref[...]

    def bwd(n):
        rows = pl.ds(n * c, c)
        st = st_scr[...]
        y = y_scr[rows, :] + _dot(qbs[rows, :], st.astype(BF16))
        st_scr[...] = st * cd_b + kvb_scr[n]
        mu = jnp.mean(y, axis=-1, keepdims=True)
        yc = y - mu
        var = jnp.mean(yc * yc, axis=-1, keepdims=True)
        yn = yc * lax.rsqrt(var + GN_EPS) * gn
        o_ref[rows, :] = rg_ref[rows, :] * yn.astype(o_ref.dtype)

    for n in reversed(range(n_chunks)):
        bwd(n)


def _retention(proj3, dec_f, dec_b, gn_gain):
    b, s, _ = proj3.shape
    hps = RET_HEADS_PER_STEP
    dk, dv = RET_QK_DIM, RET_V_DIM
    qb, vb = hps * dk, hps * dv
    smem = pl.BlockSpec(memory_space=pltpu.SMEM)
    return pl.pallas_call(
        _retention_kernel,
        grid=(b, RET_HEADS // hps),
        in_specs=[
            smem,
            smem,
            pl.BlockSpec((None, s, qb), lambda i, h: (i, 0, OFF_RQ // qb + h)),
            pl.BlockSpec((None, s, qb), lambda i, h: (i, 0, OFF_RK // qb + h)),
            pl.BlockSpec((None, s, vb), lambda i, h: (i, 0, OFF_RV // vb + h)),
            pl.BlockSpec((None, s, vb), lambda i, h: (i, 0, OFF_RG // vb + h)),
            pl.BlockSpec((1, vb), lambda i, h: (0, h)),
        ],
        out_specs=pl.BlockSpec((None, s, vb), lambda i, h: (i, 0, h)),
        out_shape=jax.ShapeDtypeStruct((b, s, RET_V_WIDTH), BF16),
        scratch_shapes=[
            pltpu.VMEM((hps, s, dk), BF16),
            pltpu.VMEM((hps, s, dk), BF16),
            pltpu.VMEM((hps, s, dv), F32),
            pltpu.VMEM((hps, s // RET_CHUNK, dk, dv), F32),
            pltpu.VMEM((hps, s // RET_CHUNK, dk, dv), F32),
            pltpu.VMEM((hps, dk, dv), F32),
        ],
        compiler_params=_params("parallel", "parallel"),
        name="retention",
    )(dec_f, dec_b, proj3, proj3, proj3, proj3, gn_gain)


def _rope_tables(seq):
    half = RET_QK_DIM // 2
    inv_freq = ROPE_BASE ** (-np.arange(half, dtype=np.float64) / half)
    ang = np.arange(seq, dtype=np.float64)[:, None] * inv_freq[None, :]
    cos, sin = np.cos(ang), np.sin(ang)
    return (jnp.asarray(np.concatenate([cos, cos], axis=-1), F32),
            jnp.asarray(np.concatenate([-sin, sin], axis=-1), F32))


def _na_bias_table(rpb):
    heads = rpb.shape[0]
    w = GRID_W
    cols = np.arange(w)
    col_start = np.clip(cols - NA_COLS // 2, 0, w - NA_COLS)
    col_off = cols[None, :] - col_start[:, None]
    col_mask = (col_off >= 0) & (col_off < NA_COLS)
    rel_c = np.clip(cols[None, :] - cols[:, None], -(NA_COLS - 1), NA_COLS - 1) + (NA_COLS - 1)
    onehot = (rel_c[:, :, None] == np.arange(2 * NA_COLS - 1)).astype(np.float32)
    toe = jnp.einsum('hrc,qkc->hrqk', rpb.astype(F32), onehot, precision=lax.Precision.HIGHEST)
    toe = jnp.where(col_mask[None, None], toe, -jnp.inf)
    two = jnp.concatenate([toe[:, :-1], toe[:, 1:]], axis=-1)
    two = two.reshape(heads // 2, 2, 2 * NA_ROWS - 2, w, 2 * w)
    return two.transpose(0, 2, 1, 3, 4).reshape(heads // 2, 2 * NA_ROWS - 2, 2 * w, 2 * w)


NA_ROWS_PER_STEP = 32


def _na_kernel(q_ref, k_ref, v_ref, bias_ref, o_ref, s_scr, p_scr):
    w = GRID_W
    seq = q_ref.shape[0]
    rows_n = seq // w
    kr = NA_ROWS
    first = lax.broadcasted_iota(jnp.int32, (w, LANES), 1) < NA_HEAD_DIM
    scale = NA_HEAD_DIM ** -0.5

    def window_row(r):
        return jnp.clip(r - kr // 2, 0, rows_n - kr)

    def body(it, carry):
        r0 = it * NA_ROWS_PER_STEP
        for u in range(NA_ROWS_PER_STEP):
            r = r0 + u
            rs = window_row(r)
            qr = q_ref[pl.ds(pl.multiple_of(r * w, w), w), :] * scale
            zero = jnp.zeros_like(qr)
            q2 = jnp.concatenate([jnp.where(first, qr, zero), jnp.where(first, zero, qr)], axis=0)
            kk = k_ref[pl.ds(pl.multiple_of(rs * w, w), kr * w), :]
            off = rs - r + (NA_ROWS - 1)
            bias = jnp.concatenate([bias_ref[off + 2 * i] for i in range(kr // 2)], axis=1)
            s_scr[u] = _dot_nt(q2, kk) + bias
        for u in range(NA_ROWS_PER_STEP):
            m = jnp.max(s_scr[u], axis=-1, keepdims=True)
            p_scr[u] = jnp.exp(s_scr[u] - m).astype(BF16)
        for u in range(NA_ROWS_PER_STEP):
            r = r0 + u
            vv = v_ref[pl.ds(pl.multiple_of(window_row(r) * w, w), kr * w), :]
            o2 = _dot(p_scr[u], jnp.concatenate([vv, jnp.ones_like(vv)], axis=1))
            o2 = o2[:, :LANES] / o2[:, LANES:]
            o_ref[pl.ds(pl.multiple_of(r * w, w), w), :] = jnp.where(first, o2[:w], o2[w:]).astype(o_ref.dtype)
        return carry

    lax.fori_loop(0, rows_n // NA_ROWS_PER_STEP, body, 0)


def _neighbourhood_attention(proj3, bias_tab):
    b, s, _ = proj3.shape
    pairs = NA_HEADS // 2
    return pl.pallas_call(
        _na_kernel,
        grid=(b, pairs),
        in_specs=[
            pl.BlockSpec((None, s, LANES), lambda i, p: (i, 0, OFF_NQ // LANES + p)),
            pl.BlockSpec((None, s, LANES), lambda i, p: (i, 0, OFF_NK // LANES + p)),
            pl.BlockSpec((None, s, LANES), lambda i, p: (i, 0, OFF_NV // LANES + p)),
            pl.BlockSpec((None, 2 * NA_ROWS - 2, 2 * GRID_W, 2 * GRID_W), lambda i, p: (p, 0, 0, 0)),
        ],
        out_specs=pl.BlockSpec((None, s, LANES), lambda i, p: (i, 0, p)),
        out_shape=jax.ShapeDtypeStruct((b, s, NA_WIDTH), BF16),
        scratch_shapes=[
            pltpu.VMEM((NA_ROWS_PER_STEP, 2 * GRID_W, NA_ROWS * GRID_W), F32),
            pltpu.VMEM((NA_ROWS_PER_STEP, 2 * GRID_W, NA_ROWS * GRID_W), BF16),
        ],
        compiler_params=_params("parallel", "parallel"),
        name="nbr_attn",
    )(proj3, proj3, proj3, bias_tab)


def _xa_kernel(q_ref, mem_ref, g_ref, wkv_ref, o_ref, kv_scr):
    @pl.when(pl.program_id(1) == 0)
    def _():
        mn = _rmsnorm_f32(mem_ref[...], g_ref[...]).astype(BF16)
        kv_scr[...] = _dot(mn, wkv_ref[...]).astype(BF16)

    dh = XA_HEAD_DIM
    scale = dh ** -0.5
    for h in range(XA_HEADS):
        q = q_ref[:, h * dh:(h + 1) * dh]
        k = kv_scr[:, h * dh:(h + 1) * dh]
        v = kv_scr[:, XA_WIDTH + h * dh:XA_WIDTH + (h + 1) * dh]
        s = _dot_nt(q, k) * scale
        m = jnp.max(s, axis=-1, keepdims=True)
        p = jnp.exp(s - m).astype(BF16)
        o2 = _dot(p, jnp.concatenate([v, jnp.ones_like(v)], axis=1))
        o_ref[:, h * dh:(h + 1) * dh] = (o2[:, :dh] / o2[:, dh:]).astype(o_ref.dtype)


def _memory_attention(proj3, mem, g_mem, wkv_bf16, ts=2048):
    b, s, _ = proj3.shape
    m, d = mem.shape[1], mem.shape[2]
    return pl.pallas_call(
        _xa_kernel,
        grid=(b, s // ts),
        in_specs=[
            pl.BlockSpec((None, ts, XA_WIDTH), lambda i, j: (i, j, OFF_XQ // XA_WIDTH)),
            pl.BlockSpec((None, m, d), lambda i, j: (i, 0, 0)),
            pl.BlockSpec((1, d), lambda i, j: (0, 0)),
            pl.BlockSpec((d, 2 * XA_WIDTH), lambda i, j: (0, 0)),
        ],
        out_specs=pl.BlockSpec((None, ts, XA_WIDTH), lambda i, j: (i, j, 0)),
        out_shape=jax.ShapeDtypeStruct((b, s, XA_WIDTH), BF16),
        scratch_shapes=[pltpu.VMEM((m, 2 * XA_WIDTH), BF16)],
        compiler_params=_params("parallel", "arbitrary"),
        name="mem_attn",
    )(proj3, mem, g_mem, wkv_bf16)


def _masked_lane_max(v, mask):
    return jnp.max(jnp.where(mask, v, -jnp.inf), axis=-1, keepdims=True)


def _first_lane_eq(v, target, mask, lane):
    return jnp.min(jnp.where(mask & (v == target), lane, float(LANES)), axis=-1, keepdims=True)


def _route(logits):
    g_n, e_n = MOE_GROUPS, MOE_EXPERTS_PER_GROUP
    lane = lax.broadcasted_iota(jnp.int32, logits.shape, 1).astype(F32)
    is_grp = lane < g_n
    gmax = _masked_lane_max(logits, is_grp)
    gsum = jnp.sum(jnp.where(is_grp, jnp.exp(logits - gmax), 0.0), axis=-1, keepdims=True)
    grp_w = 1.0 / gsum
    gidx = _first_lane_eq(logits, gmax, is_grp, lane)
    lo = ROUTER_EXPERT_LANE0 + gidx * e_n
    in_grp = (lane >= lo) & (lane < lo + e_n)
    emax = _masked_lane_max(logits, in_grp)
    ex = jnp.where(in_grp, jnp.exp(logits - emax), 0.0)
    prob = ex / jnp.sum(ex, axis=-1, keepdims=True)
    p1 = _masked_lane_max(prob, in_grp)
    i1 = _first_lane_eq(prob, p1, in_grp, lane)
    rest = in_grp & (lane != i1)
    p2 = _masked_lane_max(prob, rest)
    i2 = _first_lane_eq(prob, p2, rest, lane)
    tot = p1 + p2
    w1 = p1 / tot * grp_w
    w2 = p2 / tot * grp_w
    return gidx, jnp.where(lane == i1 - lo, w1, jnp.where(lane == i2 - lo, w2, 0.0))


SORT_TILE = 512
ROW_ALIGN = 16
SORTED_ROWS = 640
WIN_ROWS = SORT_TILE
HALF_WIN = WIN_ROWS // 2
SCR_ROWS = SORTED_ROWS + WIN_ROWS
EXPERT_TILE = 512


def _group_buf_rows(t):
    return t + 2 * WIN_ROWS


def _pad_rows(n):
    return (n + (ROW_ALIGN - 1)) // ROW_ALIGN * ROW_ALIGN


def _window_copies(hs, cs, hbuf, cbuf, sems, slot, kind, g, src_row, dst_row):
    return (
        pltpu.make_async_copy(hs.at[slot, pl.ds(src_row, WIN_ROWS)], hbuf.at[g, pl.ds(dst_row, WIN_ROWS)],
                              sems.at[slot, 2 * kind, g]),
        pltpu.make_async_copy(cs.at[slot, pl.ds(src_row, WIN_ROWS)], cbuf.at[g, pl.ds(dst_row, WIN_ROWS)],
                              sems.at[slot, 2 * kind + 1, g]),
    )


def _merge_kernel(yret_ref, yna_ref, yxa_ref, gr_ref, gn_ref, gx_ref, x_ref, wro_ref, wno_ref, wxo_ref, wout_ref,
                  gffn_ref, wr_ref, br_ref, before_ref,
                  x2_ref, pos_ref, cnt_ref, out_start_ref,
                  slot_grp, slot_blk, slot_live, slot_first, slot_next, hbuf, cbuf,
                  hs, cs, run, start_ref, sems):
    i = pl.program_id(0)
    n_tiles = pl.num_programs(0)
    slot = lax.rem(i, 2)
    tm = x_ref.shape[0]
    g_n = MOE_GROUPS

    @pl.when(i == 0)
    def _():
        for g in range(g_n):
            run[g] = 0
        hs[:, SORTED_ROWS:, :] = jnp.zeros((2, WIN_ROWS, hs.shape[2]), hs.dtype)
        cs[:, SORTED_ROWS:, :] = jnp.zeros((2, WIN_ROWS, cs.shape[2]), cs.dtype)

    y_ret = _dot(yret_ref[...], wro_ref[...])
    y_na = _dot(yna_ref[...], wno_ref[...])
    y_xa = _dot(yxa_ref[...], wxo_ref[...])
    mix = (_sigmoid(gr_ref[...].astype(F32)) * y_ret + _sigmoid(gn_ref[...].astype(F32)) * y_na
           + _sigmoid(gx_ref[...].astype(F32)) * y_xa)
    x2 = x_ref[...] + _dot(mix.astype(BF16), wout_ref[...])
    x2_ref[...] = x2
    h2 = _rmsnorm_f32(x2, gffn_ref[...])
    h_hi = h2.astype(BF16)
    h_lo = (h2 - h_hi.astype(F32)).astype(BF16)
    wr = wr_ref[...]
    wr_hi = wr.astype(BF16)
    wr_lo = (wr - wr_hi.astype(F32)).astype(BF16)
    hi_terms = _dot(h_hi, jnp.concatenate([wr_hi, wr_lo], axis=1))
    logits = hi_terms[:, :LANES] + hi_terms[:, LANES:] + _dot(h_lo, wr_hi) + br_ref[...]

    gidx, w4 = _route(logits)

    lane = lax.broadcasted_iota(jnp.int32, (tm, LANES), 1).astype(F32)
    onehot = jnp.where(lane == gidx, 1.0, 0.0)
    rank = _dot(before_ref[...], onehot.astype(BF16))
    count_row = rank[tm - 1:tm, :] + onehot[tm - 1:tm, :]
    lane_row = lax.broadcasted_iota(jnp.int32, (1, LANES), 1)
    counts = [jnp.sum(jnp.where(lane_row == g, count_row, 0.0)).astype(jnp.int32) for g in range(g_n)]
    seg_start = []
    acc = jnp.int32(0)
    for g in range(g_n):
        seg_start.append(acc)
        acc = acc + _pad_rows(counts[g])
    start_row = jnp.zeros((1, LANES), F32)
    for g in range(g_n):
        start_row = jnp.where(lane_row == g, seg_start[g].astype(F32), start_row)
    pos = jnp.sum(onehot * (rank + start_row), axis=-1, keepdims=True)
    pos_ref[...] = jnp.broadcast_to(pos, (tm, LANES))

    pos_lanes = jnp.transpose(jnp.broadcast_to(pos, (tm, LANES)))[0:1, :]
    perm = jnp.where(lax.broadcasted_iota(jnp.int32, (SORTED_ROWS, tm), 0).astype(F32) == pos_lanes, 1.0, 0.0)
    perm = perm.astype(BF16)
    hs[slot, 0:SORTED_ROWS, :] = _dot(perm, h_hi).astype(hs.dtype)
    e_n = MOE_EXPERTS_PER_GROUP
    w_hi = w4.astype(BF16).astype(F32)
    r1 = w4 - w_hi
    w_mid = r1.astype(BF16).astype(F32)
    w_lo = r1 - w_mid
    pieces = (w_hi + pltpu.roll(w_mid, e_n, 1) + pltpu.roll(w_lo, 2 * e_n, 1)).astype(BF16)
    sorted_pieces = _dot(perm, pieces)
    cs[slot, 0:SORTED_ROWS, :] = (sorted_pieces + pltpu.roll(sorted_pieces, LANES - e_n, 1)
                                  + pltpu.roll(sorted_pieces, LANES - 2 * e_n, 1))

    copies = functools.partial(_window_copies, hs, cs, hbuf, cbuf, sems)

    def wait_all(which_slot):
        for kind in range(2):
            for g in range(g_n):
                for cp in copies(which_slot, kind, g, 0, 0):
                    cp.wait()

    @pl.when(i > 0)
    def _():
        wait_all(1 - slot)

    @pl.when(i == 0)
    def _():
        top = hbuf.shape[1] - WIN_ROWS
        for g in range(g_n):
            for cp in copies(slot, 1, g, SORTED_ROWS, top):
                cp.start()
        for g in range(g_n):
            for cp in copies(slot, 1, g, 0, 0):
                cp.wait()

    tiles_left = n_tiles - 1 - i
    for g in range(g_n):
        c_g = run[g]
        c_next = c_g + _pad_rows(counts[g])
        start_ref[i, g] = c_g
        cnt_ref[i, g] = counts[g]
        run[g] = c_next
        for cp in copies(slot, 0, g, pl.multiple_of(seg_start[g], ROW_ALIGN), pl.multiple_of(c_g, ROW_ALIGN)):
            cp.start()
        dead = c_next + tiles_left * WIN_ROWS + WIN_ROWS
        for cp in copies(slot, 1, g, SORTED_ROWS, pl.multiple_of(dead, ROW_ALIGN)):
            cp.start()

    @pl.when(i == n_tiles - 1)
    def _():
        wait_all(slot)
        for g in range(g_n):
            for cp in copies(slot, 0, g, SORTED_ROWS, pl.multiple_of(run[g], ROW_ALIGN)):
                cp.start()
        for g in range(g_n):
            for cp in copies(slot, 0, g, 0, 0):
                cp.wait()
        _write_slot_tables(run, start_ref, (slot_grp, slot_blk, slot_live, slot_first, slot_next), out_start_ref)


def _merge(y_ret, y_na, y_xa, proj, x2d, w_ret_o, w_na_o, w_xa_o, w_out, g_ffn, w_router, b_router):
    t, d = x2d.shape
    tm = SORT_TILE
    n_tiles = t // tm
    rows = _group_buf_rows(t)
    gate_blk = OFF_GATE // d
    before = jnp.asarray(np.tril(np.ones((tm, tm), np.float32), -1), BF16)
    full = lambda a: pl.BlockSpec(a.shape, lambda i: (0,) * a.ndim)
    smem = pl.BlockSpec(memory_space=pltpu.SMEM)
    hbm = pl.BlockSpec(memory_space=pl.ANY)
    return pl.pallas_call(
        _merge_kernel,
        grid=(n_tiles,),
        in_specs=[
            pl.BlockSpec((tm, RET_V_WIDTH), lambda i: (i, 0)),
            pl.BlockSpec((tm, NA_WIDTH), lambda i: (i, 0)),
            pl.BlockSpec((tm, XA_WIDTH), lambda i: (i, 0)),
            pl.BlockSpec((tm, d), lambda i: (i, gate_blk)),
            pl.BlockSpec((tm, d), lambda i: (i, gate_blk + 1)),
            pl.BlockSpec((tm, d), lambda i: (i, gate_blk + 2)),
            pl.BlockSpec((tm, d), lambda i: (i, 0)),
            full(w_ret_o), full(w_na_o), full(w_xa_o), full(w_out), full(g_ffn), full(w_router), full(b_router),
            full(before),
        ],
        out_specs=[
            pl.BlockSpec((tm, d), lambda i: (i, 0)),
            pl.BlockSpec((tm, LANES), lambda i: (i, 0)),
            smem, smem, *([smem] * 5), hbm, hbm,
        ],
        out_shape=[
            jax.ShapeDtypeStruct((t, d), F32),
            jax.ShapeDtypeStruct((t, LANES), F32),
            jax.ShapeDtypeStruct((n_tiles, MOE_GROUPS), jnp.int32),
            jax.ShapeDtypeStruct((n_tiles, MOE_GROUPS), jnp.int32),
            *([jax.ShapeDtypeStruct((_expert_slot_count(t),), jnp.int32)] * 5),
            jax.ShapeDtypeStruct((MOE_GROUPS, rows, d), BF16),
            jax.ShapeDtypeStruct((MOE_GROUPS, rows, LANES), F32),
        ],
        scratch_shapes=[
            pltpu.VMEM((2, SCR_ROWS, d), BF16),
            pltpu.VMEM((2, SCR_ROWS, LANES), F32),
            pltpu.SMEM((MOE_GROUPS,), jnp.int32),
            pltpu.SMEM((n_tiles, MOE_GROUPS), jnp.int32),
            pltpu.SemaphoreType.DMA((2, 4, MOE_GROUPS)),
        ],
        compiler_params=_params("arbitrary"),
        name="merge_router",
    )(y_ret, y_na, y_xa, proj, proj, proj, x2d, w_ret_o, w_na_o, w_xa_o, w_out, g_ffn, w_router, b_router, before)


def _expert_slot_count(t):
    return (t + (t // SORT_TILE) * MOE_GROUPS * (ROW_ALIGN - 1)) // EXPERT_TILE + MOE_GROUPS + 1


def _write_slot_tables(run, start_ref, slot_refs, out_start_ref):
    nblk, first = [], []
    total = jnp.int32(0)
    for g in range(MOE_GROUPS):
        n = (run[g] + EXPERT_TILE - 1) // EXPERT_TILE
        first.append(total)
        nblk.append(n)
        total = total + n

    rows = [run[g] for g in range(MOE_GROUPS)]

    def fill_slot(j, carry):
        for ref, value in zip(slot_refs, _expert_slot(j, rows, nblk, first, total)):
            ref[j] = value.astype(jnp.int32)
        return carry

    lax.fori_loop(0, slot_refs[0].shape[0], fill_slot, 0)

    def fill_tile(i, carry):
        for g in range(MOE_GROUPS):
            out_start_ref[i, g] = first[g] * EXPERT_TILE + start_ref[i, g]
        return carry

    lax.fori_loop(0, out_start_ref.shape[0], fill_tile, 0)


def _expert_slot(j, rows, nblk, first, total):
    valid = j < total
    jc = jnp.clip(j, 0, jnp.maximum(total - 1, 0))
    grp = jnp.int32(0)
    for g in range(MOE_GROUPS):
        grp = grp + (jc >= first[g] + nblk[g]).astype(jnp.int32)
    grp = jnp.minimum(grp, MOE_GROUPS - 1)
    first_blk = jnp.int32(0)
    grp_rows = jnp.int32(0)
    nxt = jnp.int32(-1)
    for g in reversed(range(MOE_GROUPS)):
        first_blk = jnp.where(grp == g, first[g], first_blk)
        grp_rows = jnp.where(grp == g, rows[g], grp_rows)
        nxt = jnp.where((g > grp) & (nblk[g] > 0), g, nxt)
    blk = jc - first_blk
    live = jnp.where(valid, jnp.minimum(grp_rows - blk * EXPERT_TILE, EXPERT_TILE), 0)
    return grp, blk, live, valid & (jc == first_blk), nxt


def _experts_kernel(grp_ref, blk_ref, live_ref, first_ref, next_ref, h_ref, c_ref, wg_hbm, wu_hbm, wd_hbm, o_ref,
                    stage_g, stage_u, stage_d, wg_ref, wu_ref, wd_ref, sems):
    j = pl.program_id(0)
    grp, live, is_first, nxt = grp_ref[j], live_ref[j], first_ref[j] == 1, next_ref[j]

    def weight_copies(g):
        return (pltpu.make_async_copy(wg_hbm.at[g], stage_g, sems.at[0]),
                pltpu.make_async_copy(wu_hbm.at[g], stage_u, sems.at[1]),
                pltpu.make_async_copy(wd_hbm.at[g], stage_d, sems.at[2]))

    @pl.when(j == 0)
    def _():
        for cp in weight_copies(grp):
            cp.start()

    @pl.when(is_first)
    def _():
        for cp in weight_copies(grp):
            cp.wait()
        for stage, dst in ((stage_g, wg_ref), (stage_u, wu_ref), (stage_d, wd_ref)):
            for e in range(MOE_EXPERTS_PER_GROUP):
                def convert(i, carry, stage=stage, dst=dst, e=e):
                    rows = pl.ds(pl.multiple_of(i * CAST_ROWS, CAST_ROWS), CAST_ROWS)
                    dst[e, rows, :] = stage[e, rows, :].astype(dst.dtype)
                    return carry

                lax.fori_loop(0, stage.shape[1] // CAST_ROWS, convert, 0)

        @pl.when(nxt >= 0)
        def _():
            for cp in weight_copies(nxt):
                cp.start()

    def run_experts(n_rows):
        h = h_ref[:n_rows, :]
        c = c_ref[:n_rows, :]
        lane = lax.broadcasted_iota(jnp.int32, c.shape, 1)
        out = None
        for e in range(MOE_EXPERTS_PER_GROUP):
            a = _dot(h, wg_ref[e])
            u = _dot(h, wu_ref[e])
            cw = jnp.sum(jnp.where(lane == e, c, 0.0), axis=-1, keepdims=True)
            hid = (_silu(a) * u * cw).astype(BF16)
            part = _dot(hid, wd_ref[e])
            out = part if out is None else out + part
        o_ref[:n_rows, :] = out.astype(o_ref.dtype)
        if n_rows < o_ref.shape[0]:
            o_ref[n_rows:, :] = jnp.zeros((o_ref.shape[0] - n_rows, o_ref.shape[1]), o_ref.dtype)

    half_rows = EXPERT_TILE // 2
    pl.when(live > half_rows)(functools.partial(run_experts, EXPERT_TILE))
    pl.when((live > 0) & (live <= half_rows))(functools.partial(run_experts, half_rows))

    @pl.when(live == 0)
    def _():
        o_ref[...] = jnp.zeros_like(o_ref)


def _experts(hbuf, cbuf, wg, wu, wd, slots):
    g_n, rows, d = hbuf.shape
    e_n, f = wg.shape[1], wg.shape[3]
    tm = EXPERT_TILE
    n_slots = slots[0].shape[0]
    hbm = pl.BlockSpec(memory_space=pl.ANY)
    grid_spec = pltpu.PrefetchScalarGridSpec(
        num_scalar_prefetch=5,
        grid=(n_slots,),
        in_specs=[
            pl.BlockSpec((None, tm, d), lambda j, grp, blk, v, gf, gx: (grp[j], blk[j], 0)),
            pl.BlockSpec((None, tm, LANES), lambda j, grp, blk, v, gf, gx: (grp[j], blk[j], 0)),
            hbm, hbm, hbm,
        ],
        out_specs=pl.BlockSpec((tm, d), lambda j, grp, blk, v, gf, gx: (j, 0)),
        scratch_shapes=[
            pltpu.VMEM((e_n, d, f), wg.dtype),
            pltpu.VMEM((e_n, d, f), wu.dtype),
            pltpu.VMEM((e_n, f, d), wd.dtype),
            pltpu.VMEM((e_n, d, f), BF16),
            pltpu.VMEM((e_n, d, f), BF16),
            pltpu.VMEM((e_n, f, d), BF16),
            pltpu.SemaphoreType.DMA((3,)),
        ],
    )
    return pl.pallas_call(
        _experts_kernel,
        grid_spec=grid_spec,
        out_shape=jax.ShapeDtypeStruct((n_slots * tm, d), BF16),
        compiler_params=pltpu.CompilerParams(dimension_semantics=("arbitrary",),
                                             vmem_limit_bytes=STAGING_VMEM_LIMIT),
        name="experts",
    )(*slots, hbuf, cbuf, wg, wu, wd)


def _finish_kernel(start_ref, cnt_ref, *refs, final_norm):
    wins = refs[:2 * MOE_GROUPS]
    pos_ref, x2_ref, gfin_ref, o_ref, sorted_scr = refs[2 * MOE_GROUPS:]
    i = pl.program_id(0)
    tm = x2_ref.shape[0]
    half_pieces = HALF_WIN // ROW_ALIGN

    @pl.when(i == 0)
    def _():
        sorted_scr[...] = jnp.zeros_like(sorted_scr)

    seg = jnp.int32(0)
    for g in range(MOE_GROUPS):
        padded = _pad_rows(cnt_ref[i, g])
        pieces = padded // ROW_ALIGN
        for half, win in enumerate(wins[2 * g:2 * g + 2]):
            base = seg + half * HALF_WIN

            def copy_piece(k, carry, win=win, base=base):
                src = pl.multiple_of(k * ROW_ALIGN, ROW_ALIGN)
                dst = pl.multiple_of(base + k * ROW_ALIGN, ROW_ALIGN)
                sorted_scr[pl.ds(dst, ROW_ALIGN), :] = win[pl.ds(src, ROW_ALIGN), :]
                return carry

            lax.fori_loop(0, jnp.clip(pieces - half * half_pieces, 0, half_pieces), copy_piece, 0)
        seg = seg + padded

    pos = pos_ref[:, 0:1]
    unperm = jnp.where(lax.broadcasted_iota(jnp.int32, (tm, SORTED_ROWS), 1).astype(F32) == pos, 1.0, 0.0)
    y = x2_ref[...] + _dot(unperm.astype(BF16), sorted_scr[...])
    if final_norm:
        y = _rmsnorm_f32(y, gfin_ref[...])
    o_ref[...] = y


def _finish(mbuf, starts, counts, pos, x2, g_final, final_norm):
    t, d = x2.shape
    tm = SORT_TILE

    def window(g, half):
        def index(i, st, ct):
            row = st[i, g] + half * HALF_WIN
            if half:
                row = jnp.where(_pad_rows(ct[i, g]) > HALF_WIN, row, 0)
            return pl.multiple_of(row, ROW_ALIGN), 0

        return pl.BlockSpec((pl.Element(HALF_WIN), pl.Element(d)), index)

    grid_spec = pltpu.PrefetchScalarGridSpec(
        num_scalar_prefetch=2,
        grid=(t // tm,),
        in_specs=[
            *[window(g, half) for g in range(MOE_GROUPS) for half in range(2)],
            pl.BlockSpec((tm, LANES), lambda i, st, ct: (i, 0)),
            pl.BlockSpec((tm, d), lambda i, st, ct: (i, 0)),
            pl.BlockSpec((1, d), lambda i, st, ct: (0, 0)),
        ],
        out_specs=pl.BlockSpec((tm, d), lambda i, st, ct: (i, 0)),
        scratch_shapes=[pltpu.VMEM((SORTED_ROWS, d), BF16)],
    )
    return pl.pallas_call(
        functools.partial(_finish_kernel, final_norm=final_norm),
        grid_spec=grid_spec,
        out_shape=jax.ShapeDtypeStruct((t, d), F32),
        compiler_params=_params("arbitrary"),
        name="finish",
    )(starts, counts, *([mbuf] * (2 * MOE_GROUPS)), pos, x2, g_final)


def kernel(x, mem, g_mix, w_in, ret_decay_fwd, ret_decay_bwd, ret_norm_gain, w_ret_o, na_rpb, w_na_o, g_mem, w_mem_kv, w_xa_o, w_out, g_ffn, w_router_group, b_router_group, w_router_expert, b_router_expert, w_exp_gate, w_exp_up, w_exp_down, g_final):
    b, s, d = x.shape
    depth = w_in.shape[0]
    t = b * s
    cos_t, sin_t = _rope_tables(s)
    row = lambda v: v.reshape(1, -1).astype(F32)
    x2d = x.reshape(t, d)
    for l in range(depth):
        proj = _inproj(x2d, row(g_mix[l]), w_in[l], cos_t, sin_t)
        proj3 = proj.reshape(b, s, IN_WIDTH)
        y_ret = _retention(proj3, ret_decay_fwd[l].astype(F32), ret_decay_bwd[l].astype(F32),
                           row(ret_norm_gain[l]))
        y_na = _neighbourhood_attention(proj3, _na_bias_table(na_rpb[l]))
        y_xa = _memory_attention(proj3, mem, row(g_mem[l]), w_mem_kv[l].astype(BF16))
        n_r = MOE_GROUPS + N_EXPERTS
        w_router = jnp.pad(jnp.concatenate([w_router_group[l], w_router_expert[l]], axis=1).astype(F32),
                           ((0, 0), (0, LANES - n_r)))
        b_router = jnp.pad(jnp.concatenate([b_router_group[l], b_router_expert[l]]).astype(F32),
                           (0, LANES - n_r)).reshape(1, LANES)
        x2, pos, counts, out_starts, *slots, hbuf, cbuf = _merge(
            y_ret.reshape(t, -1), y_na.reshape(t, -1), y_xa.reshape(t, -1), proj, x2d,
            w_ret_o[l].astype(BF16), w_na_o[l].astype(BF16), w_xa_o[l].astype(BF16),
            w_out[l].astype(BF16), row(g_ffn[l]), w_router, b_router)
        mbuf = _experts(hbuf, cbuf, w_exp_gate[l], w_exp_up[l], w_exp_down[l], slots)
        x2d = _finish(mbuf, out_starts, counts, pos, x2, row(g_final), final_norm=(l == depth - 1))
    return x2d.reshape(b, s, d)
```

```python
import functools

import jax
import jax.numpy as jnp
import numpy as np
from jax import lax
from jax.experimental import pallas as pl
from jax.experimental.pallas import tpu as pltpu

D_MODEL = 1024
GRID_W = 64
N_BRANCHES = 3
RET_HEADS = 4
RET_QK_DIM = 128
RET_V_DIM = 256
RET_CHUNK = 128
ROPE_BASE = 10000.0
NA_HEADS = 8
NA_HEAD_DIM = 64
NA_ROWS = 8
NA_COLS = 16
XA_HEADS = 4
XA_HEAD_DIM = 128
MOE_GROUPS = 4
MOE_EXPERTS_PER_GROUP = 4
MOE_TOP_K = 2
RMS_EPS = 1e-6
GN_EPS = 1e-5

RET_QK_WIDTH = RET_HEADS * RET_QK_DIM
RET_V_WIDTH = RET_HEADS * RET_V_DIM
NA_WIDTH = NA_HEADS * NA_HEAD_DIM
XA_WIDTH = XA_HEADS * XA_HEAD_DIM
IN_WIDTH = 2 * RET_QK_WIDTH + 2 * RET_V_WIDTH + 3 * NA_WIDTH + XA_WIDTH + N_BRANCHES * D_MODEL

OFF_RQ = 0
OFF_RK = OFF_RQ + RET_QK_WIDTH
OFF_RV = OFF_RK + RET_QK_WIDTH
OFF_RG = OFF_RV + RET_V_WIDTH
OFF_NQ = OFF_RG + RET_V_WIDTH
OFF_NK = OFF_NQ + NA_WIDTH
OFF_NV = OFF_NK + NA_WIDTH
OFF_XQ = OFF_NV + NA_WIDTH
OFF_GATE = OFF_XQ + XA_WIDTH

N_EXPERTS = MOE_GROUPS * MOE_EXPERTS_PER_GROUP
LANES = 128
ROUTER_EXPERT_LANE0 = MOE_GROUPS

VMEM_LIMIT = 48 * 1024 * 1024
STAGING_VMEM_LIMIT = 56 * 1024 * 1024
CAST_ROWS = 256

F32 = jnp.float32
BF16 = jnp.bfloat16


def _params(*sem):
    return pltpu.CompilerParams(dimension_semantics=sem, vmem_limit_bytes=VMEM_LIMIT)


def _rmsnorm_f32(x, g):
    return x * lax.rsqrt(jnp.mean(x * x, axis=-1, keepdims=True) + RMS_EPS) * g


def _sigmoid(x):
    return 0.5 * jnp.tanh(0.5 * x) + 0.5


def _silu(x):
    h = 0.5 * x
    return h + h * jnp.tanh(h)


def _dot(a, b):
    return jnp.dot(a, b, preferred_element_type=F32)


def _dot_nt(a, b):
    return lax.dot_general(a, b, (((1,), (1,)), ((), ())), preferred_element_type=F32)


INPROJ_STAGE_COLS = 1024


def _inproj_kernel(x_ref, g_ref, cos_ref, sin_ref, w_hbm, o_ref, h_scr, w_scr, stage, sems):
    i = pl.program_id(0)
    j = pl.program_id(1)
    n_col = pl.num_programs(1)
    tn = o_ref.shape[1]
    per_block = tn // INPROJ_STAGE_COLS

    def chunk_copy(col_block, k):
        cols = pl.ds(pl.multiple_of((col_block * per_block + k) * INPROJ_STAGE_COLS, INPROJ_STAGE_COLS),
                     INPROJ_STAGE_COLS)
        return pltpu.make_async_copy(w_hbm.at[:, cols], stage.at[k], sems.at[k])

    @pl.when((i == 0) & (j == 0))
    def _():
        for k in range(per_block):
            chunk_copy(0, k).start()

    @pl.when(i == 0)
    def _():
        for k in range(per_block):
            chunk_copy(j, k).wait()

            def convert(r, carry, k=k):
                rows = pl.ds(pl.multiple_of(r * CAST_ROWS, CAST_ROWS), CAST_ROWS)
                w_scr[j, rows, k * INPROJ_STAGE_COLS:(k + 1) * INPROJ_STAGE_COLS] = stage[k, rows, :].astype(BF16)
                return carry

            lax.fori_loop(0, stage.shape[1] // CAST_ROWS, convert, 0)

        @pl.when(j + 1 < n_col)
        def _():
            for k in range(per_block):
                chunk_copy(j + 1, k).start()

    @pl.when(j == 0)
    def _():
        h_scr[...] = _rmsnorm_f32(x_ref[...], g_ref[...]).astype(BF16)

    half = RET_QK_DIM // 2
    for jj in range(w_scr.shape[0]):
        @pl.when(j == jj)
        def _(jj=jj):
            res = _dot(h_scr[...], w_scr[jj])
            for c in range(0, tn, RET_QK_DIM):
                a = res[:, c:c + RET_QK_DIM]
                col = jj * tn + c
                if OFF_RQ <= col < OFF_RV:
                    a = a * cos_ref[...] + pltpu.roll(a, half, 1) * sin_ref[...]
                    if col < OFF_RK:
                        a = a * (RET_QK_DIM ** -0.5)
                elif OFF_RG <= col < OFF_NQ:
                    a = _silu(a)
                o_ref[:, c:c + RET_QK_DIM] = a.astype(o_ref.dtype)


def _inproj(x2d, g, w, cos_t, sin_t, tm=1024, tn=2048):
    t, d = x2d.shape
    n = w.shape[1]
    per_block = tn // INPROJ_STAGE_COLS
    seq_tiles = cos_t.shape[0] // tm
    return pl.pallas_call(
        _inproj_kernel,
        grid=(t // tm, n // tn),
        in_specs=[
            pl.BlockSpec((tm, d), lambda i, j: (i, 0)),
            pl.BlockSpec((1, d), lambda i, j: (0, 0)),
            pl.BlockSpec((tm, RET_QK_DIM), lambda i, j: (i % seq_tiles, 0)),
            pl.BlockSpec((tm, RET_QK_DIM), lambda i, j: (i % seq_tiles, 0)),
            pl.BlockSpec(memory_space=pl.ANY),
        ],
        out_specs=pl.BlockSpec((tm, tn), lambda i, j: (i, j)),
        out_shape=jax.ShapeDtypeStruct((t, n), BF16),
        scratch_shapes=[
            pltpu.VMEM((tm, d), BF16),
            pltpu.VMEM((n // tn, d, tn), BF16),
            pltpu.VMEM((per_block, d, INPROJ_STAGE_COLS), w.dtype),
            pltpu.SemaphoreType.DMA((per_block,)),
        ],
        compiler_params=pltpu.CompilerParams(dimension_semantics=("arbitrary", "arbitrary"),
                                             vmem_limit_bytes=STAGING_VMEM_LIMIT),
        name="inproj",
    )(x2d, g, cos_t, sin_t, w)


def _log_sigmoid(x):
    return jnp.minimum(x, 0.0) - jnp.log1p(jnp.exp(-jnp.abs(x)))


RET_HEADS_PER_STEP = 2


def _retention_kernel(decf_ref, decb_ref, q_ref, k_ref, v_ref, rg_ref, gn_ref, o_ref, *scratch):
    qk, dv = RET_QK_DIM, RET_V_DIM
    for hh in range(RET_HEADS_PER_STEP):
        qcols = slice(hh * qk, (hh + 1) * qk)
        vcols = slice(hh * dv, (hh + 1) * dv)
        _retention_head(pl.program_id(1) * RET_HEADS_PER_STEP + hh, decf_ref, decb_ref,
                        q_ref.at[:, qcols], k_ref.at[:, qcols], v_ref.at[:, vcols], rg_ref.at[:, vcols],
                        gn_ref.at[:, vcols], o_ref.at[:, vcols], *[s.at[hh] for s in scratch])


def _retention_head(h, decf_ref, decb_ref, q_ref, k_ref, v_ref, rg_ref, gn_ref, o_ref,
                    qfs, qbs, y_scr, kvf_scr, kvb_scr, st_scr):
    c = RET_CHUNK
    seq = q_ref.shape[0]
    n_chunks = seq // c

    lgf = _log_sigmoid(jnp.full((1, 1), decf_ref[h], F32))
    lgb = _log_sigmoid(jnp.full((1, 1), decb_ref[h], F32))

    ii = lax.broadcasted_iota(jnp.int32, (c, c), 0)
    jj = lax.broadcasted_iota(jnp.int32, (c, c), 1)
    diff = (ii - jj).astype(F32)
    dmat = jnp.exp(jnp.where(diff >= 0, diff * lgf, (-diff) * lgb))
    pos = lax.broadcasted_iota(jnp.int32, (c, 1), 0).astype(F32)
    qd_f = jnp.exp((pos + 1.0) * lgf)
    kd_f = jnp.exp((c - 1.0 - pos) * lgf)
    cd_f = jnp.exp(c * lgf)
    qd_b = jnp.exp((c - pos) * lgb)
    kd_b = jnp.exp(pos * lgb)
    cd_b = jnp.exp(c * lgb)

    def intra(n):
        rows = pl.ds(n * c, c)
        q = q_ref[rows, :].astype(F32)
        qfs[rows, :] = (q * qd_f).astype(BF16)
        qbs[rows, :] = (q * qd_b).astype(BF16)
        k = k_ref[rows, :].astype(F32)
        s = _dot_nt(q_ref[rows, :], k_ref[rows, :]) * dmat
        lhs = jnp.concatenate([s.astype(BF16), (k * kd_f).T.astype(BF16), (k * kd_b).T.astype(BF16)], axis=0)
        r = _dot(lhs, v_ref[rows, :])
        y_scr[rows, :] = r[:c]
        kvf_scr[n] = r[c:2 * c]
        kvb_scr[n] = r[2 * c:]

    for n in range(n_chunks):
        intra(n)

    st_scr[...] = jnp.zeros_like(st_scr)

    def fwd(n):
        rows = pl.ds(n * c, c)
        st = st_scr[...]
        y_scr[rows, :] += _dot(qfs[rows, :], st.astype(BF16))
        st_scr[...] = st * cd_f + kvf_scr[n]

    for n in range(n_chunks):
        fwd(n)

    st_scr[...] = jnp.zeros_like(st_scr)
    gn = gn_ref[...]

    def bwd(n):
        rows = pl.ds(n * c, c)
        st = st_scr[...]
        y = y_scr[rows, :] + _dot(qbs[rows, :], st.astype(BF16))
        st_scr[...] = st * cd_b + kvb_scr[n]
        mu = jnp.mean(y, axis=-1, keepdims=True)
        yc = y - mu
        var = jnp.mean(yc * yc, axis=-1, keepdims=True)
        yn = yc * lax.rsqrt(var + GN_EPS) * gn
        o_ref[rows, :] = rg_ref[rows, :] * yn.astype(o_ref.dtype)

    for n in reversed(range(n_chunks)):
        bwd(n)


def _retention(proj3, dec_f, dec_b, gn_gain):
    b, s, _ = proj3.shape
    hps = RET_HEADS_PER_STEP
    dk, dv = RET_QK_DIM, RET_V_DIM
    qb, vb = hps * dk, hps * dv
    smem = pl.BlockSpec(memory_space=pltpu.SMEM)
    return pl.pallas_call(
        _retention_kernel,
        grid=(b, RET_HEADS // hps),
        in_specs=[
            smem,
            smem,
            pl.BlockSpec((None, s, qb), lambda i, h: (i, 0, OFF_RQ // qb + h)),
            pl.BlockSpec((None, s, qb), lambda i, h: (i, 0, OFF_RK // qb + h)),
            pl.BlockSpec((None, s, vb), lambda i, h: (i, 0, OFF_RV // vb + h)),
            pl.BlockSpec((None, s, vb), lambda i, h: (i, 0, OFF_RG // vb + h)),
            pl.BlockSpec((1, vb), lambda i, h: (0, h)),
        ],
        out_specs=pl.BlockSpec((None, s, vb), lambda i, h: (i, 0, h)),
        out_shape=jax.ShapeDtypeStruct((b, s, RET_V_WIDTH), BF16),
        scratch_shapes=[
            pltpu.VMEM((hps, s, dk), BF16),
            pltpu.VMEM((hps, s, dk), BF16),
            pltpu.VMEM((hps, s, dv), F32),
            pltpu.VMEM((hps, s // RET_CHUNK, dk, dv), F32),
            pltpu.VMEM((hps, s // RET_CHUNK, dk, dv), F32),
            pltpu.VMEM((hps, dk, dv), F32),
        ],
        compiler_params=_params("parallel", "parallel"),
        name="retention",
    )(dec_f, dec_b, proj3, proj3, proj3, proj3, gn_gain)


def _rope_tables(seq):
    half = RET_QK_DIM // 2
    inv_freq = ROPE_BASE ** (-np.arange(half, dtype=np.float64) / half)
    ang = np.arange(seq, dtype=np.float64)[:, None] * inv_freq[None, :]
    cos, sin = np.cos(ang), np.sin(ang)
    return (jnp.asarray(np.concatenate([cos, cos], axis=-1), F32),
            jnp.asarray(np.concatenate([-sin, sin], axis=-1), F32))


def _na_bias_table(rpb):
    heads = rpb.shape[0]
    w = GRID_W
    cols = np.arange(w)
    col_start = np.clip(cols - NA_COLS // 2, 0, w - NA_COLS)
    col_off = cols[None, :] - col_start[:, None]
    col_mask = (col_off >= 0) & (col_off < NA_COLS)
    rel_c = np.clip(cols[None, :] - cols[:, None], -(NA_COLS - 1), NA_COLS - 1) + (NA_COLS - 1)
    onehot = (rel_c[:, :, None] == np.arange(2 * NA_COLS - 1)).astype(np.float32)
    toe = jnp.einsum('hrc,qkc->hrqk', rpb.astype(F32), onehot, precision=lax.Precision.HIGHEST)
    toe = jnp.where(col_mask[None, None], toe, -jnp.inf)
    two = jnp.concatenate([toe[:, :-1], toe[:, 1:]], axis=-1)
    two = two.reshape(heads // 2, 2, 2 * NA_ROWS - 2, w, 2 * w)
    return two.transpose(0, 2, 1, 3, 4).reshape(heads // 2, 2 * NA_ROWS - 2, 2 * w, 2 * w)


NA_ROWS_PER_STEP = 32


NA_PAIRS_PER_STEP = 2


def _na_kernel(q_ref, k_ref, v_ref, bias_ref, o_ref, s_scr, p_scr):
    for pp in range(NA_PAIRS_PER_STEP):
        lanes = slice(pp * LANES, (pp + 1) * LANES)
        _na_pair(q_ref.at[:, lanes], k_ref.at[:, lanes], v_ref.at[:, lanes], bias_ref.at[pp], o_ref.at[:, lanes],
                 s_scr, p_scr)


def _na_pair(q_ref, k_ref, v_ref, bias_ref, o_ref, s_scr, p_scr):
    w = GRID_W
    seq = q_ref.shape[0]
    rows_n = seq // w
    kr = NA_ROWS
    first = lax.broadcasted_iota(jnp.int32, (w, LANES), 1) < NA_HEAD_DIM
    scale = NA_HEAD_DIM ** -0.5

    def window_row(r):
        return jnp.clip(r - kr // 2, 0, rows_n - kr)

    def body(it, carry):
        r0 = it * NA_ROWS_PER_STEP
        for u in range(NA_ROWS_PER_STEP):
            r = r0 + u
            rs = window_row(r)
            qr = q_ref[pl.ds(pl.multiple_of(r * w, w), w), :] * scale
            zero = jnp.zeros_like(qr)
            q2 = jnp.concatenate([jnp.where(first, qr, zero), jnp.where(first, zero, qr)], axis=0)
            kk = k_ref[pl.ds(pl.multiple_of(rs * w, w), kr * w), :]
            off = rs - r + (NA_ROWS - 1)
            bias = jnp.concatenate([bias_ref[off + 2 * i] for i in range(kr // 2)], axis=1)
            s_scr[u] = _dot_nt(q2, kk) + bias
        for u in range(NA_ROWS_PER_STEP):
            m = jnp.max(s_scr[u], axis=-1, keepdims=True)
            p_scr[u] = jnp.exp(s_scr[u] - m).astype(BF16)
        for u in range(NA_ROWS_PER_STEP):
            r = r0 + u
            vv = v_ref[pl.ds(pl.multiple_of(window_row(r) * w, w), kr * w), :]
            o2 = _dot(p_scr[u], jnp.concatenate([vv, jnp.ones_like(vv)], axis=1))
            o2 = o2[:, :LANES] / o2[:, LANES:]
            o_ref[pl.ds(pl.multiple_of(r * w, w), w), :] = jnp.where(first, o2[:w], o2[w:]).astype(o_ref.dtype)
        return carry

    lax.fori_loop(0, rows_n // NA_ROWS_PER_STEP, body, 0)


def _neighbourhood_attention(proj3, bias_tab):
    b, s, _ = proj3.shape
    pairs = NA_HEADS // 2
    pps = NA_PAIRS_PER_STEP
    blk = pps * LANES
    return pl.pallas_call(
        _na_kernel,
        grid=(b, pairs // pps),
        in_specs=[
            pl.BlockSpec((None, s, blk), lambda i, p: (i, 0, OFF_NQ // blk + p)),
            pl.BlockSpec((None, s, blk), lambda i, p: (i, 0, OFF_NK // blk + p)),
            pl.BlockSpec((None, s, blk), lambda i, p: (i, 0, OFF_NV // blk + p)),
            pl.BlockSpec((pps, 2 * NA_ROWS - 2, 2 * GRID_W, 2 * GRID_W), lambda i, p: (p, 0, 0, 0)),
        ],
        out_specs=pl.BlockSpec((None, s, blk), lambda i, p: (i, 0, p)),
        out_shape=jax.ShapeDtypeStruct((b, s, NA_WIDTH), BF16),
        scratch_shapes=[
            pltpu.VMEM((NA_ROWS_PER_STEP, 2 * GRID_W, NA_ROWS * GRID_W), F32),
            pltpu.VMEM((NA_ROWS_PER_STEP, 2 * GRID_W, NA_ROWS * GRID_W), BF16),
        ],
        compiler_params=_params("parallel", "parallel"),
        name="nbr_attn",
    )(proj3, proj3, proj3, bias_tab)


def _xa_kernel(q_ref, mem_ref, g_ref, wkv_ref, o_ref, kv_scr):
    @pl.when(pl.program_id(1) == 0)
    def _():
        mn = _rmsnorm_f32(mem_ref[...], g_ref[...]).astype(BF16)
        kv_scr[...] = _dot(mn, wkv_ref[...]).astype(BF16)

    dh = XA_HEAD_DIM
    scale = dh ** -0.5
    for h in range(XA_HEADS):
        q = q_ref[:, h * dh:(h + 1) * dh]
        k = kv_scr[:, h * dh:(h + 1) * dh]
        v = kv_scr[:, XA_WIDTH + h * dh:XA_WIDTH + (h + 1) * dh]
        s = _dot_nt(q, k) * scale
        m = jnp.max(s, axis=-1, keepdims=True)
        p = jnp.exp(s - m).astype(BF16)
        o2 = _dot(p, jnp.concatenate([v, jnp.ones_like(v)], axis=1))
        o_ref[:, h * dh:(h + 1) * dh] = (o2[:, :dh] / o2[:, dh:]).astype(o_ref.dtype)


def _memory_attention(proj3, mem, g_mem, wkv_bf16, ts=2048):
    b, s, _ = proj3.shape
    m, d = mem.shape[1], mem.shape[2]
    return pl.pallas_call(
        _xa_kernel,
        grid=(b, s // ts),
        in_specs=[
            pl.BlockSpec((None, ts, XA_WIDTH), lambda i, j: (i, j, OFF_XQ // XA_WIDTH)),
            pl.BlockSpec((None, m, d), lambda i, j: (i, 0, 0)),
            pl.BlockSpec((1, d), lambda i, j: (0, 0)),
            pl.BlockSpec((d, 2 * XA_WIDTH), lambda i, j: (0, 0)),
        ],
        out_specs=pl.BlockSpec((None, ts, XA_WIDTH), lambda i, j: (i, j, 0)),
        out_shape=jax.ShapeDtypeStruct((b, s, XA_WIDTH), BF16),
        scratch_shapes=[pltpu.VMEM((m, 2 * XA_WIDTH), BF16)],
        compiler_params=_params("parallel", "arbitrary"),
        name="mem_attn",
    )(proj3, mem, g_mem, wkv_bf16)


def _masked_lane_max(v, mask):
    return jnp.max(jnp.where(mask, v, -jnp.inf), axis=-1, keepdims=True)


def _first_lane_eq(v, target, mask, lane):
    return jnp.min(jnp.where(mask & (v == target), lane, float(LANES)), axis=-1, keepdims=True)


def _route(logits):
    assert MOE_TOP_K == 2
    g_n, e_n = MOE_GROUPS, MOE_EXPERTS_PER_GROUP
    lane = lax.broadcasted_iota(jnp.int32, logits.shape, 1).astype(F32)
    is_grp = lane < g_n
    gmax = _masked_lane_max(logits, is_grp)
    gsum = jnp.sum(jnp.where(is_grp, jnp.exp(logits - gmax), 0.0), axis=-1, keepdims=True)
    grp_w = 1.0 / gsum
    gidx = _first_lane_eq(logits, gmax, is_grp, lane)
    lo = ROUTER_EXPERT_LANE0 + gidx * e_n
    in_grp = (lane >= lo) & (lane < lo + e_n)
    emax = _masked_lane_max(logits, in_grp)
    ex = jnp.where(in_grp, jnp.exp(logits - emax), 0.0)
    prob = ex / jnp.sum(ex, axis=-1, keepdims=True)
    p1 = _masked_lane_max(prob, in_grp)
    i1 = _first_lane_eq(prob, p1, in_grp, lane)
    rest = in_grp & (lane != i1)
    p2 = _masked_lane_max(prob, rest)
    i2 = _first_lane_eq(prob, p2, rest, lane)
    tot = p1 + p2
    w1 = p1 / tot * grp_w
    w2 = p2 / tot * grp_w
    return gidx, jnp.where(lane == i1 - lo, w1, jnp.where(lane == i2 - lo, w2, 0.0))


SORT_TILE = 512
ROW_ALIGN = 16
SORTED_ROWS = 640
WIN_ROWS = SORT_TILE
HALF_WIN = WIN_ROWS // 2
SCR_ROWS = SORTED_ROWS + WIN_ROWS
EXPERT_TILE = 512


def _group_buf_rows(t):
    return t + 2 * WIN_ROWS


def _pad_rows(n):
    return (n + (ROW_ALIGN - 1)) // ROW_ALIGN * ROW_ALIGN


def _window_copies(hs, cs, hbuf, cbuf, sems, slot, kind, g, src_row, dst_row):
    return (
        pltpu.make_async_copy(hs.at[slot, pl.ds(src_row, WIN_ROWS)], hbuf.at[g, pl.ds(dst_row, WIN_ROWS)],
                              sems.at[slot, 2 * kind, g]),
        pltpu.make_async_copy(cs.at[slot, pl.ds(src_row, WIN_ROWS)], cbuf.at[g, pl.ds(dst_row, WIN_ROWS)],
                              sems.at[slot, 2 * kind + 1, g]),
    )


def _merge_kernel(yret_ref, yna_ref, yxa_ref, gr_ref, gn_ref, gx_ref, x_ref, wro_ref, wno_ref, wxo_ref, wout_ref,
                  gffn_ref, wr_ref, br_ref, before_ref,
                  x2_ref, pos_ref, cnt_ref, out_start_ref,
                  slot_grp, slot_blk, slot_valid, slot_first, slot_next, hbuf, cbuf,
                  hs, cs, run, start_ref, sems):
    i = pl.program_id(0)
    n_tiles = pl.num_programs(0)
    slot = lax.rem(i, 2)
    tm = x_ref.shape[0]
    g_n = MOE_GROUPS

    @pl.when(i == 0)
    def _():
        for g in range(g_n):
            run[g] = 0
        hs[:, SORTED_ROWS:, :] = jnp.zeros((2, WIN_ROWS, hs.shape[2]), hs.dtype)
        cs[:, SORTED_ROWS:, :] = jnp.zeros((2, WIN_ROWS, cs.shape[2]), cs.dtype)

    y_ret = _dot(yret_ref[...], wro_ref[...])
    y_na = _dot(yna_ref[...], wno_ref[...])
    y_xa = _dot(yxa_ref[...], wxo_ref[...])
    mix = (_sigmoid(gr_ref[...].astype(F32)) * y_ret + _sigmoid(gn_ref[...].astype(F32)) * y_na
           + _sigmoid(gx_ref[...].astype(F32)) * y_xa)
    x2 = x_ref[...] + _dot(mix.astype(BF16), wout_ref[...])
    x2_ref[...] = x2
    h2 = _rmsnorm_f32(x2, gffn_ref[...])
    h_hi = h2.astype(BF16)
    h_lo = (h2 - h_hi.astype(F32)).astype(BF16)
    wr = wr_ref[...]
    wr_hi = wr.astype(BF16)
    wr_lo = (wr - wr_hi.astype(F32)).astype(BF16)
    hi_terms = _dot(h_hi, jnp.concatenate([wr_hi, wr_lo], axis=1))
    logits = hi_terms[:, :LANES] + hi_terms[:, LANES:] + _dot(h_lo, wr_hi) + br_ref[...]

    gidx, w4 = _route(logits)

    lane = lax.broadcasted_iota(jnp.int32, (tm, LANES), 1).astype(F32)
    onehot = jnp.where(lane == gidx, 1.0, 0.0)
    rank = _dot(before_ref[...], onehot.astype(BF16))
    count_row = rank[tm - 1:tm, :] + onehot[tm - 1:tm, :]
    lane_row = lax.broadcasted_iota(jnp.int32, (1, LANES), 1)
    counts = [jnp.sum(jnp.where(lane_row == g, count_row, 0.0)).astype(jnp.int32) for g in range(g_n)]
    seg_start = []
    acc = jnp.int32(0)
    for g in range(g_n):
        seg_start.append(acc)
        acc = acc + _pad_rows(counts[g])
    start_row = jnp.zeros((1, LANES), F32)
    for g in range(g_n):
        start_row = jnp.where(lane_row == g, seg_start[g].astype(F32), start_row)
    pos = jnp.sum(onehot * (rank + start_row), axis=-1, keepdims=True)
    pos_ref[...] = jnp.broadcast_to(pos, (tm, LANES))

    pos_lanes = jnp.transpose(jnp.broadcast_to(pos, (tm, LANES)))[0:1, :]
    perm = jnp.where(lax.broadcasted_iota(jnp.int32, (SORTED_ROWS, tm), 0).astype(F32) == pos_lanes, 1.0, 0.0)
    perm = perm.astype(BF16)
    hs[slot, 0:SORTED_ROWS, :] = _dot(perm, h_hi).astype(hs.dtype)
    e_n = MOE_EXPERTS_PER_GROUP
    w_hi = w4.astype(BF16).astype(F32)
    r1 = w4 - w_hi
    w_mid = r1.astype(BF16).astype(F32)
    w_lo = r1 - w_mid
    pieces = (w_hi + pltpu.roll(w_mid, e_n, 1) + pltpu.roll(w_lo, 2 * e_n, 1)).astype(BF16)
    sorted_pieces = _dot(perm, pieces)
    cs[slot, 0:SORTED_ROWS, :] = (sorted_pieces + pltpu.roll(sorted_pieces, LANES - e_n, 1)
                                  + pltpu.roll(sorted_pieces, LANES - 2 * e_n, 1))

    copies = functools.partial(_window_copies, hs, cs, hbuf, cbuf, sems)

    def wait_all(which_slot):
        for kind in range(2):
            for g in range(g_n):
                for cp in copies(which_slot, kind, g, 0, 0):
                    cp.wait()

    @pl.when(i > 0)
    def _():
        wait_all(1 - slot)

    @pl.when(i == 0)
    def _():
        top = hbuf.shape[1] - WIN_ROWS
        for g in range(g_n):
            for cp in copies(slot, 1, g, SORTED_ROWS, top):
                cp.start()
        for g in range(g_n):
            for cp in copies(slot, 1, g, 0, 0):
                cp.wait()

    tiles_left = n_tiles - 1 - i
    for g in range(g_n):
        c_g = run[g]
        c_next = c_g + _pad_rows(counts[g])
        start_ref[i, g] = c_g
        cnt_ref[i, g] = counts[g]
        run[g] = c_next
        for cp in copies(slot, 0, g, pl.multiple_of(seg_start[g], ROW_ALIGN), pl.multiple_of(c_g, ROW_ALIGN)):
            cp.start()
        dead = c_next + tiles_left * WIN_ROWS + WIN_ROWS
        for cp in copies(slot, 1, g, SORTED_ROWS, pl.multiple_of(dead, ROW_ALIGN)):
            cp.start()

    @pl.when(i == n_tiles - 1)
    def _():
        wait_all(slot)
        for g in range(g_n):
            for cp in copies(slot, 0, g, SORTED_ROWS, pl.multiple_of(run[g], ROW_ALIGN)):
                cp.start()
        for g in range(g_n):
            for cp in copies(slot, 0, g, 0, 0):
                cp.wait()
        _write_slot_tables(run, start_ref, (slot_grp, slot_blk, slot_valid, slot_first, slot_next), out_start_ref)


def _merge(y_ret, y_na, y_xa, proj, x2d, w_ret_o, w_na_o, w_xa_o, w_out, g_ffn, w_router, b_router):
    t, d = x2d.shape
    tm = SORT_TILE
    n_tiles = t // tm
    rows = _group_buf_rows(t)
    gate_blk = OFF_GATE // d
    before = jnp.asarray(np.tril(np.ones((tm, tm), np.float32), -1), BF16)
    full = lambda a: pl.BlockSpec(a.shape, lambda i: (0,) * a.ndim)
    smem = pl.BlockSpec(memory_space=pltpu.SMEM)
    hbm = pl.BlockSpec(memory_space=pl.ANY)
    return pl.pallas_call(
        _merge_kernel,
        grid=(n_tiles,),
        in_specs=[
            pl.BlockSpec((tm, RET_V_WIDTH), lambda i: (i, 0)),
            pl.BlockSpec((tm, NA_WIDTH), lambda i: (i, 0)),
            pl.BlockSpec((tm, XA_WIDTH), lambda i: (i, 0)),
            pl.BlockSpec((tm, d), lambda i: (i, gate_blk)),
            pl.BlockSpec((tm, d), lambda i: (i, gate_blk + 1)),
            pl.BlockSpec((tm, d), lambda i: (i, gate_blk + 2)),
            pl.BlockSpec((tm, d), lambda i: (i, 0)),
            full(w_ret_o), full(w_na_o), full(w_xa_o), full(w_out), full(g_ffn), full(w_router), full(b_router),
            full(before),
        ],
        out_specs=[
            pl.BlockSpec((tm, d), lambda i: (i, 0)),
            pl.BlockSpec((tm, LANES), lambda i: (i, 0)),
            smem, smem, *([smem] * 5), hbm, hbm,
        ],
        out_shape=[
            jax.ShapeDtypeStruct((t, d), F32),
            jax.ShapeDtypeStruct((t, LANES), F32),
            jax.ShapeDtypeStruct((n_tiles, MOE_GROUPS), jnp.int32),
            jax.ShapeDtypeStruct((n_tiles, MOE_GROUPS), jnp.int32),
            *([jax.ShapeDtypeStruct((_expert_slot_count(t),), jnp.int32)] * 5),
            jax.ShapeDtypeStruct((MOE_GROUPS, rows, d), BF16),
            jax.ShapeDtypeStruct((MOE_GROUPS, rows, LANES), F32),
        ],
        scratch_shapes=[
            pltpu.VMEM((2, SCR_ROWS, d), BF16),
            pltpu.VMEM((2, SCR_ROWS, LANES), F32),
            pltpu.SMEM((MOE_GROUPS,), jnp.int32),
            pltpu.SMEM((n_tiles, MOE_GROUPS), jnp.int32),
            pltpu.SemaphoreType.DMA((2, 4, MOE_GROUPS)),
        ],
        compiler_params=_params("arbitrary"),
        name="merge_router",
    )(y_ret, y_na, y_xa, proj, proj, proj, x2d, w_ret_o, w_na_o, w_xa_o, w_out, g_ffn, w_router, b_router, before)


def _expert_slot_count(t):
    return (t + (t // SORT_TILE) * MOE_GROUPS * (ROW_ALIGN - 1)) // EXPERT_TILE + MOE_GROUPS + 1


def _write_slot_tables(run, start_ref, slot_refs, out_start_ref):
    nblk, first = [], []
    total = jnp.int32(0)
    for g in range(MOE_GROUPS):
        n = (run[g] + EXPERT_TILE - 1) // EXPERT_TILE
        first.append(total)
        nblk.append(n)
        total = total + n

    def fill_slot(j, carry):
        for ref, value in zip(slot_refs, _expert_slot(j, nblk, first, total)):
            ref[j] = value.astype(jnp.int32)
        return carry

    lax.fori_loop(0, slot_refs[0].shape[0], fill_slot, 0)

    def fill_tile(i, carry):
        for g in range(MOE_GROUPS):
            out_start_ref[i, g] = first[g] * EXPERT_TILE + start_ref[i, g]
        return carry

    lax.fori_loop(0, out_start_ref.shape[0], fill_tile, 0)


def _expert_slot(j, nblk, first, total):
    valid = j < total
    jc = jnp.clip(j, 0, jnp.maximum(total - 1, 0))
    grp = jnp.int32(0)
    for g in range(MOE_GROUPS):
        grp = grp + (jc >= first[g] + nblk[g]).astype(jnp.int32)
    grp = jnp.minimum(grp, MOE_GROUPS - 1)
    first_blk = jnp.int32(0)
    nxt = jnp.int32(-1)
    for g in reversed(range(MOE_GROUPS)):
        first_blk = jnp.where(grp == g, first[g], first_blk)
        nxt = jnp.where((g > grp) & (nblk[g] > 0), g, nxt)
    return grp, jc - first_blk, valid, valid & (jc == first_blk), nxt


def _experts_kernel(grp_ref, blk_ref, valid_ref, first_ref, next_ref, h_ref, c_ref, wg_hbm, wu_hbm, wd_hbm, o_ref,
                    stage_g, stage_u, stage_d, wg_ref, wu_ref, wd_ref, sems):
    j = pl.program_id(0)
    grp, valid, is_first, nxt = grp_ref[j], valid_ref[j] == 1, first_ref[j] == 1, next_ref[j]

    def weight_copies(g):
        return (pltpu.make_async_copy(wg_hbm.at[g], stage_g, sems.at[0]),
                pltpu.make_async_copy(wu_hbm.at[g], stage_u, sems.at[1]),
                pltpu.make_async_copy(wd_hbm.at[g], stage_d, sems.at[2]))

    @pl.when(j == 0)
    def _():
        for cp in weight_copies(grp):
            cp.start()

    @pl.when(is_first)
    def _():
        for cp in weight_copies(grp):
            cp.wait()
        for stage, dst in ((stage_g, wg_ref), (stage_u, wu_ref), (stage_d, wd_ref)):
            for e in range(MOE_EXPERTS_PER_GROUP):
                def convert(i, carry, stage=stage, dst=dst, e=e):
                    rows = pl.ds(pl.multiple_of(i * CAST_ROWS, CAST_ROWS), CAST_ROWS)
                    dst[e, rows, :] = stage[e, rows, :].astype(dst.dtype)
                    return carry

                lax.fori_loop(0, stage.shape[1] // CAST_ROWS, convert, 0)

        @pl.when(nxt >= 0)
        def _():
            for cp in weight_copies(nxt):
                cp.start()

    @pl.when(valid)
    def _():
        h = h_ref[...]
        c = c_ref[...]
        lane = lax.broadcasted_iota(jnp.int32, c.shape, 1)
        out = None
        for e in range(MOE_EXPERTS_PER_GROUP):
            a = _dot(h, wg_ref[e])
            u = _dot(h, wu_ref[e])
            cw = jnp.sum(jnp.where(lane == e, c, 0.0), axis=-1, keepdims=True)
            hid = (_silu(a) * u * cw).astype(BF16)
            part = _dot(hid, wd_ref[e])
            out = part if out is None else out + part
        o_ref[...] = out.astype(o_ref.dtype)

    @pl.when(jnp.logical_not(valid))
    def _():
        o_ref[...] = jnp.zeros_like(o_ref)


def _experts(hbuf, cbuf, wg, wu, wd, slots):
    g_n, rows, d = hbuf.shape
    e_n, f = wg.shape[1], wg.shape[3]
    tm = EXPERT_TILE
    n_slots = slots[0].shape[0]
    hbm = pl.BlockSpec(memory_space=pl.ANY)
    grid_spec = pltpu.PrefetchScalarGridSpec(
        num_scalar_prefetch=5,
        grid=(n_slots,),
        in_specs=[
            pl.BlockSpec((None, tm, d), lambda j, grp, blk, v, gf, gx: (grp[j], blk[j], 0)),
            pl.BlockSpec((None, tm, LANES), lambda j, grp, blk, v, gf, gx: (grp[j], blk[j], 0)),
            hbm, hbm, hbm,
        ],
        out_specs=pl.BlockSpec((tm, d), lambda j, grp, blk, v, gf, gx: (j, 0)),
        scratch_shapes=[
            pltpu.VMEM((e_n, d, f), wg.dtype),
            pltpu.VMEM((e_n, d, f), wu.dtype),
            pltpu.VMEM((e_n, f, d), wd.dtype),
            pltpu.VMEM((e_n, d, f), BF16),
            pltpu.VMEM((e_n, d, f), BF16),
            pltpu.VMEM((e_n, f, d), BF16),
            pltpu.SemaphoreType.DMA((3,)),
        ],
    )
    return pl.pallas_call(
        _experts_kernel,
        grid_spec=grid_spec,
        out_shape=jax.ShapeDtypeStruct((n_slots * tm, d), BF16),
        compiler_params=pltpu.CompilerParams(dimension_semantics=("arbitrary",),
                                             vmem_limit_bytes=STAGING_VMEM_LIMIT),
        name="experts",
    )(*slots, hbuf, cbuf, wg, wu, wd)


FINISH_TILES = 2


def _finish_kernel(start_ref, cnt_ref, *refs, final_norm):
    n_win = 2 * MOE_GROUPS
    wins = refs[:FINISH_TILES * n_win]
    pos_ref, x2_ref, gfin_ref, o_ref, sorted_scr = refs[FINISH_TILES * n_win:]
    i = pl.program_id(0)
    tm = SORT_TILE
    half_pieces = HALF_WIN // ROW_ALIGN

    @pl.when(i == 0)
    def _():
        sorted_scr[...] = jnp.zeros_like(sorted_scr)

    for tt in range(FINISH_TILES):
        tile = i * FINISH_TILES + tt
        rows = pl.ds(tt * tm, tm)
        seg = jnp.int32(0)
        for g in range(MOE_GROUPS):
            padded = _pad_rows(cnt_ref[tile, g])
            pieces = padded // ROW_ALIGN
            for half in range(2):
                win = wins[tt * n_win + 2 * g + half]
                base = seg + half * HALF_WIN

                def copy_piece(k, carry, win=win, base=base, tt=tt):
                    src = pl.multiple_of(k * ROW_ALIGN, ROW_ALIGN)
                    dst = pl.multiple_of(base + k * ROW_ALIGN, ROW_ALIGN)
                    sorted_scr[tt, pl.ds(dst, ROW_ALIGN), :] = win[pl.ds(src, ROW_ALIGN), :]
                    return carry

                lax.fori_loop(0, jnp.clip(pieces - half * half_pieces, 0, half_pieces), copy_piece, 0)
            seg = seg + padded

        pos = pos_ref[rows, 0:1]
        unperm = jnp.where(lax.broadcasted_iota(jnp.int32, (tm, SORTED_ROWS), 1).astype(F32) == pos, 1.0, 0.0)
        y = x2_ref[rows, :] + _dot(unperm.astype(BF16), sorted_scr[tt])
        if final_norm:
            y = _rmsnorm_f32(y, gfin_ref[...])
        o_ref[rows, :] = y


def _finish(mbuf, starts, counts, pos, x2, g_final, final_norm):
    t, d = x2.shape
    tm = SORT_TILE * FINISH_TILES

    def window(tt, g, half):
        def index(i, st, ct):
            tile = i * FINISH_TILES + tt
            row = st[tile, g] + half * HALF_WIN
            if half:
                row = jnp.where(_pad_rows(ct[tile, g]) > HALF_WIN, row, 0)
            return pl.multiple_of(row, ROW_ALIGN), 0

        return pl.BlockSpec((pl.Element(HALF_WIN), pl.Element(d)), index)

    windows = [window(tt, g, half) for tt in range(FINISH_TILES) for g in range(MOE_GROUPS) for half in range(2)]
    grid_spec = pltpu.PrefetchScalarGridSpec(
        num_scalar_prefetch=2,
        grid=(t // tm,),
        in_specs=[
            *windows,
            pl.BlockSpec((tm, LANES), lambda i, st, ct: (i, 0)),
            pl.BlockSpec((tm, d), lambda i, st, ct: (i, 0)),
            pl.BlockSpec((1, d), lambda i, st, ct: (0, 0)),
        ],
        out_specs=pl.BlockSpec((tm, d), lambda i, st, ct: (i, 0)),
        scratch_shapes=[pltpu.VMEM((FINISH_TILES, SORTED_ROWS, d), BF16)],
    )
    return pl.pallas_call(
        functools.partial(_finish_kernel, final_norm=final_norm),
        grid_spec=grid_spec,
        out_shape=jax.ShapeDtypeStruct((t, d), F32),
        compiler_params=_params("arbitrary"),
        name="finish",
    )(starts, counts, *([mbuf] * len(windows)), pos, x2, g_final)


def kernel(x, mem, g_mix, w_in, ret_decay_fwd, ret_decay_bwd, ret_norm_gain, w_ret_o, na_rpb, w_na_o, g_mem, w_mem_kv, w_xa_o, w_out, g_ffn, w_router_group, b_router_group, w_router_expert, b_router_expert, w_exp_gate, w_exp_up, w_exp_down, g_final):
    b, s, d = x.shape
    depth = w_in.shape[0]
    t = b * s
    cos_t, sin_t = _rope_tables(s)
    row = lambda v: v.reshape(1, -1).astype(F32)
    x2d = x.reshape(t, d)
    for l in range(depth):
        proj = _inproj(x2d, row(g_mix[l]), w_in[l], cos_t, sin_t)
        proj3 = proj.reshape(b, s, IN_WIDTH)
        y_ret = _retention(proj3, ret_decay_fwd[l].astype(F32), ret_decay_bwd[l].astype(F32),
                           row(ret_norm_gain[l]))
        y_na = _neighbourhood_attention(proj3, _na_bias_table(na_rpb[l]))
        y_xa = _memory_attention(proj3, mem, row(g_mem[l]), w_mem_kv[l].astype(BF16))
        n_r = MOE_GROUPS + N_EXPERTS
        w_router = jnp.pad(jnp.concatenate([w_router_group[l], w_router_expert[l]], axis=1).astype(F32),
                           ((0, 0), (0, LANES - n_r)))
        b_router = jnp.pad(jnp.concatenate([b_router_group[l], b_router_expert[l]]).astype(F32),
                           (0, LANES - n_r)).reshape(1, LANES)
        x2, pos, counts, out_starts, *slots, hbuf, cbuf = _merge(
            y_ret.reshape(t, -1), y_na.reshape(t, -1), y_xa.reshape(t, -1), proj, x2d,
            w_ret_o[l].astype(BF16), w_na_o[l].astype(BF16), w_xa_o[l].astype(BF16),
            w_out[l].astype(BF16), row(g_ffn[l]), w_router, b_router)
        mbuf = _experts(hbuf, cbuf, w_exp_gate[l], w_exp_up[l], w_exp_down[l], slots)
        x2d = _finish(mbuf, out_starts, counts, pos, x2, row(g_final), final_norm=(l == depth - 1))
    return x2d.reshape(b, s, d)
```

```python
import functools

import jax
import jax.numpy as jnp
import numpy as np
from jax import lax
from jax.experimental import pallas as pl
from jax.experimental.pallas import tpu as pltpu

D_MODEL = 1024
GRID_W = 64
N_BRANCHES = 3
RET_HEADS = 4
RET_QK_DIM = 128
RET_V_DIM = 256
RET_CHUNK = 128
ROPE_BASE = 10000.0
NA_HEADS = 8
NA_HEAD_DIM = 64
NA_ROWS = 8
NA_COLS = 16
XA_HEADS = 4
XA_HEAD_DIM = 128
MOE_GROUPS = 4
MOE_EXPERTS_PER_GROUP = 4
MOE_TOP_K = 2
RMS_EPS = 1e-6
GN_EPS = 1e-5

RET_QK_WIDTH = RET_HEADS * RET_QK_DIM
RET_V_WIDTH = RET_HEADS * RET_V_DIM
NA_WIDTH = NA_HEADS * NA_HEAD_DIM
XA_WIDTH = XA_HEADS * XA_HEAD_DIM
IN_WIDTH = 2 * RET_QK_WIDTH + 2 * RET_V_WIDTH + 3 * NA_WIDTH + XA_WIDTH + N_BRANCHES * D_MODEL

OFF_RQ = 0
OFF_RK = OFF_RQ + RET_QK_WIDTH
OFF_RV = OFF_RK + RET_QK_WIDTH
OFF_RG = OFF_RV + RET_V_WIDTH
OFF_NQ = OFF_RG + RET_V_WIDTH
OFF_NK = OFF_NQ + NA_WIDTH
OFF_NV = OFF_NK + NA_WIDTH
OFF_XQ = OFF_NV + NA_WIDTH
OFF_GATE = OFF_XQ + XA_WIDTH

N_EXPERTS = MOE_GROUPS * MOE_EXPERTS_PER_GROUP
LANES = 128
ROUTER_EXPERT_LANE0 = MOE_GROUPS

VMEM_LIMIT = 48 * 1024 * 1024
STAGING_VMEM_LIMIT = 56 * 1024 * 1024
CAST_ROWS = 256

F32 = jnp.float32
BF16 = jnp.bfloat16


def _params(*sem):
    return pltpu.CompilerParams(dimension_semantics=sem, vmem_limit_bytes=VMEM_LIMIT)


def _rmsnorm_f32(x, g):
    return x * lax.rsqrt(jnp.mean(x * x, axis=-1, keepdims=True) + RMS_EPS) * g


def _sigmoid(x):
    return 0.5 * jnp.tanh(0.5 * x) + 0.5


def _silu(x):
    h = 0.5 * x
    return h + h * jnp.tanh(h)


def _dot(a, b):
    return jnp.dot(a, b, preferred_element_type=F32)


def _dot_nt(a, b):
    return lax.dot_general(a, b, (((1,), (1,)), ((), ())), preferred_element_type=F32)


INPROJ_STAGE_COLS = 1024


def _inproj_kernel(x_ref, g_ref, cos_ref, sin_ref, w_hbm, o_ref, h_scr, w_scr, stage, sems):
    i = pl.program_id(0)
    j = pl.program_id(1)
    n_col = pl.num_programs(1)
    tn = o_ref.shape[1]
    per_block = tn // INPROJ_STAGE_COLS

    def chunk_copy(col_block, k):
        cols = pl.ds(pl.multiple_of((col_block * per_block + k) * INPROJ_STAGE_COLS, INPROJ_STAGE_COLS),
                     INPROJ_STAGE_COLS)
        return pltpu.make_async_copy(w_hbm.at[:, cols], stage.at[k], sems.at[k])

    @pl.when((i == 0) & (j == 0))
    def _():
        for k in range(per_block):
            chunk_copy(0, k).start()

    @pl.when(i == 0)
    def _():
        for k in range(per_block):
            chunk_copy(j, k).wait()

            def convert(r, carry, k=k):
                rows = pl.ds(pl.multiple_of(r * CAST_ROWS, CAST_ROWS), CAST_ROWS)
                w_scr[j, rows, k * INPROJ_STAGE_COLS:(k + 1) * INPROJ_STAGE_COLS] = stage[k, rows, :].astype(BF16)
                return carry

            lax.fori_loop(0, stage.shape[1] // CAST_ROWS, convert, 0)

        @pl.when(j + 1 < n_col)
        def _():
            for k in range(per_block):
                chunk_copy(j + 1, k).start()

    @pl.when(j == 0)
    def _():
        h_scr[...] = _rmsnorm_f32(x_ref[...], g_ref[...]).astype(BF16)

    half = RET_QK_DIM // 2
    for jj in range(w_scr.shape[0]):
        @pl.when(j == jj)
        def _(jj=jj):
            res = _dot(h_scr[...], w_scr[jj])
            for c in range(0, tn, RET_QK_DIM):
                a = res[:, c:c + RET_QK_DIM]
                col = jj * tn + c
                if OFF_RQ <= col < OFF_RV:
                    a = a * cos_ref[...] + pltpu.roll(a, half, 1) * sin_ref[...]
                    if col < OFF_RK:
                        a = a * (RET_QK_DIM ** -0.5)
                elif OFF_RG <= col < OFF_NQ:
                    a = _silu(a)
                o_ref[:, c:c + RET_QK_DIM] = a.astype(o_ref.dtype)


def _inproj(x2d, g, w, cos_t, sin_t, tm=1024, tn=2048):
    t, d = x2d.shape
    n = w.shape[1]
    per_block = tn // INPROJ_STAGE_COLS
    seq_tiles = cos_t.shape[0] // tm
    return pl.pallas_call(
        _inproj_kernel,
        grid=(t // tm, n // tn),
        in_specs=[
            pl.BlockSpec((tm, d), lambda i, j: (i, 0)),
            pl.BlockSpec((1, d), lambda i, j: (0, 0)),
            pl.BlockSpec((tm, RET_QK_DIM), lambda i, j: (i % seq_tiles, 0)),
            pl.BlockSpec((tm, RET_QK_DIM), lambda i, j: (i % seq_tiles, 0)),
            pl.BlockSpec(memory_space=pl.ANY),
        ],
        out_specs=pl.BlockSpec((tm, tn), lambda i, j: (i, j)),
        out_shape=jax.ShapeDtypeStruct((t, n), BF16),
        scratch_shapes=[
            pltpu.VMEM((tm, d), BF16),
            pltpu.VMEM((n // tn, d, tn), BF16),
            pltpu.VMEM((per_block, d, INPROJ_STAGE_COLS), w.dtype),
            pltpu.SemaphoreType.DMA((per_block,)),
        ],
        compiler_params=pltpu.CompilerParams(dimension_semantics=("arbitrary", "arbitrary"),
                                             vmem_limit_bytes=STAGING_VMEM_LIMIT),
        name="inproj",
    )(x2d, g, cos_t, sin_t, w)


def _log_sigmoid(x):
    return jnp.minimum(x, 0.0) - jnp.log1p(jnp.exp(-jnp.abs(x)))


RET_HEADS_PER_STEP = 2


def _retention_kernel(decf_ref, decb_ref, q_ref, k_ref, v_ref, rg_ref, gn_ref, o_ref, *scratch):
    qk, dv = RET_QK_DIM, RET_V_DIM
    for hh in range(RET_HEADS_PER_STEP):
        qcols = slice(hh * qk, (hh + 1) * qk)
        vcols = slice(hh * dv, (hh + 1) * dv)
        _retention_head(pl.program_id(1) * RET_HEADS_PER_STEP + hh, decf_ref, decb_ref,
                        q_ref.at[:, qcols], k_ref.at[:, qcols], v_ref.at[:, vcols], rg_ref.at[:, vcols],
                        gn_ref.at[:, vcols], o_ref.at[:, vcols], *[s.at[hh] for s in scratch])


def _retention_head(h, decf_ref, decb_ref, q_ref, k_ref, v_ref, rg_ref, gn_ref, o_ref,
                    qfs, qbs, y_scr, kvf_scr, kvb_scr, st_scr):
    c = RET_CHUNK
    seq = q_ref.shape[0]
    n_chunks = seq // c

    lgf = _log_sigmoid(jnp.full((1, 1), decf_ref[h], F32))
    lgb = _log_sigmoid(jnp.full((1, 1), decb_ref[h], F32))

    ii = lax.broadcasted_iota(jnp.int32, (c, c), 0)
    jj = lax.broadcasted_iota(jnp.int32, (c, c), 1)
    diff = (ii - jj).astype(F32)
    dmat = jnp.exp(jnp.where(diff >= 0, diff * lgf, (-diff) * lgb))
    pos = lax.broadcasted_iota(jnp.int32, (c, 1), 0).astype(F32)
    qd_f = jnp.exp((pos + 1.0) * lgf)
    kd_f = jnp.exp((c - 1.0 - pos) * lgf)
    cd_f = jnp.exp(c * lgf)
    qd_b = jnp.exp((c - pos) * lgb)
    kd_b = jnp.exp(pos * lgb)
    cd_b = jnp.exp(c * lgb)

    def intra(n):
        rows = pl.ds(n * c, c)
        q = q_ref[rows, :].astype(F32)
        qfs[rows, :] = (q * qd_f).astype(BF16)
        qbs[rows, :] = (q * qd_b).astype(BF16)
        k = k_ref[rows, :].astype(F32)
        s = _dot_nt(q_ref[rows, :], k_ref[rows, :]) * dmat
        lhs = jnp.concatenate([s.astype(BF16), (k * kd_f).T.astype(BF16), (k * kd_b).T.astype(BF16)], axis=0)
        r = _dot(lhs, v_ref[rows, :])
        y_scr[rows, :] = r[:c]
        kvf_scr[n] = r[c:2 * c]
        kvb_scr[n] = r[2 * c:]

    for n in range(n_chunks):
        intra(n)

    st_scr[...] = jnp.zeros_like(st_scr)

    def fwd(n):
        rows = pl.ds(n * c, c)
        st = st_scr[...]
        y_scr[rows, :] += _dot(qfs[rows, :], st.astype(BF16))
        st_scr[...] = st * cd_f + kvf_scr[n]

    for n in range(n_chunks):
        fwd(n)

    st_scr[...] = jnp.zeros_like(st_scr)
    gn = gn_ref[...]

    def bwd(n):
        rows = pl.ds(n * c, c)
        st = st_scr[...]
        y = y_scr[rows, :] + _dot(qbs[rows, :], st.astype(BF16))
        st_scr[...] = st * cd_b + kvb_scr[n]
        mu = jnp.mean(y, axis=-1, keepdims=True)
        yc = y - mu
        var = jnp.mean(yc * yc, axis=-1, keepdims=True)
        yn = yc * lax.rsqrt(var + GN_EPS) * gn
        o_ref[rows, :] = rg_ref[rows, :] * yn.astype(o_ref.dtype)

    for n in reversed(range(n_chunks)):
        bwd(n)


def _retention(proj3, dec_f, dec_b, gn_gain):
    b, s, _ = proj3.shape
    hps = RET_HEADS_PER_STEP
    dk, dv = RET_QK_DIM, RET_V_DIM
    qb, vb = hps * dk, hps * dv
    smem = pl.BlockSpec(memory_space=pltpu.SMEM)
    return pl.pallas_call(
        _retention_kernel,
        grid=(b, RET_HEADS // hps),
        in_specs=[
            smem,
            smem,
            pl.BlockSpec((None, s, qb), lambda i, h: (i, 0, OFF_RQ // qb + h)),
            pl.BlockSpec((None, s, qb), lambda i, h: (i, 0, OFF_RK // qb + h)),
            pl.BlockSpec((None, s, vb), lambda i, h: (i, 0, OFF_RV // vb + h)),
            pl.BlockSpec((None, s, vb), lambda i, h: (i, 0, OFF_RG // vb + h)),
            pl.BlockSpec((1, vb), lambda i, h: (0, h)),
        ],
        out_specs=pl.BlockSpec((None, s, vb), lambda i, h: (i, 0, h)),
        out_shape=jax.ShapeDtypeStruct((b, s, RET_V_WIDTH), BF16),
        scratch_shapes=[
            pltpu.VMEM((hps, s, dk), BF16),
            pltpu.VMEM((hps, s, dk), BF16),
            pltpu.VMEM((hps, s, dv), F32),
            pltpu.VMEM((hps, s // RET_CHUNK, dk, dv), F32),
            pltpu.VMEM((hps, s // RET_CHUNK, dk, dv), F32),
            pltpu.VMEM((hps, dk, dv), F32),
        ],
        compiler_params=_params("parallel", "parallel"),
        name="retention",
    )(dec_f, dec_b, proj3, proj3, proj3, proj3, gn_gain)


def _rope_tables(seq):
    half = RET_QK_DIM // 2
    inv_freq = ROPE_BASE ** (-np.arange(half, dtype=np.float64) / half)
    ang = np.arange(seq, dtype=np.float64)[:, None] * inv_freq[None, :]
    cos, sin = np.cos(ang), np.sin(ang)
    return (jnp.asarray(np.concatenate([cos, cos], axis=-1), F32),
            jnp.asarray(np.concatenate([-sin, sin], axis=-1), F32))


def _na_bias_table(rpb):
    heads = rpb.shape[0]
    w = GRID_W
    cols = np.arange(w)
    col_start = np.clip(cols - NA_COLS // 2, 0, w - NA_COLS)
    col_off = cols[None, :] - col_start[:, None]
    col_mask = (col_off >= 0) & (col_off < NA_COLS)
    rel_c = np.clip(cols[None, :] - cols[:, None], -(NA_COLS - 1), NA_COLS - 1) + (NA_COLS - 1)
    onehot = (rel_c[:, :, None] == np.arange(2 * NA_COLS - 1)).astype(np.float32)
    toe = jnp.einsum('hrc,qkc->hrqk', rpb.astype(F32), onehot, precision=lax.Precision.HIGHEST)
    toe = jnp.where(col_mask[None, None], toe, -jnp.inf)
    two = jnp.concatenate([toe[:, :-1], toe[:, 1:]], axis=-1)
    two = two.reshape(heads // 2, 2, 2 * NA_ROWS - 2, w, 2 * w)
    return two.transpose(0, 2, 1, 3, 4).reshape(heads // 2, 2 * NA_ROWS - 2, 2 * w, 2 * w)


NA_ROWS_PER_STEP = 32


NA_PAIRS_PER_STEP = 2


def _na_kernel(q_ref, k_ref, v_ref, bias_ref, o_ref, s_scr, p_scr):
    for pp in range(NA_PAIRS_PER_STEP):
        lanes = slice(pp * LANES, (pp + 1) * LANES)
        _na_pair(q_ref.at[:, lanes], k_ref.at[:, lanes], v_ref.at[:, lanes], bias_ref.at[pp], o_ref.at[:, lanes],
                 s_scr, p_scr)


def _na_pair(q_ref, k_ref, v_ref, bias_ref, o_ref, s_scr, p_scr):
    w = GRID_W
    seq = q_ref.shape[0]
    rows_n = seq // w
    kr = NA_ROWS
    first = lax.broadcasted_iota(jnp.int32, (w, LANES), 1) < NA_HEAD_DIM
    scale = NA_HEAD_DIM ** -0.5

    def window_row(r):
        return jnp.clip(r - kr // 2, 0, rows_n - kr)

    def body(it, carry):
        r0 = it * NA_ROWS_PER_STEP
        for u in range(NA_ROWS_PER_STEP):
            r = r0 + u
            rs = window_row(r)
            qr = q_ref[pl.ds(pl.multiple_of(r * w, w), w), :] * scale
            zero = jnp.zeros_like(qr)
            q2 = jnp.concatenate([jnp.where(first, qr, zero), jnp.where(first, zero, qr)], axis=0)
            kk = k_ref[pl.ds(pl.multiple_of(rs * w, w), kr * w), :]
            off = rs - r + (NA_ROWS - 1)
            bias = jnp.concatenate([bias_ref[off + 2 * i] for i in range(kr // 2)], axis=1)
            s_scr[u] = _dot_nt(q2, kk) + bias
        for u in range(NA_ROWS_PER_STEP):
            m = jnp.max(s_scr[u], axis=-1, keepdims=True)
            p_scr[u] = jnp.exp(s_scr[u] - m).astype(BF16)
        for u in range(NA_ROWS_PER_STEP):
            r = r0 + u
            vv = v_ref[pl.ds(pl.multiple_of(window_row(r) * w, w), kr * w), :]
            o2 = _dot(p_scr[u], jnp.concatenate([vv, jnp.ones_like(vv)], axis=1))
            o2 = o2[:, :LANES] / o2[:, LANES:]
            o_ref[pl.ds(pl.multiple_of(r * w, w), w), :] = jnp.where(first, o2[:w], o2[w:]).astype(o_ref.dtype)
        return carry

    lax.fori_loop(0, rows_n // NA_ROWS_PER_STEP, body, 0)


def _neighbourhood_attention(proj3, bias_tab):
    b, s, _ = proj3.shape
    pairs = NA_HEADS // 2
    pps = NA_PAIRS_PER_STEP
    blk = pps * LANES
    return pl.pallas_call(
        _na_kernel,
        grid=(b, pairs // pps),
        in_specs=[
            pl.BlockSpec((None, s, blk), lambda i, p: (i, 0, OFF_NQ // blk + p)),
            pl.BlockSpec((None, s, blk), lambda i, p: (i, 0, OFF_NK // blk + p)),
            pl.BlockSpec((None, s, blk), lambda i, p: (i, 0, OFF_NV // blk + p)),
            pl.BlockSpec((pps, 2 * NA_ROWS - 2, 2 * GRID_W, 2 * GRID_W), lambda i, p: (p, 0, 0, 0)),
        ],
        out_specs=pl.BlockSpec((None, s, blk), lambda i, p: (i, 0, p)),
        out_shape=jax.ShapeDtypeStruct((b, s, NA_WIDTH), BF16),
        scratch_shapes=[
            pltpu.VMEM((NA_ROWS_PER_STEP, 2 * GRID_W, NA_ROWS * GRID_W), F32),
            pltpu.VMEM((NA_ROWS_PER_STEP, 2 * GRID_W, NA_ROWS * GRID_W), BF16),
        ],
        compiler_params=_params("parallel", "parallel"),
        name="nbr_attn",
    )(proj3, proj3, proj3, bias_tab)


def _xa_kernel(q_ref, mem_ref, g_ref, wkv_ref, o_ref, kv_scr):
    @pl.when(pl.program_id(1) == 0)
    def _():
        mn = _rmsnorm_f32(mem_ref[...], g_ref[...]).astype(BF16)
        kv_scr[...] = _dot(mn, wkv_ref[...]).astype(BF16)

    dh = XA_HEAD_DIM
    scale = dh ** -0.5
    for h in range(XA_HEADS):
        q = q_ref[:, h * dh:(h + 1) * dh]
        k = kv_scr[:, h * dh:(h + 1) * dh]
        v = kv_scr[:, XA_WIDTH + h * dh:XA_WIDTH + (h + 1) * dh]
        s = _dot_nt(q, k) * scale
        m = jnp.max(s, axis=-1, keepdims=True)
        p = jnp.exp(s - m).astype(BF16)
        o2 = _dot(p, jnp.concatenate([v, jnp.ones_like(v)], axis=1))
        o_ref[:, h * dh:(h + 1) * dh] = (o2[:, :dh] / o2[:, dh:]).astype(o_ref.dtype)


def _memory_attention(proj3, mem, g_mem, wkv_bf16, ts=2048):
    b, s, _ = proj3.shape
    m, d = mem.shape[1], mem.shape[2]
    return pl.pallas_call(
        _xa_kernel,
        grid=(b, s // ts),
        in_specs=[
            pl.BlockSpec((None, ts, XA_WIDTH), lambda i, j: (i, j, OFF_XQ // XA_WIDTH)),
            pl.BlockSpec((None, m, d), lambda i, j: (i, 0, 0)),
            pl.BlockSpec((1, d), lambda i, j: (0, 0)),
            pl.BlockSpec((d, 2 * XA_WIDTH), lambda i, j: (0, 0)),
        ],
        out_specs=pl.BlockSpec((None, ts, XA_WIDTH), lambda i, j: (i, j, 0)),
        out_shape=jax.ShapeDtypeStruct((b, s, XA_WIDTH), BF16),
        scratch_shapes=[pltpu.VMEM((m, 2 * XA_WIDTH), BF16)],
        compiler_params=_params("parallel", "arbitrary"),
        name="mem_attn",
    )(proj3, mem, g_mem, wkv_bf16)


def _masked_lane_max(v, mask):
    return jnp.max(jnp.where(mask, v, -jnp.inf), axis=-1, keepdims=True)


def _first_lane_eq(v, target, mask, lane):
    return jnp.min(jnp.where(mask & (v == target), lane, float(LANES)), axis=-1, keepdims=True)


def _route(logits):
    assert MOE_TOP_K == 2
    g_n, e_n = MOE_GROUPS, MOE_EXPERTS_PER_GROUP
    lane = lax.broadcasted_iota(jnp.int32, logits.shape, 1).astype(F32)
    is_grp = lane < g_n
    gmax = _masked_lane_max(logits, is_grp)
    gsum = jnp.sum(jnp.where(is_grp, jnp.exp(logits - gmax), 0.0), axis=-1, keepdims=True)
    grp_w = 1.0 / gsum
    gidx = _first_lane_eq(logits, gmax, is_grp, lane)
    lo = ROUTER_EXPERT_LANE0 + gidx * e_n
    in_grp = (lane >= lo) & (lane < lo + e_n)
    emax = _masked_lane_max(logits, in_grp)
    ex = jnp.where(in_grp, jnp.exp(logits - emax), 0.0)
    prob = ex / jnp.sum(ex, axis=-1, keepdims=True)
    p1 = _masked_lane_max(prob, in_grp)
    i1 = _first_lane_eq(prob, p1, in_grp, lane)
    rest = in_grp & (lane != i1)
    p2 = _masked_lane_max(prob, rest)
    i2 = _first_lane_eq(prob, p2, rest, lane)
    tot = p1 + p2
    w1 = p1 / tot * grp_w
    w2 = p2 / tot * grp_w
    return gidx, jnp.where(lane == i1 - lo, w1, jnp.where(lane == i2 - lo, w2, 0.0))


SORT_TILE = 512
ROW_ALIGN = 16
SORTED_ROWS = 640
WIN_ROWS = SORT_TILE
HALF_WIN = WIN_ROWS // 2
SCR_ROWS = SORTED_ROWS + WIN_ROWS
EXPERT_TILE = 512


def _group_buf_rows(t):
    return t + 2 * WIN_ROWS


def _pad_rows(n):
    return (n + (ROW_ALIGN - 1)) // ROW_ALIGN * ROW_ALIGN


def _window_copies(hs, cs, hbuf, cbuf, sems, slot, kind, g, src_row, dst_row):
    return (
        pltpu.make_async_copy(hs.at[slot, pl.ds(src_row, WIN_ROWS)], hbuf.at[g, pl.ds(dst_row, WIN_ROWS)],
                              sems.at[slot, 2 * kind, g]),
        pltpu.make_async_copy(cs.at[slot, pl.ds(src_row, WIN_ROWS)], cbuf.at[g, pl.ds(dst_row, WIN_ROWS)],
                              sems.at[slot, 2 * kind + 1, g]),
    )


def _merge_kernel(yret_ref, yna_ref, yxa_ref, gr_ref, gn_ref, gx_ref, x_ref, wro_ref, wno_ref, wxo_ref, wout_ref,
                  gffn_ref, wr_ref, br_ref, before_ref,
                  x2_ref, pos_ref, cnt_ref, out_start_ref,
                  slot_grp, slot_blk, slot_valid, slot_first, slot_next, hbuf, cbuf,
                  hs, cs, run, start_ref, sems):
    i = pl.program_id(0)
    n_tiles = pl.num_programs(0)
    slot = lax.rem(i, 2)
    tm = x_ref.shape[0]
    g_n = MOE_GROUPS

    @pl.when(i == 0)
    def _():
        for g in range(g_n):
            run[g] = 0
        hs[:, SORTED_ROWS:, :] = jnp.zeros((2, WIN_ROWS, hs.shape[2]), hs.dtype)
        cs[:, SORTED_ROWS:, :] = jnp.zeros((2, WIN_ROWS, cs.shape[2]), cs.dtype)

    y_ret = _dot(yret_ref[...], wro_ref[...])
    y_na = _dot(yna_ref[...], wno_ref[...])
    y_xa = _dot(yxa_ref[...], wxo_ref[...])
    mix = (_sigmoid(gr_ref[...].astype(F32)) * y_ret + _sigmoid(gn_ref[...].astype(F32)) * y_na
           + _sigmoid(gx_ref[...].astype(F32)) * y_xa)
    x2 = x_ref[...] + _dot(mix.astype(BF16), wout_ref[...])
    x2_ref[...] = x2
    h2 = _rmsnorm_f32(x2, gffn_ref[...])
    h_hi = h2.astype(BF16)
    h_lo = (h2 - h_hi.astype(F32)).astype(BF16)
    wr = wr_ref[...]
    wr_hi = wr.astype(BF16)
    wr_lo = (wr - wr_hi.astype(F32)).astype(BF16)
    hi_terms = _dot(h_hi, jnp.concatenate([wr_hi, wr_lo], axis=1))
    logits = hi_terms[:, :LANES] + hi_terms[:, LANES:] + _dot(h_lo, wr_hi) + br_ref[...]

    gidx, w4 = _route(logits)

    lane = lax.broadcasted_iota(jnp.int32, (tm, LANES), 1).astype(F32)
    onehot = jnp.where(lane == gidx, 1.0, 0.0)
    rank = _dot(before_ref[...], onehot.astype(BF16))
    count_row = rank[tm - 1:tm, :] + onehot[tm - 1:tm, :]
    lane_row = lax.broadcasted_iota(jnp.int32, (1, LANES), 1)
    counts = [jnp.sum(jnp.where(lane_row == g, count_row, 0.0)).astype(jnp.int32) for g in range(g_n)]
    seg_start = []
    acc = jnp.int32(0)
    for g in range(g_n):
        seg_start.append(acc)
        acc = acc + _pad_rows(counts[g])
    start_row = jnp.zeros((1, LANES), F32)
    for g in range(g_n):
        start_row = jnp.where(lane_row == g, seg_start[g].astype(F32), start_row)
    pos = jnp.sum(onehot * (rank + start_row), axis=-1, keepdims=True)
    pos_ref[...] = jnp.broadcast_to(pos, (tm, LANES))

    pos_lanes = jnp.transpose(jnp.broadcast_to(pos, (tm, LANES)))[0:1, :]
    perm = jnp.where(lax.broadcasted_iota(jnp.int32, (SORTED_ROWS, tm), 0).astype(F32) == pos_lanes, 1.0, 0.0)
    perm = perm.astype(BF16)
    hs[slot, 0:SORTED_ROWS, :] = _dot(perm, h_hi).astype(hs.dtype)
    e_n = MOE_EXPERTS_PER_GROUP
    w_hi = w4.astype(BF16).astype(F32)
    r1 = w4 - w_hi
    w_mid = r1.astype(BF16).astype(F32)
    w_lo = r1 - w_mid
    pieces = (w_hi + pltpu.roll(w_mid, e_n, 1) + pltpu.roll(w_lo, 2 * e_n, 1)).astype(BF16)
    sorted_pieces = _dot(perm, pieces)
    cs[slot, 0:SORTED_ROWS, :] = (sorted_pieces + pltpu.roll(sorted_pieces, LANES - e_n, 1)
                                  + pltpu.roll(sorted_pieces, LANES - 2 * e_n, 1))

    copies = functools.partial(_window_copies, hs, cs, hbuf, cbuf, sems)

    def wait_all(which_slot):
        for kind in range(2):
            for g in range(g_n):
                for cp in copies(which_slot, kind, g, 0, 0):
                    cp.wait()

    @pl.when(i > 0)
    def _():
        wait_all(1 - slot)

    @pl.when(i == 0)
    def _():
        top = hbuf.shape[1] - WIN_ROWS
        for g in range(g_n):
            for cp in copies(slot, 1, g, SORTED_ROWS, top):
                cp.start()
        for g in range(g_n):
            for cp in copies(slot, 1, g, 0, 0):
                cp.wait()

    tiles_left = n_tiles - 1 - i
    for g in range(g_n):
        c_g = run[g]
        c_next = c_g + _pad_rows(counts[g])
        start_ref[i, g] = c_g
        cnt_ref[i, g] = counts[g]
        run[g] = c_next
        for cp in copies(slot, 0, g, pl.multiple_of(seg_start[g], ROW_ALIGN), pl.multiple_of(c_g, ROW_ALIGN)):
            cp.start()
        dead = c_next + tiles_left * WIN_ROWS + WIN_ROWS
        for cp in copies(slot, 1, g, SORTED_ROWS, pl.multiple_of(dead, ROW_ALIGN)):
            cp.start()

    @pl.when(i == n_tiles - 1)
    def _():
        wait_all(slot)
        for g in range(g_n):
            for cp in copies(slot, 0, g, SORTED_ROWS, pl.multiple_of(run[g], ROW_ALIGN)):
                cp.start()
        for g in range(g_n):
            for cp in copies(slot, 0, g, 0, 0):
                cp.wait()
        _write_slot_tables(run, start_ref, (slot_grp, slot_blk, slot_valid, slot_first, slot_next), out_start_ref)


def _merge(y_ret, y_na, y_xa, proj, x2d, w_ret_o, w_na_o, w_xa_o, w_out, g_ffn, w_router, b_router):
    t, d = x2d.shape
    tm = SORT_TILE
    n_tiles = t // tm
    rows = _group_buf_rows(t)
    gate_blk = OFF_GATE // d
    before = jnp.asarray(np.tril(np.ones((tm, tm), np.float32), -1), BF16)
    full = lambda a: pl.BlockSpec(a.shape, lambda i: (0,) * a.ndim)
    smem = pl.BlockSpec(memory_space=pltpu.SMEM)
    hbm = pl.BlockSpec(memory_space=pl.ANY)
    return pl.pallas_call(
        _merge_kernel,
        grid=(n_tiles,),
        in_specs=[
            pl.BlockSpec((tm, RET_V_WIDTH), lambda i: (i, 0)),
            pl.BlockSpec((tm, NA_WIDTH), lambda i: (i, 0)),
            pl.BlockSpec((tm, XA_WIDTH), lambda i: (i, 0)),
            pl.BlockSpec((tm, d), lambda i: (i, gate_blk)),
            pl.BlockSpec((tm, d), lambda i: (i, gate_blk + 1)),
            pl.BlockSpec((tm, d), lambda i: (i, gate_blk + 2)),
            pl.BlockSpec((tm, d), lambda i: (i, 0)),
            full(w_ret_o), full(w_na_o), full(w_xa_o), full(w_out), full(g_ffn), full(w_router), full(b_router),
            full(before),
        ],
        out_specs=[
            pl.BlockSpec((tm, d), lambda i: (i, 0)),
            pl.BlockSpec((tm, LANES), lambda i: (i, 0)),
            smem, smem, *([smem] * 5), hbm, hbm,
        ],
        out_shape=[
            jax.ShapeDtypeStruct((t, d), F32),
            jax.ShapeDtypeStruct((t, LANES), F32),
            jax.ShapeDtypeStruct((n_tiles, MOE_GROUPS), jnp.int32),
            jax.ShapeDtypeStruct((n_tiles, MOE_GROUPS), jnp.int32),
            *([jax.ShapeDtypeStruct((_expert_slot_count(t),), jnp.int32)] * 5),
            jax.ShapeDtypeStruct((MOE_GROUPS, rows, d), BF16),
            jax.ShapeDtypeStruct((MOE_GROUPS, rows, LANES), F32),
        ],
        scratch_shapes=[
            pltpu.VMEM((2, SCR_ROWS, d), BF16),
            pltpu.VMEM((2, SCR_ROWS, LANES), F32),
            pltpu.SMEM((MOE_GROUPS,), jnp.int32),
            pltpu.SMEM((n_tiles, MOE_GROUPS), jnp.int32),
            pltpu.SemaphoreType.DMA((2, 4, MOE_GROUPS)),
        ],
        compiler_params=_params("arbitrary"),
        name="merge_router",
    )(y_ret, y_na, y_xa, proj, proj, proj, x2d, w_ret_o, w_na_o, w_xa_o, w_out, g_ffn, w_router, b_router, before)


def _expert_slot_count(t):
    return (t + (t // SORT_TILE) * MOE_GROUPS * (ROW_ALIGN - 1)) // EXPERT_TILE + MOE_GROUPS + 1


def _write_slot_tables(run, start_ref, slot_refs, out_start_ref):
    nblk, first = [], []
    total = jnp.int32(0)
    for g in range(MOE_GROUPS):
        n = (run[g] + EXPERT_TILE - 1) // EXPERT_TILE
        first.append(total)
        nblk.append(n)
        total = total + n

    def fill_slot(j, carry):
        for ref, value in zip(slot_refs, _expert_slot(j, nblk, first, total)):
            ref[j] = value.astype(jnp.int32)
        return carry

    lax.fori_loop(0, slot_refs[0].shape[0], fill_slot, 0)

    def fill_tile(i, carry):
        for g in range(MOE_GROUPS):
            out_start_ref[i, g] = first[g] * EXPERT_TILE + start_ref[i, g]
        return carry

    lax.fori_loop(0, out_start_ref.shape[0], fill_tile, 0)


def _expert_slot(j, nblk, first, total):
    valid = j < total
    jc = jnp.clip(j, 0, jnp.maximum(total - 1, 0))
    grp = jnp.int32(0)
    for g in range(MOE_GROUPS):
        grp = grp + (jc >= first[g] + nblk[g]).astype(jnp.int32)
    grp = jnp.minimum(grp, MOE_GROUPS - 1)
    first_blk = jnp.int32(0)
    nxt = jnp.int32(-1)
    for g in reversed(range(MOE_GROUPS)):
        first_blk = jnp.where(grp == g, first[g], first_blk)
        nxt = jnp.where((g > grp) & (nblk[g] > 0), g, nxt)
    return grp, jc - first_blk, valid, valid & (jc == first_blk), nxt


def _experts_kernel(grp_ref, blk_ref, valid_ref, first_ref, next_ref, h_ref, c_ref, wg_hbm, wu_hbm, wd_hbm, o_ref,
                    stage_g, stage_u, stage_d, wg_ref, wu_ref, wd_ref, sems):
    j = pl.program_id(0)
    grp, valid, is_first, nxt = grp_ref[j], valid_ref[j] == 1, first_ref[j] == 1, next_ref[j]

    def expert_copies(g, e):
        return (pltpu.make_async_copy(wg_hbm.at[g, e], stage_g.at[e], sems.at[0, e]),
                pltpu.make_async_copy(wu_hbm.at[g, e], stage_u.at[e], sems.at[1, e]),
                pltpu.make_async_copy(wd_hbm.at[g, e], stage_d.at[e], sems.at[2, e]))

    @pl.when(j == 0)
    def _():
        for e in range(MOE_EXPERTS_PER_GROUP):
            for cp in expert_copies(grp, e):
                cp.start()

    def run_block(convert_first):
        h = h_ref[...]
        c = c_ref[...]
        lane = lax.broadcasted_iota(jnp.int32, c.shape, 1)
        out = None
        for e in range(MOE_EXPERTS_PER_GROUP):
            if convert_first:
                for cp in expert_copies(grp, e):
                    cp.wait()
                for stage, dst in ((stage_g, wg_ref), (stage_u, wu_ref), (stage_d, wd_ref)):
                    for r in range(0, stage.shape[1], CAST_ROWS):
                        dst[e, r:r + CAST_ROWS, :] = stage[e, r:r + CAST_ROWS, :].astype(dst.dtype)
            a = _dot(h, wg_ref[e])
            u = _dot(h, wu_ref[e])
            cw = jnp.sum(jnp.where(lane == e, c, 0.0), axis=-1, keepdims=True)
            hid = (_silu(a) * u * cw).astype(BF16)
            part = _dot(hid, wd_ref[e])
            out = part if out is None else out + part
        o_ref[...] = out.astype(o_ref.dtype)

    @pl.when(is_first)
    def _():
        run_block(True)

        @pl.when(nxt >= 0)
        def _():
            for e in range(MOE_EXPERTS_PER_GROUP):
                for cp in expert_copies(nxt, e):
                    cp.start()

    @pl.when(valid & jnp.logical_not(is_first))
    def _():
        run_block(False)

    @pl.when(jnp.logical_not(valid))
    def _():
        o_ref[...] = jnp.zeros_like(o_ref)


def _experts(hbuf, cbuf, wg, wu, wd, slots):
    g_n, rows, d = hbuf.shape
    e_n, f = wg.shape[1], wg.shape[3]
    tm = EXPERT_TILE
    n_slots = slots[0].shape[0]
    hbm = pl.BlockSpec(memory_space=pl.ANY)
    grid_spec = pltpu.PrefetchScalarGridSpec(
        num_scalar_prefetch=5,
        grid=(n_slots,),
        in_specs=[
            pl.BlockSpec((None, tm, d), lambda j, grp, blk, v, gf, gx: (grp[j], blk[j], 0)),
            pl.BlockSpec((None, tm, LANES), lambda j, grp, blk, v, gf, gx: (grp[j], blk[j], 0)),
            hbm, hbm, hbm,
        ],
        out_specs=pl.BlockSpec((tm, d), lambda j, grp, blk, v, gf, gx: (j, 0)),
        scratch_shapes=[
            pltpu.VMEM((e_n, d, f), wg.dtype),
            pltpu.VMEM((e_n, d, f), wu.dtype),
            pltpu.VMEM((e_n, f, d), wd.dtype),
            pltpu.VMEM((e_n, d, f), BF16),
            pltpu.VMEM((e_n, d, f), BF16),
            pltpu.VMEM((e_n, f, d), BF16),
            pltpu.SemaphoreType.DMA((3, e_n)),
        ],
    )
    return pl.pallas_call(
        _experts_kernel,
        grid_spec=grid_spec,
        out_shape=jax.ShapeDtypeStruct((n_slots * tm, d), BF16),
        compiler_params=pltpu.CompilerParams(dimension_semantics=("arbitrary",),
                                             vmem_limit_bytes=STAGING_VMEM_LIMIT),
        name="experts",
    )(*slots, hbuf, cbuf, wg, wu, wd)


FINISH_TILES = 2


def _finish_kernel(start_ref, cnt_ref, *refs, final_norm):
    n_win = 2 * MOE_GROUPS
    wins = refs[:FINISH_TILES * n_win]
    pos_ref, x2_ref, gfin_ref, o_ref, sorted_scr = refs[FINISH_TILES * n_win:]
    i = pl.program_id(0)
    tm = SORT_TILE
    half_pieces = HALF_WIN // ROW_ALIGN

    @pl.when(i == 0)
    def _():
        sorted_scr[...] = jnp.zeros_like(sorted_scr)

    for tt in range(FINISH_TILES):
        tile = i * FINISH_TILES + tt
        rows = pl.ds(tt * tm, tm)
        seg = jnp.int32(0)
        for g in range(MOE_GROUPS):
            padded = _pad_rows(cnt_ref[tile, g])
            pieces = padded // ROW_ALIGN
            for half in range(2):
                win = wins[tt * n_win + 2 * g + half]
                base = seg + half * HALF_WIN

                def copy_piece(k, carry, win=win, base=base, tt=tt):
                    src = pl.multiple_of(k * ROW_ALIGN, ROW_ALIGN)
                    dst = pl.multiple_of(base + k * ROW_ALIGN, ROW_ALIGN)
                    sorted_scr[tt, pl.ds(dst, ROW_ALIGN), :] = win[pl.ds(src, ROW_ALIGN), :]
                    return carry

                lax.fori_loop(0, jnp.clip(pieces - half * half_pieces, 0, half_pieces), copy_piece, 0)
            seg = seg + padded

        pos = pos_ref[rows, 0:1]
        unperm = jnp.where(lax.broadcasted_iota(jnp.int32, (tm, SORTED_ROWS), 1).astype(F32) == pos, 1.0, 0.0)
        y = x2_ref[rows, :] + _dot(unperm.astype(BF16), sorted_scr[tt])
        if final_norm:
            y = _rmsnorm_f32(y, gfin_ref[...])
        o_ref[rows, :] = y


def _finish(mbuf, starts, counts, pos, x2, g_final, final_norm):
    t, d = x2.shape
    tm = SORT_TILE * FINISH_TILES

    def window(tt, g, half):
        def index(i, st, ct):
            tile = i * FINISH_TILES + tt
            row = st[tile, g] + half * HALF_WIN
            if half:
                row = jnp.where(_pad_rows(ct[tile, g]) > HALF_WIN, row, 0)
            return pl.multiple_of(row, ROW_ALIGN), 0

        return pl.BlockSpec((pl.Element(HALF_WIN), pl.Element(d)), index)

    windows = [window(tt, g, half) for tt in range(FINISH_TILES) for g in range(MOE_GROUPS) for half in range(2)]
    grid_spec = pltpu.PrefetchScalarGridSpec(
        num_scalar_prefetch=2,
        grid=(t // tm,),
        in_specs=[
            *windows,
            pl.BlockSpec((tm, LANES), lambda i, st, ct: (i, 0)),
            pl.BlockSpec((tm, d), lambda i, st, ct: (i, 0)),
            pl.BlockSpec((1, d), lambda i, st, ct: (0, 0)),
        ],
        out_specs=pl.BlockSpec((tm, d), lambda i, st, ct: (i, 0)),
        scratch_shapes=[pltpu.VMEM((FINISH_TILES, SORTED_ROWS, d), BF16)],
    )
    return pl.pallas_call(
        functools.partial(_finish_kernel, final_norm=final_norm),
        grid_spec=grid_spec,
        out_shape=jax.ShapeDtypeStruct((t, d), F32),
        compiler_params=_params("arbitrary"),
        name="finish",
    )(starts, counts, *([mbuf] * len(windows)), pos, x2, g_final)


def kernel(x, mem, g_mix, w_in, ret_decay_fwd, ret_decay_bwd, ret_norm_gain, w_ret_o, na_rpb, w_na_o, g_mem, w_mem_kv, w_xa_o, w_out, g_ffn, w_router_group, b_router_group, w_router_expert, b_router_expert, w_exp_gate, w_exp_up, w_exp_down, g_final):
    b, s, d = x.shape
    depth = w_in.shape[0]
    t = b * s
    cos_t, sin_t = _rope_tables(s)
    row = lambda v: v.reshape(1, -1).astype(F32)
    x2d = x.reshape(t, d)
    for l in range(depth):
        proj = _inproj(x2d, row(g_mix[l]), w_in[l], cos_t, sin_t)
        proj3 = proj.reshape(b, s, IN_WIDTH)
        y_ret = _retention(proj3, ret_decay_fwd[l].astype(F32), ret_decay_bwd[l].astype(F32),
                           row(ret_norm_gain[l]))
        y_na = _neighbourhood_attention(proj3, _na_bias_table(na_rpb[l]))
        y_xa = _memory_attention(proj3, mem, row(g_mem[l]), w_mem_kv[l].astype(BF16))
        n_r = MOE_GROUPS + N_EXPERTS
        w_router = jnp.pad(jnp.concatenate([w_router_group[l], w_router_expert[l]], axis=1).astype(F32),
                           ((0, 0), (0, LANES - n_r)))
        b_router = jnp.pad(jnp.concatenate([b_router_group[l], b_router_expert[l]]).astype(F32),
                           (0, LANES - n_r)).reshape(1, LANES)
        x2, pos, counts, out_starts, *slots, hbuf, cbuf = _merge(
            y_ret.reshape(t, -1), y_na.reshape(t, -1), y_xa.reshape(t, -1), proj, x2d,
            w_ret_o[l].astype(BF16), w_na_o[l].astype(BF16), w_xa_o[l].astype(BF16),
            w_out[l].astype(BF16), row(g_ffn[l]), w_router, b_router)
        mbuf = _experts(hbuf, cbuf, w_exp_gate[l], w_exp_up[l], w_exp_down[l], slots)
        x2d = _finish(mbuf, out_starts, counts, pos, x2, row(g_final), final_norm=(l == depth - 1))
    return x2d.reshape(b, s, d)
```

```python
import functools

import jax
import jax.numpy as jnp
import numpy as np
from jax import lax
from jax.experimental import pallas as pl
from jax.experimental.pallas import tpu as pltpu

D_MODEL = 1024
GRID_W = 64
N_BRANCHES = 3
RET_HEADS = 4
RET_QK_DIM = 128
RET_V_DIM = 256
RET_CHUNK = 128
ROPE_BASE = 10000.0
NA_HEADS = 8
NA_HEAD_DIM = 64
NA_ROWS = 8
NA_COLS = 16
XA_HEADS = 4
XA_HEAD_DIM = 128
MOE_GROUPS = 4
MOE_EXPERTS_PER_GROUP = 4
MOE_TOP_K = 2
RMS_EPS = 1e-6
GN_EPS = 1e-5

RET_QK_WIDTH = RET_HEADS * RET_QK_DIM
RET_V_WIDTH = RET_HEADS * RET_V_DIM
NA_WIDTH = NA_HEADS * NA_HEAD_DIM
XA_WIDTH = XA_HEADS * XA_HEAD_DIM
IN_WIDTH = 2 * RET_QK_WIDTH + 2 * RET_V_WIDTH + 3 * NA_WIDTH + XA_WIDTH + N_BRANCHES * D_MODEL

OFF_RQ = 0
OFF_RK = OFF_RQ + RET_QK_WIDTH
OFF_RV = OFF_RK + RET_QK_WIDTH
OFF_RG = OFF_RV + RET_V_WIDTH
OFF_NQ = OFF_RG + RET_V_WIDTH
OFF_NK = OFF_NQ + NA_WIDTH
OFF_NV = OFF_NK + NA_WIDTH
OFF_XQ = OFF_NV + NA_WIDTH
OFF_GATE = OFF_XQ + XA_WIDTH

N_EXPERTS = MOE_GROUPS * MOE_EXPERTS_PER_GROUP
LANES = 128
ROUTER_EXPERT_LANE0 = MOE_GROUPS

VMEM_LIMIT = 48 * 1024 * 1024
STAGING_VMEM_LIMIT = 56 * 1024 * 1024
CAST_ROWS = 256

F32 = jnp.float32
BF16 = jnp.bfloat16


def _params(*sem):
    return pltpu.CompilerParams(dimension_semantics=sem, vmem_limit_bytes=VMEM_LIMIT)


def _rmsnorm_f32(x, g):
    return x * lax.rsqrt(jnp.mean(x * x, axis=-1, keepdims=True) + RMS_EPS) * g


def _sigmoid(x):
    return 0.5 * jnp.tanh(0.5 * x) + 0.5


def _silu(x):
    h = 0.5 * x
    return h + h * jnp.tanh(h)


def _dot(a, b):
    return jnp.dot(a, b, preferred_element_type=F32)


def _dot_nt(a, b):
    return lax.dot_general(a, b, (((1,), (1,)), ((), ())), preferred_element_type=F32)


INPROJ_STAGE_COLS = 1024


def _inproj_kernel(x_ref, g_ref, cos_ref, sin_ref, w_hbm, o_ref, h_scr, w_scr, stage, sems):
    i = pl.program_id(0)
    j = pl.program_id(1)
    n_col = pl.num_programs(1)
    tn = o_ref.shape[1]
    per_block = tn // INPROJ_STAGE_COLS

    def chunk_copy(col_block, k):
        cols = pl.ds(pl.multiple_of((col_block * per_block + k) * INPROJ_STAGE_COLS, INPROJ_STAGE_COLS),
                     INPROJ_STAGE_COLS)
        return pltpu.make_async_copy(w_hbm.at[:, cols], stage.at[k], sems.at[k])

    @pl.when((i == 0) & (j == 0))
    def _():
        for k in range(per_block):
            chunk_copy(0, k).start()

    @pl.when(i == 0)
    def _():
        for k in range(per_block):
            chunk_copy(j, k).wait()

            def convert(r, carry, k=k):
                rows = pl.ds(pl.multiple_of(r * CAST_ROWS, CAST_ROWS), CAST_ROWS)
                w_scr[j, rows, k * INPROJ_STAGE_COLS:(k + 1) * INPROJ_STAGE_COLS] = stage[k, rows, :].astype(BF16)
                return carry

            lax.fori_loop(0, stage.shape[1] // CAST_ROWS, convert, 0)

        @pl.when(j + 1 < n_col)
        def _():
            for k in range(per_block):
                chunk_copy(j + 1, k).start()

    @pl.when(j == 0)
    def _():
        h_scr[...] = _rmsnorm_f32(x_ref[...], g_ref[...]).astype(BF16)

    half = RET_QK_DIM // 2
    for jj in range(w_scr.shape[0]):
        @pl.when(j == jj)
        def _(jj=jj):
            res = _dot(h_scr[...], w_scr[jj])
            for c in range(0, tn, RET_QK_DIM):
                a = res[:, c:c + RET_QK_DIM]
                col = jj * tn + c
                if OFF_RQ <= col < OFF_RV:
                    a = a * cos_ref[...] + pltpu.roll(a, half, 1) * sin_ref[...]
                    if col < OFF_RK:
                        a = a * (RET_QK_DIM ** -0.5)
                elif OFF_RG <= col < OFF_NQ:
                    a = _silu(a)
                o_ref[:, c:c + RET_QK_DIM] = a.astype(o_ref.dtype)


def _inproj(x2d, g, w, cos_t, sin_t, tm=1024, tn=2048):
    t, d = x2d.shape
    n = w.shape[1]
    per_block = tn // INPROJ_STAGE_COLS
    seq_tiles = cos_t.shape[0] // tm
    return pl.pallas_call(
        _inproj_kernel,
        grid=(t // tm, n // tn),
        in_specs=[
            pl.BlockSpec((tm, d), lambda i, j: (i, 0)),
            pl.BlockSpec((1, d), lambda i, j: (0, 0)),
            pl.BlockSpec((tm, RET_QK_DIM), lambda i, j: (i % seq_tiles, 0)),
            pl.BlockSpec((tm, RET_QK_DIM), lambda i, j: (i % seq_tiles, 0)),
            pl.BlockSpec(memory_space=pl.ANY),
        ],
        out_specs=pl.BlockSpec((tm, tn), lambda i, j: (i, j)),
        out_shape=jax.ShapeDtypeStruct((t, n), BF16),
        scratch_shapes=[
            pltpu.VMEM((tm, d), BF16),
            pltpu.VMEM((n // tn, d, tn), BF16),
            pltpu.VMEM((per_block, d, INPROJ_STAGE_COLS), w.dtype),
            pltpu.SemaphoreType.DMA((per_block,)),
        ],
        compiler_params=pltpu.CompilerParams(dimension_semantics=("arbitrary", "arbitrary"),
                                             vmem_limit_bytes=STAGING_VMEM_LIMIT),
        name="inproj",
    )(x2d, g, cos_t, sin_t, w)


def _log_sigmoid(x):
    return jnp.minimum(x, 0.0) - jnp.log1p(jnp.exp(-jnp.abs(x)))


RET_HEADS_PER_STEP = 2


RET_GROUPS_PER_STEP = 2


def _retention_kernel(decf_ref, decb_ref, q_ref, k_ref, v_ref, rg_ref, gn_ref, o_ref, *scratch):
    qk, dv = RET_QK_DIM, RET_V_DIM
    heads_per_step = RET_HEADS_PER_STEP * RET_GROUPS_PER_STEP
    for hh in range(heads_per_step):
        qcols = slice(hh * qk, (hh + 1) * qk)
        vcols = slice(hh * dv, (hh + 1) * dv)
        _retention_head(pl.program_id(1) * heads_per_step + hh, decf_ref, decb_ref,
                        q_ref.at[:, qcols], k_ref.at[:, qcols], v_ref.at[:, vcols], rg_ref.at[:, vcols],
                        gn_ref.at[:, vcols], o_ref.at[:, vcols],
                        *[s.at[hh % RET_HEADS_PER_STEP] for s in scratch])


def _retention_head(h, decf_ref, decb_ref, q_ref, k_ref, v_ref, rg_ref, gn_ref, o_ref,
                    qfs, qbs, y_scr, kvf_scr, kvb_scr, st_scr):
    c = RET_CHUNK
    seq = q_ref.shape[0]
    n_chunks = seq // c

    lgf = _log_sigmoid(jnp.full((1, 1), decf_ref[h], F32))
    lgb = _log_sigmoid(jnp.full((1, 1), decb_ref[h], F32))

    ii = lax.broadcasted_iota(jnp.int32, (c, c), 0)
    jj = lax.broadcasted_iota(jnp.int32, (c, c), 1)
    diff = (ii - jj).astype(F32)
    dmat = jnp.exp(jnp.where(diff >= 0, diff * lgf, (-diff) * lgb))
    pos = lax.broadcasted_iota(jnp.int32, (c, 1), 0).astype(F32)
    qd_f = jnp.exp((pos + 1.0) * lgf)
    kd_f = jnp.exp((c - 1.0 - pos) * lgf)
    cd_f = jnp.exp(c * lgf)
    qd_b = jnp.exp((c - pos) * lgb)
    kd_b = jnp.exp(pos * lgb)
    cd_b = jnp.exp(c * lgb)

    def intra(n):
        rows = pl.ds(n * c, c)
        q = q_ref[rows, :].astype(F32)
        qfs[rows, :] = (q * qd_f).astype(BF16)
        qbs[rows, :] = (q * qd_b).astype(BF16)
        k = k_ref[rows, :].astype(F32)
        s = _dot_nt(q_ref[rows, :], k_ref[rows, :]) * dmat
        lhs = jnp.concatenate([s.astype(BF16), (k * kd_f).T.astype(BF16), (k * kd_b).T.astype(BF16)], axis=0)
        r = _dot(lhs, v_ref[rows, :])
        y_scr[rows, :] = r[:c]
        kvf_scr[n] = r[c:2 * c]
        kvb_scr[n] = r[2 * c:]

    for n in range(n_chunks):
        intra(n)

    st_scr[...] = jnp.zeros_like(st_scr)

    def fwd(n):
        rows = pl.ds(n * c, c)
        st = st_scr[...]
        y_scr[rows, :] += _dot(qfs[rows, :], st.astype(BF16))
        st_scr[...] = st * cd_f + kvf_scr[n]

    for n in range(n_chunks):
        fwd(n)

    st_scr[...] = jnp.zeros_like(st_scr)
    gn = gn_ref[...]

    def bwd(n):
        rows = pl.ds(n * c, c)
        st = st_scr[...]
        y = y_scr[rows, :] + _dot(qbs[rows, :], st.astype(BF16))
        st_scr[...] = st * cd_b + kvb_scr[n]
        mu = jnp.mean(y, axis=-1, keepdims=True)
        yc = y - mu
        var = jnp.mean(yc * yc, axis=-1, keepdims=True)
        yn = yc * lax.rsqrt(var + GN_EPS) * gn
        o_ref[rows, :] = rg_ref[rows, :] * yn.astype(o_ref.dtype)

    for n in reversed(range(n_chunks)):
        bwd(n)


def _retention(proj3, dec_f, dec_b, gn_gain):
    b, s, _ = proj3.shape
    hps = RET_HEADS_PER_STEP
    heads_per_step = hps * RET_GROUPS_PER_STEP
    dk, dv = RET_QK_DIM, RET_V_DIM
    qb, vb = heads_per_step * dk, heads_per_step * dv
    smem = pl.BlockSpec(memory_space=pltpu.SMEM)
    return pl.pallas_call(
        _retention_kernel,
        grid=(b, RET_HEADS // heads_per_step),
        in_specs=[
            smem,
            smem,
            pl.BlockSpec((None, s, qb), lambda i, h: (i, 0, OFF_RQ // qb + h)),
            pl.BlockSpec((None, s, qb), lambda i, h: (i, 0, OFF_RK // qb + h)),
            pl.BlockSpec((None, s, vb), lambda i, h: (i, 0, OFF_RV // vb + h)),
            pl.BlockSpec((None, s, vb), lambda i, h: (i, 0, OFF_RG // vb + h)),
            pl.BlockSpec((1, vb), lambda i, h: (0, h)),
        ],
        out_specs=pl.BlockSpec((None, s, vb), lambda i, h: (i, 0, h)),
        out_shape=jax.ShapeDtypeStruct((b, s, RET_V_WIDTH), BF16),
        scratch_shapes=[
            pltpu.VMEM((hps, s, dk), BF16),
            pltpu.VMEM((hps, s, dk), BF16),
            pltpu.VMEM((hps, s, dv), F32),
            pltpu.VMEM((hps, s // RET_CHUNK, dk, dv), F32),
            pltpu.VMEM((hps, s // RET_CHUNK, dk, dv), F32),
            pltpu.VMEM((hps, dk, dv), F32),
        ],
        compiler_params=_params("parallel", "parallel"),
        name="retention",
    )(dec_f, dec_b, proj3, proj3, proj3, proj3, gn_gain)


def _rope_tables(seq):
    half = RET_QK_DIM // 2
    inv_freq = ROPE_BASE ** (-np.arange(half, dtype=np.float64) / half)
    ang = np.arange(seq, dtype=np.float64)[:, None] * inv_freq[None, :]
    cos, sin = np.cos(ang), np.sin(ang)
    return (jnp.asarray(np.concatenate([cos, cos], axis=-1), F32),
            jnp.asarray(np.concatenate([-sin, sin], axis=-1), F32))


def _na_bias_table(rpb):
    heads = rpb.shape[0]
    w = GRID_W
    cols = np.arange(w)
    col_start = np.clip(cols - NA_COLS // 2, 0, w - NA_COLS)
    col_off = cols[None, :] - col_start[:, None]
    col_mask = (col_off >= 0) & (col_off < NA_COLS)
    rel_c = np.clip(cols[None, :] - cols[:, None], -(NA_COLS - 1), NA_COLS - 1) + (NA_COLS - 1)
    onehot = (rel_c[:, :, None] == np.arange(2 * NA_COLS - 1)).astype(np.float32)
    toe = jnp.einsum('hrc,qkc->hrqk', rpb.astype(F32), onehot, precision=lax.Precision.HIGHEST)
    toe = jnp.where(col_mask[None, None], toe, -jnp.inf)
    two = jnp.concatenate([toe[:, :-1], toe[:, 1:]], axis=-1)
    two = two.reshape(heads // 2, 2, 2 * NA_ROWS - 2, w, 2 * w)
    return two.transpose(0, 2, 1, 3, 4).reshape(heads // 2, 2 * NA_ROWS - 2, 2 * w, 2 * w)


NA_ROWS_PER_STEP = 32


NA_PAIRS_PER_STEP = 2


def _na_kernel(q_ref, k_ref, v_ref, bias_ref, o_ref, s_scr, p_scr):
    for pp in range(NA_PAIRS_PER_STEP):
        lanes = slice(pp * LANES, (pp + 1) * LANES)
        _na_pair(q_ref.at[:, lanes], k_ref.at[:, lanes], v_ref.at[:, lanes], bias_ref.at[pp], o_ref.at[:, lanes],
                 s_scr, p_scr)


def _na_pair(q_ref, k_ref, v_ref, bias_ref, o_ref, s_scr, p_scr):
    w = GRID_W
    seq = q_ref.shape[0]
    rows_n = seq // w
    kr = NA_ROWS
    first = lax.broadcasted_iota(jnp.int32, (w, LANES), 1) < NA_HEAD_DIM
    scale = NA_HEAD_DIM ** -0.5

    def window_row(r):
        return jnp.clip(r - kr // 2, 0, rows_n - kr)

    def body(it, carry):
        r0 = it * NA_ROWS_PER_STEP
        for u in range(NA_ROWS_PER_STEP):
            r = r0 + u
            rs = window_row(r)
            qr = q_ref[pl.ds(pl.multiple_of(r * w, w), w), :] * scale
            zero = jnp.zeros_like(qr)
            q2 = jnp.concatenate([jnp.where(first, qr, zero), jnp.where(first, zero, qr)], axis=0)
            kk = k_ref[pl.ds(pl.multiple_of(rs * w, w), kr * w), :]
            off = rs - r + (NA_ROWS - 1)
            bias = jnp.concatenate([bias_ref[off + 2 * i] for i in range(kr // 2)], axis=1)
            s_scr[u] = _dot_nt(q2, kk) + bias
        for u in range(NA_ROWS_PER_STEP):
            m = jnp.max(s_scr[u], axis=-1, keepdims=True)
            p_scr[u] = jnp.exp(s_scr[u] - m).astype(BF16)
        for u in range(NA_ROWS_PER_STEP):
            r = r0 + u
            vv = v_ref[pl.ds(pl.multiple_of(window_row(r) * w, w), kr * w), :]
            o2 = _dot(p_scr[u], jnp.concatenate([vv, jnp.ones_like(vv)], axis=1))
            o2 = o2[:, :LANES] / o2[:, LANES:]
            o_ref[pl.ds(pl.multiple_of(r * w, w), w), :] = jnp.where(first, o2[:w], o2[w:]).astype(o_ref.dtype)
        return carry

    lax.fori_loop(0, rows_n // NA_ROWS_PER_STEP, body, 0)


def _neighbourhood_attention(proj3, bias_tab):
    b, s, _ = proj3.shape
    pairs = NA_HEADS // 2
    pps = NA_PAIRS_PER_STEP
    blk = pps * LANES
    return pl.pallas_call(
        _na_kernel,
        grid=(b, pairs // pps),
        in_specs=[
            pl.BlockSpec((None, s, blk), lambda i, p: (i, 0, OFF_NQ // blk + p)),
            pl.BlockSpec((None, s, blk), lambda i, p: (i, 0, OFF_NK // blk + p)),
            pl.BlockSpec((None, s, blk), lambda i, p: (i, 0, OFF_NV // blk + p)),
            pl.BlockSpec((pps, 2 * NA_ROWS - 2, 2 * GRID_W, 2 * GRID_W), lambda i, p: (p, 0, 0, 0)),
        ],
        out_specs=pl.BlockSpec((None, s, blk), lambda i, p: (i, 0, p)),
        out_shape=jax.ShapeDtypeStruct((b, s, NA_WIDTH), BF16),
        scratch_shapes=[
            pltpu.VMEM((NA_ROWS_PER_STEP, 2 * GRID_W, NA_ROWS * GRID_W), F32),
            pltpu.VMEM((NA_ROWS_PER_STEP, 2 * GRID_W, NA_ROWS * GRID_W), BF16),
        ],
        compiler_params=_params("parallel", "parallel"),
        name="nbr_attn",
    )(proj3, proj3, proj3, bias_tab)


def _xa_kernel(q_ref, mem_ref, g_ref, wkv_ref, o_ref, kv_scr):
    @pl.when(pl.program_id(1) == 0)
    def _():
        mn = _rmsnorm_f32(mem_ref[...], g_ref[...]).astype(BF16)
        kv_scr[...] = _dot(mn, wkv_ref[...]).astype(BF16)

    dh = XA_HEAD_DIM
    scale = dh ** -0.5
    for h in range(XA_HEADS):
        q = q_ref[:, h * dh:(h + 1) * dh]
        k = kv_scr[:, h * dh:(h + 1) * dh]
        v = kv_scr[:, XA_WIDTH + h * dh:XA_WIDTH + (h + 1) * dh]
        s = _dot_nt(q, k) * scale
        m = jnp.max(s, axis=-1, keepdims=True)
        p = jnp.exp(s - m).astype(BF16)
        o2 = _dot(p, jnp.concatenate([v, jnp.ones_like(v)], axis=1))
        o_ref[:, h * dh:(h + 1) * dh] = (o2[:, :dh] / o2[:, dh:]).astype(o_ref.dtype)


def _memory_attention(proj3, mem, g_mem, wkv_bf16, ts=2048):
    b, s, _ = proj3.shape
    m, d = mem.shape[1], mem.shape[2]
    return pl.pallas_call(
        _xa_kernel,
        grid=(b, s // ts),
        in_specs=[
            pl.BlockSpec((None, ts, XA_WIDTH), lambda i, j: (i, j, OFF_XQ // XA_WIDTH)),
            pl.BlockSpec((None, m, d), lambda i, j: (i, 0, 0)),
            pl.BlockSpec((1, d), lambda i, j: (0, 0)),
            pl.BlockSpec((d, 2 * XA_WIDTH), lambda i, j: (0, 0)),
        ],
        out_specs=pl.BlockSpec((None, ts, XA_WIDTH), lambda i, j: (i, j, 0)),
        out_shape=jax.ShapeDtypeStruct((b, s, XA_WIDTH), BF16),
        scratch_shapes=[pltpu.VMEM((m, 2 * XA_WIDTH), BF16)],
        compiler_params=_params("parallel", "arbitrary"),
        name="mem_attn",
    )(proj3, mem, g_mem, wkv_bf16)


def _masked_lane_max(v, mask):
    return jnp.max(jnp.where(mask, v, -jnp.inf), axis=-1, keepdims=True)


def _first_lane_eq(v, target, mask, lane):
    return jnp.min(jnp.where(mask & (v == target), lane, float(LANES)), axis=-1, keepdims=True)


def _route(logits):
    assert MOE_TOP_K == 2
    g_n, e_n = MOE_GROUPS, MOE_EXPERTS_PER_GROUP
    lane = lax.broadcasted_iota(jnp.int32, logits.shape, 1).astype(F32)
    is_grp = lane < g_n
    gmax = _masked_lane_max(logits, is_grp)
    gsum = jnp.sum(jnp.where(is_grp, jnp.exp(logits - gmax), 0.0), axis=-1, keepdims=True)
    grp_w = 1.0 / gsum
    gidx = _first_lane_eq(logits, gmax, is_grp, lane)
    lo = ROUTER_EXPERT_LANE0 + gidx * e_n
    in_grp = (lane >= lo) & (lane < lo + e_n)
    emax = _masked_lane_max(logits, in_grp)
    ex = jnp.where(in_grp, jnp.exp(logits - emax), 0.0)
    prob = ex / jnp.sum(ex, axis=-1, keepdims=True)
    p1 = _masked_lane_max(prob, in_grp)
    i1 = _first_lane_eq(prob, p1, in_grp, lane)
    rest = in_grp & (lane != i1)
    p2 = _masked_lane_max(prob, rest)
    i2 = _first_lane_eq(prob, p2, rest, lane)
    tot = p1 + p2
    w1 = p1 / tot * grp_w
    w2 = p2 / tot * grp_w
    return gidx, jnp.where(lane == i1 - lo, w1, jnp.where(lane == i2 - lo, w2, 0.0))


SORT_TILE = 512
ROW_ALIGN = 16
SORTED_ROWS = 640
WIN_ROWS = SORT_TILE
HALF_WIN = WIN_ROWS // 2
SCR_ROWS = SORTED_ROWS + WIN_ROWS
EXPERT_TILE = 512


def _group_buf_rows(t):
    return t + 2 * WIN_ROWS


def _pad_rows(n):
    return (n + (ROW_ALIGN - 1)) // ROW_ALIGN * ROW_ALIGN


def _window_copies(hs, cs, hbuf, cbuf, sems, slot, kind, g, src_row, dst_row):
    return (
        pltpu.make_async_copy(hs.at[slot, pl.ds(src_row, WIN_ROWS)], hbuf.at[g, pl.ds(dst_row, WIN_ROWS)],
                              sems.at[slot, 2 * kind, g]),
        pltpu.make_async_copy(cs.at[slot, pl.ds(src_row, WIN_ROWS)], cbuf.at[g, pl.ds(dst_row, WIN_ROWS)],
                              sems.at[slot, 2 * kind + 1, g]),
    )


def _merge_kernel(yret_ref, yna_ref, yxa_ref, gr_ref, gn_ref, gx_ref, x_ref, wro_ref, wno_ref, wxo_ref, wout_ref,
                  gffn_ref, wr_ref, br_ref, before_ref,
                  x2_ref, pos_ref, cnt_ref, out_start_ref,
                  slot_grp, slot_blk, slot_valid, slot_first, slot_next, hbuf, cbuf,
                  hs, cs, run, start_ref, sems):
    i = pl.program_id(0)
    n_tiles = pl.num_programs(0)
    slot = lax.rem(i, 2)
    tm = x_ref.shape[0]
    g_n = MOE_GROUPS

    @pl.when(i == 0)
    def _():
        for g in range(g_n):
            run[g] = 0
        hs[:, SORTED_ROWS:, :] = jnp.zeros((2, WIN_ROWS, hs.shape[2]), hs.dtype)
        cs[:, SORTED_ROWS:, :] = jnp.zeros((2, WIN_ROWS, cs.shape[2]), cs.dtype)

    y_ret = _dot(yret_ref[...], wro_ref[...])
    y_na = _dot(yna_ref[...], wno_ref[...])
    y_xa = _dot(yxa_ref[...], wxo_ref[...])
    mix = (_sigmoid(gr_ref[...].astype(F32)) * y_ret + _sigmoid(gn_ref[...].astype(F32)) * y_na
           + _sigmoid(gx_ref[...].astype(F32)) * y_xa)
    x2 = x_ref[...] + _dot(mix.astype(BF16), wout_ref[...])
    x2_ref[...] = x2
    h2 = _rmsnorm_f32(x2, gffn_ref[...])
    h_hi = h2.astype(BF16)
    h_lo = (h2 - h_hi.astype(F32)).astype(BF16)
    wr = wr_ref[...]
    wr_hi = wr.astype(BF16)
    wr_lo = (wr - wr_hi.astype(F32)).astype(BF16)
    hi_terms = _dot(h_hi, jnp.concatenate([wr_hi, wr_lo], axis=1))
    logits = hi_terms[:, :LANES] + hi_terms[:, LANES:] + _dot(h_lo, wr_hi) + br_ref[...]

    gidx, w4 = _route(logits)

    lane = lax.broadcasted_iota(jnp.int32, (tm, LANES), 1).astype(F32)
    onehot = jnp.where(lane == gidx, 1.0, 0.0)
    rank = _dot(before_ref[...], onehot.astype(BF16))
    count_row = rank[tm - 1:tm, :] + onehot[tm - 1:tm, :]
    lane_row = lax.broadcasted_iota(jnp.int32, (1, LANES), 1)
    counts = [jnp.sum(jnp.where(lane_row == g, count_row, 0.0)).astype(jnp.int32) for g in range(g_n)]
    seg_start = []
    acc = jnp.int32(0)
    for g in range(g_n):
        seg_start.append(acc)
        acc = acc + _pad_rows(counts[g])
    start_row = jnp.zeros((1, LANES), F32)
    for g in range(g_n):
        start_row = jnp.where(lane_row == g, seg_start[g].astype(F32), start_row)
    pos = jnp.sum(onehot * (rank + start_row), axis=-1, keepdims=True)
    pos_ref[...] = jnp.broadcast_to(pos, (tm, LANES))

    pos_lanes = jnp.transpose(jnp.broadcast_to(pos, (tm, LANES)))[0:1, :]
    perm = jnp.where(lax.broadcasted_iota(jnp.int32, (SORTED_ROWS, tm), 0).astype(F32) == pos_lanes, 1.0, 0.0)
    perm = perm.astype(BF16)
    hs[slot, 0:SORTED_ROWS, :] = _dot(perm, h_hi).astype(hs.dtype)
    e_n = MOE_EXPERTS_PER_GROUP
    w_hi = w4.astype(BF16).astype(F32)
    r1 = w4 - w_hi
    w_mid = r1.astype(BF16).astype(F32)
    w_lo = r1 - w_mid
    pieces = (w_hi + pltpu.roll(w_mid, e_n, 1) + pltpu.roll(w_lo, 2 * e_n, 1)).astype(BF16)
    sorted_pieces = _dot(perm, pieces)
    cs[slot, 0:SORTED_ROWS, :] = (sorted_pieces + pltpu.roll(sorted_pieces, LANES - e_n, 1)
                                  + pltpu.roll(sorted_pieces, LANES - 2 * e_n, 1))

    copies = functools.partial(_window_copies, hs, cs, hbuf, cbuf, sems)

    def wait_all(which_slot):
        for kind in range(2):
            for g in range(g_n):
                for cp in copies(which_slot, kind, g, 0, 0):
                    cp.wait()

    @pl.when(i > 0)
    def _():
        wait_all(1 - slot)

    @pl.when(i == 0)
    def _():
        top = hbuf.shape[1] - WIN_ROWS
        for g in range(g_n):
            for cp in copies(slot, 1, g, SORTED_ROWS, top):
                cp.start()
        for g in range(g_n):
            for cp in copies(slot, 1, g, 0, 0):
                cp.wait()

    tiles_left = n_tiles - 1 - i
    for g in range(g_n):
        c_g = run[g]
        c_next = c_g + _pad_rows(counts[g])
        start_ref[i, g] = c_g
        cnt_ref[i, g] = counts[g]
        run[g] = c_next
        for cp in copies(slot, 0, g, pl.multiple_of(seg_start[g], ROW_ALIGN), pl.multiple_of(c_g, ROW_ALIGN)):
            cp.start()
        dead = c_next + tiles_left * WIN_ROWS + WIN_ROWS
        for cp in copies(slot, 1, g, SORTED_ROWS, pl.multiple_of(dead, ROW_ALIGN)):
            cp.start()

    @pl.when(i == n_tiles - 1)
    def _():
        wait_all(slot)
        for g in range(g_n):
            for cp in copies(slot, 0, g, SORTED_ROWS, pl.multiple_of(run[g], ROW_ALIGN)):
                cp.start()
        for g in range(g_n):
            for cp in copies(slot, 0, g, 0, 0):
                cp.wait()
        _write_slot_tables(run, start_ref, (slot_grp, slot_blk, slot_valid, slot_first, slot_next), out_start_ref)


def _merge(y_ret, y_na, y_xa, proj, x2d, w_ret_o, w_na_o, w_xa_o, w_out, g_ffn, w_router, b_router):
    t, d = x2d.shape
    tm = SORT_TILE
    n_tiles = t // tm
    rows = _group_buf_rows(t)
    gate_blk = OFF_GATE // d
    before = jnp.asarray(np.tril(np.ones((tm, tm), np.float32), -1), BF16)
    full = lambda a: pl.BlockSpec(a.shape, lambda i: (0,) * a.ndim)
    smem = pl.BlockSpec(memory_space=pltpu.SMEM)
    hbm = pl.BlockSpec(memory_space=pl.ANY)
    return pl.pallas_call(
        _merge_kernel,
        grid=(n_tiles,),
        in_specs=[
            pl.BlockSpec((tm, RET_V_WIDTH), lambda i: (i, 0)),
            pl.BlockSpec((tm, NA_WIDTH), lambda i: (i, 0)),
            pl.BlockSpec((tm, XA_WIDTH), lambda i: (i, 0)),
            pl.BlockSpec((tm, d), lambda i: (i, gate_blk)),
            pl.BlockSpec((tm, d), lambda i: (i, gate_blk + 1)),
            pl.BlockSpec((tm, d), lambda i: (i, gate_blk + 2)),
            pl.BlockSpec((tm, d), lambda i: (i, 0)),
            full(w_ret_o), full(w_na_o), full(w_xa_o), full(w_out), full(g_ffn), full(w_router), full(b_router),
            full(before),
        ],
        out_specs=[
            pl.BlockSpec((tm, d), lambda i: (i, 0)),
            pl.BlockSpec((tm, LANES), lambda i: (i, 0)),
            smem, smem, *([smem] * 5), hbm, hbm,
        ],
        out_shape=[
            jax.ShapeDtypeStruct((t, d), F32),
            jax.ShapeDtypeStruct((t, LANES), F32),
            jax.ShapeDtypeStruct((n_tiles, MOE_GROUPS), jnp.int32),
            jax.ShapeDtypeStruct((n_tiles, MOE_GROUPS), jnp.int32),
            *([jax.ShapeDtypeStruct((_expert_slot_count(t),), jnp.int32)] * 5),
            jax.ShapeDtypeStruct((MOE_GROUPS, rows, d), BF16),
            jax.ShapeDtypeStruct((MOE_GROUPS, rows, LANES), F32),
        ],
        scratch_shapes=[
            pltpu.VMEM((2, SCR_ROWS, d), BF16),
            pltpu.VMEM((2, SCR_ROWS, LANES), F32),
            pltpu.SMEM((MOE_GROUPS,), jnp.int32),
            pltpu.SMEM((n_tiles, MOE_GROUPS), jnp.int32),
            pltpu.SemaphoreType.DMA((2, 4, MOE_GROUPS)),
        ],
        compiler_params=_params("arbitrary"),
        name="merge_router",
    )(y_ret, y_na, y_xa, proj, proj, proj, x2d, w_ret_o, w_na_o, w_xa_o, w_out, g_ffn, w_router, b_router, before)


def _expert_slot_count(t):
    return (t + (t // SORT_TILE) * MOE_GROUPS * (ROW_ALIGN - 1)) // EXPERT_TILE + MOE_GROUPS + 1


def _write_slot_tables(run, start_ref, slot_refs, out_start_ref):
    nblk, first = [], []
    total = jnp.int32(0)
    for g in range(MOE_GROUPS):
        n = (run[g] + EXPERT_TILE - 1) // EXPERT_TILE
        first.append(total)
        nblk.append(n)
        total = total + n

    def fill_slot(j, carry):
        for ref, value in zip(slot_refs, _expert_slot(j, nblk, first, total)):
            ref[j] = value.astype(jnp.int32)
        return carry

    lax.fori_loop(0, slot_refs[0].shape[0], fill_slot, 0)

    def fill_tile(i, carry):
        for g in range(MOE_GROUPS):
            out_start_ref[i, g] = first[g] * EXPERT_TILE + start_ref[i, g]
        return carry

    lax.fori_loop(0, out_start_ref.shape[0], fill_tile, 0)


def _expert_slot(j, nblk, first, total):
    valid = j < total
    jc = jnp.clip(j, 0, jnp.maximum(total - 1, 0))
    grp = jnp.int32(0)
    for g in range(MOE_GROUPS):
        grp = grp + (jc >= first[g] + nblk[g]).astype(jnp.int32)
    grp = jnp.minimum(grp, MOE_GROUPS - 1)
    first_blk = jnp.int32(0)
    nxt = jnp.int32(-1)
    for g in reversed(range(MOE_GROUPS)):
        first_blk = jnp.where(grp == g, first[g], first_blk)
        nxt = jnp.where((g > grp) & (nblk[g] > 0), g, nxt)
    return grp, jc - first_blk, valid, valid & (jc == first_blk), nxt


def _experts_kernel(grp_ref, blk_ref, valid_ref, first_ref, next_ref, h_ref, c_ref, wg_hbm, wu_hbm, wd_hbm, o_ref,
                    stage_g, stage_u, stage_d, wg_ref, wu_ref, wd_ref, sems):
    j = pl.program_id(0)
    grp, valid, is_first, nxt = grp_ref[j], valid_ref[j] == 1, first_ref[j] == 1, next_ref[j]

    def expert_copies(g, e):
        return (pltpu.make_async_copy(wg_hbm.at[g, e], stage_g.at[e], sems.at[0, e]),
                pltpu.make_async_copy(wu_hbm.at[g, e], stage_u.at[e], sems.at[1, e]),
                pltpu.make_async_copy(wd_hbm.at[g, e], stage_d.at[e], sems.at[2, e]))

    @pl.when(j == 0)
    def _():
        for e in range(MOE_EXPERTS_PER_GROUP):
            for cp in expert_copies(grp, e):
                cp.start()

    def run_block(convert_first):
        h = h_ref[...]
        c = c_ref[...]
        lane = lax.broadcasted_iota(jnp.int32, c.shape, 1)
        out = None
        for e in range(MOE_EXPERTS_PER_GROUP):
            if convert_first:
                for cp in expert_copies(grp, e):
                    cp.wait()
                for stage, dst in ((stage_g, wg_ref), (stage_u, wu_ref), (stage_d, wd_ref)):
                    for r in range(0, stage.shape[1], CAST_ROWS):
                        dst[e, r:r + CAST_ROWS, :] = stage[e, r:r + CAST_ROWS, :].astype(dst.dtype)
            a = _dot(h, wg_ref[e])
            u = _dot(h, wu_ref[e])
            cw = jnp.sum(jnp.where(lane == e, c, 0.0), axis=-1, keepdims=True)
            hid = (_silu(a) * u * cw).astype(BF16)
            part = _dot(hid, wd_ref[e])
            out = part if out is None else out + part
        o_ref[...] = out.astype(o_ref.dtype)

    @pl.when(is_first)
    def _():
        run_block(True)

        @pl.when(nxt >= 0)
        def _():
            for e in range(MOE_EXPERTS_PER_GROUP):
                for cp in expert_copies(nxt, e):
                    cp.start()

    @pl.when(valid & jnp.logical_not(is_first))
    def _():
        run_block(False)

    @pl.when(jnp.logical_not(valid))
    def _():
        o_ref[...] = jnp.zeros_like(o_ref)


def _experts(hbuf, cbuf, wg, wu, wd, slots):
    g_n, rows, d = hbuf.shape
    e_n, f = wg.shape[1], wg.shape[3]
    tm = EXPERT_TILE
    n_slots = slots[0].shape[0]
    hbm = pl.BlockSpec(memory_space=pl.ANY)
    grid_spec = pltpu.PrefetchScalarGridSpec(
        num_scalar_prefetch=5,
        grid=(n_slots,),
        in_specs=[
            pl.BlockSpec((None, tm, d), lambda j, grp, blk, v, gf, gx: (grp[j], blk[j], 0)),
            pl.BlockSpec((None, tm, LANES), lambda j, grp, blk, v, gf, gx: (grp[j], blk[j], 0)),
            hbm, hbm, hbm,
        ],
        out_specs=pl.BlockSpec((tm, d), lambda j, grp, blk, v, gf, gx: (j, 0)),
        scratch_shapes=[
            pltpu.VMEM((e_n, d, f), wg.dtype),
            pltpu.VMEM((e_n, d, f), wu.dtype),
            pltpu.VMEM((e_n, f, d), wd.dtype),
            pltpu.VMEM((e_n, d, f), BF16),
            pltpu.VMEM((e_n, d, f), BF16),
            pltpu.VMEM((e_n, f, d), BF16),
            pltpu.SemaphoreType.DMA((3, e_n)),
        ],
    )
    return pl.pallas_call(
        _experts_kernel,
        grid_spec=grid_spec,
        out_shape=jax.ShapeDtypeStruct((n_slots * tm, d), BF16),
        compiler_params=pltpu.CompilerParams(dimension_semantics=("arbitrary",),
                                             vmem_limit_bytes=STAGING_VMEM_LIMIT),
        name="experts",
    )(*slots, hbuf, cbuf, wg, wu, wd)


FINISH_TILES = 2


def _finish_kernel(start_ref, cnt_ref, *refs, final_norm):
    n_win = 2 * MOE_GROUPS
    wins = refs[:FINISH_TILES * n_win]
    pos_ref, x2_ref, gfin_ref, o_ref, sorted_scr = refs[FINISH_TILES * n_win:]
    i = pl.program_id(0)
    tm = SORT_TILE
    half_pieces = HALF_WIN // ROW_ALIGN

    @pl.when(i == 0)
    def _():
        sorted_scr[...] = jnp.zeros_like(sorted_scr)

    for tt in range(FINISH_TILES):
        tile = i * FINISH_TILES + tt
        rows = pl.ds(tt * tm, tm)
        seg = jnp.int32(0)
        for g in range(MOE_GROUPS):
            padded = _pad_rows(cnt_ref[tile, g])
            pieces = padded // ROW_ALIGN
            for half in range(2):
                win = wins[tt * n_win + 2 * g + half]
                base = seg + half * HALF_WIN

                def copy_piece(k, carry, win=win, base=base, tt=tt):
                    src = pl.multiple_of(k * ROW_ALIGN, ROW_ALIGN)
                    dst = pl.multiple_of(base + k * ROW_ALIGN, ROW_ALIGN)
                    sorted_scr[tt, pl.ds(dst, ROW_ALIGN), :] = win[pl.ds(src, ROW_ALIGN), :]
                    return carry

                lax.fori_loop(0, jnp.clip(pieces - half * half_pieces, 0, half_pieces), copy_piece, 0)
            seg = seg + padded

        pos = pos_ref[rows, 0:1]
        unperm = jnp.where(lax.broadcasted_iota(jnp.int32, (tm, SORTED_ROWS), 1).astype(F32) == pos, 1.0, 0.0)
        y = x2_ref[rows, :] + _dot(unperm.astype(BF16), sorted_scr[tt])
        if final_norm:
            y = _rmsnorm_f32(y, gfin_ref[...])
        o_ref[rows, :] = y


def _finish(mbuf, starts, counts, pos, x2, g_final, final_norm):
    t, d = x2.shape
    tm = SORT_TILE * FINISH_TILES

    def window(tt, g, half):
        def index(i, st, ct):
            tile = i * FINISH_TILES + tt
            row = st[tile, g] + half * HALF_WIN
            if half:
                row = jnp.where(_pad_rows(ct[tile, g]) > HALF_WIN, row, 0)
            return pl.multiple_of(row, ROW_ALIGN), 0

        return pl.BlockSpec((pl.Element(HALF_WIN), pl.Element(d)), index)

    windows = [window(tt, g, half) for tt in range(FINISH_TILES) for g in range(MOE_GROUPS) for half in range(2)]
    grid_spec = pltpu.PrefetchScalarGridSpec(
        num_scalar_prefetch=2,
        grid=(t // tm,),
        in_specs=[
            *windows,
            pl.BlockSpec((tm, LANES), lambda i, st, ct: (i, 0)),
            pl.BlockSpec((tm, d), lambda i, st, ct: (i, 0)),
            pl.BlockSpec((1, d), lambda i, st, ct: (0, 0)),
        ],
        out_specs=pl.BlockSpec((tm, d), lambda i, st, ct: (i, 0)),
        scratch_shapes=[pltpu.VMEM((FINISH_TILES, SORTED_ROWS, d), BF16)],
    )
    return pl.pallas_call(
        functools.partial(_finish_kernel, final_norm=final_norm),
        grid_spec=grid_spec,
        out_shape=jax.ShapeDtypeStruct((t, d), F32),
        compiler_params=_params("arbitrary"),
        name="finish",
    )(starts, counts, *([mbuf] * len(windows)), pos, x2, g_final)


def kernel(x, mem, g_mix, w_in, ret_decay_fwd, ret_decay_bwd, ret_norm_gain, w_ret_o, na_rpb, w_na_o, g_mem, w_mem_kv, w_xa_o, w_out, g_ffn, w_router_group, b_router_group, w_router_expert, b_router_expert, w_exp_gate, w_exp_up, w_exp_down, g_final):
    b, s, d = x.shape
    depth = w_in.shape[0]
    t = b * s
    cos_t, sin_t = _rope_tables(s)
    row = lambda v: v.reshape(1, -1).astype(F32)
    x2d = x.reshape(t, d)
    for l in range(depth):
        proj = _inproj(x2d, row(g_mix[l]), w_in[l], cos_t, sin_t)
        proj3 = proj.reshape(b, s, IN_WIDTH)
        y_ret = _retention(proj3, ret_decay_fwd[l].astype(F32), ret_decay_bwd[l].astype(F32),
                           row(ret_norm_gain[l]))
        y_na = _neighbourhood_attention(proj3, _na_bias_table(na_rpb[l]))
        y_xa = _memory_attention(proj3, mem, row(g_mem[l]), w_mem_kv[l].astype(BF16))
        n_r = MOE_GROUPS + N_EXPERTS
        w_router = jnp.pad(jnp.concatenate([w_router_group[l], w_router_expert[l]], axis=1).astype(F32),
                           ((0, 0), (0, LANES - n_r)))
        b_router = jnp.pad(jnp.concatenate([b_router_group[l], b_router_expert[l]]).astype(F32),
                           (0, LANES - n_r)).reshape(1, LANES)
        x2, pos, counts, out_starts, *slots, hbuf, cbuf = _merge(
            y_ret.reshape(t, -1), y_na.reshape(t, -1), y_xa.reshape(t, -1), proj, x2d,
            w_ret_o[l].astype(BF16), w_na_o[l].astype(BF16), w_xa_o[l].astype(BF16),
            w_out[l].astype(BF16), row(g_ffn[l]), w_router, b_router)
        mbuf = _experts(hbuf, cbuf, w_exp_gate[l], w_exp_up[l], w_exp_down[l], slots)
        x2d = _finish(mbuf, out_starts, counts, pos, x2, row(g_final), final_norm=(l == depth - 1))
    return x2d.reshape(b, s, d)
```
